```python
import jax, jax.numpy as jnp
from jax import lax
import numpy as np

D_MODEL = 1024
BATCH = 2
SEQ = 8192
DEPTH = 2

N_MIXERS = 2
N_A = (DEPTH + 1) // 2
N_B = DEPTH // 2
HG_EXPAND = 128
HG_HEADS = D_MODEL // 128
HG_FDIM = HG_HEADS * HG_EXPAND
HG_VDIM = D_MODEL // HG_HEADS
HG_CHUNK = 64
FOX_HEAD_DIM = 64
FOX_HEADS = D_MODEL // FOX_HEAD_DIM
FOX_BLOCK = 128
D_FF = 4 * D_MODEL
N_MOD = 6
EPS = 1e-6

kernel_name = "hybrid_hgrn2_fox_adaln_block"


def rmsnorm(x, w):
    xf = x.astype(jnp.float32)
    y = xf * lax.rsqrt(jnp.mean(xf * xf, axis=-1, keepdims=True) + EPS)
    return (y * w.astype(jnp.float32)).astype(x.dtype)


def modulate(x, w, shift, scale):
    return rmsnorm(x, w) * (1 + scale[:, None, :]) + shift[:, None, :]


def hgrn2_mixer(h, w_in, w_out, lb, gn_w):
    B, S, _ = h.shape
    H, dk, dv, C = HG_HEADS, HG_EXPAND, HG_VDIM, HG_CHUNK
    N = S // C
    proj = h @ w_in
    q, fz, i, g = jnp.split(proj, [HG_FDIM, 2 * HG_FDIM, 2 * HG_FDIM + D_MODEL], axis=-1)
    q = jax.nn.silu(q.astype(jnp.float32))
    lbf = lb.astype(jnp.float32)
    logf = jnp.logaddexp(jnp.log(lbf), jnp.log1p(-lbf) + jax.nn.log_sigmoid(fz.astype(jnp.float32)))
    k = -jnp.expm1(logf)
    v = i.astype(jnp.float32)

    def to_chunks(t, d):
        return t.reshape(B, N, C, H, d).transpose(1, 0, 3, 2, 4)

    qc, kc, gc, vc = to_chunks(q, dk), to_chunks(k, dk), to_chunks(logf, dk), to_chunks(v, dv)
    causal = jnp.tril(jnp.ones((C, C), dtype=bool))

    def step(state, inp):
        qb, kb, gb, vb = inp
        G = jnp.cumsum(gb, axis=2)
        diff = G[:, :, :, None, :] - G[:, :, None, :, :]
        decay = jnp.exp(jnp.where(causal[:, :, None], diff, -jnp.inf))
        A = jnp.einsum('bhtk,bhsk,bhtsk->bhts', qb, kb, decay)
        o = jnp.einsum('bhts,bhsv->bhtv', A, vb) + jnp.einsum('bhtk,bhkv->bhtv', qb * jnp.exp(G), state)
        G_last = G[:, :, -1:, :]
        new_state = jnp.exp(G_last[:, :, 0, :])[..., None] * state + \
            jnp.einsum('bhsk,bhsv->bhkv', kb * jnp.exp(G_last - G), vb)
        return new_state, o

    state0 = jnp.zeros((B, H, dk, dv), jnp.float32)
    _, o = lax.scan(step, state0, (qc, kc, gc, vc))
    o = o.transpose(1, 0, 3, 2, 4).reshape(B, S, H, dv)
    o = rmsnorm(o, gn_w).reshape(B, S, D_MODEL)
    o = (o * jax.nn.silu(g.astype(jnp.float32))).astype(h.dtype)
    return o @ w_out


def fox_mixer(h, w_in, b_f, qn_w, kn_w, w_out):
    B, S, _ = h.shape
    H, dh, blk = FOX_HEADS, FOX_HEAD_DIM, FOX_BLOCK
    proj = h @ w_in
    q, k, v, g, fz = jnp.split(proj, [D_MODEL, 2 * D_MODEL, 3 * D_MODEL, 4 * D_MODEL], axis=-1)
    q = rmsnorm(q.reshape(B, S, H, dh), qn_w).transpose(0, 2, 1, 3)
    k = rmsnorm(k.reshape(B, S, H, dh), kn_w).transpose(0, 2, 1, 3)
    v = v.reshape(B, S, H, dh).transpose(0, 2, 1, 3)
    logf = jax.nn.log_sigmoid(fz.astype(jnp.float32) + b_f.astype(jnp.float32))
    F = jnp.cumsum(logf, axis=1).transpose(0, 2, 1)
    scale = 1.0 / np.sqrt(dh)
    kpos = jnp.arange(S)

    def block(bi):
        start = bi * blk
        qb = lax.dynamic_slice_in_dim(q, start, blk, axis=2)
        Fq = lax.dynamic_slice_in_dim(F, start, blk, axis=2)
        s = jnp.einsum('bhqd,bhkd->bhqk', qb, k).astype(jnp.float32) * scale
        s = s + (Fq[..., :, None] - F[..., None, :])
        qpos = start + jnp.arange(blk)
        s = jnp.where(kpos[None, :] <= qpos[:, None], s, -jnp.inf)
        p = jax.nn.softmax(s, axis=-1)
        return jnp.einsum('bhqk,bhkd->bhqd', p.astype(v.dtype), v)

    o = lax.map(block, jnp.arange(S // blk))
    o = o.transpose(1, 0, 3, 2, 4).reshape(B, S, D_MODEL)
    o = o * jax.nn.sigmoid(g)
    return o @ w_out


def sqrelu_mlp(h, w1, w2):
    a = jax.nn.relu(h @ w1)
    return (a * a) @ w2


def setup_inputs(seed: int = 0) -> dict:
    key = jax.random.key(seed)
    ks = jax.random.split(key, 20)
    D, F = D_MODEL, HG_FDIM
    nrm = jax.random.normal
    return {
        "x": nrm(ks[0], (BATCH, SEQ, D), jnp.float32),
        "c": nrm(ks[1], (BATCH, D), jnp.float32),
        "w_mod": nrm(ks[2], (DEPTH, D, N_MOD * D), jnp.float32) * (0.5 * D ** -0.5),
        "b_mod": nrm(ks[3], (DEPTH, N_MOD * D), jnp.float32) * 0.02,
        "norm1_w": 1.0 + 0.05 * nrm(ks[4], (DEPTH, D), jnp.float32),
        "norm2_w": 1.0 + 0.05 * nrm(ks[5], (DEPTH, D), jnp.float32),
        "hg_w_in": nrm(ks[6], (N_A, D, 2 * F + 2 * D), jnp.float32) * D ** -0.5,
        "hg_w_out": nrm(ks[7], (N_A, D, D), jnp.float32) * D ** -0.5,
        "hg_lb": 0.5 * nrm(ks[8], (DEPTH + 1, F), jnp.float32),
        "hg_gn_w": 1.0 + 0.05 * nrm(ks[9], (N_A, HG_VDIM), jnp.float32),
        "fox_w_in": nrm(ks[10], (N_B, D, 4 * D + FOX_HEADS), jnp.float32) * D ** -0.5,
        "fox_b_f": jax.random.uniform(ks[11], (N_B, FOX_HEADS), jnp.float32, 1.0, 4.0),
        "fox_qn_w": 1.0 + 0.05 * nrm(ks[12], (N_B, FOX_HEAD_DIM), jnp.float32),
        "fox_kn_w": 1.0 + 0.05 * nrm(ks[13], (N_B, FOX_HEAD_DIM), jnp.float32),
        "fox_w_out": nrm(ks[14], (N_B, D, D), jnp.float32) * D ** -0.5,
        "mlp_w1": nrm(ks[15], (DEPTH, D, D_FF), jnp.float32) * D ** -0.5,
        "mlp_w2": nrm(ks[16], (DEPTH, D_FF, D), jnp.float32) * D_FF ** -0.5,
        "final_w": 1.0 + 0.05 * nrm(ks[17], (D,), jnp.float32),
    }


def reference(x, c, w_mod, b_mod, norm1_w, norm2_w, hg_w_in, hg_w_out, hg_lb, hg_gn_w,
              fox_w_in, fox_b_f, fox_qn_w, fox_kn_w, fox_w_out, mlp_w1, mlp_w2, final_w):
    lb_all = jnp.cumsum(jax.nn.softmax(hg_lb.astype(jnp.float32), axis=0), axis=0)
    c_act = jax.nn.silu(c)
    for i in range(DEPTH):
        mod = c_act @ w_mod[i] + b_mod[i]
        sh1, sc1, g1, sh2, sc2, g2 = jnp.split(mod, N_MOD, axis=-1)
        h = modulate(x, norm1_w[i], sh1, sc1)
        j = i // N_MIXERS
        if i % N_MIXERS == 0:
            y = hgrn2_mixer(h, hg_w_in[j], hg_w_out[j], lb_all[i], hg_gn_w[j])
        else:
            y = fox_mixer(h, fox_w_in[j], fox_b_f[j], fox_qn_w[j], fox_kn_w[j], fox_w_out[j])
        x = x + g1[:, None, :] * y
        h = modulate(x, norm2_w[i], sh2, sc2)
        x = x + g2[:, None, :] * sqrelu_mlp(h, mlp_w1[i], mlp_w2[i])
    return rmsnorm(x, final_w)
```

```python
import functools

import numpy as np
import jax
import jax.numpy as jnp
from jax import lax
from jax.experimental import pallas as pl
from jax.experimental.pallas import tpu as pltpu

F32 = jnp.float32
BF16 = jnp.bfloat16
EPS = 1e-6
N_MOD = 6
HG_HEAD = 128
HG_CHUNK = 64
HG_SUB = 16
FOX_HEAD = 64
LANES = 128
VMEM_LIMIT = 56 * 1024 * 1024

_HI = lax.Precision.HIGHEST


def _cparams(sem):
    return pltpu.CompilerParams(dimension_semantics=sem, vmem_limit_bytes=VMEM_LIMIT)


def _resident(shape):
    nd = len(shape)
    return pl.BlockSpec(shape, lambda *_: (0,) * nd, pipeline_mode=pl.Buffered(1))


def _sigmoid(x):
    return 1.0 / (1.0 + jnp.exp(-x))


def _modulate(x, nw, shift, scale):
    ms = jnp.mean(x * x, axis=-1, keepdims=True)
    y = x * lax.rsqrt(ms + EPS)
    return (y * nw) * (1.0 + scale) + shift


def _mod_kernel(c_ref, w_ref, b_ref, o_ref):
    c = c_ref[...]
    ca = c * _sigmoid(c)
    o_ref[0] = jnp.dot(ca, w_ref[0], precision=_HI, preferred_element_type=F32) + b_ref[0]


def _mod_call(c, w_mod, b_mod):
    depth, d, n = w_mod.shape
    b = c.shape[0]
    tn = 1024
    return pl.pallas_call(
        _mod_kernel,
        grid=(depth, n // tn),
        in_specs=[
            pl.BlockSpec((b, d), lambda l, j: (0, 0)),
            pl.BlockSpec((1, d, tn), lambda l, j: (l, 0, j)),
            pl.BlockSpec((1, 1, tn), lambda l, j: (l, 0, j)),
        ],
        out_specs=pl.BlockSpec((1, b, tn), lambda l, j: (l, 0, j)),
        out_shape=jax.ShapeDtypeStruct((depth, b, n), F32),
        compiler_params=_cparams(("arbitrary", "arbitrary")),
        name="mod",
    )(c, w_mod, b_mod.reshape(depth, 1, n))


def _hg_proj_kernel(x_ref, mod_ref, nw_ref, lb_ref, w_ref, q_ref, k_ref, lf_ref, v_ref, g_ref, *, layer, d):
    mod = mod_ref[0]
    h = _modulate(x_ref[...], nw_ref[...], mod[:, 0:d], mod[:, d:2 * d]).astype(BF16)

    pq = jnp.dot(h, w_ref[:, 0:d], preferred_element_type=F32)
    q_ref[...] = (pq * _sigmoid(pq)).astype(BF16)

    lbp = lb_ref[...]
    e = jnp.exp(lbp - jnp.max(lbp, axis=0, keepdims=True))
    lb = jnp.sum(e[0:layer + 1], axis=0, keepdims=True) / jnp.sum(e, axis=0, keepdims=True)

    z = jnp.dot(h, w_ref[:, d:2 * d], preferred_element_type=F32)
    kk = (1.0 - lb) * _sigmoid(-z)
    f = lb + (1.0 - lb) * _sigmoid(z)
    lf_ref[...] = jnp.where(kk < 0.5, jnp.log1p(-kk), jnp.log(f))
    k_ref[...] = kk.astype(BF16)

    v_ref[...] = jnp.dot(h, w_ref[:, 2 * d:3 * d], preferred_element_type=F32).astype(BF16)
    pg = jnp.dot(h, w_ref[:, 3 * d:4 * d], preferred_element_type=F32)
    g_ref[...] = (pg * _sigmoid(pg)).astype(BF16)


def _hg_proj_call(x2, mod3, nw, hg_lb, w_in, *, layer, batch, tm):
    t, d = x2.shape
    tiles_per_batch = (t // batch) // tm
    row = pl.BlockSpec((tm, d), lambda i: (i, 0))
    out_bf = jax.ShapeDtypeStruct((t, d), BF16)
    return pl.pallas_call(
        functools.partial(_hg_proj_kernel, layer=layer, d=d),
        grid=(t // tm,),
        in_specs=[
            row,
            pl.BlockSpec((1, 1, N_MOD * d), lambda i: (layer * batch + i // tiles_per_batch, 0, 0)),
            _resident((1, d)),
            _resident(hg_lb.shape),
            _resident(w_in.shape),
        ],
        out_specs=[row, row, row, row, row],
        out_shape=[out_bf, out_bf, jax.ShapeDtypeStruct((t, d), F32), out_bf, out_bf],
        compiler_params=_cparams(("arbitrary",)),
        name="hg_proj",
    )(x2, mod3, nw, hg_lb, w_in)


def _hg_rec_kernel(q_ref, k_ref, lf_ref, v_ref, g_ref, gnw_ref, o_ref, st_ref, gs_ref, ks_ref, *, n_chunks):
    c, sub = HG_CHUNK, HG_SUB
    n_sub = c // sub

    @pl.when(pl.program_id(2) == 0)
    def _():
        st_ref[...] = jnp.zeros_like(st_ref)

    r_i = lax.broadcasted_iota(jnp.int32, (c, c), 0)
    c_i = lax.broadcasted_iota(jnp.int32, (c, c), 1)
    tri = (c_i <= r_i).astype(F32)
    ones_w = jnp.ones((HG_HEAD, c), BF16)
    sub_row = lax.broadcasted_iota(jnp.int32, (sub, HG_HEAD), 0)
    a_lane = lax.broadcasted_iota(jnp.int32, (sub, c), 1)
    gnw = gnw_ref[...]
    nt = (((1,), (1,)), ((), ()))
    tn = (((0,), (0,)), ((), ()))

    def chunk(i, carry):
        c0 = pl.multiple_of(i * c, c)
        rows = pl.ds(c0, c)
        lf = lf_ref[rows, :]
        q = q_ref[rows, :].astype(F32)
        k = k_ref[rows, :].astype(F32)
        v = v_ref[rows, :]
        gcum = jnp.dot(tri, lf, precision=_HI, preferred_element_type=F32)
        gs_ref[...] = gcum
        ks_ref[...] = k
        g_last = gcum[c - 1:c, :]

        st = st_ref[...]
        qg = (q * jnp.exp(gcum)).astype(BF16)
        o = lax.dot_general(qg, st.astype(BF16), nt, preferred_element_type=F32)
        kd = (k * jnp.exp(g_last - gcum)).astype(BF16)
        st_ref[...] = st * jnp.exp(g_last) + lax.dot_general(v, kd, tn, preferred_element_type=F32)

        a_rows = []
        for j in range(n_sub):
            lo = j * sub
            gt = gcum[lo:lo + sub, :]
            qt = q[lo:lo + sub, :]
            ps = []
            for s in range(sub):
                g_s = gs_ref[pl.ds(lo + s, 1), :]
                k_s = ks_ref[pl.ds(lo + s, 1), :]
                dlt = jnp.where(sub_row >= s, gt - g_s, -jnp.inf)
                ps.append(((qt * k_s) * jnp.exp(dlt)).astype(BF16))
            r = jnp.dot(jnp.concatenate(ps, axis=0), ones_w, preferred_element_type=F32)
            a_j = jnp.zeros((sub, c), F32)
            for s in range(sub):
                a_j = jnp.where(a_lane == lo + s, r[s * sub:(s + 1) * sub, :], a_j)
            if j > 0:
                g_b = gcum[lo - 1:lo, :]
                qh = (qt * jnp.exp(gt - g_b)).astype(BF16)
                kh = (k[0:lo, :] * jnp.exp(g_b - gcum[0:lo, :])).astype(BF16)
                kh = jnp.concatenate([kh, jnp.zeros((c - lo, HG_HEAD), BF16)], axis=0)
                a_j = a_j + lax.dot_general(qh, kh, nt, preferred_element_type=F32)
            a_rows.append(a_j)
        a = jnp.concatenate(a_rows, axis=0).astype(BF16)
        o = o + jnp.dot(a, v, preferred_element_type=F32)

        on = o * lax.rsqrt(jnp.mean(o * o, axis=-1, keepdims=True) + EPS) * gnw
        o_ref[rows, :] = (on * g_ref[rows, :].astype(F32)).astype(BF16)
        return carry

    lax.fori_loop(0, n_chunks, chunk, 0)


def _hg_rec_call(q, k, lf, v, g, gnw, *, batch, tc):
    t, d = q.shape
    heads = d // HG_HEAD
    steps = (t // batch) // tc
    blk = pl.BlockSpec((tc, HG_HEAD), lambda b, h, s: (b * steps + s, h))
    return pl.pallas_call(
        functools.partial(_hg_rec_kernel, n_chunks=tc // HG_CHUNK),
        grid=(batch, heads, steps),
        in_specs=[blk, blk, blk, blk, blk, pl.BlockSpec((1, HG_HEAD), lambda b, h, s: (0, 0))],
        out_specs=blk,
        out_shape=jax.ShapeDtypeStruct((t, d), BF16),
        scratch_shapes=[
            pltpu.VMEM((HG_HEAD, HG_HEAD), F32),
            pltpu.VMEM((HG_CHUNK, HG_HEAD), F32),
            pltpu.VMEM((HG_CHUNK, HG_HEAD), F32),
        ],
        compiler_params=_cparams(("arbitrary", "arbitrary", "arbitrary")),
        name="hg_rec",
    )(q, k, lf, v, g, gnw)


def _head_rmsnorm(p, w_row):
    low = lax.broadcasted_iota(jnp.int32, (1, LANES), 1) < FOX_HEAD
    outs = []
    for j in range(p.shape[1] // LANES):
        pj = p[:, j * LANES:(j + 1) * LANES]
        ss = pj * pj
        s_lo = jnp.sum(jnp.where(low, ss, 0.0), axis=-1, keepdims=True)
        s_hi = jnp.sum(jnp.where(low, 0.0, ss), axis=-1, keepdims=True)
        inv = lax.rsqrt(jnp.where(low, s_lo, s_hi) * (1.0 / FOX_HEAD) + EPS)
        outs.append(pj * inv)
    return jnp.concatenate(outs, axis=1) * w_row


def _fox_proj_kernel(x_ref, mod_ref, nw_ref, qw_ref, kw_ref, bf_ref, w_ref, wf_ref,
                     q_ref, k_ref, v_ref, g_ref, lf_ref, *, d):
    mod = mod_ref[0]
    h = _modulate(x_ref[...], nw_ref[...], mod[:, 0:d], mod[:, d:2 * d]).astype(BF16)
    scale = 1.0 / np.sqrt(FOX_HEAD)
    pq = jnp.dot(h, w_ref[:, 0:d], preferred_element_type=F32)
    q_ref[...] = (_head_rmsnorm(pq, qw_ref[...]) * scale).astype(BF16)
    pk = jnp.dot(h, w_ref[:, d:2 * d], preferred_element_type=F32)
    k_ref[...] = _head_rmsnorm(pk, kw_ref[...]).astype(BF16)
    v_ref[...] = jnp.dot(h, w_ref[:, 2 * d:3 * d], preferred_element_type=F32).astype(BF16)
    pg = jnp.dot(h, w_ref[:, 3 * d:4 * d], preferred_element_type=F32)
    g_ref[...] = _sigmoid(pg).astype(BF16)
    u = jnp.dot(h, wf_ref[...], preferred_element_type=F32) + bf_ref[...]
    lf_ref[...] = jnp.minimum(u, 0.0) - jnp.log1p(jnp.exp(-jnp.abs(u)))


def _fox_proj_call(x2, mod3, nw, qw, kw, bf, w_main, w_f, *, layer, batch, tm):
    t, d = x2.shape
    nh = w_f.shape[1]
    tiles_per_batch = (t // batch) // tm
    row = pl.BlockSpec((tm, d), lambda i: (i, 0))
    out_bf = jax.ShapeDtypeStruct((t, d), BF16)
    return pl.pallas_call(
        functools.partial(_fox_proj_kernel, d=d),
        grid=(t // tm,),
        in_specs=[
            row,
            pl.BlockSpec((1, 1, N_MOD * d), lambda i: (layer * batch + i // tiles_per_batch, 0, 0)),
            _resident((1, d)), _resident((1, d)), _resident((1, d)), _resident((1, nh)),
            _resident(w_main.shape), _resident(w_f.shape),
        ],
        out_specs=[row, row, row, row, pl.BlockSpec((tm, nh), lambda i: (i, 0))],
        out_shape=[out_bf, out_bf, out_bf, out_bf, jax.ShapeDtypeStruct((t, nh), F32)],
        compiler_params=_cparams(("arbitrary",)),
        name="fox_proj",
    )(x2, mod3, nw, qw, kw, bf, w_main, w_f)


def _cumsum_kernel(x_ref, o_ref, *, blk):
    rows, n = x_ref.shape
    r_i = lax.broadcasted_iota(jnp.int32, (blk, blk), 0)
    c_i = lax.broadcasted_iota(jnp.int32, (blk, blk), 1)
    upper = (r_i <= c_i).astype(F32)
    carry = jnp.zeros((rows, 1), F32)
    for j in range(n // blk):
        xb = x_ref[:, j * blk:(j + 1) * blk]
        fb = jnp.dot(xb, upper, precision=_HI, preferred_element_type=F32) + carry
        o_ref[:, j * blk:(j + 1) * blk] = fb
        carry = fb[:, blk - 1:blk]


def _cumsum_call(x):
    return pl.pallas_call(
        functools.partial(_cumsum_kernel, blk=512),
        out_shape=jax.ShapeDtypeStruct(x.shape, F32),
        compiler_params=pltpu.CompilerParams(vmem_limit_bytes=VMEM_LIMIT),
        name="fox_cumsum",
    )(x)


def _fox_attn_kernel(qi_ref, ki_ref, q_ref, k_ref, v_ref, g_ref, fc_ref, fr_ref, o_ref,
                     qm_ref, fq_ref, m_ref, l_ref, acc_ref, *, tq, tk):
    step = pl.program_id(2)
    pair = pl.program_id(1)
    qi = qi_ref[step]
    ki = ki_ref[step]
    reps = tk // LANES
    low = lax.broadcasted_iota(jnp.int32, (1, LANES), 1) < FOX_HEAD
    nt = (((1,), (1,)), ((), ()))

    @pl.when(ki == 0)
    def _():
        q = q_ref[...]
        zero = jnp.zeros_like(q)
        qm_ref[0] = jnp.where(low, q, zero)
        qm_ref[1] = jnp.where(low, zero, q)
        fc = fc_ref[...]
        hl = lax.broadcasted_iota(jnp.int32, fc.shape, 1)
        for e in range(2):
            col = jnp.sum(jnp.where(hl == 2 * pair + e, fc, 0.0), axis=-1, keepdims=True)
            fq_ref[e] = jnp.broadcast_to(col, (tq, LANES))
        m_ref[...] = jnp.full_like(m_ref, -jnp.inf)
        l_ref[...] = jnp.zeros_like(l_ref)
        acc_ref[...] = jnp.zeros_like(acc_ref)

    def tile(masked):
        k = k_ref[...]
        v = v_ref[...]
        alphas, pvs = [], []
        for e in range(2):
            s = lax.dot_general(qm_ref[e], k, nt, preferred_element_type=F32)
            s = s + jnp.concatenate([fq_ref[e]] * reps, axis=1) - fr_ref[0, e:e + 1, :]
            if masked:
                rpos = qi * tq + lax.broadcasted_iota(jnp.int32, (tq, tk), 0)
                cpos = ki * tk + lax.broadcasted_iota(jnp.int32, (tq, tk), 1)
                s = jnp.where(cpos <= rpos, s, -jnp.inf)
            m_prev = m_ref[e]
            m_new = jnp.maximum(m_prev, jnp.max(s, axis=-1, keepdims=True))
            p = jnp.exp(s - jnp.concatenate([m_new] * reps, axis=1))
            alpha = jnp.exp(m_prev - m_new)
            l_ref[e] = alpha * l_ref[e] + jnp.sum(p, axis=-1, keepdims=True)
            m_ref[e] = m_new
            alphas.append(alpha)
            pvs.append(jnp.dot(p.astype(BF16), v, preferred_element_type=F32))
        acc_ref[...] = acc_ref[...] * jnp.where(low, alphas[0], alphas[1]) + jnp.where(low, pvs[0], pvs[1])

    diag = (ki + 1) * tk > qi * tq

    @pl.when(diag)
    def _():
        tile(True)

    @pl.when(jnp.logical_not(diag))
    def _():
        tile(False)

    @pl.when((ki + 1) * tk == (qi + 1) * tq)
    def _():
        inv = 1.0 / jnp.where(low, l_ref[0], l_ref[1])
        o_ref[...] = (acc_ref[...] * inv * g_ref[...].astype(F32)).astype(BF16)


def _fox_attn_call(q, k, v, g, f_col, f_row, *, batch, seq, tq, tk):
    t, d = q.shape
    pairs = d // LANES
    nq, nk = seq // tq, seq // tk
    qi_tab, ki_tab = [], []
    for a in range(nq):
        for b_ in range(((a + 1) * tq) // tk):
            qi_tab.append(a)
            ki_tab.append(b_)
    n_steps = len(qi_tab)
    qi_tab = jnp.asarray(qi_tab, jnp.int32)
    ki_tab = jnp.asarray(ki_tab, jnp.int32)

    q_spec = pl.BlockSpec((tq, LANES), lambda b, p, s, qt, kt: (b * nq + qt[s], p))
    k_spec = pl.BlockSpec((tk, LANES), lambda b, p, s, qt, kt: (b * nk + kt[s], p))
    grid_spec = pltpu.PrefetchScalarGridSpec(
        num_scalar_prefetch=2,
        grid=(batch, pairs, n_steps),
        in_specs=[
            q_spec, k_spec, k_spec, q_spec,
            pl.BlockSpec((tq, f_col.shape[1]), lambda b, p, s, qt, kt: (b * nq + qt[s], 0)),
            pl.BlockSpec((1, 2, tk), lambda b, p, s, qt, kt: (b * pairs + p, 0, kt[s])),
        ],
        out_specs=q_spec,
        scratch_shapes=[
            pltpu.VMEM((2, tq, LANES), BF16),
            pltpu.VMEM((2, tq, LANES), F32),
            pltpu.VMEM((2, tq, LANES), F32),
            pltpu.VMEM((2, tq, LANES), F32),
            pltpu.VMEM((tq, LANES), F32),
        ],
    )
    return pl.pallas_call(
        functools.partial(_fox_attn_kernel, tq=tq, tk=tk),
        grid_spec=grid_spec,
        out_shape=jax.ShapeDtypeStruct((t, d), BF16),
        compiler_params=_cparams(("arbitrary", "arbitrary", "arbitrary")),
        name="fox_attn",
    )(qi_tab, ki_tab, q, k, v, g, f_col, f_row)


def _post_kernel(x_ref, y_ref, mod_ref, nw_ref, fw_ref, wo_ref, w1_ref, w2_ref, o_ref, *, d, ff_blk, final):
    mod = mod_ref[0]
    g1 = mod[:, 2 * d:3 * d]
    sh2, sc2, g2 = mod[:, 3 * d:4 * d], mod[:, 4 * d:5 * d], mod[:, 5 * d:6 * d]
    x1 = x_ref[...] + g1 * jnp.dot(y_ref[...], wo_ref[...], preferred_element_type=F32)
    h = _modulate(x1, nw_ref[...], sh2, sc2).astype(BF16)
    acc = jnp.zeros_like(x1)
    for j in range(w1_ref.shape[1] // ff_blk):
        a = jnp.maximum(jnp.dot(h, w1_ref[:, j * ff_blk:(j + 1) * ff_blk], preferred_element_type=F32), 0.0)
        acc = acc + jnp.dot((a * a).astype(BF16), w2_ref[j * ff_blk:(j + 1) * ff_blk, :],
                            preferred_element_type=F32)
    x2 = x1 + g2 * acc
    if final:
        x2 = x2 * lax.rsqrt(jnp.mean(x2 * x2, axis=-1, keepdims=True) + EPS) * fw_ref[...]
    o_ref[...] = x2


def _post_call(x2, y, mod3, nw, fw, w_out, w1, w2, *, layer, batch, tm, final):
    t, d = x2.shape
    tiles_per_batch = (t // batch) // tm
    row = pl.BlockSpec((tm, d), lambda i: (i, 0))
    return pl.pallas_call(
        functools.partial(_post_kernel, d=d, ff_blk=1024, final=final),
        grid=(t // tm,),
        in_specs=[
            row, row,
            pl.BlockSpec((1, 1, N_MOD * d), lambda i: (layer * batch + i // tiles_per_batch, 0, 0)),
            _resident((1, d)), _resident((1, d)),
            _resident(w_out.shape), _resident(w1.shape), _resident(w2.shape),
        ],
        out_specs=row,
        out_shape=jax.ShapeDtypeStruct((t, d), F32),
        compiler_params=_cparams(("arbitrary",)),
        name="post",
    )(x2, y, mod3, nw, fw, w_out, w1, w2)


def kernel(x, c, w_mod, b_mod, norm1_w, norm2_w, hg_w_in, hg_w_out, hg_lb, hg_gn_w, fox_w_in, fox_b_f,
           fox_qn_w, fox_kn_w, fox_w_out, mlp_w1, mlp_w2, final_w):
    batch, seq, d = x.shape
    depth = w_mod.shape[0]
    t = batch * seq
    fox_heads = d // FOX_HEAD

    mod3 = _mod_call(c, w_mod, b_mod).reshape(depth * batch, 1, N_MOD * d)
    xs = x.reshape(t, d)
    fw = final_w.reshape(1, d)

    for i in range(depth):
        j = i // 2
        n1 = norm1_w[i].reshape(1, d)
        if i % 2 == 0:
            q, k, lf, v, g = _hg_proj_call(xs, mod3, n1, hg_lb, hg_w_in[j].astype(BF16),
                                           layer=i, batch=batch, tm=512)
            y = _hg_rec_call(q, k, lf, v, g, hg_gn_w[j].reshape(1, HG_HEAD), batch=batch, tc=1024)
            w_out = hg_w_out[j]
        else:
            w_in = fox_w_in[j].astype(BF16)
            q, k, v, g, lf = _fox_proj_call(
                xs, mod3, n1,
                jnp.tile(fox_qn_w[j], fox_heads).reshape(1, d),
                jnp.tile(fox_kn_w[j], fox_heads).reshape(1, d),
                fox_b_f[j].reshape(1, fox_heads),
                w_in[:, :4 * d], w_in[:, 4 * d:],
                layer=i, batch=batch, tm=512)
            f_row = _cumsum_call(lf.reshape(batch, seq, fox_heads).transpose(0, 2, 1).reshape(batch * fox_heads, seq))
            f_col = f_row.reshape(batch, fox_heads, seq).transpose(0, 2, 1).reshape(t, fox_heads)
            y = _fox_attn_call(q, k, v, g, f_col, f_row.reshape(batch * fox_heads // 2, 2, seq),
                               batch=batch, seq=seq, tq=512, tk=512)
            w_out = fox_w_out[j]
        xs = _post_call(xs, y, mod3, norm2_w[i].reshape(1, d), fw, w_out.astype(BF16),
                        mlp_w1[i].astype(BF16), mlp_w2[i].astype(BF16),
                        layer=i, batch=batch, tm=512, final=(i == depth - 1))
    return xs.reshape(batch, seq, d)
```

```python
import functools

import numpy as np
import jax
import jax.numpy as jnp
from jax import lax
from jax.experimental import pallas as pl
from jax.experimental.pallas import tpu as pltpu

F32 = jnp.float32
BF16 = jnp.bfloat16
EPS = 1e-6
N_MOD = 6
HG_HEAD = 128
HG_CHUNK = 64
HG_SUB = 16
FOX_HEAD = 64
LANES = 128
LOG2E = float(np.log2(np.e))
FOX_STAB_MAX = 30.0
VMEM_LIMIT = 56 * 1024 * 1024

_HI = lax.Precision.HIGHEST


def _cparams(sem):
    return pltpu.CompilerParams(dimension_semantics=sem, vmem_limit_bytes=VMEM_LIMIT)


def _resident(shape):
    nd = len(shape)
    return pl.BlockSpec(shape, lambda *_: (0,) * nd, pipeline_mode=pl.Buffered(1))


def _sigmoid(x):
    return 1.0 / (1.0 + jnp.exp(-x))


def _modulate(x, nw, shift, scale):
    ms = jnp.mean(x * x, axis=-1, keepdims=True)
    y = x * lax.rsqrt(ms + EPS)
    return (y * nw) * (1.0 + scale) + shift


def _mod_kernel(c_ref, w_ref, b_ref, o_ref):
    c = c_ref[...]
    ca = c * _sigmoid(c)
    o_ref[0] = jnp.dot(ca, w_ref[0], precision=_HI, preferred_element_type=F32) + b_ref[0]


def _mod_call(c, w_mod, b_mod):
    depth, d, n = w_mod.shape
    b = c.shape[0]
    tn = 1024
    return pl.pallas_call(
        _mod_kernel,
        grid=(depth, n // tn),
        in_specs=[
            pl.BlockSpec((b, d), lambda l, j: (0, 0)),
            pl.BlockSpec((1, d, tn), lambda l, j: (l, 0, j)),
            pl.BlockSpec((1, 1, tn), lambda l, j: (l, 0, j)),
        ],
        out_specs=pl.BlockSpec((1, b, tn), lambda l, j: (l, 0, j)),
        out_shape=jax.ShapeDtypeStruct((depth, b, n), F32),
        compiler_params=_cparams(("arbitrary", "arbitrary")),
        name="mod",
    )(c, w_mod, b_mod.reshape(depth, 1, n))


def _hg_proj_kernel(x_ref, mod_ref, nw_ref, lb_ref, w_ref, q_ref, k_ref, lf_ref, v_ref, g_ref, *, layer, d):
    mod = mod_ref[0]
    h = _modulate(x_ref[...], nw_ref[...], mod[:, 0:d], mod[:, d:2 * d]).astype(BF16)

    pq = jnp.dot(h, w_ref[:, 0:d], preferred_element_type=F32)
    q_ref[...] = (pq * _sigmoid(pq)).astype(BF16)

    lbp = lb_ref[...]
    e = jnp.exp(lbp - jnp.max(lbp, axis=0, keepdims=True))
    lb = jnp.sum(e[0:layer + 1], axis=0, keepdims=True) / jnp.sum(e, axis=0, keepdims=True)

    z = jnp.dot(h, w_ref[:, d:2 * d], preferred_element_type=F32)
    kk = (1.0 - lb) * _sigmoid(-z)
    f = lb + (1.0 - lb) * _sigmoid(z)
    lf_ref[...] = jnp.where(kk < 0.5, jnp.log1p(-kk), jnp.log(f))
    k_ref[...] = kk.astype(BF16)

    v_ref[...] = jnp.dot(h, w_ref[:, 2 * d:3 * d], preferred_element_type=F32).astype(BF16)
    pg = jnp.dot(h, w_ref[:, 3 * d:4 * d], preferred_element_type=F32)
    g_ref[...] = (pg * _sigmoid(pg)).astype(BF16)


def _hg_proj_call(x2, mod3, nw, hg_lb, w_in, *, layer, batch, tm):
    t, d = x2.shape
    tiles_per_batch = (t // batch) // tm
    row = pl.BlockSpec((tm, d), lambda i: (i, 0))
    out_bf = jax.ShapeDtypeStruct((t, d), BF16)
    return pl.pallas_call(
        functools.partial(_hg_proj_kernel, layer=layer, d=d),
        grid=(t // tm,),
        in_specs=[
            row,
            pl.BlockSpec((1, 1, N_MOD * d), lambda i: (layer * batch + i // tiles_per_batch, 0, 0)),
            _resident((1, d)),
            _resident(hg_lb.shape),
            _resident(w_in.shape),
        ],
        out_specs=[row, row, row, row, row],
        out_shape=[out_bf, out_bf, jax.ShapeDtypeStruct((t, d), F32), out_bf, out_bf],
        compiler_params=_cparams(("arbitrary",)),
        name="hg_proj",
    )(x2, mod3, nw, hg_lb, w_in)


def _hg_rec_kernel(q_ref, k_ref, lf_ref, v_ref, g_ref, gnw_ref, o_ref, st_ref, gs_ref, ks_ref, *, n_chunks):
    c, sub = HG_CHUNK, HG_SUB
    n_sub = c // sub

    @pl.when(pl.program_id(2) == 0)
    def _():
        st_ref[...] = jnp.zeros_like(st_ref)

    r_i = lax.broadcasted_iota(jnp.int32, (c, c), 0)
    c_i = lax.broadcasted_iota(jnp.int32, (c, c), 1)
    tri = (c_i <= r_i).astype(F32)
    ones_w = jnp.ones((HG_HEAD, c), BF16)
    sub_row = lax.broadcasted_iota(jnp.int32, (sub, HG_HEAD), 0)
    a_lane = lax.broadcasted_iota(jnp.int32, (sub, c), 1)
    gnw = gnw_ref[...]
    nt = (((1,), (1,)), ((), ()))
    tn = (((0,), (0,)), ((), ()))

    def chunk(i, carry):
        c0 = pl.multiple_of(i * c, c)
        rows = pl.ds(c0, c)
        lf = lf_ref[rows, :]
        q = q_ref[rows, :].astype(F32)
        k = k_ref[rows, :].astype(F32)
        v = v_ref[rows, :]
        gcum = jnp.dot(tri, lf, precision=_HI, preferred_element_type=F32)
        gs_ref[...] = gcum
        ks_ref[...] = k
        g_last = gcum[c - 1:c, :]

        st = st_ref[...]
        qg = (q * jnp.exp(gcum)).astype(BF16)
        o = lax.dot_general(qg, st.astype(BF16), nt, preferred_element_type=F32)
        kd = (k * jnp.exp(g_last - gcum)).astype(BF16)
        st_ref[...] = st * jnp.exp(g_last) + lax.dot_general(v, kd, tn, preferred_element_type=F32)

        a_rows = []
        for j in range(n_sub):
            lo = j * sub
            gt = gcum[lo:lo + sub, :]
            qt = q[lo:lo + sub, :]
            ps = []
            for s in range(sub):
                g_s = gs_ref[pl.ds(lo + s, 1), :]
                k_s = ks_ref[pl.ds(lo + s, 1), :]
                dlt = jnp.where(sub_row >= s, gt - g_s, -jnp.inf)
                ps.append(((qt * k_s) * jnp.exp(dlt)).astype(BF16))
            r = jnp.dot(jnp.concatenate(ps, axis=0), ones_w, preferred_element_type=F32)
            a_j = jnp.zeros((sub, c), F32)
            for s in range(sub):
                a_j = jnp.where(a_lane == lo + s, r[s * sub:(s + 1) * sub, :], a_j)
            if j > 0:
                g_b = gcum[lo - 1:lo, :]
                qh = (qt * jnp.exp(gt - g_b)).astype(BF16)
                kh = (k[0:lo, :] * jnp.exp(g_b - gcum[0:lo, :])).astype(BF16)
                kh = jnp.concatenate([kh, jnp.zeros((c - lo, HG_HEAD), BF16)], axis=0)
                a_j = a_j + lax.dot_general(qh, kh, nt, preferred_element_type=F32)
            a_rows.append(a_j)
        a = jnp.concatenate(a_rows, axis=0).astype(BF16)
        o = o + jnp.dot(a, v, preferred_element_type=F32)

        on = o * lax.rsqrt(jnp.mean(o * o, axis=-1, keepdims=True) + EPS) * gnw
        o_ref[rows, :] = (on * g_ref[rows, :].astype(F32)).astype(BF16)
        return carry

    lax.fori_loop(0, n_chunks, chunk, 0)


def _hg_rec_call(q, k, lf, v, g, gnw, *, batch, tc):
    t, d = q.shape
    heads = d // HG_HEAD
    steps = (t // batch) // tc
    blk = pl.BlockSpec((tc, HG_HEAD), lambda b, h, s: (b * steps + s, h))
    return pl.pallas_call(
        functools.partial(_hg_rec_kernel, n_chunks=tc // HG_CHUNK),
        grid=(batch, heads, steps),
        in_specs=[blk, blk, blk, blk, blk, pl.BlockSpec((1, HG_HEAD), lambda b, h, s: (0, 0))],
        out_specs=blk,
        out_shape=jax.ShapeDtypeStruct((t, d), BF16),
        scratch_shapes=[
            pltpu.VMEM((HG_HEAD, HG_HEAD), F32),
            pltpu.VMEM((HG_CHUNK, HG_HEAD), F32),
            pltpu.VMEM((HG_CHUNK, HG_HEAD), F32),
        ],
        compiler_params=_cparams(("arbitrary", "arbitrary", "arbitrary")),
        name="hg_rec",
    )(q, k, lf, v, g, gnw)


def _head_rmsnorm(p, w_row):
    low = lax.broadcasted_iota(jnp.int32, (1, LANES), 1) < FOX_HEAD
    outs = []
    for j in range(p.shape[1] // LANES):
        pj = p[:, j * LANES:(j + 1) * LANES]
        ss = pj * pj
        s_lo = jnp.sum(jnp.where(low, ss, 0.0), axis=-1, keepdims=True)
        s_hi = jnp.sum(jnp.where(low, 0.0, ss), axis=-1, keepdims=True)
        inv = lax.rsqrt(jnp.where(low, s_lo, s_hi) * (1.0 / FOX_HEAD) + EPS)
        outs.append(pj * inv)
    return jnp.concatenate(outs, axis=1) * w_row


def _fox_proj_kernel(x_ref, mod_ref, nw_ref, qw_ref, kw_ref, bf_ref, w_ref, wf_ref,
                     q_ref, k_ref, v_ref, g_ref, lf_ref, *, d):
    mod = mod_ref[0]
    h = _modulate(x_ref[...], nw_ref[...], mod[:, 0:d], mod[:, d:2 * d]).astype(BF16)
    scale = LOG2E / np.sqrt(FOX_HEAD)
    pq = jnp.dot(h, w_ref[:, 0:d], preferred_element_type=F32)
    q_ref[...] = (_head_rmsnorm(pq, qw_ref[...]) * scale).astype(BF16)
    pk = jnp.dot(h, w_ref[:, d:2 * d], preferred_element_type=F32)
    k_ref[...] = _head_rmsnorm(pk, kw_ref[...]).astype(BF16)
    v_ref[...] = jnp.dot(h, w_ref[:, 2 * d:3 * d], preferred_element_type=F32).astype(BF16)
    pg = jnp.dot(h, w_ref[:, 3 * d:4 * d], preferred_element_type=F32)
    g_ref[...] = _sigmoid(pg).astype(BF16)
    u = jnp.dot(h, wf_ref[...], preferred_element_type=F32) + bf_ref[...]
    lf_ref[...] = jnp.minimum(u, 0.0) - jnp.log1p(jnp.exp(-jnp.abs(u)))


def _fox_proj_call(x2, mod3, nw, qw, kw, bf, w_main, w_f, *, layer, batch, tm):
    t, d = x2.shape
    nh = w_f.shape[1]
    tiles_per_batch = (t // batch) // tm
    row = pl.BlockSpec((tm, d), lambda i: (i, 0))
    out_bf = jax.ShapeDtypeStruct((t, d), BF16)
    return pl.pallas_call(
        functools.partial(_fox_proj_kernel, d=d),
        grid=(t // tm,),
        in_specs=[
            row,
            pl.BlockSpec((1, 1, N_MOD * d), lambda i: (layer * batch + i // tiles_per_batch, 0, 0)),
            _resident((1, d)), _resident((1, d)), _resident((1, d)), _resident((1, nh)),
            _resident(w_main.shape), _resident(w_f.shape),
        ],
        out_specs=[row, row, row, row, pl.BlockSpec((tm, nh), lambda i: (i, 0))],
        out_shape=[out_bf, out_bf, out_bf, out_bf, jax.ShapeDtypeStruct((t, nh), F32)],
        compiler_params=_cparams(("arbitrary",)),
        name="fox_proj",
    )(x2, mod3, nw, qw, kw, bf, w_main, w_f)


FOX_AUG = 6


def _split3(x):
    hi = x.astype(BF16)
    r1 = x - hi.astype(F32)
    mid = r1.astype(BF16)
    lo = (r1 - mid.astype(F32)).astype(BF16)
    return jnp.concatenate([hi, mid, lo], axis=1)


def _fox_bias_kernel(lf_ref, qw_ref, kw_ref, pq_ref, pk_ref, oq_ref, ok_ref, aq_ref, ak_ref, flag_ref, carry_ref):
    tp = lf_ref.shape[0]

    @pl.when(pl.program_id(1) == 0)
    def _():
        carry_ref[...] = jnp.zeros_like(carry_ref)

    bound = 1.01 * LOG2E * np.sqrt(FOX_HEAD) * jnp.max(jnp.abs(qw_ref[...] * kw_ref[...]), axis=-1, keepdims=True)
    fast = bound <= FOX_STAB_MAX
    flag_ref[...] = jnp.broadcast_to(fast.astype(jnp.int32), flag_ref.shape)
    stab = jnp.where(fast, bound, 0.0)

    r_i = lax.broadcasted_iota(jnp.int32, (tp, tp), 0)
    c_i = lax.broadcasted_iota(jnp.int32, (tp, tp), 1)
    tri = (c_i <= r_i).astype(F32)
    f = jnp.dot(tri, lf_ref[...], precision=_HI, preferred_element_type=F32) + carry_ref[...]
    carry_ref[...] = f[tp - 1:tp, :]
    f2 = f * LOG2E
    aq_ref[...] = (jnp.dot(_split3(f2 - stab), pq_ref[...], preferred_element_type=F32) + oq_ref[...]).astype(BF16)
    ak_ref[...] = (jnp.dot(_split3(f2), pk_ref[...], preferred_element_type=F32) + ok_ref[...]).astype(BF16)


def _fox_bias_call(lf, qw, kw, *, batch, d, tp):
    t = lf.shape[0]
    heads = d // FOX_HEAD
    steps = (t // batch) // tp
    pq = np.zeros((3 * LANES, d), np.float32)
    pk = np.zeros((3 * LANES, d), np.float32)
    oq = np.zeros((1, d), np.float32)
    ok = np.zeros((1, d), np.float32)
    for h in range(heads):
        base = LANES * (h // 2) + FOX_AUG * (h % 2)
        for i in range(3):
            pq[i * LANES + h, base + i] = 1.0
            ok[0, base + i] = 1.0
            oq[0, base + 3 + i] = 1.0
            pk[i * LANES + h, base + 3 + i] = -1.0
    row = pl.BlockSpec((tp, d), lambda b, s: (b * steps + s, 0))
    return pl.pallas_call(
        _fox_bias_kernel,
        grid=(batch, steps),
        in_specs=[
            pl.BlockSpec((tp, LANES), lambda b, s: (b * steps + s, 0)),
            _resident((1, FOX_HEAD)), _resident((1, FOX_HEAD)),
            _resident(pq.shape), _resident(pk.shape), _resident(oq.shape), _resident(ok.shape),
        ],
        out_specs=[row, row, pl.BlockSpec((8, LANES), lambda b, s: (0, 0))],
        out_shape=[jax.ShapeDtypeStruct((t, d), BF16), jax.ShapeDtypeStruct((t, d), BF16),
                   jax.ShapeDtypeStruct((8, LANES), jnp.int32)],
        scratch_shapes=[pltpu.VMEM((1, LANES), F32)],
        compiler_params=_cparams(("arbitrary", "arbitrary")),
        name="fox_bias",
    )(lf, qw, kw, jnp.asarray(pq, BF16), jnp.asarray(pk, BF16), jnp.asarray(oq), jnp.asarray(ok))


def _fox_attn_kernel(flag_ref, q_ref, aq_ref, k_ref, ak_ref, v_ref, g_ref, o_ref,
                     v2_ref, qm_ref, acc_ref, m_ref, *, tq, tk):
    qi = pl.program_id(2)
    sub = tq // tk
    lane2 = lax.broadcasted_iota(jnp.int32, (1, 2 * LANES), 1)
    head0 = (lane2 < FOX_HEAD) | ((lane2 >= LANES) & (lane2 < LANES + FOX_AUG))
    head1 = ((lane2 >= FOX_HEAD) & (lane2 < LANES)) | ((lane2 >= LANES + FOX_AUG) & (lane2 < LANES + 2 * FOX_AUG))
    low = lax.broadcasted_iota(jnp.int32, (1, LANES), 1) < FOX_HEAD
    nt = (((1,), (1,)), ((), ()))

    @pl.when(qi == 0)
    def _():
        one_lo = jnp.broadcast_to(jnp.where(low, 1.0, 0.0).astype(BF16), (tk, LANES))
        one_hi = jnp.broadcast_to(jnp.where(low, 0.0, 1.0).astype(BF16), (tk, LANES))

        def build(j, carry):
            v = v_ref[pl.ds(pl.multiple_of(j * tk, tk), tk), :]
            zero = jnp.zeros_like(v)
            r0 = pl.multiple_of(2 * j * tk, tk)
            v2_ref[pl.ds(r0, tk), :] = jnp.concatenate([jnp.where(low, v, zero), one_lo], axis=1)
            v2_ref[pl.ds(r0 + tk, tk), :] = jnp.concatenate([jnp.where(low, zero, v), one_hi], axis=1)
            return carry

        lax.fori_loop(0, v_ref.shape[0] // tk, build, 0)

    def tile(j, masked, online):
        rows = pl.ds(pl.multiple_of(j * tk, tk), tk)
        kf = jnp.concatenate([k_ref[rows, :], ak_ref[rows, :]], axis=1)
        ps, alphas = [], []
        for e in range(2):
            s = lax.dot_general(qm_ref[e], kf, nt, preferred_element_type=F32)
            if masked:
                rpos = qi * tq + lax.broadcasted_iota(jnp.int32, (tq, tk), 0)
                cpos = j * tk + lax.broadcasted_iota(jnp.int32, (tq, tk), 1)
                s = jnp.where(cpos <= rpos, s, -jnp.inf)
            if online:
                m_prev = m_ref[e]
                m_new = jnp.maximum(m_prev, jnp.max(s, axis=-1, keepdims=True))
                alphas.append(jnp.exp2(m_prev - m_new))
                m_ref[e] = m_new
                s = s - jnp.concatenate([m_new] * (tk // LANES), axis=1)
            ps.append(jnp.exp2(s).astype(BF16))
        pv = jnp.dot(jnp.concatenate(ps, axis=1), v2_ref[pl.ds(pl.multiple_of(2 * j * tk, tk), 2 * tk), :],
                     preferred_element_type=F32)
        if online:
            a = jnp.where(low, alphas[0], alphas[1])
            acc_ref[...] = acc_ref[...] * jnp.concatenate([a, a], axis=1) + pv
        else:
            acc_ref[...] += pv

    def run(online):
        qf = jnp.concatenate([q_ref[...], aq_ref[...]], axis=1)
        zero = jnp.zeros_like(qf)
        qm_ref[0] = jnp.where(head0, qf, zero)
        qm_ref[1] = jnp.where(head1, qf, zero)
        acc_ref[...] = jnp.zeros_like(acc_ref)
        if online:
            m_ref[...] = jnp.full_like(m_ref, -jnp.inf)

        def body(j, carry):
            tile(j, False, online)
            return carry

        lax.fori_loop(0, qi * sub, body, 0)
        for jj in range(sub):
            tile(qi * sub + jj, True, online)
        acc = acc_ref[...]
        o_ref[...] = (acc[:, :LANES] / acc[:, LANES:] * g_ref[...].astype(F32)).astype(BF16)

    @pl.when(flag_ref[0] == 1)
    def _():
        run(False)

    @pl.when(flag_ref[0] != 1)
    def _():
        run(True)


def _fox_attn_call(flag, q, aq, k, ak, v, g, *, batch, seq, tq, tk):
    t, d = q.shape
    pairs = d // LANES
    nq = seq // tq
    q_spec = pl.BlockSpec((tq, LANES), lambda b, p, i, fl: (b * nq + i, p))
    kv_spec = pl.BlockSpec((seq, LANES), lambda b, p, i, fl: (b, p))
    grid_spec = pltpu.PrefetchScalarGridSpec(
        num_scalar_prefetch=1,
        grid=(batch, pairs, nq),
        in_specs=[q_spec, q_spec, kv_spec, kv_spec, kv_spec, q_spec],
        out_specs=q_spec,
        scratch_shapes=[
            pltpu.VMEM((2 * seq, 2 * LANES), BF16),
            pltpu.VMEM((2, tq, 2 * LANES), BF16),
            pltpu.VMEM((tq, 2 * LANES), F32),
            pltpu.VMEM((2, tq, LANES), F32),
        ],
    )
    return pl.pallas_call(
        functools.partial(_fox_attn_kernel, tq=tq, tk=tk),
        grid_spec=grid_spec,
        out_shape=jax.ShapeDtypeStruct((t, d), BF16),
        compiler_params=_cparams(("arbitrary", "arbitrary", "arbitrary")),
        name="fox_attn",
    )(flag, q, aq, k, ak, v, g)


def _post_kernel(x_ref, y_ref, mod_ref, nw_ref, fw_ref, wo_ref, w1_ref, w2_ref, o_ref, *, d, ff_blk, final):
    mod = mod_ref[0]
    g1 = mod[:, 2 * d:3 * d]
    sh2, sc2, g2 = mod[:, 3 * d:4 * d], mod[:, 4 * d:5 * d], mod[:, 5 * d:6 * d]
    x1 = x_ref[...] + g1 * jnp.dot(y_ref[...], wo_ref[...], preferred_element_type=F32)
    h = _modulate(x1, nw_ref[...], sh2, sc2).astype(BF16)
    acc = jnp.zeros_like(x1)
    for j in range(w1_ref.shape[1] // ff_blk):
        a = jnp.maximum(jnp.dot(h, w1_ref[:, j * ff_blk:(j + 1) * ff_blk], preferred_element_type=F32), 0.0)
        acc = acc + jnp.dot((a * a).astype(BF16), w2_ref[j * ff_blk:(j + 1) * ff_blk, :],
                            preferred_element_type=F32)
    x2 = x1 + g2 * acc
    if final:
        x2 = x2 * lax.rsqrt(jnp.mean(x2 * x2, axis=-1, keepdims=True) + EPS) * fw_ref[...]
    o_ref[...] = x2


def _post_call(x2, y, mod3, nw, fw, w_out, w1, w2, *, layer, batch, tm, final):
    t, d = x2.shape
    tiles_per_batch = (t // batch) // tm
    row = pl.BlockSpec((tm, d), lambda i: (i, 0))
    return pl.pallas_call(
        functools.partial(_post_kernel, d=d, ff_blk=1024, final=final),
        grid=(t // tm,),
        in_specs=[
            row, row,
            pl.BlockSpec((1, 1, N_MOD * d), lambda i: (layer * batch + i // tiles_per_batch, 0, 0)),
            _resident((1, d)), _resident((1, d)),
            _resident(w_out.shape), _resident(w1.shape), _resident(w2.shape),
        ],
        out_specs=row,
        out_shape=jax.ShapeDtypeStruct((t, d), F32),
        compiler_params=_cparams(("arbitrary",)),
        name="post",
    )(x2, y, mod3, nw, fw, w_out, w1, w2)


def kernel(x, c, w_mod, b_mod, norm1_w, norm2_w, hg_w_in, hg_w_out, hg_lb, hg_gn_w, fox_w_in, fox_b_f,
           fox_qn_w, fox_kn_w, fox_w_out, mlp_w1, mlp_w2, final_w):
    batch, seq, d = x.shape
    depth = w_mod.shape[0]
    t = batch * seq
    fox_heads = d // FOX_HEAD

    mod3 = _mod_call(c, w_mod, b_mod).reshape(depth * batch, 1, N_MOD * d)
    xs = x.reshape(t, d)
    fw = final_w.reshape(1, d)

    for i in range(depth):
        j = i // 2
        n1 = norm1_w[i].reshape(1, d)
        if i % 2 == 0:
            q, k, lf, v, g = _hg_proj_call(xs, mod3, n1, hg_lb, hg_w_in[j].astype(BF16),
                                           layer=i, batch=batch, tm=512)
            y = _hg_rec_call(q, k, lf, v, g, hg_gn_w[j].reshape(1, HG_HEAD), batch=batch, tc=1024)
            w_out = hg_w_out[j]
        else:
            w_in = fox_w_in[j].astype(BF16)
            pad = ((0, 0), (0, LANES - fox_heads))
            q, k, v, g, lf = _fox_proj_call(
                xs, mod3, n1,
                jnp.tile(fox_qn_w[j], fox_heads).reshape(1, d),
                jnp.tile(fox_kn_w[j], fox_heads).reshape(1, d),
                jnp.pad(fox_b_f[j].reshape(1, fox_heads), pad),
                w_in[:, :4 * d], jnp.pad(w_in[:, 4 * d:], pad),
                layer=i, batch=batch, tm=512)
            aq, ak, flag = _fox_bias_call(lf, fox_qn_w[j].reshape(1, FOX_HEAD), fox_kn_w[j].reshape(1, FOX_HEAD),
                                          batch=batch, d=d, tp=512)
            y = _fox_attn_call(flag[0, :1], q, aq, k, ak, v, g, batch=batch, seq=seq, tq=512, tk=512)
            w_out = fox_w_out[j]
        xs = _post_call(xs, y, mod3, norm2_w[i].reshape(1, d), fw, w_out.astype(BF16),
                        mlp_w1[i].astype(BF16), mlp_w2[i].astype(BF16),
                        layer=i, batch=batch, tm=512, final=(i == depth - 1))
    return xs.reshape(batch, seq, d)
```

```python
import functools

import numpy as np
import jax
import jax.numpy as jnp
from jax import lax
from jax.experimental import pallas as pl
from jax.experimental.pallas import tpu as pltpu

F32 = jnp.float32
BF16 = jnp.bfloat16
EPS = 1e-6
N_MOD = 6
HG_HEAD = 128
HG_CHUNK = 64
HG_SUB = 16
HG_SAFE_DECAY = 56.0
FOX_HEAD = 64
LANES = 128
LOG2E = float(np.log2(np.e))
FOX_STAB_MAX = 30.0
VMEM_LIMIT = 56 * 1024 * 1024

_HI = lax.Precision.HIGHEST


def _cparams(sem):
    return pltpu.CompilerParams(dimension_semantics=sem, vmem_limit_bytes=VMEM_LIMIT)


def _resident(shape):
    nd = len(shape)
    return pl.BlockSpec(shape, lambda *_: (0,) * nd, pipeline_mode=pl.Buffered(1))


def _sigmoid(x):
    return 1.0 / (1.0 + jnp.exp(-x))


def _modulate(x, nw, shift, scale):
    ms = jnp.mean(x * x, axis=-1, keepdims=True)
    y = x * lax.rsqrt(ms + EPS)
    return (y * nw) * (1.0 + scale) + shift


def _mod_kernel(c_ref, w_ref, b_ref, o_ref):
    c = c_ref[...]
    ca = c * _sigmoid(c)
    o_ref[0] = jnp.dot(ca, w_ref[0], precision=_HI, preferred_element_type=F32) + b_ref[0]


def _mod_call(c, w_mod, b_mod):
    depth, d, n = w_mod.shape
    b = c.shape[0]
    tn = 1024
    return pl.pallas_call(
        _mod_kernel,
        grid=(depth, n // tn),
        in_specs=[
            pl.BlockSpec((b, d), lambda l, j: (0, 0)),
            pl.BlockSpec((1, d, tn), lambda l, j: (l, 0, j)),
            pl.BlockSpec((1, 1, tn), lambda l, j: (l, 0, j)),
        ],
        out_specs=pl.BlockSpec((1, b, tn), lambda l, j: (l, 0, j)),
        out_shape=jax.ShapeDtypeStruct((depth, b, n), F32),
        compiler_params=_cparams(("arbitrary", "arbitrary")),
        name="mod",
    )(c, w_mod, b_mod.reshape(depth, 1, n))


def _hg_proj_kernel(x_ref, mod_ref, nw_ref, lb_ref, w_ref, q_ref, k_ref, lf_ref, v_ref, g_ref, *, layer, d):
    mod = mod_ref[0]
    h = _modulate(x_ref[...], nw_ref[...], mod[:, 0:d], mod[:, d:2 * d]).astype(BF16)

    pq = jnp.dot(h, w_ref[:, 0:d], preferred_element_type=F32)
    q_ref[...] = (pq * _sigmoid(pq)).astype(BF16)

    lbp = lb_ref[...]
    e = jnp.exp(lbp - jnp.max(lbp, axis=0, keepdims=True))
    lb = jnp.sum(e[0:layer + 1], axis=0, keepdims=True) / jnp.sum(e, axis=0, keepdims=True)

    z = jnp.dot(h, w_ref[:, d:2 * d], preferred_element_type=F32)
    kk = (1.0 - lb) * _sigmoid(-z)
    f = lb + (1.0 - lb) * _sigmoid(z)
    lf_ref[...] = jnp.where(kk < 0.5, jnp.log1p(-kk), jnp.log(f))
    k_ref[...] = kk.astype(BF16)

    v_ref[...] = jnp.dot(h, w_ref[:, 2 * d:3 * d], preferred_element_type=F32).astype(BF16)
    pg = jnp.dot(h, w_ref[:, 3 * d:4 * d], preferred_element_type=F32)
    g_ref[...] = (pg * _sigmoid(pg)).astype(BF16)


def _hg_proj_call(x2, mod3, nw, hg_lb, w_in, *, layer, batch, tm):
    t, d = x2.shape
    tiles_per_batch = (t // batch) // tm
    row = pl.BlockSpec((tm, d), lambda i: (i, 0))
    out_bf = jax.ShapeDtypeStruct((t, d), BF16)
    return pl.pallas_call(
        functools.partial(_hg_proj_kernel, layer=layer, d=d),
        grid=(t // tm,),
        in_specs=[
            row,
            pl.BlockSpec((1, 1, N_MOD * d), lambda i: (layer * batch + i // tiles_per_batch, 0, 0)),
            _resident((1, d)),
            _resident(hg_lb.shape),
            _resident(w_in.shape),
        ],
        out_specs=[row, row, row, row, row],
        out_shape=[out_bf, out_bf, jax.ShapeDtypeStruct((t, d), F32), out_bf, out_bf],
        compiler_params=_cparams(("arbitrary",)),
        name="hg_proj",
    )(x2, mod3, nw, hg_lb, w_in)


def _hg_rec_kernel(q_ref, k_ref, lf_ref, v_ref, g_ref, gnw_ref, o_ref, st_ref, gs_ref, ks_ref, *,
                   n_chunks, unroll_bounded, unroll_exact):
    c, sub = HG_CHUNK, HG_SUB
    n_sub = c // sub
    half = sub // 2

    @pl.when(pl.program_id(2) == 0)
    def _():
        st_ref[...] = jnp.zeros_like(st_ref)

    r_i = lax.broadcasted_iota(jnp.int32, (c, c), 0)
    c_i = lax.broadcasted_iota(jnp.int32, (c, c), 1)
    tri = (c_i <= r_i).astype(BF16)
    ones_w = jnp.ones((HG_HEAD, c), BF16)
    half_row = lax.broadcasted_iota(jnp.int32, (half, HG_HEAD), 0)
    a_lane = lax.broadcasted_iota(jnp.int32, (half, c), 1)
    gnw = gnw_ref[...]
    nt = (((1,), (1,)), ((), ()))
    tn = (((0,), (0,)), ((), ()))

    def offdiag_operands(q, k, gcum, j, bounded):
        lo = j * sub
        hi = lo + sub if bounded else lo
        g_b = gcum[lo - 1:lo, :] if j > 0 else jnp.zeros((1, HG_HEAD), F32)
        qh = (q[lo:lo + sub, :] * jnp.exp2(gcum[lo:lo + sub, :] - g_b)).astype(BF16)
        kh = (k[0:hi, :] * jnp.exp2(g_b - gcum[0:hi, :])).astype(BF16)
        if hi < c:
            kh = jnp.concatenate([kh, jnp.zeros((c - hi, HG_HEAD), BF16)], axis=0)
        return qh, kh

    def diag_products(q, k, gcum, slot, j):
        lo = j * sub
        gt = (gcum[lo:lo + half, :], gcum[lo + half:lo + sub, :])
        qt = (q[lo:lo + half, :], q[lo + half:lo + sub, :])
        ps = []
        for s in range(sub):
            g_s = gs_ref[slot, pl.ds(lo + s, 1), :]
            k_s = ks_ref[slot, pl.ds(lo + s, 1), :]
            for hf in range(2):
                if s >= half and hf == 0:
                    continue
                dlt = gt[hf] - g_s
                if (s >= half) == (hf == 1):
                    dlt = jnp.where(half_row >= s % half, dlt, -jnp.inf)
                ps.append((qt[hf] * k_s) * jnp.exp2(dlt))
        return jnp.concatenate(ps, axis=0).astype(BF16)

    def diag_scatter(r, j):
        lo = j * sub
        a_top = jnp.zeros((half, c), F32)
        a_bot = jnp.zeros((half, c), F32)
        for s in range(sub):
            if s < half:
                a_top = jnp.where(a_lane == lo + s, r[2 * s * half:(2 * s + 1) * half, :], a_top)
                a_bot = jnp.where(a_lane == lo + s, r[(2 * s + 1) * half:(2 * s + 2) * half, :], a_bot)
            else:
                a_bot = jnp.where(a_lane == lo + s, r[(half + s) * half:(half + s + 1) * half, :], a_bot)
        return jnp.concatenate([a_top, a_bot], axis=0)

    def run(bounded, unroll):
        def group(i, st):
            us = range(unroll)
            rows = [pl.ds(pl.multiple_of((i * unroll + u) * c, c), c) for u in us]
            g3 = [jnp.dot(tri, _split3(lf_ref[r, :]), preferred_element_type=F32) for r in rows]
            q = [q_ref[r, :].astype(F32) for r in rows]
            k = [k_ref[r, :].astype(F32) for r in rows]
            gcum = [(g[:, 0:HG_HEAD] + g[:, HG_HEAD:2 * HG_HEAD] + g[:, 2 * HG_HEAD:]) * LOG2E for g in g3]
            g_last = [g[c - 1:c, :] for g in gcum]
            qg = [(q[u] * jnp.exp2(gcum[u])).astype(BF16) for u in us]
            kd = [(k[u] * jnp.exp2(g_last[u] - gcum[u])).astype(BF16) for u in us]
            first = 0 if bounded else 1
            off = [[offdiag_operands(q[u], k[u], gcum[u], j, bounded) for j in range(first, n_sub)] for u in us]
            if not bounded:
                for u in us:
                    gs_ref[u] = gcum[u]
                    ks_ref[u] = k[u]
                pst = [[diag_products(q[u], k[u], gcum[u], u, j) for j in range(n_sub)] for u in us]
            kv = [lax.dot_general(v_ref[rows[u], :], kd[u], tn, preferred_element_type=F32) for u in us]
            blocks = [[lax.dot_general(qh, kh, nt, preferred_element_type=F32) for qh, kh in off[u]] for u in us]
            if bounded:
                a = [jnp.where(c_i <= r_i, jnp.concatenate(blocks[u], axis=0), 0.0) for u in us]
            else:
                sums = [[jnp.dot(p, ones_w, preferred_element_type=F32) for p in pst[u]] for u in us]
                a = []
                for u in us:
                    a_rows = [diag_scatter(sums[u][j], j) for j in range(n_sub)]
                    for j in range(1, n_sub):
                        a_rows[j] = a_rows[j] + blocks[u][j - 1]
                    a.append(jnp.concatenate(a_rows, axis=0))
            o_intra = [jnp.dot(a[u].astype(BF16), v_ref[rows[u], :], preferred_element_type=F32) for u in us]
            for u in us:
                o = o_intra[u] + lax.dot_general(qg[u], st.astype(BF16), nt, preferred_element_type=F32)
                st = st * jnp.exp2(g_last[u]) + kv[u]
                on = o * lax.rsqrt(jnp.mean(o * o, axis=-1, keepdims=True) + EPS) * gnw
                o_ref[rows[u], :] = (on * g_ref[rows[u], :].astype(F32)).astype(BF16)
            return st

        st_ref[...] = lax.fori_loop(0, n_chunks // unroll, group, st_ref[...])

    w_r = lax.broadcasted_iota(jnp.int32, (n_chunks * n_sub, n_chunks * c), 0)
    w_c = lax.broadcasted_iota(jnp.int32, (n_chunks * n_sub, n_chunks * c), 1)
    win = (lax.shift_right_logical(w_c, sub.bit_length() - 1) == w_r).astype(BF16)
    wsum = jnp.dot(win, lf_ref[...].astype(BF16), preferred_element_type=F32)
    bounded = jnp.min(wsum) * LOG2E >= -HG_SAFE_DECAY

    @pl.when(bounded)
    def _():
        run(True, unroll_bounded)

    @pl.when(jnp.logical_not(bounded))
    def _():
        run(False, unroll_exact)


def _hg_rec_call(q, k, lf, v, g, gnw, *, batch, tc, unroll_bounded, unroll_exact):
    t, d = q.shape
    heads = d // HG_HEAD
    steps = (t // batch) // tc
    unroll = unroll_exact
    blk = pl.BlockSpec((tc, HG_HEAD), lambda b, h, s: (b * steps + s, h))
    return pl.pallas_call(
        functools.partial(_hg_rec_kernel, n_chunks=tc // HG_CHUNK, unroll_bounded=unroll_bounded,
                          unroll_exact=unroll_exact),
        grid=(batch, heads, steps),
        in_specs=[blk, blk, blk, blk, blk, pl.BlockSpec((1, HG_HEAD), lambda b, h, s: (0, 0))],
        out_specs=blk,
        out_shape=jax.ShapeDtypeStruct((t, d), BF16),
        scratch_shapes=[
            pltpu.VMEM((HG_HEAD, HG_HEAD), F32),
            pltpu.VMEM((unroll, HG_CHUNK, HG_HEAD), F32),
            pltpu.VMEM((unroll, HG_CHUNK, HG_HEAD), F32),
        ],
        compiler_params=_cparams(("arbitrary", "arbitrary", "arbitrary")),
        name="hg_rec",
    )(q, k, lf, v, g, gnw)


def _head_rmsnorm(p, w_row):
    low = lax.broadcasted_iota(jnp.int32, (1, LANES), 1) < FOX_HEAD
    outs = []
    for j in range(p.shape[1] // LANES):
        pj = p[:, j * LANES:(j + 1) * LANES]
        ss = pj * pj
        s_lo = jnp.sum(jnp.where(low, ss, 0.0), axis=-1, keepdims=True)
        s_hi = jnp.sum(jnp.where(low, 0.0, ss), axis=-1, keepdims=True)
        inv = lax.rsqrt(jnp.where(low, s_lo, s_hi) * (1.0 / FOX_HEAD) + EPS)
        outs.append(pj * inv)
    return jnp.concatenate(outs, axis=1) * w_row


def _fox_proj_kernel(x_ref, mod_ref, nw_ref, qw_ref, kw_ref, bf_ref, w_ref, wf_ref,
                     q_ref, k_ref, v_ref, g_ref, lf_ref, *, d):
    mod = mod_ref[0]
    h = _modulate(x_ref[...], nw_ref[...], mod[:, 0:d], mod[:, d:2 * d]).astype(BF16)
    scale = LOG2E / np.sqrt(FOX_HEAD)
    pq = jnp.dot(h, w_ref[:, 0:d], preferred_element_type=F32)
    q_ref[...] = (_head_rmsnorm(pq, qw_ref[...]) * scale).astype(BF16)
    pk = jnp.dot(h, w_ref[:, d:2 * d], preferred_element_type=F32)
    k_ref[...] = _head_rmsnorm(pk, kw_ref[...]).astype(BF16)
    v_ref[...] = jnp.dot(h, w_ref[:, 2 * d:3 * d], preferred_element_type=F32).astype(BF16)
    pg = jnp.dot(h, w_ref[:, 3 * d:4 * d], preferred_element_type=F32)
    g_ref[...] = _sigmoid(pg).astype(BF16)
    u = jnp.dot(h, wf_ref[...], preferred_element_type=F32) + bf_ref[...]
    lf_ref[...] = jnp.minimum(u, 0.0) - jnp.log1p(jnp.exp(-jnp.abs(u)))


def _fox_proj_call(x2, mod3, nw, qw, kw, bf, w_main, w_f, *, layer, batch, tm):
    t, d = x2.shape
    nh = w_f.shape[1]
    tiles_per_batch = (t // batch) // tm
    row = pl.BlockSpec((tm, d), lambda i: (i, 0))
    out_bf = jax.ShapeDtypeStruct((t, d), BF16)
    return pl.pallas_call(
        functools.partial(_fox_proj_kernel, d=d),
        grid=(t // tm,),
        in_specs=[
            row,
            pl.BlockSpec((1, 1, N_MOD * d), lambda i: (layer * batch + i // tiles_per_batch, 0, 0)),
            _resident((1, d)), _resident((1, d)), _resident((1, d)), _resident((1, nh)),
            _resident(w_main.shape), _resident(w_f.shape),
        ],
        out_specs=[row, row, row, row, pl.BlockSpec((tm, nh), lambda i: (i, 0))],
        out_shape=[out_bf, out_bf, out_bf, out_bf, jax.ShapeDtypeStruct((t, nh), F32)],
        compiler_params=_cparams(("arbitrary",)),
        name="fox_proj",
    )(x2, mod3, nw, qw, kw, bf, w_main, w_f)


FOX_AUG = 6


def _split3(x):
    hi = x.astype(BF16)
    r1 = x - hi.astype(F32)
    mid = r1.astype(BF16)
    lo = (r1 - mid.astype(F32)).astype(BF16)
    return jnp.concatenate([hi, mid, lo], axis=1)


def _fox_bias_kernel(lf_ref, qw_ref, kw_ref, pq_ref, pk_ref, oq_ref, ok_ref, aq_ref, ak_ref, flag_ref, carry_ref):
    tp = lf_ref.shape[0]

    @pl.when(pl.program_id(1) == 0)
    def _():
        carry_ref[...] = jnp.zeros_like(carry_ref)

    bound = 1.01 * LOG2E * np.sqrt(FOX_HEAD) * jnp.max(jnp.abs(qw_ref[...] * kw_ref[...]), axis=-1, keepdims=True)
    fast = bound <= FOX_STAB_MAX
    flag_ref[...] = jnp.broadcast_to(fast.astype(jnp.int32), flag_ref.shape)
    stab = jnp.where(fast, bound, 0.0)

    r_i = lax.broadcasted_iota(jnp.int32, (tp, tp), 0)
    c_i = lax.broadcasted_iota(jnp.int32, (tp, tp), 1)
    tri = (c_i <= r_i).astype(F32)
    f = jnp.dot(tri, lf_ref[...], precision=_HI, preferred_element_type=F32) + carry_ref[...]
    carry_ref[...] = f[tp - 1:tp, :]
    f2 = f * LOG2E
    aq_ref[...] = (jnp.dot(_split3(f2 - stab), pq_ref[...], preferred_element_type=F32) + oq_ref[...]).astype(BF16)
    ak_ref[...] = (jnp.dot(_split3(f2), pk_ref[...], preferred_element_type=F32) + ok_ref[...]).astype(BF16)


def _fox_bias_call(lf, qw, kw, *, batch, d, tp):
    t = lf.shape[0]
    heads = d // FOX_HEAD
    steps = (t // batch) // tp
    pq = np.zeros((3 * LANES, d), np.float32)
    pk = np.zeros((3 * LANES, d), np.float32)
    oq = np.zeros((1, d), np.float32)
    ok = np.zeros((1, d), np.float32)
    for h in range(heads):
        base = LANES * (h // 2) + FOX_AUG * (h % 2)
        for i in range(3):
            pq[i * LANES + h, base + i] = 1.0
            ok[0, base + i] = 1.0
            oq[0, base + 3 + i] = 1.0
            pk[i * LANES + h, base + 3 + i] = -1.0
    row = pl.BlockSpec((tp, d), lambda b, s: (b * steps + s, 0))
    return pl.pallas_call(
        _fox_bias_kernel,
        grid=(batch, steps),
        in_specs=[
            pl.BlockSpec((tp, LANES), lambda b, s: (b * steps + s, 0)),
            _resident((1, FOX_HEAD)), _resident((1, FOX_HEAD)),
            _resident(pq.shape), _resident(pk.shape), _resident(oq.shape), _resident(ok.shape),
        ],
        out_specs=[row, row, pl.BlockSpec((8, LANES), lambda b, s: (0, 0))],
        out_shape=[jax.ShapeDtypeStruct((t, d), BF16), jax.ShapeDtypeStruct((t, d), BF16),
                   jax.ShapeDtypeStruct((8, LANES), jnp.int32)],
        scratch_shapes=[pltpu.VMEM((1, LANES), F32)],
        compiler_params=_cparams(("arbitrary", "arbitrary")),
        name="fox_bias",
    )(lf, qw, kw, jnp.asarray(pq, BF16), jnp.asarray(pk, BF16), jnp.asarray(oq), jnp.asarray(ok))


def _fox_attn_kernel(flag_ref, q_ref, aq_ref, k_ref, ak_ref, v_ref, g_ref, o_ref,
                     v2_ref, qm_ref, acc_ref, m_ref, pa_ref, pb_ref, *, tq, tk):
    qi = pl.program_id(2)
    sub = tq // tk
    lane2 = lax.broadcasted_iota(jnp.int32, (1, 2 * LANES), 1)
    head0 = (lane2 < FOX_HEAD) | ((lane2 >= LANES) & (lane2 < LANES + FOX_AUG))
    head1 = ((lane2 >= FOX_HEAD) & (lane2 < LANES)) | ((lane2 >= LANES + FOX_AUG) & (lane2 < LANES + 2 * FOX_AUG))
    low = lax.broadcasted_iota(jnp.int32, (1, LANES), 1) < FOX_HEAD
    nt = (((1,), (1,)), ((), ()))

    @pl.when(qi == 0)
    def _():
        one_lo = jnp.broadcast_to(jnp.where(low, 1.0, 0.0).astype(BF16), (tk, LANES))
        one_hi = jnp.broadcast_to(jnp.where(low, 0.0, 1.0).astype(BF16), (tk, LANES))

        def build(j, carry):
            v = v_ref[pl.ds(pl.multiple_of(j * tk, tk), tk), :]
            zero = jnp.zeros_like(v)
            r0 = pl.multiple_of(2 * j * tk, tk)
            v2_ref[pl.ds(r0, tk), :] = jnp.concatenate([jnp.where(low, v, zero), one_lo], axis=1)
            v2_ref[pl.ds(r0 + tk, tk), :] = jnp.concatenate([jnp.where(low, zero, v), one_hi], axis=1)
            return carry

        lax.fori_loop(0, v_ref.shape[0] // tk, build, 0)

    def probs(j, masked, online):
        rows = pl.ds(pl.multiple_of(j * tk, tk), tk)
        kf = jnp.concatenate([k_ref[rows, :], ak_ref[rows, :]], axis=1)
        ps, alphas = [], []
        for e in range(2):
            s = lax.dot_general(qm_ref[e], kf, nt, preferred_element_type=F32)
            if masked:
                rpos = qi * tq + lax.broadcasted_iota(jnp.int32, (tq, tk), 0)
                cpos = j * tk + lax.broadcasted_iota(jnp.int32, (tq, tk), 1)
                s = jnp.where(cpos <= rpos, s, -jnp.inf)
            if online:
                m_prev = m_ref[e]
                m_new = jnp.maximum(m_prev, jnp.max(s, axis=-1, keepdims=True))
                alphas.append(jnp.exp2(m_prev - m_new))
                m_ref[e] = m_new
                s = s - jnp.concatenate([m_new] * (tk // LANES), axis=1)
            ps.append(jnp.exp2(s).astype(BF16))
        return jnp.concatenate(ps, axis=1), alphas

    def accumulate(p, j, alphas=None):
        pv = jnp.dot(p, v2_ref[pl.ds(pl.multiple_of(2 * j * tk, tk), 2 * tk), :],
                     preferred_element_type=F32)
        if alphas:
            a = jnp.where(low, alphas[0], alphas[1])
            acc_ref[...] = acc_ref[...] * jnp.concatenate([a, a], axis=1) + pv
        else:
            acc_ref[...] += pv

    def start():
        qf = jnp.concatenate([q_ref[...], aq_ref[...]], axis=1)
        zero = jnp.zeros_like(qf)
        qm_ref[0] = jnp.where(head0, qf, zero)
        qm_ref[1] = jnp.where(head1, qf, zero)
        acc_ref[...] = jnp.zeros_like(acc_ref)

    def finish():
        acc = acc_ref[...]
        o_ref[...] = (acc[:, :LANES] / acc[:, LANES:] * g_ref[...].astype(F32)).astype(BF16)

    n_full = qi * sub
    last_diag = n_full + sub - 1

    @pl.when(flag_ref[0] == 1)
    def _():
        start()
        for jj in range(sub):
            p, _ = probs(n_full + jj, True, False)
            if jj < sub - 1:
                accumulate(p, n_full + jj)
            else:
                pa_ref[...] = p

        def two_tiles(i, carry):
            p, _ = probs(2 * i, False, False)
            pb_ref[...] = p
            accumulate(pa_ref[...], jnp.where(i == 0, last_diag, 2 * i - 1))
            p, _ = probs(2 * i + 1, False, False)
            pa_ref[...] = p
            accumulate(pb_ref[...], 2 * i)
            return carry

        lax.fori_loop(0, n_full // 2, two_tiles, 0)

        @pl.when(n_full % 2 == 1)
        def _():
            p, _ = probs(n_full - 1, False, False)
            accumulate(pa_ref[...], jnp.where(n_full == 1, last_diag, n_full - 2))
            accumulate(p, n_full - 1)

        @pl.when(n_full % 2 == 0)
        def _():
            accumulate(pa_ref[...], jnp.where(n_full == 0, last_diag, n_full - 1))

        finish()

    @pl.when(flag_ref[0] != 1)
    def _():
        start()
        m_ref[...] = jnp.full_like(m_ref, -jnp.inf)

        def body(j, carry):
            p, alphas = probs(j, False, True)
            accumulate(p, j, alphas)
            return carry

        lax.fori_loop(0, n_full, body, 0)
        for jj in range(sub):
            p, alphas = probs(n_full + jj, True, True)
            accumulate(p, n_full + jj, alphas)
        finish()


def _fox_attn_call(flag, q, aq, k, ak, v, g, *, batch, seq, tq, tk):
    t, d = q.shape
    pairs = d // LANES
    nq = seq // tq
    q_spec = pl.BlockSpec((tq, LANES), lambda b, p, i, fl: (b * nq + i, p))
    kv_spec = pl.BlockSpec((seq, LANES), lambda b, p, i, fl: (b, p))
    grid_spec = pltpu.PrefetchScalarGridSpec(
        num_scalar_prefetch=1,
        grid=(batch, pairs, nq),
        in_specs=[q_spec, q_spec, kv_spec, kv_spec, kv_spec, q_spec],
        out_specs=q_spec,
        scratch_shapes=[
            pltpu.VMEM((2 * seq, 2 * LANES), BF16),
            pltpu.VMEM((2, tq, 2 * LANES), BF16),
            pltpu.VMEM((tq, 2 * LANES), F32),
            pltpu.VMEM((2, tq, LANES), F32),
            pltpu.VMEM((tq, 2 * tk), BF16),
            pltpu.VMEM((tq, 2 * tk), BF16),
        ],
    )
    return pl.pallas_call(
        functools.partial(_fox_attn_kernel, tq=tq, tk=tk),
        grid_spec=grid_spec,
        out_shape=jax.ShapeDtypeStruct((t, d), BF16),
        compiler_params=_cparams(("arbitrary", "arbitrary", "arbitrary")),
        name="fox_attn",
    )(flag, q, aq, k, ak, v, g)


def _post_kernel(x_ref, y_ref, mod_ref, nw_ref, fw_ref, wo_ref, w1_ref, w2_ref, o_ref, *, d, ff_blk, final):
    mod = mod_ref[0]
    g1 = mod[:, 2 * d:3 * d]
    sh2, sc2, g2 = mod[:, 3 * d:4 * d], mod[:, 4 * d:5 * d], mod[:, 5 * d:6 * d]
    x1 = x_ref[...] + g1 * jnp.dot(y_ref[...], wo_ref[...], preferred_element_type=F32)
    h = _modulate(x1, nw_ref[...], sh2, sc2).astype(BF16)
    acc = jnp.zeros_like(x1)
    for j in range(w1_ref.shape[1] // ff_blk):
        a = jnp.maximum(jnp.dot(h, w1_ref[:, j * ff_blk:(j + 1) * ff_blk], preferred_element_type=F32), 0.0)
        acc = acc + jnp.dot((a * a).astype(BF16), w2_ref[j * ff_blk:(j + 1) * ff_blk, :],
                            preferred_element_type=F32)
    x2 = x1 + g2 * acc
    if final:
        x2 = x2 * lax.rsqrt(jnp.mean(x2 * x2, axis=-1, keepdims=True) + EPS) * fw_ref[...]
    o_ref[...] = x2


def _post_call(x2, y, mod3, nw, fw, w_out, w1, w2, *, layer, batch, tm, final):
    t, d = x2.shape
    tiles_per_batch = (t // batch) // tm
    row = pl.BlockSpec((tm, d), lambda i: (i, 0))
    return pl.pallas_call(
        functools.partial(_post_kernel, d=d, ff_blk=1024, final=final),
        grid=(t // tm,),
        in_specs=[
            row, row,
            pl.BlockSpec((1, 1, N_MOD * d), lambda i: (layer * batch + i // tiles_per_batch, 0, 0)),
            _resident((1, d)), _resident((1, d)),
            _resident(w_out.shape), _resident(w1.shape), _resident(w2.shape),
        ],
        out_specs=row,
        out_shape=jax.ShapeDtypeStruct((t, d), F32),
        compiler_params=_cparams(("arbitrary",)),
        name="post",
    )(x2, y, mod3, nw, fw, w_out, w1, w2)


def kernel(x, c, w_mod, b_mod, norm1_w, norm2_w, hg_w_in, hg_w_out, hg_lb, hg_gn_w, fox_w_in, fox_b_f,
           fox_qn_w, fox_kn_w, fox_w_out, mlp_w1, mlp_w2, final_w):
    batch, seq, d = x.shape
    depth = w_mod.shape[0]
    t = batch * seq
    fox_heads = d // FOX_HEAD

    mod3 = _mod_call(c, w_mod, b_mod).reshape(depth * batch, 1, N_MOD * d)
    xs = x.reshape(t, d)
    fw = final_w.reshape(1, d)

    for i in range(depth):
        j = i // 2
        n1 = norm1_w[i].reshape(1, d)
        if i % 2 == 0:
            q, k, lf, v, g = _hg_proj_call(xs, mod3, n1, hg_lb, hg_w_in[j].astype(BF16),
                                           layer=i, batch=batch, tm=512)
            y = _hg_rec_call(q, k, lf, v, g, hg_gn_w[j].reshape(1, HG_HEAD), batch=batch, tc=1024,
                             unroll_bounded=8, unroll_exact=4)
            w_out = hg_w_out[j]
        else:
            w_in = fox_w_in[j].astype(BF16)
            pad = ((0, 0), (0, LANES - fox_heads))
            q, k, v, g, lf = _fox_proj_call(
                xs, mod3, n1,
                jnp.tile(fox_qn_w[j], fox_heads).reshape(1, d),
                jnp.tile(fox_kn_w[j], fox_heads).reshape(1, d),
                jnp.pad(fox_b_f[j].reshape(1, fox_heads), pad),
                w_in[:, :4 * d], jnp.pad(w_in[:, 4 * d:], pad),
                layer=i, batch=batch, tm=512)
            aq, ak, flag = _fox_bias_call(lf, fox_qn_w[j].reshape(1, FOX_HEAD), fox_kn_w[j].reshape(1, FOX_HEAD),
                                          batch=batch, d=d, tp=512)
            y = _fox_attn_call(flag[0, :1], q, aq, k, ak, v, g, batch=batch, seq=seq, tq=512, tk=512)
            w_out = fox_w_out[j]
        xs = _post_call(xs, y, mod3, norm2_w[i].reshape(1, d), fw, w_out.astype(BF16),
                        mlp_w1[i].astype(BF16), mlp_w2[i].astype(BF16),
                        layer=i, batch=batch, tm=512, final=(i == depth - 1))
    return xs.reshape(batch, seq, d)
```

```python
import functools

import numpy as np
import jax
import jax.numpy as jnp
from jax import lax
from jax.experimental import pallas as pl
from jax.experimental.pallas import tpu as pltpu

F32 = jnp.float32
BF16 = jnp.bfloat16
EPS = 1e-6
N_MOD = 6
HG_HEAD = 128
HG_CHUNK = 64
HG_SUB = 16
HG_SAFE_DECAY = 56.0
FOX_HEAD = 64
LANES = 128
LOG2E = float(np.log2(np.e))
FOX_STAB_MAX = 30.0
FOX_SKIP = 160.0
VMEM_LIMIT = 56 * 1024 * 1024

_HI = lax.Precision.HIGHEST


def _cparams(sem):
    return pltpu.CompilerParams(dimension_semantics=sem, vmem_limit_bytes=VMEM_LIMIT)


def _resident(shape):
    nd = len(shape)
    return pl.BlockSpec(shape, lambda *_: (0,) * nd, pipeline_mode=pl.Buffered(1))


def _sigmoid(x):
    return 1.0 / (1.0 + jnp.exp(-x))


def _modulate(x, nw, shift, scale):
    ms = jnp.mean(x * x, axis=-1, keepdims=True)
    y = x * lax.rsqrt(ms + EPS)
    return (y * nw) * (1.0 + scale) + shift


def _mod_kernel(c_ref, w_ref, b_ref, o_ref):
    c = c_ref[...]
    ca = c * _sigmoid(c)
    o_ref[0] = jnp.dot(ca, w_ref[0], precision=_HI, preferred_element_type=F32) + b_ref[0]


def _mod_call(c, w_mod, b_mod):
    depth, d, n = w_mod.shape
    b = c.shape[0]
    tn = 1024
    return pl.pallas_call(
        _mod_kernel,
        grid=(depth, n // tn),
        in_specs=[
            pl.BlockSpec((b, d), lambda l, j: (0, 0)),
            pl.BlockSpec((1, d, tn), lambda l, j: (l, 0, j)),
            pl.BlockSpec((1, 1, tn), lambda l, j: (l, 0, j)),
        ],
        out_specs=pl.BlockSpec((1, b, tn), lambda l, j: (l, 0, j)),
        out_shape=jax.ShapeDtypeStruct((depth, b, n), F32),
        compiler_params=_cparams(("arbitrary", "arbitrary")),
        name="mod",
    )(c, w_mod, b_mod.reshape(depth, 1, n))


def _hg_proj_kernel(x_ref, mod_ref, nw_ref, lb_ref, w_ref, q_ref, k_ref, lf_ref, v_ref, g_ref, *, layer, d):
    mod = mod_ref[0]
    h = _modulate(x_ref[...], nw_ref[...], mod[:, 0:d], mod[:, d:2 * d]).astype(BF16)

    pq = jnp.dot(h, w_ref[:, 0:d], preferred_element_type=F32)
    q_ref[...] = (pq * _sigmoid(pq)).astype(BF16)

    lbp = lb_ref[...]
    e = jnp.exp(lbp - jnp.max(lbp, axis=0, keepdims=True))
    lb = jnp.sum(e[0:layer + 1], axis=0, keepdims=True) / jnp.sum(e, axis=0, keepdims=True)

    z = jnp.dot(h, w_ref[:, d:2 * d], preferred_element_type=F32)
    kk = (1.0 - lb) * _sigmoid(-z)
    f = lb + (1.0 - lb) * _sigmoid(z)
    lf_ref[...] = jnp.where(kk < 0.5, jnp.log1p(-kk), jnp.log(f))
    k_ref[...] = kk.astype(BF16)

    v_ref[...] = jnp.dot(h, w_ref[:, 2 * d:3 * d], preferred_element_type=F32).astype(BF16)
    pg = jnp.dot(h, w_ref[:, 3 * d:4 * d], preferred_element_type=F32)
    g_ref[...] = (pg * _sigmoid(pg)).astype(BF16)


def _hg_proj_call(x2, mod3, nw, hg_lb, w_in, *, layer, batch, tm):
    t, d = x2.shape
    tiles_per_batch = (t // batch) // tm
    row = pl.BlockSpec((tm, d), lambda i: (i, 0))
    out_bf = jax.ShapeDtypeStruct((t, d), BF16)
    return pl.pallas_call(
        functools.partial(_hg_proj_kernel, layer=layer, d=d),
        grid=(t // tm,),
        in_specs=[
            row,
            pl.BlockSpec((1, 1, N_MOD * d), lambda i: (layer * batch + i // tiles_per_batch, 0, 0)),
            _resident((1, d)),
            _resident(hg_lb.shape),
            _resident(w_in.shape),
        ],
        out_specs=[row, row, row, row, row],
        out_shape=[out_bf, out_bf, jax.ShapeDtypeStruct((t, d), F32), out_bf, out_bf],
        compiler_params=_cparams(("arbitrary",)),
        name="hg_proj",
    )(x2, mod3, nw, hg_lb, w_in)


def _hg_rec_kernel(q_ref, k_ref, lf_ref, v_ref, g_ref, gnw_ref, o_ref, st_ref, gs_ref, ks_ref, *,
                   n_chunks, unroll_bounded, unroll_exact):
    c, sub = HG_CHUNK, HG_SUB
    n_sub = c // sub
    half = sub // 2

    @pl.when(pl.program_id(2) == 0)
    def _():
        st_ref[...] = jnp.zeros_like(st_ref)

    r_i = lax.broadcasted_iota(jnp.int32, (c, c), 0)
    c_i = lax.broadcasted_iota(jnp.int32, (c, c), 1)
    tri = (c_i <= r_i).astype(BF16)
    ones_w = jnp.ones((HG_HEAD, c), BF16)
    half_row = lax.broadcasted_iota(jnp.int32, (half, HG_HEAD), 0)
    a_lane = lax.broadcasted_iota(jnp.int32, (half, c), 1)
    gnw = gnw_ref[...]
    nt = (((1,), (1,)), ((), ()))
    tn = (((0,), (0,)), ((), ()))

    def offdiag_operands(q, k, gcum, j, bounded):
        lo = j * sub
        hi = lo + sub if bounded else lo
        g_b = gcum[lo - 1:lo, :] if j > 0 else jnp.zeros((1, HG_HEAD), F32)
        qh = (q[lo:lo + sub, :] * jnp.exp2(gcum[lo:lo + sub, :] - g_b)).astype(BF16)
        kh = (k[0:hi, :] * jnp.exp2(g_b - gcum[0:hi, :])).astype(BF16)
        if hi < c:
            kh = jnp.concatenate([kh, jnp.zeros((c - hi, HG_HEAD), BF16)], axis=0)
        return qh, kh

    def diag_products(q, k, gcum, slot, j):
        lo = j * sub
        gt = (gcum[lo:lo + half, :], gcum[lo + half:lo + sub, :])
        qt = (q[lo:lo + half, :], q[lo + half:lo + sub, :])
        ps = []
        for s in range(sub):
            g_s = gs_ref[slot, pl.ds(lo + s, 1), :]
            k_s = ks_ref[slot, pl.ds(lo + s, 1), :]
            for hf in range(2):
                if s >= half and hf == 0:
                    continue
                dlt = gt[hf] - g_s
                if (s >= half) == (hf == 1):
                    dlt = jnp.where(half_row >= s % half, dlt, -jnp.inf)
                ps.append((qt[hf] * k_s) * jnp.exp2(dlt))
        return jnp.concatenate(ps, axis=0).astype(BF16)

    def diag_scatter(r, j):
        lo = j * sub
        a_top = jnp.zeros((half, c), F32)
        a_bot = jnp.zeros((half, c), F32)
        for s in range(sub):
            if s < half:
                a_top = jnp.where(a_lane == lo + s, r[2 * s * half:(2 * s + 1) * half, :], a_top)
                a_bot = jnp.where(a_lane == lo + s, r[(2 * s + 1) * half:(2 * s + 2) * half, :], a_bot)
            else:
                a_bot = jnp.where(a_lane == lo + s, r[(half + s) * half:(half + s + 1) * half, :], a_bot)
        return jnp.concatenate([a_top, a_bot], axis=0)

    def run(bounded, unroll):
        def group(i, st):
            us = range(unroll)
            rows = [pl.ds(pl.multiple_of((i * unroll + u) * c, c), c) for u in us]
            g3 = [jnp.dot(tri, _split3(lf_ref[r, :]), preferred_element_type=F32) for r in rows]
            q = [q_ref[r, :].astype(F32) for r in rows]
            k = [k_ref[r, :].astype(F32) for r in rows]
            gcum = [(g[:, 0:HG_HEAD] + g[:, HG_HEAD:2 * HG_HEAD] + g[:, 2 * HG_HEAD:]) * LOG2E for g in g3]
            g_last = [g[c - 1:c, :] for g in gcum]
            qg = [(q[u] * jnp.exp2(gcum[u])).astype(BF16) for u in us]
            kd = [(k[u] * jnp.exp2(g_last[u] - gcum[u])).astype(BF16) for u in us]
            first = 0 if bounded else 1
            off = [[offdiag_operands(q[u], k[u], gcum[u], j, bounded) for j in range(first, n_sub)] for u in us]
            if not bounded:
                for u in us:
                    gs_ref[u] = gcum[u]
                    ks_ref[u] = k[u]
                pst = [[diag_products(q[u], k[u], gcum[u], u, j) for j in range(n_sub)] for u in us]
            kv = [lax.dot_general(v_ref[rows[u], :], kd[u], tn, preferred_element_type=F32) for u in us]
            blocks = [[lax.dot_general(qh, kh, nt, preferred_element_type=F32) for qh, kh in off[u]] for u in us]
            if bounded:
                a = [jnp.where(c_i <= r_i, jnp.concatenate(blocks[u], axis=0), 0.0) for u in us]
            else:
                sums = [[jnp.dot(p, ones_w, preferred_element_type=F32) for p in pst[u]] for u in us]
                a = []
                for u in us:
                    a_rows = [diag_scatter(sums[u][j], j) for j in range(n_sub)]
                    for j in range(1, n_sub):
                        a_rows[j] = a_rows[j] + blocks[u][j - 1]
                    a.append(jnp.concatenate(a_rows, axis=0))
            o_intra = [jnp.dot(a[u].astype(BF16), v_ref[rows[u], :], preferred_element_type=F32) for u in us]
            for u in us:
                o = o_intra[u] + lax.dot_general(qg[u], st.astype(BF16), nt, preferred_element_type=F32)
                st = st * jnp.exp2(g_last[u]) + kv[u]
                on = o * lax.rsqrt(jnp.mean(o * o, axis=-1, keepdims=True) + EPS) * gnw
                o_ref[rows[u], :] = (on * g_ref[rows[u], :].astype(F32)).astype(BF16)
            return st

        st_ref[...] = lax.fori_loop(0, n_chunks // unroll, group, st_ref[...])

    w_r = lax.broadcasted_iota(jnp.int32, (n_chunks * n_sub, n_chunks * c), 0)
    w_c = lax.broadcasted_iota(jnp.int32, (n_chunks * n_sub, n_chunks * c), 1)
    win = (lax.shift_right_logical(w_c, sub.bit_length() - 1) == w_r).astype(BF16)
    wsum = jnp.dot(win, lf_ref[...].astype(BF16), preferred_element_type=F32)
    bounded = jnp.min(wsum) * LOG2E >= -HG_SAFE_DECAY

    @pl.when(bounded)
    def _():
        run(True, unroll_bounded)

    @pl.when(jnp.logical_not(bounded))
    def _():
        run(False, unroll_exact)


def _hg_rec_call(q, k, lf, v, g, gnw, *, batch, tc, unroll_bounded, unroll_exact):
    t, d = q.shape
    heads = d // HG_HEAD
    steps = (t // batch) // tc
    unroll = unroll_exact
    blk = pl.BlockSpec((tc, HG_HEAD), lambda b, h, s: (b * steps + s, h))
    return pl.pallas_call(
        functools.partial(_hg_rec_kernel, n_chunks=tc // HG_CHUNK, unroll_bounded=unroll_bounded,
                          unroll_exact=unroll_exact),
        grid=(batch, heads, steps),
        in_specs=[blk, blk, blk, blk, blk, pl.BlockSpec((1, HG_HEAD), lambda b, h, s: (0, 0))],
        out_specs=blk,
        out_shape=jax.ShapeDtypeStruct((t, d), BF16),
        scratch_shapes=[
            pltpu.VMEM((HG_HEAD, HG_HEAD), F32),
            pltpu.VMEM((unroll, HG_CHUNK, HG_HEAD), F32),
            pltpu.VMEM((unroll, HG_CHUNK, HG_HEAD), F32),
        ],
        compiler_params=_cparams(("arbitrary", "arbitrary", "arbitrary")),
        name="hg_rec",
    )(q, k, lf, v, g, gnw)


def _head_rmsnorm(p, w_row):
    low = lax.broadcasted_iota(jnp.int32, (1, LANES), 1) < FOX_HEAD
    outs = []
    for j in range(p.shape[1] // LANES):
        pj = p[:, j * LANES:(j + 1) * LANES]
        ss = pj * pj
        s_lo = jnp.sum(jnp.where(low, ss, 0.0), axis=-1, keepdims=True)
        s_hi = jnp.sum(jnp.where(low, 0.0, ss), axis=-1, keepdims=True)
        inv = lax.rsqrt(jnp.where(low, s_lo, s_hi) * (1.0 / FOX_HEAD) + EPS)
        outs.append(pj * inv)
    return jnp.concatenate(outs, axis=1) * w_row


def _fox_proj_kernel(x_ref, mod_ref, nw_ref, qw_ref, kw_ref, bf_ref, w_ref, wf_ref,
                     q_ref, k_ref, v_ref, g_ref, lf_ref, *, d):
    mod = mod_ref[0]
    h = _modulate(x_ref[...], nw_ref[...], mod[:, 0:d], mod[:, d:2 * d]).astype(BF16)
    scale = LOG2E / np.sqrt(FOX_HEAD)
    pq = jnp.dot(h, w_ref[:, 0:d], preferred_element_type=F32)
    q_ref[...] = (_head_rmsnorm(pq, qw_ref[...]) * scale).astype(BF16)
    pk = jnp.dot(h, w_ref[:, d:2 * d], preferred_element_type=F32)
    k_ref[...] = _head_rmsnorm(pk, kw_ref[...]).astype(BF16)
    v_ref[...] = jnp.dot(h, w_ref[:, 2 * d:3 * d], preferred_element_type=F32).astype(BF16)
    pg = jnp.dot(h, w_ref[:, 3 * d:4 * d], preferred_element_type=F32)
    g_ref[...] = _sigmoid(pg).astype(BF16)
    u = jnp.dot(h, wf_ref[...], preferred_element_type=F32) + bf_ref[...]
    lf_ref[...] = jnp.minimum(u, 0.0) - jnp.log1p(jnp.exp(-jnp.abs(u)))


def _fox_proj_call(x2, mod3, nw, qw, kw, bf, w_main, w_f, *, layer, batch, tm):
    t, d = x2.shape
    nh = w_f.shape[1]
    tiles_per_batch = (t // batch) // tm
    row = pl.BlockSpec((tm, d), lambda i: (i, 0))
    out_bf = jax.ShapeDtypeStruct((t, d), BF16)
    return pl.pallas_call(
        functools.partial(_fox_proj_kernel, d=d),
        grid=(t // tm,),
        in_specs=[
            row,
            pl.BlockSpec((1, 1, N_MOD * d), lambda i: (layer * batch + i // tiles_per_batch, 0, 0)),
            _resident((1, d)), _resident((1, d)), _resident((1, d)), _resident((1, nh)),
            _resident(w_main.shape), _resident(w_f.shape),
        ],
        out_specs=[row, row, row, row, pl.BlockSpec((tm, nh), lambda i: (i, 0))],
        out_shape=[out_bf, out_bf, out_bf, out_bf, jax.ShapeDtypeStruct((t, nh), F32)],
        compiler_params=_cparams(("arbitrary",)),
        name="fox_proj",
    )(x2, mod3, nw, qw, kw, bf, w_main, w_f)


FOX_AUG = 6


def _split3(x):
    hi = x.astype(BF16)
    r1 = x - hi.astype(F32)
    mid = r1.astype(BF16)
    lo = (r1 - mid.astype(F32)).astype(BF16)
    return jnp.concatenate([hi, mid, lo], axis=1)


def _fox_bias_kernel(lf_ref, qw_ref, kw_ref, pq_ref, pk_ref, oq_ref, ok_ref,
                     aq_ref, ak_ref, edge_ref, flag_ref, carry_ref):
    tp = lf_ref.shape[0]

    @pl.when(pl.program_id(1) == 0)
    def _():
        carry_ref[...] = jnp.zeros_like(carry_ref)

    bound = 1.01 * LOG2E * np.sqrt(FOX_HEAD) * jnp.max(jnp.abs(qw_ref[...] * kw_ref[...]), axis=-1, keepdims=True)
    fast = bound <= FOX_STAB_MAX
    flag_ref[...] = jnp.broadcast_to(fast.astype(jnp.int32), flag_ref.shape)
    stab = jnp.where(fast, bound, 0.0)

    r_i = lax.broadcasted_iota(jnp.int32, (tp, tp), 0)
    c_i = lax.broadcasted_iota(jnp.int32, (tp, tp), 1)
    tri = (c_i <= r_i).astype(BF16)
    f3 = jnp.dot(tri, _split3(lf_ref[...]), preferred_element_type=F32)
    f = f3[:, 0:LANES] + f3[:, LANES:2 * LANES] + f3[:, 2 * LANES:] + carry_ref[...]
    carry_ref[...] = f[tp - 1:tp, :]
    f2 = f * LOG2E
    edge_ref[0] = jnp.concatenate([f2[0:1, :], f2[tp - 1:tp, :]], axis=0)
    aq_ref[...] = jnp.dot(_split3(f2 - stab), pq_ref[...], preferred_element_type=F32) + oq_ref[...]
    ak_ref[...] = jnp.dot(_split3(f2), pk_ref[...], preferred_element_type=F32) + ok_ref[...]


def _fox_bias_call(lf, qw, kw, *, batch, d, tp):
    t = lf.shape[0]
    heads = d // FOX_HEAD
    steps = (t // batch) // tp
    pq = np.zeros((3 * LANES, LANES), np.float32)
    pk = np.zeros((3 * LANES, LANES), np.float32)
    oq = np.zeros((1, LANES), np.float32)
    ok = np.zeros((1, LANES), np.float32)
    for h in range(heads):
        base = FOX_AUG * h
        for i in range(3):
            pq[i * LANES + h, base + i] = 1.0
            ok[0, base + i] = 1.0
            oq[0, base + 3 + i] = 1.0
            pk[i * LANES + h, base + 3 + i] = -1.0
    row = pl.BlockSpec((tp, LANES), lambda b, s: (b * steps + s, 0))
    return pl.pallas_call(
        _fox_bias_kernel,
        grid=(batch, steps),
        in_specs=[
            row,
            _resident((1, FOX_HEAD)), _resident((1, FOX_HEAD)),
            _resident(pq.shape), _resident(pk.shape), _resident(oq.shape), _resident(ok.shape),
        ],
        out_specs=[row, row,
                   pl.BlockSpec((1, 2, LANES), lambda b, s: (b * steps + s, 0, 0)),
                   pl.BlockSpec((8, LANES), lambda b, s: (0, 0))],
        out_shape=[jax.ShapeDtypeStruct((t, LANES), F32), jax.ShapeDtypeStruct((t, LANES), F32),
                   jax.ShapeDtypeStruct((batch * steps, 2, LANES), F32),
                   jax.ShapeDtypeStruct((8, LANES), jnp.int32)],
        scratch_shapes=[pltpu.VMEM((1, LANES), F32)],
        compiler_params=_cparams(("arbitrary", "arbitrary")),
        name="fox_bias",
    )(lf, qw, kw, jnp.asarray(pq, BF16), jnp.asarray(pk, BF16), jnp.asarray(oq), jnp.asarray(ok))


def _fox_attn_kernel(flag_ref, fs_ref, fe_ref, q_ref, aq_ref, k_ref, ak_ref, v_ref, g_ref, o_ref,
                     v2_ref, ka_ref, qm_ref, acc_ref, m_ref, pa_ref, pb_ref, *, tq, tk, heads):
    bi, pair, qi = pl.program_id(0), pl.program_id(1), pl.program_id(2)
    nq = pl.num_programs(2)
    nk = v_ref.shape[0] // tk
    sub = tq // tk
    lane2 = lax.broadcasted_iota(jnp.int32, (1, 2 * LANES), 1)
    head0 = (lane2 < FOX_HEAD) | ((lane2 >= LANES) & (lane2 < LANES + FOX_AUG))
    head1 = ((lane2 >= FOX_HEAD) & (lane2 < LANES)) | ((lane2 >= LANES + FOX_AUG) & (lane2 < LANES + 2 * FOX_AUG))
    lane1 = lax.broadcasted_iota(jnp.int32, (1, LANES), 1)
    low = lane1 < FOX_HEAD
    nt = (((1,), (1,)), ((), ()))

    def bias_lanes(a):
        shifted = pltpu.roll(a, (LANES - 2 * FOX_AUG * pair) % LANES, axis=1)
        return jnp.where(lane1 < 2 * FOX_AUG, shifted, 0.0).astype(BF16)

    @pl.when(qi == 0)
    def _():
        one_lo = jnp.broadcast_to(jnp.where(low, 1.0, 0.0).astype(BF16), (tk, LANES))
        one_hi = jnp.broadcast_to(jnp.where(low, 0.0, 1.0).astype(BF16), (tk, LANES))

        def build(j, carry):
            rows = pl.ds(pl.multiple_of(j * tk, tk), tk)
            v = v_ref[rows, :]
            zero = jnp.zeros_like(v)
            r0 = pl.multiple_of(2 * j * tk, tk)
            v2_ref[pl.ds(r0, tk), :] = jnp.concatenate([jnp.where(low, v, zero), one_lo], axis=1)
            v2_ref[pl.ds(r0 + tk, tk), :] = jnp.concatenate([jnp.where(low, zero, v), one_hi], axis=1)
            ka_ref[rows, :] = bias_lanes(ak_ref[rows, :])
            return carry

        lax.fori_loop(0, nk, build, 0)

    def probs(j, masked, online):
        rows = pl.ds(pl.multiple_of(j * tk, tk), tk)
        kf = jnp.concatenate([k_ref[rows, :], ka_ref[rows, :]], axis=1)
        ps, alphas = [], []
        for e in range(2):
            s = lax.dot_general(qm_ref[e], kf, nt, preferred_element_type=F32)
            if masked:
                rpos = qi * tq + lax.broadcasted_iota(jnp.int32, (tq, tk), 0)
                cpos = j * tk + lax.broadcasted_iota(jnp.int32, (tq, tk), 1)
                s = jnp.where(cpos <= rpos, s, -jnp.inf)
            if online:
                m_prev = m_ref[e]
                m_new = jnp.maximum(m_prev, jnp.max(s, axis=-1, keepdims=True))
                alphas.append(jnp.exp2(m_prev - m_new))
                m_ref[e] = m_new
                s = s - jnp.concatenate([m_new] * (tk // LANES), axis=1)
            ps.append(jnp.exp2(s).astype(BF16))
        return jnp.concatenate(ps, axis=1), alphas

    def accumulate(p, j, alphas=None):
        pv = jnp.dot(p, v2_ref[pl.ds(pl.multiple_of(2 * j * tk, tk), 2 * tk), :],
                     preferred_element_type=F32)
        if alphas:
            a = jnp.where(low, alphas[0], alphas[1])
            acc_ref[...] = acc_ref[...] * jnp.concatenate([a, a], axis=1) + pv
        else:
            acc_ref[...] += pv

    def start():
        qf = jnp.concatenate([q_ref[...], bias_lanes(aq_ref[...])], axis=1)
        zero = jnp.zeros_like(qf)
        qm_ref[0] = jnp.where(head0, qf, zero)
        qm_ref[1] = jnp.where(head1, qf, zero)
        acc_ref[...] = jnp.zeros_like(acc_ref)

    def finish():
        acc = acc_ref[...]
        o_ref[...] = (acc[:, :LANES] / acc[:, LANES:] * g_ref[...].astype(F32)).astype(BF16)

    n_full = qi * sub
    last_diag = n_full + sub - 1

    @pl.when(flag_ref[0] == 1)
    def _():
        start()
        for jj in range(sub):
            p, _ = probs(n_full + jj, True, False)
            if jj < sub - 1:
                accumulate(p, n_full + jj)
            else:
                pa_ref[...] = p

        h0 = bi * heads + 2 * pair
        fq0 = fs_ref[h0 * nq + qi]
        fq1 = fs_ref[(h0 + 1) * nq + qi]

        def dead(j, cnt):
            gone0 = fq0 - fe_ref[h0 * nk + j] < -FOX_SKIP
            gone1 = fq1 - fe_ref[(h0 + 1) * nk + j] < -FOX_SKIP
            return cnt + jnp.logical_and(gone0, gone1).astype(jnp.int32)

        first = lax.fori_loop(0, n_full, dead, 0)
        n_live = n_full - first

        def two_tiles(i, carry):
            j = first + 2 * i
            p, _ = probs(j, False, False)
            pb_ref[...] = p
            accumulate(pa_ref[...], jnp.where(i == 0, last_diag, j - 1))
            p, _ = probs(j + 1, False, False)
            pa_ref[...] = p
            accumulate(pb_ref[...], j)
            return carry

        lax.fori_loop(0, n_live // 2, two_tiles, 0)

        @pl.when(n_live % 2 == 1)
        def _():
            p, _ = probs(n_full - 1, False, False)
            accumulate(pa_ref[...], jnp.where(n_live == 1, last_diag, n_full - 2))
            accumulate(p, n_full - 1)

        @pl.when(n_live % 2 == 0)
        def _():
            accumulate(pa_ref[...], jnp.where(n_live == 0, last_diag, n_full - 1))

        finish()

    @pl.when(flag_ref[0] != 1)
    def _():
        start()
        m_ref[...] = jnp.full_like(m_ref, -jnp.inf)

        def body(j, carry):
            p, alphas = probs(j, False, True)
            accumulate(p, j, alphas)
            return carry

        lax.fori_loop(0, n_full, body, 0)
        for jj in range(sub):
            p, alphas = probs(n_full + jj, True, True)
            accumulate(p, n_full + jj, alphas)
        finish()


def _fox_attn_call(flag, f_first, f_last, q, aq, k, ak, v, g, *, batch, seq, tq, tk):
    t, d = q.shape
    pairs = d // LANES
    nq = seq // tq
    q_spec = pl.BlockSpec((tq, LANES), lambda b, p, i, *_: (b * nq + i, p))
    kv_spec = pl.BlockSpec((seq, LANES), lambda b, p, i, *_: (b, p))
    grid_spec = pltpu.PrefetchScalarGridSpec(
        num_scalar_prefetch=3,
        grid=(batch, pairs, nq),
        in_specs=[q_spec, pl.BlockSpec((tq, LANES), lambda b, p, i, *_: (b * nq + i, 0)),
                  kv_spec, pl.BlockSpec((seq, LANES), lambda b, p, i, *_: (b, 0)),
                  kv_spec, q_spec],
        out_specs=q_spec,
        scratch_shapes=[
            pltpu.VMEM((2 * seq, 2 * LANES), BF16),
            pltpu.VMEM((seq, LANES), BF16),
            pltpu.VMEM((2, tq, 2 * LANES), BF16),
            pltpu.VMEM((tq, 2 * LANES), F32),
            pltpu.VMEM((2, tq, LANES), F32),
            pltpu.VMEM((tq, 2 * tk), BF16),
            pltpu.VMEM((tq, 2 * tk), BF16),
        ],
    )
    return pl.pallas_call(
        functools.partial(_fox_attn_kernel, tq=tq, tk=tk, heads=d // FOX_HEAD),
        grid_spec=grid_spec,
        out_shape=jax.ShapeDtypeStruct((t, d), BF16),
        compiler_params=_cparams(("arbitrary", "arbitrary", "arbitrary")),
        name="fox_attn",
    )(flag, f_first, f_last, q, aq, k, ak, v, g)


def _post_kernel(x_ref, y_ref, mod_ref, nw_ref, fw_ref, wo_ref, w1_ref, w2_ref, o_ref, *, d, ff_blk, final):
    mod = mod_ref[0]
    g1 = mod[:, 2 * d:3 * d]
    sh2, sc2, g2 = mod[:, 3 * d:4 * d], mod[:, 4 * d:5 * d], mod[:, 5 * d:6 * d]
    x1 = x_ref[...] + g1 * jnp.dot(y_ref[...], wo_ref[...], preferred_element_type=F32)
    h = _modulate(x1, nw_ref[...], sh2, sc2).astype(BF16)
    acc = jnp.zeros_like(x1)
    for j in range(w1_ref.shape[1] // ff_blk):
        a = jnp.maximum(jnp.dot(h, w1_ref[:, j * ff_blk:(j + 1) * ff_blk], preferred_element_type=F32), 0.0)
        acc = acc + jnp.dot((a * a).astype(BF16), w2_ref[j * ff_blk:(j + 1) * ff_blk, :],
                            preferred_element_type=F32)
    x2 = x1 + g2 * acc
    if final:
        x2 = x2 * lax.rsqrt(jnp.mean(x2 * x2, axis=-1, keepdims=True) + EPS) * fw_ref[...]
    o_ref[...] = x2


def _post_call(x2, y, mod3, nw, fw, w_out, w1, w2, *, layer, batch, tm, final):
    t, d = x2.shape
    tiles_per_batch = (t // batch) // tm
    row = pl.BlockSpec((tm, d), lambda i: (i, 0))
    return pl.pallas_call(
        functools.partial(_post_kernel, d=d, ff_blk=1024, final=final),
        grid=(t // tm,),
        in_specs=[
            row, row,
            pl.BlockSpec((1, 1, N_MOD * d), lambda i: (layer * batch + i // tiles_per_batch, 0, 0)),
            _resident((1, d)), _resident((1, d)),
            _resident(w_out.shape), _resident(w1.shape), _resident(w2.shape),
        ],
        out_specs=row,
        out_shape=jax.ShapeDtypeStruct((t, d), F32),
        compiler_params=_cparams(("arbitrary",)),
        name="post",
    )(x2, y, mod3, nw, fw, w_out, w1, w2)


def kernel(x, c, w_mod, b_mod, norm1_w, norm2_w, hg_w_in, hg_w_out, hg_lb, hg_gn_w, fox_w_in, fox_b_f,
           fox_qn_w, fox_kn_w, fox_w_out, mlp_w1, mlp_w2, final_w):
    batch, seq, d = x.shape
    depth = w_mod.shape[0]
    t = batch * seq
    fox_heads = d // FOX_HEAD

    mod3 = _mod_call(c, w_mod, b_mod).reshape(depth * batch, 1, N_MOD * d)
    xs = x.reshape(t, d)
    fw = final_w.reshape(1, d)

    for i in range(depth):
        j = i // 2
        n1 = norm1_w[i].reshape(1, d)
        if i % 2 == 0:
            q, k, lf, v, g = _hg_proj_call(xs, mod3, n1, hg_lb, hg_w_in[j].astype(BF16),
                                           layer=i, batch=batch, tm=512)
            y = _hg_rec_call(q, k, lf, v, g, hg_gn_w[j].reshape(1, HG_HEAD), batch=batch, tc=1024,
                             unroll_bounded=8, unroll_exact=4)
            w_out = hg_w_out[j]
        else:
            w_in = fox_w_in[j].astype(BF16)
            pad = ((0, 0), (0, LANES - fox_heads))
            q, k, v, g, lf = _fox_proj_call(
                xs, mod3, n1,
                jnp.tile(fox_qn_w[j], fox_heads).reshape(1, d),
                jnp.tile(fox_kn_w[j], fox_heads).reshape(1, d),
                jnp.pad(fox_b_f[j].reshape(1, fox_heads), pad),
                w_in[:, :4 * d], jnp.pad(w_in[:, 4 * d:], pad),
                layer=i, batch=batch, tm=512)
            ta = 512
            aq, ak, edge, flag = _fox_bias_call(lf, fox_qn_w[j].reshape(1, FOX_HEAD),
                                                fox_kn_w[j].reshape(1, FOX_HEAD), batch=batch, d=d, tp=ta)
            edge = edge.reshape(batch, seq // ta, 2, LANES)[..., :fox_heads].transpose(2, 0, 3, 1)
            y = _fox_attn_call(flag[0, :1], edge[0].reshape(-1), edge[1].reshape(-1), q, aq, k, ak, v, g,
                               batch=batch, seq=seq, tq=ta, tk=ta)
            w_out = fox_w_out[j]
        xs = _post_call(xs, y, mod3, norm2_w[i].reshape(1, d), fw, w_out.astype(BF16),
                        mlp_w1[i].astype(BF16), mlp_w2[i].astype(BF16),
                        layer=i, batch=batch, tm=512, final=(i == depth - 1))
    return xs.reshape(batch, seq, d)
```

```python
import functools

import numpy as np
import jax
import jax.numpy as jnp
from jax import lax
from jax.experimental import pallas as pl
from jax.experimental.pallas import tpu as pltpu

F32 = jnp.float32
BF16 = jnp.bfloat16
EPS = 1e-6
N_MOD = 6
HG_HEAD = 128
HG_CHUNK = 64
HG_SUB = 16
HG_SAFE_DECAY = 56.0
FOX_HEAD = 64
LANES = 128
LOG2E = float(np.log2(np.e))
FOX_STAB_MAX = 30.0
FOX_SKIP = 160.0
ROW_SLAB = 256
VMEM_LIMIT = 56 * 1024 * 1024

_HI = lax.Precision.HIGHEST


def _cparams(sem):
    return pltpu.CompilerParams(dimension_semantics=sem, vmem_limit_bytes=VMEM_LIMIT)


def _resident(shape):
    nd = len(shape)
    return pl.BlockSpec(shape, lambda *_: (0,) * nd, pipeline_mode=pl.Buffered(1))


def _sigmoid(x):
    return 0.5 * jnp.tanh(0.5 * x) + 0.5


def _modulate(x, nw, shift, scale):
    ms = jnp.mean(x * x, axis=-1, keepdims=True)
    y = x * lax.rsqrt(ms + EPS)
    return (y * nw) * (1.0 + scale) + shift


def _mod_kernel(c_ref, w_ref, b_ref, o_ref):
    c = c_ref[...]
    ca = c * _sigmoid(c)
    o_ref[0] = jnp.dot(ca, w_ref[0], precision=_HI, preferred_element_type=F32) + b_ref[0]


def _mod_call(c, w_mod, b_mod):
    depth, d, n = w_mod.shape
    b = c.shape[0]
    tn = 1024
    return pl.pallas_call(
        _mod_kernel,
        grid=(depth, n // tn),
        in_specs=[
            pl.BlockSpec((b, d), lambda l, j: (0, 0)),
            pl.BlockSpec((1, d, tn), lambda l, j: (l, 0, j)),
            pl.BlockSpec((1, 1, tn), lambda l, j: (l, 0, j)),
        ],
        out_specs=pl.BlockSpec((1, b, tn), lambda l, j: (l, 0, j)),
        out_shape=jax.ShapeDtypeStruct((depth, b, n), F32),
        compiler_params=_cparams(("arbitrary", "arbitrary")),
        name="mod",
    )(c, w_mod, b_mod.reshape(depth, 1, n))


def _hg_proj_kernel(x_ref, mod_ref, nw_ref, lb_ref, w_ref, q_ref, k_ref, lf_ref, v_ref, g_ref, *, layer, d):
    mod = mod_ref[0]
    lbp = lb_ref[...]
    e = jnp.exp(lbp - jnp.max(lbp, axis=0, keepdims=True))
    lb = jnp.sum(e[0:layer + 1], axis=0, keepdims=True) / jnp.sum(e, axis=0, keepdims=True)

    for r0 in range(0, x_ref.shape[0], ROW_SLAB):
        rows = slice(r0, r0 + ROW_SLAB)
        h = _modulate(x_ref[rows, :], nw_ref[...], mod[:, 0:d], mod[:, d:2 * d]).astype(BF16)

        pq = jnp.dot(h, w_ref[:, 0:d], preferred_element_type=F32)
        q_ref[rows, :] = (pq * _sigmoid(pq)).astype(BF16)

        z = jnp.dot(h, w_ref[:, d:2 * d], preferred_element_type=F32)
        th = 0.5 * jnp.tanh(0.5 * z)
        kk = (1.0 - lb) * (0.5 - th)
        f = lb + (1.0 - lb) * (0.5 + th)
        lf_ref[rows, :] = jnp.log(jnp.where(kk < 0.5, 1.0 - kk, f))
        k_ref[rows, :] = kk.astype(BF16)

        v_ref[rows, :] = jnp.dot(h, w_ref[:, 2 * d:3 * d], preferred_element_type=F32).astype(BF16)
        pg = jnp.dot(h, w_ref[:, 3 * d:4 * d], preferred_element_type=F32)
        g_ref[rows, :] = (pg * _sigmoid(pg)).astype(BF16)


def _hg_proj_call(x2, mod3, nw, hg_lb, w_in, *, layer, batch, tm):
    t, d = x2.shape
    tiles_per_batch = (t // batch) // tm
    row = pl.BlockSpec((tm, d), lambda i: (i, 0))
    out_bf = jax.ShapeDtypeStruct((t, d), BF16)
    return pl.pallas_call(
        functools.partial(_hg_proj_kernel, layer=layer, d=d),
        grid=(t // tm,),
        in_specs=[
            row,
            pl.BlockSpec((1, 1, N_MOD * d), lambda i: (layer * batch + i // tiles_per_batch, 0, 0)),
            _resident((1, d)),
            _resident(hg_lb.shape),
            _resident(w_in.shape),
        ],
        out_specs=[row, row, row, row, row],
        out_shape=[out_bf, out_bf, jax.ShapeDtypeStruct((t, d), F32), out_bf, out_bf],
        compiler_params=_cparams(("arbitrary",)),
        name="hg_proj",
    )(x2, mod3, nw, hg_lb, w_in)


def _hg_rec_kernel(q_ref, k_ref, lf_ref, v_ref, g_ref, gnw_ref, o_ref, st_ref, gs_ref, ks_ref, *,
                   n_chunks, unroll_bounded, unroll_exact):
    c, sub = HG_CHUNK, HG_SUB
    n_sub = c // sub
    half = sub // 2

    @pl.when(pl.program_id(2) == 0)
    def _():
        st_ref[...] = jnp.zeros_like(st_ref)

    r_i = lax.broadcasted_iota(jnp.int32, (c, c), 0)
    c_i = lax.broadcasted_iota(jnp.int32, (c, c), 1)
    tri = (c_i <= r_i).astype(BF16)
    ones_w = jnp.ones((HG_HEAD, c), BF16)
    half_row = lax.broadcasted_iota(jnp.int32, (half, HG_HEAD), 0)
    a_lane = lax.broadcasted_iota(jnp.int32, (half, c), 1)
    gnw = gnw_ref[...]
    nt = (((1,), (1,)), ((), ()))
    tn = (((0,), (0,)), ((), ()))

    def offdiag_operands(q, k, gcum, j, bounded):
        lo = j * sub
        hi = lo + sub if bounded else lo
        g_b = gcum[lo - 1:lo, :] if j > 0 else jnp.zeros((1, HG_HEAD), F32)
        qh = (q[lo:lo + sub, :] * jnp.exp2(gcum[lo:lo + sub, :] - g_b)).astype(BF16)
        kh = (k[0:hi, :] * jnp.exp2(g_b - gcum[0:hi, :])).astype(BF16)
        if hi < c:
            kh = jnp.concatenate([kh, jnp.zeros((c - hi, HG_HEAD), BF16)], axis=0)
        return qh, kh

    def diag_products(q, k, gcum, slot, j):
        lo = j * sub
        gt = (gcum[lo:lo + half, :], gcum[lo + half:lo + sub, :])
        qt = (q[lo:lo + half, :], q[lo + half:lo + sub, :])
        ps = []
        for s in range(sub):
            g_s = gs_ref[slot, pl.ds(lo + s, 1), :]
            k_s = ks_ref[slot, pl.ds(lo + s, 1), :]
            for hf in range(2):
                if s >= half and hf == 0:
                    continue
                dlt = gt[hf] - g_s
                if (s >= half) == (hf == 1):
                    dlt = jnp.where(half_row >= s % half, dlt, -jnp.inf)
                ps.append((qt[hf] * k_s) * jnp.exp2(dlt))
        return jnp.concatenate(ps, axis=0).astype(BF16)

    def diag_scatter(r, j):
        lo = j * sub
        a_top = jnp.zeros((half, c), F32)
        a_bot = jnp.zeros((half, c), F32)
        for s in range(sub):
            if s < half:
                a_top = jnp.where(a_lane == lo + s, r[2 * s * half:(2 * s + 1) * half, :], a_top)
                a_bot = jnp.where(a_lane == lo + s, r[(2 * s + 1) * half:(2 * s + 2) * half, :], a_bot)
            else:
                a_bot = jnp.where(a_lane == lo + s, r[(half + s) * half:(half + s + 1) * half, :], a_bot)
        return jnp.concatenate([a_top, a_bot], axis=0)

    def run(bounded, unroll):
        def group(i, st):
            us = range(unroll)
            rows = [pl.ds(pl.multiple_of((i * unroll + u) * c, c), c) for u in us]
            g3 = [jnp.dot(tri, _split3(lf_ref[r, :]), preferred_element_type=F32) for r in rows]
            q = [q_ref[r, :].astype(F32) for r in rows]
            k = [k_ref[r, :].astype(F32) for r in rows]
            gcum = [(g[:, 0:HG_HEAD] + g[:, HG_HEAD:2 * HG_HEAD] + g[:, 2 * HG_HEAD:]) * LOG2E for g in g3]
            g_last = [g[c - 1:c, :] for g in gcum]
            qg = [(q[u] * jnp.exp2(gcum[u])).astype(BF16) for u in us]
            kd = [(k[u] * jnp.exp2(g_last[u] - gcum[u])).astype(BF16) for u in us]
            first = 0 if bounded else 1
            off = [[offdiag_operands(q[u], k[u], gcum[u], j, bounded) for j in range(first, n_sub)] for u in us]
            if not bounded:
                for u in us:
                    gs_ref[u] = gcum[u]
                    ks_ref[u] = k[u]
                pst = [[diag_products(q[u], k[u], gcum[u], u, j) for j in range(n_sub)] for u in us]
            kv = [lax.dot_general(v_ref[rows[u], :], kd[u], tn, preferred_element_type=F32) for u in us]
            blocks = [[lax.dot_general(qh, kh, nt, preferred_element_type=F32) for qh, kh in off[u]] for u in us]
            if bounded:
                a = [jnp.where(c_i <= r_i, jnp.concatenate(blocks[u], axis=0), 0.0) for u in us]
            else:
                sums = [[jnp.dot(p, ones_w, preferred_element_type=F32) for p in pst[u]] for u in us]
                a = []
                for u in us:
                    a_rows = [diag_scatter(sums[u][j], j) for j in range(n_sub)]
                    for j in range(1, n_sub):
                        a_rows[j] = a_rows[j] + blocks[u][j - 1]
                    a.append(jnp.concatenate(a_rows, axis=0))
            o_intra = [jnp.dot(a[u].astype(BF16), v_ref[rows[u], :], preferred_element_type=F32) for u in us]
            for u in us:
                o = o_intra[u] + lax.dot_general(qg[u], st.astype(BF16), nt, preferred_element_type=F32)
                st = st * jnp.exp2(g_last[u]) + kv[u]
                on = o * lax.rsqrt(jnp.mean(o * o, axis=-1, keepdims=True) + EPS) * gnw
                o_ref[rows[u], :] = (on * g_ref[rows[u], :].astype(F32)).astype(BF16)
            return st

        st_ref[...] = lax.fori_loop(0, n_chunks // unroll, group, st_ref[...])

    w_r = lax.broadcasted_iota(jnp.int32, (n_chunks * n_sub, n_chunks * c), 0)
    w_c = lax.broadcasted_iota(jnp.int32, (n_chunks * n_sub, n_chunks * c), 1)
    win = (lax.shift_right_logical(w_c, sub.bit_length() - 1) == w_r).astype(BF16)
    wsum = jnp.dot(win, lf_ref[...].astype(BF16), preferred_element_type=F32)
    bounded = jnp.min(wsum) * LOG2E >= -HG_SAFE_DECAY

    @pl.when(bounded)
    def _():
        run(True, unroll_bounded)

    @pl.when(jnp.logical_not(bounded))
    def _():
        run(False, unroll_exact)


def _hg_rec_call(q, k, lf, v, g, gnw, *, batch, tc, unroll_bounded, unroll_exact):
    t, d = q.shape
    heads = d // HG_HEAD
    steps = (t // batch) // tc
    unroll = unroll_exact
    blk = pl.BlockSpec((tc, HG_HEAD), lambda b, h, s: (b * steps + s, h))
    return pl.pallas_call(
        functools.partial(_hg_rec_kernel, n_chunks=tc // HG_CHUNK, unroll_bounded=unroll_bounded,
                          unroll_exact=unroll_exact),
        grid=(batch, heads, steps),
        in_specs=[blk, blk, blk, blk, blk, pl.BlockSpec((1, HG_HEAD), lambda b, h, s: (0, 0))],
        out_specs=blk,
        out_shape=jax.ShapeDtypeStruct((t, d), BF16),
        scratch_shapes=[
            pltpu.VMEM((HG_HEAD, HG_HEAD), F32),
            pltpu.VMEM((unroll, HG_CHUNK, HG_HEAD), F32),
            pltpu.VMEM((unroll, HG_CHUNK, HG_HEAD), F32),
        ],
        compiler_params=_cparams(("arbitrary", "arbitrary", "arbitrary")),
        name="hg_rec",
    )(q, k, lf, v, g, gnw)


def _head_rmsnorm(p, w_row):
    low = lax.broadcasted_iota(jnp.int32, (1, LANES), 1) < FOX_HEAD
    outs = []
    for j in range(p.shape[1] // LANES):
        pj = p[:, j * LANES:(j + 1) * LANES]
        ss = pj * pj
        s_lo = jnp.sum(jnp.where(low, ss, 0.0), axis=-1, keepdims=True)
        s_hi = jnp.sum(jnp.where(low, 0.0, ss), axis=-1, keepdims=True)
        inv = lax.rsqrt(jnp.where(low, s_lo, s_hi) * (1.0 / FOX_HEAD) + EPS)
        outs.append(pj * inv)
    return jnp.concatenate(outs, axis=1) * w_row


def _fox_proj_kernel(x_ref, mod_ref, nw_ref, qw_ref, kw_ref, bf_ref, w_ref, wf_ref,
                     q_ref, k_ref, v_ref, g_ref, lf_ref, *, d):
    mod = mod_ref[0]
    scale = LOG2E / np.sqrt(FOX_HEAD)
    for r0 in range(0, x_ref.shape[0], ROW_SLAB):
        rows = slice(r0, r0 + ROW_SLAB)
        h = _modulate(x_ref[rows, :], nw_ref[...], mod[:, 0:d], mod[:, d:2 * d]).astype(BF16)
        pq = jnp.dot(h, w_ref[:, 0:d], preferred_element_type=F32)
        q_ref[rows, :] = (_head_rmsnorm(pq, qw_ref[...]) * scale).astype(BF16)
        pk = jnp.dot(h, w_ref[:, d:2 * d], preferred_element_type=F32)
        k_ref[rows, :] = _head_rmsnorm(pk, kw_ref[...]).astype(BF16)
        v_ref[rows, :] = jnp.dot(h, w_ref[:, 2 * d:3 * d], preferred_element_type=F32).astype(BF16)
        pg = jnp.dot(h, w_ref[:, 3 * d:4 * d], preferred_element_type=F32)
        g_ref[rows, :] = _sigmoid(pg).astype(BF16)
        u = jnp.dot(h, wf_ref[...], preferred_element_type=F32) + bf_ref[...]
        lf_ref[rows, :] = jnp.minimum(u, 0.0) - jnp.log1p(jnp.exp(-jnp.abs(u)))


def _fox_proj_call(x2, mod3, nw, qw, kw, bf, w_main, w_f, *, layer, batch, tm):
    t, d = x2.shape
    nh = w_f.shape[1]
    tiles_per_batch = (t // batch) // tm
    row = pl.BlockSpec((tm, d), lambda i: (i, 0))
    out_bf = jax.ShapeDtypeStruct((t, d), BF16)
    return pl.pallas_call(
        functools.partial(_fox_proj_kernel, d=d),
        grid=(t // tm,),
        in_specs=[
            row,
            pl.BlockSpec((1, 1, N_MOD * d), lambda i: (layer * batch + i // tiles_per_batch, 0, 0)),
            _resident((1, d)), _resident((1, d)), _resident((1, d)), _resident((1, nh)),
            _resident(w_main.shape), _resident(w_f.shape),
        ],
        out_specs=[row, row, row, row, pl.BlockSpec((tm, nh), lambda i: (i, 0))],
        out_shape=[out_bf, out_bf, out_bf, out_bf, jax.ShapeDtypeStruct((t, nh), F32)],
        compiler_params=_cparams(("arbitrary",)),
        name="fox_proj",
    )(x2, mod3, nw, qw, kw, bf, w_main, w_f)


FOX_AUG = 6


def _split3(x):
    hi = x.astype(BF16)
    r1 = x - hi.astype(F32)
    mid = r1.astype(BF16)
    lo = (r1 - mid.astype(F32)).astype(BF16)
    return jnp.concatenate([hi, mid, lo], axis=1)


def _fox_bias_kernel(lf_ref, qw_ref, kw_ref, pq_ref, pk_ref, oq_ref, ok_ref,
                     aq_ref, ak_ref, edge_ref, flag_ref, carry_ref):
    tp = lf_ref.shape[0]

    @pl.when(pl.program_id(1) == 0)
    def _():
        carry_ref[...] = jnp.zeros_like(carry_ref)

    bound = 1.01 * LOG2E * np.sqrt(FOX_HEAD) * jnp.max(jnp.abs(qw_ref[...] * kw_ref[...]), axis=-1, keepdims=True)
    fast = bound <= FOX_STAB_MAX
    flag_ref[...] = jnp.broadcast_to(fast.astype(jnp.int32), flag_ref.shape)
    stab = jnp.where(fast, bound, 0.0)

    r_i = lax.broadcasted_iota(jnp.int32, (tp, tp), 0)
    c_i = lax.broadcasted_iota(jnp.int32, (tp, tp), 1)
    tri = (c_i <= r_i).astype(BF16)
    f3 = jnp.dot(tri, _split3(lf_ref[...]), preferred_element_type=F32)
    f = f3[:, 0:LANES] + f3[:, LANES:2 * LANES] + f3[:, 2 * LANES:] + carry_ref[...]
    carry_ref[...] = f[tp - 1:tp, :]
    f2 = f * LOG2E
    edge_ref[0] = jnp.concatenate([f2[0:1, :], f2[tp - 1:tp, :]], axis=0)
    aq_ref[...] = jnp.dot(_split3(f2 - stab), pq_ref[...], preferred_element_type=F32) + oq_ref[...]
    ak_ref[...] = jnp.dot(_split3(f2), pk_ref[...], preferred_element_type=F32) + ok_ref[...]


def _fox_bias_call(lf, qw, kw, *, batch, d, tp):
    t = lf.shape[0]
    heads = d // FOX_HEAD
    steps = (t // batch) // tp
    pq = np.zeros((3 * LANES, LANES), np.float32)
    pk = np.zeros((3 * LANES, LANES), np.float32)
    oq = np.zeros((1, LANES), np.float32)
    ok = np.zeros((1, LANES), np.float32)
    for h in range(heads):
        base = FOX_AUG * h
        for i in range(3):
            pq[i * LANES + h, base + i] = 1.0
            ok[0, base + i] = 1.0
            oq[0, base + 3 + i] = 1.0
            pk[i * LANES + h, base + 3 + i] = -1.0
    row = pl.BlockSpec((tp, LANES), lambda b, s: (b * steps + s, 0))
    return pl.pallas_call(
        _fox_bias_kernel,
        grid=(batch, steps),
        in_specs=[
            row,
            _resident((1, FOX_HEAD)), _resident((1, FOX_HEAD)),
            _resident(pq.shape), _resident(pk.shape), _resident(oq.shape), _resident(ok.shape),
        ],
        out_specs=[row, row,
                   pl.BlockSpec((1, 2, LANES), lambda b, s: (b * steps + s, 0, 0)),
                   pl.BlockSpec((8, LANES), lambda b, s: (0, 0))],
        out_shape=[jax.ShapeDtypeStruct((t, LANES), F32), jax.ShapeDtypeStruct((t, LANES), F32),
                   jax.ShapeDtypeStruct((batch * steps, 2, LANES), F32),
                   jax.ShapeDtypeStruct((8, LANES), jnp.int32)],
        scratch_shapes=[pltpu.VMEM((1, LANES), F32)],
        compiler_params=_cparams(("arbitrary", "arbitrary")),
        name="fox_bias",
    )(lf, qw, kw, jnp.asarray(pq, BF16), jnp.asarray(pk, BF16), jnp.asarray(oq), jnp.asarray(ok))


FOX_VROWS = 80


def _fox_attn_kernel(flag_ref, fs_ref, fe_ref, q_ref, aq_ref, k_ref, ak_ref, v_ref, g_ref, o_ref,
                     vt_ref, ka_ref, qm_ref, acc_ref, m_ref, pa_ref, pb_ref, *, tq, tk, heads):
    bi, pair, qi = pl.program_id(0), pl.program_id(1), pl.program_id(2)
    nq = pl.num_programs(2)
    nk = v_ref.shape[0] // tk
    sub = tq // tk
    lane2 = lax.broadcasted_iota(jnp.int32, (1, 2 * LANES), 1)
    head0 = (lane2 < FOX_HEAD) | ((lane2 >= LANES) & (lane2 < LANES + FOX_AUG))
    head1 = ((lane2 >= FOX_HEAD) & (lane2 < LANES)) | ((lane2 >= LANES + FOX_AUG) & (lane2 < LANES + 2 * FOX_AUG))
    lane1 = lax.broadcasted_iota(jnp.int32, (1, LANES), 1)
    low = lane1 < FOX_HEAD
    nt = (((1,), (1,)), ((), ()))

    def bias_lanes(a):
        shifted = pltpu.roll(a, (LANES - 2 * FOX_AUG * pair) % LANES, axis=1)
        return jnp.where(lane1 < 2 * FOX_AUG, shifted, 0.0).astype(BF16)

    @pl.when(qi == 0)
    def _():
        pad_rows = lax.broadcasted_iota(jnp.int32, (FOX_VROWS - FOX_HEAD, tk), 0)
        ones_blk = jnp.where(pad_rows == 0, 1.0, 0.0)

        def build(j, carry):
            rows = pl.ds(pl.multiple_of(j * tk, tk), tk)
            vt = v_ref[rows, :].astype(F32).T
            for e in range(2):
                vt_ref[j, e] = jnp.concatenate([vt[e * FOX_HEAD:(e + 1) * FOX_HEAD, :], ones_blk],
                                               axis=0).astype(BF16)
            ka_ref[rows, :] = bias_lanes(ak_ref[rows, :])
            return carry

        lax.fori_loop(0, nk, build, 0)

    def probs(j, masked, online, dst=None):
        rows = pl.ds(pl.multiple_of(j * tk, tk), tk)
        kf = jnp.concatenate([k_ref[rows, :], ka_ref[rows, :]], axis=1)
        ps, alphas = [], []
        for e in range(2):
            s = lax.dot_general(kf, qm_ref[e], nt, preferred_element_type=F32)
            if masked:
                kpos = j * tk + lax.broadcasted_iota(jnp.int32, (tk, tq), 0)
                qpos = qi * tq + lax.broadcasted_iota(jnp.int32, (tk, tq), 1)
                s = jnp.where(kpos <= qpos, s, -jnp.inf)
            if online:
                m_prev = m_ref[e]
                m_new = jnp.maximum(m_prev, jnp.max(s, axis=0, keepdims=True))
                alphas.append(jnp.exp2(m_prev - m_new))
                m_ref[e] = m_new
                s = s - m_new
            p = jnp.exp2(s).astype(BF16)
            if dst is None:
                ps.append(p)
            else:
                dst[e] = p
        return ps, alphas

    def accumulate(ps, j, alphas=None):
        for e in range(2):
            pv = jnp.dot(vt_ref[j, e], ps[e], preferred_element_type=F32)
            if alphas:
                acc_ref[e] = acc_ref[e] * alphas[e] + pv
            else:
                acc_ref[e] += pv

    def start():
        qf = jnp.concatenate([q_ref[...], bias_lanes(aq_ref[...])], axis=1)
        zero = jnp.zeros_like(qf)
        qm_ref[0] = jnp.where(head0, qf, zero)
        qm_ref[1] = jnp.where(head1, qf, zero)
        acc_ref[...] = jnp.zeros_like(acc_ref)

    def finish():
        outs = []
        for e in range(2):
            acc = acc_ref[e]
            outs.append(acc[0:FOX_HEAD, :] / acc[FOX_HEAD:FOX_HEAD + 1, :])
        o_ref[...] = (jnp.concatenate(outs, axis=0).T * g_ref[...].astype(F32)).astype(BF16)

    n_full = qi * sub
    last_diag = n_full + sub - 1

    @pl.when(flag_ref[0] == 1)
    def _():
        start()
        for jj in range(sub):
            if jj < sub - 1:
                accumulate(probs(n_full + jj, True, False)[0], n_full + jj)
            else:
                probs(n_full + jj, True, False, dst=pa_ref)

        h0 = bi * heads + 2 * pair
        fq0 = fs_ref[h0 * nq + qi]
        fq1 = fs_ref[(h0 + 1) * nq + qi]

        def dead(j, cnt):
            gone0 = fq0 - fe_ref[h0 * nk + j] < -FOX_SKIP
            gone1 = fq1 - fe_ref[(h0 + 1) * nk + j] < -FOX_SKIP
            return cnt + jnp.logical_and(gone0, gone1).astype(jnp.int32)

        first = lax.fori_loop(0, n_full, dead, 0)
        n_live = n_full - first

        def two_tiles(i, carry):
            j = first + 2 * i
            probs(j, False, False, dst=pb_ref)
            accumulate(pa_ref, jnp.where(i == 0, last_diag, j - 1))
            probs(j + 1, False, False, dst=pa_ref)
            accumulate(pb_ref, j)
            return carry

        lax.fori_loop(0, n_live // 2, two_tiles, 0)

        @pl.when(n_live % 2 == 1)
        def _():
            ps, _ = probs(n_full - 1, False, False)
            accumulate(pa_ref, jnp.where(n_live == 1, last_diag, n_full - 2))
            accumulate(ps, n_full - 1)

        @pl.when(n_live % 2 == 0)
        def _():
            accumulate(pa_ref, jnp.where(n_live == 0, last_diag, n_full - 1))

        finish()

    @pl.when(flag_ref[0] != 1)
    def _():
        start()
        m_ref[...] = jnp.full_like(m_ref, -jnp.inf)

        def body(j, carry):
            ps, alphas = probs(j, False, True)
            accumulate(ps, j, alphas)
            return carry

        lax.fori_loop(0, n_full, body, 0)
        for jj in range(sub):
            ps, alphas = probs(n_full + jj, True, True)
            accumulate(ps, n_full + jj, alphas)
        finish()


def _fox_attn_call(flag, f_first, f_last, q, aq, k, ak, v, g, *, batch, seq, tq, tk):
    t, d = q.shape
    pairs = d // LANES
    nq = seq // tq
    q_spec = pl.BlockSpec((tq, LANES), lambda b, p, i, *_: (b * nq + i, p))
    kv_spec = pl.BlockSpec((seq, LANES), lambda b, p, i, *_: (b, p))
    grid_spec = pltpu.PrefetchScalarGridSpec(
        num_scalar_prefetch=3,
        grid=(batch, pairs, nq),
        in_specs=[q_spec, pl.BlockSpec((tq, LANES), lambda b, p, i, *_: (b * nq + i, 0)),
                  kv_spec, pl.BlockSpec((seq, LANES), lambda b, p, i, *_: (b, 0)),
                  kv_spec, q_spec],
        out_specs=q_spec,
        scratch_shapes=[
            pltpu.VMEM((seq // tk, 2, FOX_VROWS, tk), BF16),
            pltpu.VMEM((seq, LANES), BF16),
            pltpu.VMEM((2, tq, 2 * LANES), BF16),
            pltpu.VMEM((2, FOX_VROWS, tq), F32),
            pltpu.VMEM((2, 1, tq), F32),
            pltpu.VMEM((2, tk, tq), BF16),
            pltpu.VMEM((2, tk, tq), BF16),
        ],
    )
    return pl.pallas_call(
        functools.partial(_fox_attn_kernel, tq=tq, tk=tk, heads=d // FOX_HEAD),
        grid_spec=grid_spec,
        out_shape=jax.ShapeDtypeStruct((t, d), BF16),
        compiler_params=_cparams(("arbitrary", "arbitrary", "arbitrary")),
        name="fox_attn",
    )(flag, f_first, f_last, q, aq, k, ak, v, g)


def _post_kernel(x_ref, y_ref, mod_ref, nw_ref, fw_ref, wo_ref, w1_ref, w2_ref, o_ref, *, d, ff_blk, final):
    mod = mod_ref[0]
    g1 = mod[:, 2 * d:3 * d]
    sh2, sc2, g2 = mod[:, 3 * d:4 * d], mod[:, 4 * d:5 * d], mod[:, 5 * d:6 * d]
    x1 = x_ref[...] + g1 * jnp.dot(y_ref[...], wo_ref[...], preferred_element_type=F32)
    h = _modulate(x1, nw_ref[...], sh2, sc2).astype(BF16)
    acc = jnp.zeros_like(x1)
    for j in range(w1_ref.shape[1] // ff_blk):
        a = jnp.maximum(jnp.dot(h, w1_ref[:, j * ff_blk:(j + 1) * ff_blk], preferred_element_type=F32), 0.0)
        acc = acc + jnp.dot((a * a).astype(BF16), w2_ref[j * ff_blk:(j + 1) * ff_blk, :],
                            preferred_element_type=F32)
    x2 = x1 + g2 * acc
    if final:
        x2 = x2 * lax.rsqrt(jnp.mean(x2 * x2, axis=-1, keepdims=True) + EPS) * fw_ref[...]
    o_ref[...] = x2


def _post_call(x2, y, mod3, nw, fw, w_out, w1, w2, *, layer, batch, tm, final):
    t, d = x2.shape
    tiles_per_batch = (t // batch) // tm
    row = pl.BlockSpec((tm, d), lambda i: (i, 0))
    return pl.pallas_call(
        functools.partial(_post_kernel, d=d, ff_blk=1024, final=final),
        grid=(t // tm,),
        in_specs=[
            row, row,
            pl.BlockSpec((1, 1, N_MOD * d), lambda i: (layer * batch + i // tiles_per_batch, 0, 0)),
            _resident((1, d)), _resident((1, d)),
            _resident(w_out.shape), _resident(w1.shape), _resident(w2.shape),
        ],
        out_specs=row,
        out_shape=jax.ShapeDtypeStruct((t, d), F32),
        compiler_params=_cparams(("arbitrary",)),
        name="post",
    )(x2, y, mod3, nw, fw, w_out, w1, w2)


def kernel(x, c, w_mod, b_mod, norm1_w, norm2_w, hg_w_in, hg_w_out, hg_lb, hg_gn_w, fox_w_in, fox_b_f,
           fox_qn_w, fox_kn_w, fox_w_out, mlp_w1, mlp_w2, final_w):
    batch, seq, d = x.shape
    depth = w_mod.shape[0]
    t = batch * seq
    fox_heads = d // FOX_HEAD

    mod3 = _mod_call(c, w_mod, b_mod).reshape(depth * batch, 1, N_MOD * d)
    xs = x.reshape(t, d)
    fw = final_w.reshape(1, d)

    for i in range(depth):
        j = i // 2
        n1 = norm1_w[i].reshape(1, d)
        if i % 2 == 0:
            q, k, lf, v, g = _hg_proj_call(xs, mod3, n1, hg_lb, hg_w_in[j].astype(BF16),
                                           layer=i, batch=batch, tm=512)
            y = _hg_rec_call(q, k, lf, v, g, hg_gn_w[j].reshape(1, HG_HEAD), batch=batch, tc=2048,
                             unroll_bounded=8, unroll_exact=4)
            w_out = hg_w_out[j]
        else:
            w_in = fox_w_in[j].astype(BF16)
            pad = ((0, 0), (0, LANES - fox_heads))
            q, k, v, g, lf = _fox_proj_call(
                xs, mod3, n1,
                jnp.tile(fox_qn_w[j], fox_heads).reshape(1, d),
                jnp.tile(fox_kn_w[j], fox_heads).reshape(1, d),
                jnp.pad(fox_b_f[j].reshape(1, fox_heads), pad),
                w_in[:, :4 * d], jnp.pad(w_in[:, 4 * d:], pad),
                layer=i, batch=batch, tm=512)
            ta = 512
            aq, ak, edge, flag = _fox_bias_call(lf, fox_qn_w[j].reshape(1, FOX_HEAD),
                                                fox_kn_w[j].reshape(1, FOX_HEAD), batch=batch, d=d, tp=ta)
            edge = edge.reshape(batch, seq // ta, 2, LANES)[..., :fox_heads].transpose(2, 0, 3, 1)
            y = _fox_attn_call(flag[0, :1], edge[0].reshape(-1), edge[1].reshape(-1), q, aq, k, ak, v, g,
                               batch=batch, seq=seq, tq=ta, tk=ta)
            w_out = fox_w_out[j]
        xs = _post_call(xs, y, mod3, norm2_w[i].reshape(1, d), fw, w_out.astype(BF16),
                        mlp_w1[i].astype(BF16), mlp_w2[i].astype(BF16),
                        layer=i, batch=batch, tm=512, final=(i == depth - 1))
    return xs.reshape(batch, seq, d)
```

```python
import functools

import numpy as np
import jax
import jax.numpy as jnp
from jax import lax
from jax.experimental import pallas as pl
from jax.experimental.pallas import tpu as pltpu

F32 = jnp.float32
BF16 = jnp.bfloat16
EPS = 1e-6
N_MOD = 6
HG_HEAD = 128
HG_CHUNK = 64
HG_SUB = 16
HG_SAFE_DECAY = 56.0
FOX_HEAD = 64
LANES = 128
LOG2E = float(np.log2(np.e))
FOX_STAB_MAX = 30.0
FOX_SKIP = 160.0
ROW_SLAB = 256
VMEM_LIMIT = 56 * 1024 * 1024

_HI = lax.Precision.HIGHEST


def _cparams(sem):
    return pltpu.CompilerParams(dimension_semantics=sem, vmem_limit_bytes=VMEM_LIMIT)


def _resident(shape):
    nd = len(shape)
    return pl.BlockSpec(shape, lambda *_: (0,) * nd, pipeline_mode=pl.Buffered(1))


def _sigmoid(x):
    return 0.5 * jnp.tanh(0.5 * x) + 0.5


def _modulate(x, nw, shift, scale):
    ms = jnp.mean(x * x, axis=-1, keepdims=True)
    y = x * lax.rsqrt(ms + EPS)
    return (y * nw) * (1.0 + scale) + shift


def _mod_kernel(c_ref, w_ref, b_ref, o_ref):
    c = c_ref[...]
    ca = c * _sigmoid(c)
    o_ref[0] = jnp.dot(ca, w_ref[0], precision=_HI, preferred_element_type=F32) + b_ref[0]


def _mod_call(c, w_mod, b_mod):
    depth, d, n = w_mod.shape
    b = c.shape[0]
    tn = 1024
    return pl.pallas_call(
        _mod_kernel,
        grid=(depth, n // tn),
        in_specs=[
            pl.BlockSpec((b, d), lambda l, j: (0, 0)),
            pl.BlockSpec((1, d, tn), lambda l, j: (l, 0, j)),
            pl.BlockSpec((1, 1, tn), lambda l, j: (l, 0, j)),
        ],
        out_specs=pl.BlockSpec((1, b, tn), lambda l, j: (l, 0, j)),
        out_shape=jax.ShapeDtypeStruct((depth, b, n), F32),
        compiler_params=_cparams(("arbitrary", "arbitrary")),
        name="mod",
    )(c, w_mod, b_mod.reshape(depth, 1, n))


def _hg_proj_kernel(x_ref, mod_ref, nw_ref, lb_ref, w_ref, q_ref, k_ref, lf_ref, v_ref, g_ref, *, layer, d):
    mod = mod_ref[0]
    lbp = lb_ref[...]
    e = jnp.exp(lbp - jnp.max(lbp, axis=0, keepdims=True))
    lb = jnp.sum(e[0:layer + 1], axis=0, keepdims=True) / jnp.sum(e, axis=0, keepdims=True)

    for r0 in range(0, x_ref.shape[0], ROW_SLAB):
        rows = slice(r0, r0 + ROW_SLAB)
        h = _modulate(x_ref[rows, :], nw_ref[...], mod[:, 0:d], mod[:, d:2 * d]).astype(BF16)

        pq = jnp.dot(h, w_ref[:, 0:d], preferred_element_type=F32)
        q_ref[rows, :] = (pq * _sigmoid(pq)).astype(BF16)

        z = jnp.dot(h, w_ref[:, d:2 * d], preferred_element_type=F32)
        th = 0.5 * jnp.tanh(0.5 * z)
        kk = (1.0 - lb) * (0.5 - th)
        f = lb + (1.0 - lb) * (0.5 + th)
        lf_ref[rows, :] = jnp.log(jnp.where(kk < 0.5, 1.0 - kk, f))
        k_ref[rows, :] = kk.astype(BF16)

        v_ref[rows, :] = jnp.dot(h, w_ref[:, 2 * d:3 * d], preferred_element_type=F32).astype(BF16)
        pg = jnp.dot(h, w_ref[:, 3 * d:4 * d], preferred_element_type=F32)
        g_ref[rows, :] = (pg * _sigmoid(pg)).astype(BF16)


def _hg_proj_call(x2, mod3, nw, hg_lb, w_in, *, layer, batch, tm):
    t, d = x2.shape
    tiles_per_batch = (t // batch) // tm
    row = pl.BlockSpec((tm, d), lambda i: (i, 0))
    out_bf = jax.ShapeDtypeStruct((t, d), BF16)
    return pl.pallas_call(
        functools.partial(_hg_proj_kernel, layer=layer, d=d),
        grid=(t // tm,),
        in_specs=[
            row,
            pl.BlockSpec((1, 1, N_MOD * d), lambda i: (layer * batch + i // tiles_per_batch, 0, 0)),
            _resident((1, d)),
            _resident(hg_lb.shape),
            _resident(w_in.shape),
        ],
        out_specs=[row, row, row, row, row],
        out_shape=[out_bf, out_bf, jax.ShapeDtypeStruct((t, d), F32), out_bf, out_bf],
        compiler_params=_cparams(("arbitrary",)),
        name="hg_proj",
    )(x2, mod3, nw, hg_lb, w_in)


def _hg_rec_kernel(q_ref, k_ref, lf_ref, v_ref, g_ref, gnw_ref, o_ref, st_ref, gs_ref, ks_ref, *,
                   n_chunks, unroll_bounded, unroll_exact):
    c, sub = HG_CHUNK, HG_SUB
    n_sub = c // sub
    half = sub // 2

    @pl.when(pl.program_id(2) == 0)
    def _():
        st_ref[...] = jnp.zeros_like(st_ref)

    r_i = lax.broadcasted_iota(jnp.int32, (c, c), 0)
    c_i = lax.broadcasted_iota(jnp.int32, (c, c), 1)
    tri = (c_i <= r_i).astype(BF16)
    ones_w = jnp.ones((HG_HEAD, c), BF16)
    half_row = lax.broadcasted_iota(jnp.int32, (half, HG_HEAD), 0)
    a_lane = lax.broadcasted_iota(jnp.int32, (half, c), 1)
    gnw = gnw_ref[...]
    nt = (((1,), (1,)), ((), ()))
    tn = (((0,), (0,)), ((), ()))

    def offdiag_operands(q, k, gcum, j, bounded):
        lo = j * sub
        hi = lo + sub if bounded else lo
        g_b = gcum[lo - 1:lo, :] if j > 0 else jnp.zeros((1, HG_HEAD), F32)
        qh = (q[lo:lo + sub, :] * jnp.exp2(gcum[lo:lo + sub, :] - g_b)).astype(BF16)
        kh = (k[0:hi, :] * jnp.exp2(g_b - gcum[0:hi, :])).astype(BF16)
        if hi < c:
            kh = jnp.concatenate([kh, jnp.zeros((c - hi, HG_HEAD), BF16)], axis=0)
        return qh, kh

    def diag_products(q, k, gcum, slot, j):
        lo = j * sub
        gt = (gcum[lo:lo + half, :], gcum[lo + half:lo + sub, :])
        qt = (q[lo:lo + half, :], q[lo + half:lo + sub, :])
        ps = []
        for s in range(sub):
            g_s = gs_ref[slot, pl.ds(lo + s, 1), :]
            k_s = ks_ref[slot, pl.ds(lo + s, 1), :]
            for hf in range(2):
                if s >= half and hf == 0:
                    continue
                dlt = gt[hf] - g_s
                if (s >= half) == (hf == 1):
                    dlt = jnp.where(half_row >= s % half, dlt, -jnp.inf)
                ps.append((qt[hf] * k_s) * jnp.exp2(dlt))
        return jnp.concatenate(ps, axis=0).astype(BF16)

    def diag_scatter(r, j):
        lo = j * sub
        a_top = jnp.zeros((half, c), F32)
        a_bot = jnp.zeros((half, c), F32)
        for s in range(sub):
            if s < half:
                a_top = jnp.where(a_lane == lo + s, r[2 * s * half:(2 * s + 1) * half, :], a_top)
                a_bot = jnp.where(a_lane == lo + s, r[(2 * s + 1) * half:(2 * s + 2) * half, :], a_bot)
            else:
                a_bot = jnp.where(a_lane == lo + s, r[(half + s) * half:(half + s + 1) * half, :], a_bot)
        return jnp.concatenate([a_top, a_bot], axis=0)

    def run(bounded, unroll):
        def group(i, st):
            us = range(unroll)
            rows = [pl.ds(pl.multiple_of((i * unroll + u) * c, c), c) for u in us]
            g3 = [jnp.dot(tri, _split3(lf_ref[r, :]), preferred_element_type=F32) for r in rows]
            q = [q_ref[r, :].astype(F32) for r in rows]
            k = [k_ref[r, :].astype(F32) for r in rows]
            gcum = [(g[:, 0:HG_HEAD] + g[:, HG_HEAD:2 * HG_HEAD] + g[:, 2 * HG_HEAD:]) * LOG2E for g in g3]
            g_last = [g[c - 1:c, :] for g in gcum]
            qg = [(q[u] * jnp.exp2(gcum[u])).astype(BF16) for u in us]
            kd = [(k[u] * jnp.exp2(g_last[u] - gcum[u])).astype(BF16) for u in us]
            first = 0 if bounded else 1
            off = [[offdiag_operands(q[u], k[u], gcum[u], j, bounded) for j in range(first, n_sub)] for u in us]
            if not bounded:
                for u in us:
                    gs_ref[u] = gcum[u]
                    ks_ref[u] = k[u]
                pst = [[diag_products(q[u], k[u], gcum[u], u, j) for j in range(n_sub)] for u in us]
            kv = [lax.dot_general(v_ref[rows[u], :], kd[u], tn, preferred_element_type=F32) for u in us]
            blocks = [[lax.dot_general(qh, kh, nt, preferred_element_type=F32) for qh, kh in off[u]] for u in us]
            if bounded:
                a = [jnp.where(c_i <= r_i, jnp.concatenate(blocks[u], axis=0), 0.0) for u in us]
            else:
                sums = [[jnp.dot(p, ones_w, preferred_element_type=F32) for p in pst[u]] for u in us]
                a = []
                for u in us:
                    a_rows = [diag_scatter(sums[u][j], j) for j in range(n_sub)]
                    for j in range(1, n_sub):
                        a_rows[j] = a_rows[j] + blocks[u][j - 1]
                    a.append(jnp.concatenate(a_rows, axis=0))
            o_intra = [jnp.dot(a[u].astype(BF16), v_ref[rows[u], :], preferred_element_type=F32) for u in us]
            for u in us:
                o = o_intra[u] + lax.dot_general(qg[u], st.astype(BF16), nt, preferred_element_type=F32)
                st = st * jnp.exp2(g_last[u]) + kv[u]
                on = o * lax.rsqrt(jnp.mean(o * o, axis=-1, keepdims=True) + EPS) * gnw
                o_ref[rows[u], :] = (on * g_ref[rows[u], :].astype(F32)).astype(BF16)
            return st

        st_ref[...] = lax.fori_loop(0, n_chunks // unroll, group, st_ref[...])

    w_r = lax.broadcasted_iota(jnp.int32, (n_chunks * n_sub, n_chunks * c), 0)
    w_c = lax.broadcasted_iota(jnp.int32, (n_chunks * n_sub, n_chunks * c), 1)
    win = (lax.shift_right_logical(w_c, sub.bit_length() - 1) == w_r).astype(BF16)
    wsum = jnp.dot(win, lf_ref[...].astype(BF16), preferred_element_type=F32)
    bounded = jnp.min(wsum) * LOG2E >= -HG_SAFE_DECAY

    @pl.when(bounded)
    def _():
        run(True, unroll_bounded)

    @pl.when(jnp.logical_not(bounded))
    def _():
        run(False, unroll_exact)


def _hg_rec_call(q, k, lf, v, g, gnw, *, batch, tc, unroll_bounded, unroll_exact):
    t, d = q.shape
    heads = d // HG_HEAD
    steps = (t // batch) // tc
    unroll = unroll_exact
    blk = pl.BlockSpec((tc, HG_HEAD), lambda b, h, s: (b * steps + s, h))
    return pl.pallas_call(
        functools.partial(_hg_rec_kernel, n_chunks=tc // HG_CHUNK, unroll_bounded=unroll_bounded,
                          unroll_exact=unroll_exact),
        grid=(batch, heads, steps),
        in_specs=[blk, blk, blk, blk, blk, pl.BlockSpec((1, HG_HEAD), lambda b, h, s: (0, 0))],
        out_specs=blk,
        out_shape=jax.ShapeDtypeStruct((t, d), BF16),
        scratch_shapes=[
            pltpu.VMEM((HG_HEAD, HG_HEAD), F32),
            pltpu.VMEM((unroll, HG_CHUNK, HG_HEAD), F32),
            pltpu.VMEM((unroll, HG_CHUNK, HG_HEAD), F32),
        ],
        compiler_params=_cparams(("arbitrary", "arbitrary", "arbitrary")),
        name="hg_rec",
    )(q, k, lf, v, g, gnw)


def _head_rmsnorm(p, w_row):
    low = lax.broadcasted_iota(jnp.int32, (1, LANES), 1) < FOX_HEAD
    outs = []
    for j in range(p.shape[1] // LANES):
        pj = p[:, j * LANES:(j + 1) * LANES]
        ss = pj * pj
        s_lo = jnp.sum(jnp.where(low, ss, 0.0), axis=-1, keepdims=True)
        s_hi = jnp.sum(jnp.where(low, 0.0, ss), axis=-1, keepdims=True)
        inv = lax.rsqrt(jnp.where(low, s_lo, s_hi) * (1.0 / FOX_HEAD) + EPS)
        outs.append(pj * inv)
    return jnp.concatenate(outs, axis=1) * w_row


def _fox_proj_kernel(x_ref, mod_ref, nw_ref, qw_ref, kw_ref, bf_ref, w_ref, wf_ref,
                     q_ref, k_ref, v_ref, g_ref, lf_ref, *, d):
    mod = mod_ref[0]
    scale = LOG2E / np.sqrt(FOX_HEAD)
    for r0 in range(0, x_ref.shape[0], ROW_SLAB):
        rows = slice(r0, r0 + ROW_SLAB)
        h = _modulate(x_ref[rows, :], nw_ref[...], mod[:, 0:d], mod[:, d:2 * d]).astype(BF16)
        pq = jnp.dot(h, w_ref[:, 0:d], preferred_element_type=F32)
        q_ref[rows, :] = (_head_rmsnorm(pq, qw_ref[...]) * scale).astype(BF16)
        pk = jnp.dot(h, w_ref[:, d:2 * d], preferred_element_type=F32)
        k_ref[rows, :] = _head_rmsnorm(pk, kw_ref[...]).astype(BF16)
        v_ref[rows, :] = jnp.dot(h, w_ref[:, 2 * d:3 * d], preferred_element_type=F32).astype(BF16)
        pg = jnp.dot(h, w_ref[:, 3 * d:4 * d], preferred_element_type=F32)
        g_ref[rows, :] = _sigmoid(pg).astype(BF16)
        u = jnp.dot(h, wf_ref[...], preferred_element_type=F32) + bf_ref[...]
        lf_ref[rows, :] = jnp.minimum(u, 0.0) - jnp.log1p(jnp.exp(-jnp.abs(u)))


def _fox_proj_call(x2, mod3, nw, qw, kw, bf, w_main, w_f, *, layer, batch, tm):
    t, d = x2.shape
    nh = w_f.shape[1]
    tiles_per_batch = (t // batch) // tm
    row = pl.BlockSpec((tm, d), lambda i: (i, 0))
    out_bf = jax.ShapeDtypeStruct((t, d), BF16)
    return pl.pallas_call(
        functools.partial(_fox_proj_kernel, d=d),
        grid=(t // tm,),
        in_specs=[
            row,
            pl.BlockSpec((1, 1, N_MOD * d), lambda i: (layer * batch + i // tiles_per_batch, 0, 0)),
            _resident((1, d)), _resident((1, d)), _resident((1, d)), _resident((1, nh)),
            _resident(w_main.shape), _resident(w_f.shape),
        ],
        out_specs=[row, row, row, row, pl.BlockSpec((tm, nh), lambda i: (i, 0))],
        out_shape=[out_bf, out_bf, out_bf, out_bf, jax.ShapeDtypeStruct((t, nh), F32)],
        compiler_params=_cparams(("arbitrary",)),
        name="fox_proj",
    )(x2, mod3, nw, qw, kw, bf, w_main, w_f)


FOX_AUG = 6


def _split3(x):
    hi = x.astype(BF16)
    r1 = x - hi.astype(F32)
    mid = r1.astype(BF16)
    lo = (r1 - mid.astype(F32)).astype(BF16)
    return jnp.concatenate([hi, mid, lo], axis=1)


def _fox_bias_kernel(lf_ref, qw_ref, kw_ref, pq_ref, pk_ref, oq_ref, ok_ref,
                     aq_ref, ak_ref, edge_ref, flag_ref, carry_ref):
    tp = lf_ref.shape[0]

    @pl.when(pl.program_id(1) == 0)
    def _():
        carry_ref[...] = jnp.zeros_like(carry_ref)

    bound = 1.01 * LOG2E * np.sqrt(FOX_HEAD) * jnp.max(jnp.abs(qw_ref[...] * kw_ref[...]), axis=-1, keepdims=True)
    fast = bound <= FOX_STAB_MAX
    flag_ref[...] = jnp.broadcast_to(fast.astype(jnp.int32), flag_ref.shape)
    stab = jnp.where(fast, bound, 0.0)

    r_i = lax.broadcasted_iota(jnp.int32, (tp, tp), 0)
    c_i = lax.broadcasted_iota(jnp.int32, (tp, tp), 1)
    tri = (c_i <= r_i).astype(BF16)
    f3 = jnp.dot(tri, _split3(lf_ref[...]), preferred_element_type=F32)
    f = f3[:, 0:LANES] + f3[:, LANES:2 * LANES] + f3[:, 2 * LANES:] + carry_ref[...]
    carry_ref[...] = f[tp - 1:tp, :]
    f2 = f * LOG2E
    edge_ref[0] = jnp.concatenate([f2[0:1, :], f2[tp - 1:tp, :]], axis=0)
    aq_ref[...] = jnp.dot(_split3(f2 - stab), pq_ref[...], preferred_element_type=F32) + oq_ref[...]
    ak_ref[...] = jnp.dot(_split3(f2), pk_ref[...], preferred_element_type=F32) + ok_ref[...]


def _fox_bias_call(lf, qw, kw, *, batch, d, tp):
    t = lf.shape[0]
    heads = d // FOX_HEAD
    steps = (t // batch) // tp
    pq = np.zeros((3 * LANES, LANES), np.float32)
    pk = np.zeros((3 * LANES, LANES), np.float32)
    oq = np.zeros((1, LANES), np.float32)
    ok = np.zeros((1, LANES), np.float32)
    for h in range(heads):
        base = FOX_AUG * h
        for i in range(3):
            pq[i * LANES + h, base + i] = 1.0
            ok[0, base + i] = 1.0
            oq[0, base + 3 + i] = 1.0
            pk[i * LANES + h, base + 3 + i] = -1.0
    row = pl.BlockSpec((tp, LANES), lambda b, s: (b * steps + s, 0))
    return pl.pallas_call(
        _fox_bias_kernel,
        grid=(batch, steps),
        in_specs=[
            row,
            _resident((1, FOX_HEAD)), _resident((1, FOX_HEAD)),
            _resident(pq.shape), _resident(pk.shape), _resident(oq.shape), _resident(ok.shape),
        ],
        out_specs=[row, row,
                   pl.BlockSpec((1, 2, LANES), lambda b, s: (b * steps + s, 0, 0)),
                   pl.BlockSpec((8, LANES), lambda b, s: (0, 0))],
        out_shape=[jax.ShapeDtypeStruct((t, LANES), F32), jax.ShapeDtypeStruct((t, LANES), F32),
                   jax.ShapeDtypeStruct((batch * steps, 2, LANES), F32),
                   jax.ShapeDtypeStruct((8, LANES), jnp.int32)],
        scratch_shapes=[pltpu.VMEM((1, LANES), F32)],
        compiler_params=_cparams(("arbitrary", "arbitrary")),
        name="fox_bias",
    )(lf, qw, kw, jnp.asarray(pq, BF16), jnp.asarray(pk, BF16), jnp.asarray(oq), jnp.asarray(ok))


FOX_VROWS = 80


def _fox_attn_kernel(flag_ref, fs_ref, fe_ref, q_ref, aq_ref, k_ref, ak_ref, v_ref, g_ref, o_ref,
                     vt_ref, ka_ref, qm_ref, acc_ref, m_ref, pa_ref, pb_ref, *, tq, tk, heads):
    bi, pair, qi = pl.program_id(0), pl.program_id(1), pl.program_id(2)
    nk = v_ref.shape[0] // tk
    sub = tq // tk
    lane2 = lax.broadcasted_iota(jnp.int32, (1, 2 * LANES), 1)
    head0 = (lane2 < FOX_HEAD) | ((lane2 >= LANES) & (lane2 < LANES + FOX_AUG))
    head1 = ((lane2 >= FOX_HEAD) & (lane2 < LANES)) | ((lane2 >= LANES + FOX_AUG) & (lane2 < LANES + 2 * FOX_AUG))
    lane1 = lax.broadcasted_iota(jnp.int32, (1, LANES), 1)
    low = lane1 < FOX_HEAD
    nt = (((1,), (1,)), ((), ()))

    def bias_lanes(a):
        shifted = pltpu.roll(a, (LANES - 2 * FOX_AUG * pair) % LANES, axis=1)
        return jnp.where(lane1 < 2 * FOX_AUG, shifted, 0.0).astype(BF16)

    @pl.when(qi == 0)
    def _():
        pad_rows = lax.broadcasted_iota(jnp.int32, (FOX_VROWS - FOX_HEAD, tk), 0)
        ones_blk = jnp.where(pad_rows == 0, 1.0, 0.0)

        def build(j, carry):
            rows = pl.ds(pl.multiple_of(j * tk, tk), tk)
            vt = v_ref[rows, :].astype(F32).T
            for e in range(2):
                vt_ref[j, e] = jnp.concatenate([vt[e * FOX_HEAD:(e + 1) * FOX_HEAD, :], ones_blk],
                                               axis=0).astype(BF16)
            ka_ref[rows, :] = bias_lanes(ak_ref[rows, :])
            return carry

        lax.fori_loop(0, nk, build, 0)

    def probs(j, online, diag=None, dst=None):
        rows = pl.ds(pl.multiple_of(j * tk, tk), tk)
        kf = jnp.concatenate([k_ref[rows, :], ka_ref[rows, :]], axis=1)
        q_lo = 0 if diag is None else diag * tk
        ps, alphas = [], []
        for e in range(2):
            s = lax.dot_general(kf, qm_ref[e, q_lo:tq, :], nt, preferred_element_type=F32)
            if diag is not None:
                kpos = lax.broadcasted_iota(jnp.int32, (tk, tk), 0)
                qpos = lax.broadcasted_iota(jnp.int32, (tk, tk), 1)
                blk = jnp.where(kpos <= qpos, s[:, 0:tk], -jnp.inf)
                s = blk if q_lo + tk == tq else jnp.concatenate([blk, s[:, tk:]], axis=1)
            if online:
                m_prev = m_ref[e, :, q_lo:tq]
                m_new = jnp.maximum(m_prev, jnp.max(s, axis=0, keepdims=True))
                alphas.append(jnp.exp2(m_prev - m_new))
                m_ref[e, :, q_lo:tq] = m_new
                s = s - m_new
            p = jnp.exp2(s).astype(BF16)
            if dst is None:
                ps.append(p)
            else:
                dst[e] = p
        return ps, alphas

    def accumulate(ps, j, alphas=None, diag=None):
        q_lo = 0 if diag is None else diag * tk
        for e in range(2):
            pv = jnp.dot(vt_ref[j, e], ps[e], preferred_element_type=F32)
            if alphas:
                acc_ref[e, :, q_lo:tq] = acc_ref[e, :, q_lo:tq] * alphas[e] + pv
            else:
                acc_ref[e, :, q_lo:tq] += pv

    def start():
        qf = jnp.concatenate([q_ref[...], bias_lanes(aq_ref[...])], axis=1)
        zero = jnp.zeros_like(qf)
        qm_ref[0] = jnp.where(head0, qf, zero)
        qm_ref[1] = jnp.where(head1, qf, zero)
        acc_ref[...] = jnp.zeros_like(acc_ref)

    def finish():
        outs = []
        for e in range(2):
            acc = acc_ref[e]
            outs.append(acc[0:FOX_HEAD, :] / acc[FOX_HEAD:FOX_HEAD + 1, :])
        o_ref[...] = (jnp.concatenate(outs, axis=0).T * g_ref[...].astype(F32)).astype(BF16)

    n_full = qi * sub

    @pl.when(flag_ref[0] == 1)
    def _():
        start()
        probs(n_full, False, diag=0, dst=pa_ref)
        for a in range(1, sub):
            accumulate(probs(n_full + a, False, diag=a)[0], n_full + a, diag=a)

        h0 = bi * heads + 2 * pair
        fq0 = fs_ref[h0 * nk + n_full]
        fq1 = fs_ref[(h0 + 1) * nk + n_full]

        def dead(j, cnt):
            gone0 = fq0 - fe_ref[h0 * nk + j] < -FOX_SKIP
            gone1 = fq1 - fe_ref[(h0 + 1) * nk + j] < -FOX_SKIP
            return cnt + jnp.logical_and(gone0, gone1).astype(jnp.int32)

        first = lax.fori_loop(0, n_full, dead, 0)
        n_live = n_full - first

        def two_tiles(i, carry):
            j = first + 2 * i
            probs(j, False, dst=pb_ref)
            accumulate(pa_ref, jnp.where(i == 0, n_full, j - 1))
            probs(j + 1, False, dst=pa_ref)
            accumulate(pb_ref, j)
            return carry

        lax.fori_loop(0, n_live // 2, two_tiles, 0)

        @pl.when(n_live % 2 == 1)
        def _():
            ps, _ = probs(n_full - 1, False)
            accumulate(pa_ref, jnp.where(n_live == 1, n_full, n_full - 2))
            accumulate(ps, n_full - 1)

        @pl.when(n_live % 2 == 0)
        def _():
            accumulate(pa_ref, jnp.where(n_live == 0, n_full, n_full - 1))

        finish()

    @pl.when(flag_ref[0] != 1)
    def _():
        start()
        m_ref[...] = jnp.full_like(m_ref, -jnp.inf)

        def body(j, carry):
            ps, alphas = probs(j, True)
            accumulate(ps, j, alphas)
            return carry

        lax.fori_loop(0, n_full, body, 0)
        for a in range(sub):
            ps, alphas = probs(n_full + a, True, diag=a)
            accumulate(ps, n_full + a, alphas, diag=a)
        finish()


def _fox_attn_call(flag, f_first, f_last, q, aq, k, ak, v, g, *, batch, seq, tq, tk):
    t, d = q.shape
    pairs = d // LANES
    nq = seq // tq
    q_spec = pl.BlockSpec((tq, LANES), lambda b, p, i, *_: (b * nq + i, p))
    kv_spec = pl.BlockSpec((seq, LANES), lambda b, p, i, *_: (b, p))
    grid_spec = pltpu.PrefetchScalarGridSpec(
        num_scalar_prefetch=3,
        grid=(batch, pairs, nq),
        in_specs=[q_spec, pl.BlockSpec((tq, LANES), lambda b, p, i, *_: (b * nq + i, 0)),
                  kv_spec, pl.BlockSpec((seq, LANES), lambda b, p, i, *_: (b, 0)),
                  kv_spec, q_spec],
        out_specs=q_spec,
        scratch_shapes=[
            pltpu.VMEM((seq // tk, 2, FOX_VROWS, tk), BF16),
            pltpu.VMEM((seq, LANES), BF16),
            pltpu.VMEM((2, tq, 2 * LANES), BF16),
            pltpu.VMEM((2, FOX_VROWS, tq), F32),
            pltpu.VMEM((2, 1, tq), F32),
            pltpu.VMEM((2, tk, tq), BF16),
            pltpu.VMEM((2, tk, tq), BF16),
        ],
    )
    return pl.pallas_call(
        functools.partial(_fox_attn_kernel, tq=tq, tk=tk, heads=d // FOX_HEAD),
        grid_spec=grid_spec,
        out_shape=jax.ShapeDtypeStruct((t, d), BF16),
        compiler_params=_cparams(("arbitrary", "arbitrary", "arbitrary")),
        name="fox_attn",
    )(flag, f_first, f_last, q, aq, k, ak, v, g)


def _post_kernel(x_ref, y_ref, mod_ref, nw_ref, fw_ref, wo_ref, w1_ref, w2_ref, o_ref, *, d, ff_blk, final):
    mod = mod_ref[0]
    g1 = mod[:, 2 * d:3 * d]
    sh2, sc2, g2 = mod[:, 3 * d:4 * d], mod[:, 4 * d:5 * d], mod[:, 5 * d:6 * d]
    x1 = x_ref[...] + g1 * jnp.dot(y_ref[...], wo_ref[...], preferred_element_type=F32)
    h = _modulate(x1, nw_ref[...], sh2, sc2).astype(BF16)
    acc = jnp.zeros_like(x1)
    for j in range(w1_ref.shape[1] // ff_blk):
        a = jnp.maximum(jnp.dot(h, w1_ref[:, j * ff_blk:(j + 1) * ff_blk], preferred_element_type=F32), 0.0)
        acc = acc + jnp.dot((a * a).astype(BF16), w2_ref[j * ff_blk:(j + 1) * ff_blk, :],
                            preferred_element_type=F32)
    x2 = x1 + g2 * acc
    if final:
        x2 = x2 * lax.rsqrt(jnp.mean(x2 * x2, axis=-1, keepdims=True) + EPS) * fw_ref[...]
    o_ref[...] = x2


def _post_call(x2, y, mod3, nw, fw, w_out, w1, w2, *, layer, batch, tm, final):
    t, d = x2.shape
    tiles_per_batch = (t // batch) // tm
    row = pl.BlockSpec((tm, d), lambda i: (i, 0))
    return pl.pallas_call(
        functools.partial(_post_kernel, d=d, ff_blk=1024, final=final),
        grid=(t // tm,),
        in_specs=[
            row, row,
            pl.BlockSpec((1, 1, N_MOD * d), lambda i: (layer * batch + i // tiles_per_batch, 0, 0)),
            _resident((1, d)), _resident((1, d)),
            _resident(w_out.shape), _resident(w1.shape), _resident(w2.shape),
        ],
        out_specs=row,
        out_shape=jax.ShapeDtypeStruct((t, d), F32),
        compiler_params=_cparams(("arbitrary",)),
        name="post",
    )(x2, y, mod3, nw, fw, w_out, w1, w2)


def kernel(x, c, w_mod, b_mod, norm1_w, norm2_w, hg_w_in, hg_w_out, hg_lb, hg_gn_w, fox_w_in, fox_b_f,
           fox_qn_w, fox_kn_w, fox_w_out, mlp_w1, mlp_w2, final_w):
    batch, seq, d = x.shape
    depth = w_mod.shape[0]
    t = batch * seq
    fox_heads = d // FOX_HEAD

    mod3 = _mod_call(c, w_mod, b_mod).reshape(depth * batch, 1, N_MOD * d)
    xs = x.reshape(t, d)
    fw = final_w.reshape(1, d)

    for i in range(depth):
        j = i // 2
        n1 = norm1_w[i].reshape(1, d)
        if i % 2 == 0:
            q, k, lf, v, g = _hg_proj_call(xs, mod3, n1, hg_lb, hg_w_in[j].astype(BF16),
                                           layer=i, batch=batch, tm=512)
            y = _hg_rec_call(q, k, lf, v, g, hg_gn_w[j].reshape(1, HG_HEAD), batch=batch, tc=2048,
                             unroll_bounded=16, unroll_exact=4)
            w_out = hg_w_out[j]
        else:
            w_in = fox_w_in[j].astype(BF16)
            pad = ((0, 0), (0, LANES - fox_heads))
            q, k, v, g, lf = _fox_proj_call(
                xs, mod3, n1,
                jnp.tile(fox_qn_w[j], fox_heads).reshape(1, d),
                jnp.tile(fox_kn_w[j], fox_heads).reshape(1, d),
                jnp.pad(fox_b_f[j].reshape(1, fox_heads), pad),
                w_in[:, :4 * d], jnp.pad(w_in[:, 4 * d:], pad),
                layer=i, batch=batch, tm=512)
            ta = 512
            aq, ak, edge, flag = _fox_bias_call(lf, fox_qn_w[j].reshape(1, FOX_HEAD),
                                                fox_kn_w[j].reshape(1, FOX_HEAD), batch=batch, d=d, tp=ta)
            edge = edge.reshape(batch, seq // ta, 2, LANES)[..., :fox_heads].transpose(2, 0, 3, 1)
            y = _fox_attn_call(flag[0, :1], edge[0].reshape(-1), edge[1].reshape(-1), q, aq, k, ak, v, g,
                               batch=batch, seq=seq, tq=2 * ta, tk=ta)
            w_out = fox_w_out[j]
        xs = _post_call(xs, y, mod3, norm2_w[i].reshape(1, d), fw, w_out.astype(BF16),
                        mlp_w1[i].astype(BF16), mlp_w2[i].astype(BF16),
                        layer=i, batch=batch, tm=512, final=(i == depth - 1))
    return xs.reshape(batch, seq, d)
```

```python
import functools

import numpy as np
import jax
import jax.numpy as jnp
from jax import lax
from jax.experimental import pallas as pl
from jax.experimental.pallas import tpu as pltpu

F32 = jnp.float32
BF16 = jnp.bfloat16
EPS = 1e-6
N_MOD = 6
HG_HEAD = 128
HG_CHUNK = 64
HG_SUB = 16
HG_SAFE_DECAY = 56.0
FOX_HEAD = 64
LANES = 128
LOG2E = float(np.log2(np.e))
FOX_STAB_MAX = 30.0
FOX_SKIP = 160.0
ROW_SLAB = 256
VMEM_LIMIT = 56 * 1024 * 1024

_HI = lax.Precision.HIGHEST


def _cparams(sem):
    return pltpu.CompilerParams(dimension_semantics=sem, vmem_limit_bytes=VMEM_LIMIT)


def _resident(shape):
    nd = len(shape)
    return pl.BlockSpec(shape, lambda *_: (0,) * nd, pipeline_mode=pl.Buffered(1))


def _sigmoid(x):
    return 0.5 * jnp.tanh(0.5 * x) + 0.5


def _modulate(x, nw, shift, scale):
    ms = jnp.mean(x * x, axis=-1, keepdims=True)
    y = x * lax.rsqrt(ms + EPS)
    return (y * nw) * (1.0 + scale) + shift


def _mod_kernel(c_ref, w_ref, b_ref, o_ref):
    c = c_ref[...]
    ca = c * _sigmoid(c)
    o_ref[0] = jnp.dot(ca, w_ref[0], precision=_HI, preferred_element_type=F32) + b_ref[0]


def _mod_call(c, w_mod, b_mod):
    depth, d, n = w_mod.shape
    b = c.shape[0]
    tn = 1024
    return pl.pallas_call(
        _mod_kernel,
        grid=(depth, n // tn),
        in_specs=[
            pl.BlockSpec((b, d), lambda l, j: (0, 0)),
            pl.BlockSpec((1, d, tn), lambda l, j: (l, 0, j)),
            pl.BlockSpec((1, 1, tn), lambda l, j: (l, 0, j)),
        ],
        out_specs=pl.BlockSpec((1, b, tn), lambda l, j: (l, 0, j)),
        out_shape=jax.ShapeDtypeStruct((depth, b, n), F32),
        compiler_params=_cparams(("arbitrary", "arbitrary")),
        name="mod",
    )(c, w_mod, b_mod.reshape(depth, 1, n))


def _hg_proj_kernel(x_ref, mod_ref, nw_ref, lb_ref, w_ref, q_ref, k_ref, lf_ref, v_ref, g_ref, *, layer, d):
    mod = mod_ref[0]
    lbp = lb_ref[...]
    e = jnp.exp(lbp - jnp.max(lbp, axis=0, keepdims=True))
    lb = jnp.sum(e[0:layer + 1], axis=0, keepdims=True) / jnp.sum(e, axis=0, keepdims=True)

    for r0 in range(0, x_ref.shape[0], ROW_SLAB):
        rows = slice(r0, r0 + ROW_SLAB)
        h = _modulate(x_ref[rows, :], nw_ref[...], mod[:, 0:d], mod[:, d:2 * d]).astype(BF16)

        pq = jnp.dot(h, w_ref[:, 0:d], preferred_element_type=F32)
        q_ref[rows, :] = (pq * _sigmoid(pq)).astype(BF16)

        z = jnp.dot(h, w_ref[:, d:2 * d], preferred_element_type=F32)
        th = 0.5 * jnp.tanh(0.5 * z)
        kk = (1.0 - lb) * (0.5 - th)
        f = lb + (1.0 - lb) * (0.5 + th)
        lf_ref[rows, :] = jnp.log(jnp.where(kk < 0.5, 1.0 - kk, f))
        k_ref[rows, :] = kk.astype(BF16)

        v_ref[rows, :] = jnp.dot(h, w_ref[:, 2 * d:3 * d], preferred_element_type=F32).astype(BF16)
        pg = jnp.dot(h, w_ref[:, 3 * d:4 * d], preferred_element_type=F32)
        g_ref[rows, :] = (pg * _sigmoid(pg)).astype(BF16)


def _hg_proj_call(x2, mod3, nw, hg_lb, w_in, *, layer, batch, tm):
    t, d = x2.shape
    tiles_per_batch = (t // batch) // tm
    row = pl.BlockSpec((tm, d), lambda i: (i, 0))
    out_bf = jax.ShapeDtypeStruct((t, d), BF16)
    return pl.pallas_call(
        functools.partial(_hg_proj_kernel, layer=layer, d=d),
        grid=(t // tm,),
        in_specs=[
            row,
            pl.BlockSpec((1, 1, N_MOD * d), lambda i: (layer * batch + i // tiles_per_batch, 0, 0)),
            _resident((1, d)),
            _resident(hg_lb.shape),
            _resident(w_in.shape),
        ],
        out_specs=[row, row, row, row, row],
        out_shape=[out_bf, out_bf, jax.ShapeDtypeStruct((t, d), F32), out_bf, out_bf],
        compiler_params=_cparams(("arbitrary",)),
        name="hg_proj",
    )(x2, mod3, nw, hg_lb, w_in)


def _hg_rec_kernel(q_ref, k_ref, lf_ref, v_ref, g_ref, gnw_ref, o_ref, st_ref, gs_ref, ks_ref, *,
                   n_chunks, unroll_bounded, unroll_exact):
    c, sub = HG_CHUNK, HG_SUB
    n_sub = c // sub
    half = sub // 2

    @pl.when(pl.program_id(2) == 0)
    def _():
        st_ref[...] = jnp.zeros_like(st_ref)

    r_i = lax.broadcasted_iota(jnp.int32, (c, c), 0)
    c_i = lax.broadcasted_iota(jnp.int32, (c, c), 1)
    tri = (c_i <= r_i).astype(BF16)
    ones_w = jnp.ones((HG_HEAD, c), BF16)
    half_row = lax.broadcasted_iota(jnp.int32, (half, HG_HEAD), 0)
    a_lane = lax.broadcasted_iota(jnp.int32, (half, c), 1)
    gnw = gnw_ref[...]
    nt = (((1,), (1,)), ((), ()))
    tn = (((0,), (0,)), ((), ()))

    def offdiag_operands(q, k, gcum, j, bounded):
        lo = j * sub
        hi = lo + sub if bounded else lo
        g_b = gcum[lo - 1:lo, :] if j > 0 else jnp.zeros((1, HG_HEAD), F32)
        qh = (q[lo:lo + sub, :] * jnp.exp2(gcum[lo:lo + sub, :] - g_b)).astype(BF16)
        kh = (k[0:hi, :] * jnp.exp2(g_b - gcum[0:hi, :])).astype(BF16)
        if hi < c:
            kh = jnp.concatenate([kh, jnp.zeros((c - hi, HG_HEAD), BF16)], axis=0)
        return qh, kh

    def diag_products(q, k, gcum, slot, j):
        lo = j * sub
        gt = (gcum[lo:lo + half, :], gcum[lo + half:lo + sub, :])
        qt = (q[lo:lo + half, :], q[lo + half:lo + sub, :])
        ps = []
        for s in range(sub):
            g_s = gs_ref[slot, pl.ds(lo + s, 1), :]
            k_s = ks_ref[slot, pl.ds(lo + s, 1), :]
            for hf in range(2):
                if s >= half and hf == 0:
                    continue
                dlt = gt[hf] - g_s
                if (s >= half) == (hf == 1):
                    dlt = jnp.where(half_row >= s % half, dlt, -jnp.inf)
                ps.append((qt[hf] * k_s) * jnp.exp2(dlt))
        return jnp.concatenate(ps, axis=0).astype(BF16)

    def diag_scatter(r, j):
        lo = j * sub
        a_top = jnp.zeros((half, c), F32)
        a_bot = jnp.zeros((half, c), F32)
        for s in range(sub):
            if s < half:
                a_top = jnp.where(a_lane == lo + s, r[2 * s * half:(2 * s + 1) * half, :], a_top)
                a_bot = jnp.where(a_lane == lo + s, r[(2 * s + 1) * half:(2 * s + 2) * half, :], a_bot)
            else:
                a_bot = jnp.where(a_lane == lo + s, r[(half + s) * half:(half + s + 1) * half, :], a_bot)
        return jnp.concatenate([a_top, a_bot], axis=0)

    def run(bounded, unroll):
        def group(i, st):
            us = range(unroll)
            rows = [pl.ds(pl.multiple_of((i * unroll + u) * c, c), c) for u in us]
            g3 = [jnp.dot(tri, _split3(lf_ref[r, :]), preferred_element_type=F32) for r in rows]
            q = [q_ref[r, :].astype(F32) for r in rows]
            k = [k_ref[r, :].astype(F32) for r in rows]
            gcum = [(g[:, 0:HG_HEAD] + g[:, HG_HEAD:2 * HG_HEAD] + g[:, 2 * HG_HEAD:]) * LOG2E for g in g3]
            g_last = [g[c - 1:c, :] for g in gcum]
            qg = [(q[u] * jnp.exp2(gcum[u])).astype(BF16) for u in us]
            kd = [(k[u] * jnp.exp2(g_last[u] - gcum[u])).astype(BF16) for u in us]
            first = 0 if bounded else 1
            off = [[offdiag_operands(q[u], k[u], gcum[u], j, bounded) for j in range(first, n_sub)] for u in us]
            if not bounded:
                for u in us:
                    gs_ref[u] = gcum[u]
                    ks_ref[u] = k[u]
                pst = [[diag_products(q[u], k[u], gcum[u], u, j) for j in range(n_sub)] for u in us]
            kv = [lax.dot_general(v_ref[rows[u], :], kd[u], tn, preferred_element_type=F32) for u in us]
            blocks = [[lax.dot_general(qh, kh, nt, preferred_element_type=F32) for qh, kh in off[u]] for u in us]
            if bounded:
                a = [jnp.where(c_i <= r_i, jnp.concatenate(blocks[u], axis=0), 0.0) for u in us]
            else:
                sums = [[jnp.dot(p, ones_w, preferred_element_type=F32) for p in pst[u]] for u in us]
                a = []
                for u in us:
                    a_rows = [diag_scatter(sums[u][j], j) for j in range(n_sub)]
                    for j in range(1, n_sub):
                        a_rows[j] = a_rows[j] + blocks[u][j - 1]
                    a.append(jnp.concatenate(a_rows, axis=0))
            o_intra = [jnp.dot(a[u].astype(BF16), v_ref[rows[u], :], preferred_element_type=F32) for u in us]
            for u in us:
                o = o_intra[u] + lax.dot_general(qg[u], st.astype(BF16), nt, preferred_element_type=F32)
                st = st * jnp.exp2(g_last[u]) + kv[u]
                on = o * lax.rsqrt(jnp.mean(o * o, axis=-1, keepdims=True) + EPS) * gnw
                o_ref[rows[u], :] = (on * g_ref[rows[u], :].astype(F32)).astype(BF16)
            return st

        st_ref[...] = lax.fori_loop(0, n_chunks // unroll, group, st_ref[...])

    w_r = lax.broadcasted_iota(jnp.int32, (n_chunks * n_sub, n_chunks * c), 0)
    w_c = lax.broadcasted_iota(jnp.int32, (n_chunks * n_sub, n_chunks * c), 1)
    win = (lax.shift_right_logical(w_c, sub.bit_length() - 1) == w_r).astype(BF16)
    wsum = jnp.dot(win, lf_ref[...].astype(BF16), preferred_element_type=F32)
    bounded = jnp.min(wsum) * LOG2E >= -HG_SAFE_DECAY

    @pl.when(bounded)
    def _():
        run(True, unroll_bounded)

    @pl.when(jnp.logical_not(bounded))
    def _():
        run(False, unroll_exact)


def _hg_rec_call(q, k, lf, v, g, gnw, *, batch, tc, unroll_bounded, unroll_exact):
    t, d = q.shape
    heads = d // HG_HEAD
    steps = (t // batch) // tc
    unroll = unroll_exact
    blk = pl.BlockSpec((tc, HG_HEAD), lambda b, h, s: (b * steps + s, h))
    return pl.pallas_call(
        functools.partial(_hg_rec_kernel, n_chunks=tc // HG_CHUNK, unroll_bounded=unroll_bounded,
                          unroll_exact=unroll_exact),
        grid=(batch, heads, steps),
        in_specs=[blk, blk, blk, blk, blk, pl.BlockSpec((1, HG_HEAD), lambda b, h, s: (0, 0))],
        out_specs=blk,
        out_shape=jax.ShapeDtypeStruct((t, d), BF16),
        scratch_shapes=[
            pltpu.VMEM((HG_HEAD, HG_HEAD), F32),
            pltpu.VMEM((unroll, HG_CHUNK, HG_HEAD), F32),
            pltpu.VMEM((unroll, HG_CHUNK, HG_HEAD), F32),
        ],
        compiler_params=_cparams(("arbitrary", "arbitrary", "arbitrary")),
        name="hg_rec",
    )(q, k, lf, v, g, gnw)


def _head_rmsnorm(p, w_row):
    low = lax.broadcasted_iota(jnp.int32, (1, LANES), 1) < FOX_HEAD
    outs = []
    for j in range(p.shape[1] // LANES):
        pj = p[:, j * LANES:(j + 1) * LANES]
        ss = pj * pj
        s_lo = jnp.sum(jnp.where(low, ss, 0.0), axis=-1, keepdims=True)
        s_hi = jnp.sum(jnp.where(low, 0.0, ss), axis=-1, keepdims=True)
        inv = lax.rsqrt(jnp.where(low, s_lo, s_hi) * (1.0 / FOX_HEAD) + EPS)
        outs.append(pj * inv)
    return jnp.concatenate(outs, axis=1) * w_row


def _fox_proj_kernel(x_ref, mod_ref, nw_ref, qw_ref, kw_ref, bf_ref, w_ref, wf_ref,
                     q_ref, k_ref, v_ref, g_ref, lf_ref, *, d):
    mod = mod_ref[0]
    scale = LOG2E / np.sqrt(FOX_HEAD)
    for r0 in range(0, x_ref.shape[0], ROW_SLAB):
        rows = slice(r0, r0 + ROW_SLAB)
        h = _modulate(x_ref[rows, :], nw_ref[...], mod[:, 0:d], mod[:, d:2 * d]).astype(BF16)
        pq = jnp.dot(h, w_ref[:, 0:d], preferred_element_type=F32)
        q_ref[rows, :] = (_head_rmsnorm(pq, qw_ref[...]) * scale).astype(BF16)
        pk = jnp.dot(h, w_ref[:, d:2 * d], preferred_element_type=F32)
        k_ref[rows, :] = _head_rmsnorm(pk, kw_ref[...]).astype(BF16)
        v_ref[rows, :] = jnp.dot(h, w_ref[:, 2 * d:3 * d], preferred_element_type=F32).astype(BF16)
        pg = jnp.dot(h, w_ref[:, 3 * d:4 * d], preferred_element_type=F32)
        g_ref[rows, :] = _sigmoid(pg).astype(BF16)
        u = jnp.dot(h, wf_ref[...], preferred_element_type=F32) + bf_ref[...]
        lf_ref[rows, :] = jnp.minimum(u, 0.0) - jnp.log1p(jnp.exp(-jnp.abs(u)))


def _fox_proj_call(x2, mod3, nw, qw, kw, bf, w_main, w_f, *, layer, batch, tm):
    t, d = x2.shape
    nh = w_f.shape[1]
    tiles_per_batch = (t // batch) // tm
    row = pl.BlockSpec((tm, d), lambda i: (i, 0))
    out_bf = jax.ShapeDtypeStruct((t, d), BF16)
    return pl.pallas_call(
        functools.partial(_fox_proj_kernel, d=d),
        grid=(t // tm,),
        in_specs=[
            row,
            pl.BlockSpec((1, 1, N_MOD * d), lambda i: (layer * batch + i // tiles_per_batch, 0, 0)),
            _resident((1, d)), _resident((1, d)), _resident((1, d)), _resident((1, nh)),
            _resident((d, 4 * d)),
            _resident(w_f.shape),
        ],
        out_specs=[row, row, row, row, pl.BlockSpec((tm, nh), lambda i: (i, 0))],
        out_shape=[out_bf, out_bf, out_bf, out_bf, jax.ShapeDtypeStruct((t, nh), F32)],
        compiler_params=_cparams(("arbitrary",)),
        name="fox_proj",
    )(x2, mod3, nw, qw, kw, bf, w_main, w_f)


FOX_AUG = 6


def _split3(x):
    hi = x.astype(BF16)
    r1 = x - hi.astype(F32)
    mid = r1.astype(BF16)
    lo = (r1 - mid.astype(F32)).astype(BF16)
    return jnp.concatenate([hi, mid, lo], axis=1)


def _fox_bias_kernel(lf_ref, qw_ref, kw_ref, pq_ref, pk_ref, oq_ref, ok_ref,
                     aq_ref, ak_ref, edge_ref, flag_ref, carry_ref):
    tp = lf_ref.shape[0]

    @pl.when(pl.program_id(1) == 0)
    def _():
        carry_ref[...] = jnp.zeros_like(carry_ref)

    bound = 1.01 * LOG2E * np.sqrt(FOX_HEAD) * jnp.max(jnp.abs(qw_ref[...] * kw_ref[...]), axis=-1, keepdims=True)
    fast = bound <= FOX_STAB_MAX
    flag_ref[...] = jnp.broadcast_to(fast.astype(jnp.int32), flag_ref.shape)
    stab = jnp.where(fast, bound, 0.0)

    r_i = lax.broadcasted_iota(jnp.int32, (tp, tp), 0)
    c_i = lax.broadcasted_iota(jnp.int32, (tp, tp), 1)
    tri = (c_i <= r_i).astype(BF16)
    f3 = jnp.dot(tri, _split3(lf_ref[...]), preferred_element_type=F32)
    f = f3[:, 0:LANES] + f3[:, LANES:2 * LANES] + f3[:, 2 * LANES:] + carry_ref[...]
    carry_ref[...] = f[tp - 1:tp, :]
    f2 = f * LOG2E
    edge_ref[0] = jnp.concatenate([f2[0:1, :], f2[tp - 1:tp, :]], axis=0)
    aq_ref[...] = jnp.dot(_split3(f2 - stab), pq_ref[...], preferred_element_type=F32) + oq_ref[...]
    ak_ref[...] = jnp.dot(_split3(f2), pk_ref[...], preferred_element_type=F32) + ok_ref[...]


def _fox_bias_call(lf, qw, kw, *, batch, d, tp):
    t = lf.shape[0]
    heads = d // FOX_HEAD
    steps = (t // batch) // tp
    pq = np.zeros((3 * LANES, LANES), np.float32)
    pk = np.zeros((3 * LANES, LANES), np.float32)
    oq = np.zeros((1, LANES), np.float32)
    ok = np.zeros((1, LANES), np.float32)
    for h in range(heads):
        base = FOX_AUG * h
        for i in range(3):
            pq[i * LANES + h, base + i] = 1.0
            ok[0, base + i] = 1.0
            oq[0, base + 3 + i] = 1.0
            pk[i * LANES + h, base + 3 + i] = -1.0
    row = pl.BlockSpec((tp, LANES), lambda b, s: (b * steps + s, 0))
    return pl.pallas_call(
        _fox_bias_kernel,
        grid=(batch, steps),
        in_specs=[
            row,
            _resident((1, FOX_HEAD)), _resident((1, FOX_HEAD)),
            _resident(pq.shape), _resident(pk.shape), _resident(oq.shape), _resident(ok.shape),
        ],
        out_specs=[row, row,
                   pl.BlockSpec((1, 2, LANES), lambda b, s: (b * steps + s, 0, 0)),
                   pl.BlockSpec((8, LANES), lambda b, s: (0, 0))],
        out_shape=[jax.ShapeDtypeStruct((t, LANES), F32), jax.ShapeDtypeStruct((t, LANES), F32),
                   jax.ShapeDtypeStruct((batch * steps, 2, LANES), F32),
                   jax.ShapeDtypeStruct((8, LANES), jnp.int32)],
        scratch_shapes=[pltpu.VMEM((1, LANES), F32)],
        compiler_params=_cparams(("arbitrary", "arbitrary")),
        name="fox_bias",
    )(lf, qw, kw, jnp.asarray(pq, BF16), jnp.asarray(pk, BF16), jnp.asarray(oq), jnp.asarray(ok))


FOX_VROWS = 80


def _fox_attn_kernel(flag_ref, fs_ref, fe_ref, q_ref, aq_ref, k_ref, ak_ref, v_ref, g_ref, o_ref,
                     vt_ref, ka_ref, qm_ref, acc_ref, m_ref, pa_ref, pb_ref, *, tq, tk, heads):
    bi, pair, qi = pl.program_id(0), pl.program_id(1), pl.program_id(2)
    nk = v_ref.shape[0] // tk
    sub = tq // tk
    lane2 = lax.broadcasted_iota(jnp.int32, (1, 2 * LANES), 1)
    head0 = (lane2 < FOX_HEAD) | ((lane2 >= LANES) & (lane2 < LANES + FOX_AUG))
    head1 = ((lane2 >= FOX_HEAD) & (lane2 < LANES)) | ((lane2 >= LANES + FOX_AUG) & (lane2 < LANES + 2 * FOX_AUG))
    lane1 = lax.broadcasted_iota(jnp.int32, (1, LANES), 1)
    low = lane1 < FOX_HEAD
    nt = (((1,), (1,)), ((), ()))

    def bias_lanes(a):
        shifted = pltpu.roll(a, (LANES - 2 * FOX_AUG * pair) % LANES, axis=1)
        return jnp.where(lane1 < 2 * FOX_AUG, shifted, 0.0).astype(BF16)

    @pl.when(qi == 0)
    def _():
        pad_rows = lax.broadcasted_iota(jnp.int32, (FOX_VROWS - FOX_HEAD, tk), 0)
        ones_blk = jnp.where(pad_rows == 0, 1.0, 0.0)

        def build(j, carry):
            rows = pl.ds(pl.multiple_of(j * tk, tk), tk)
            vt = v_ref[rows, :].astype(F32).T
            for e in range(2):
                vt_ref[j, e] = jnp.concatenate([vt[e * FOX_HEAD:(e + 1) * FOX_HEAD, :], ones_blk],
                                               axis=0).astype(BF16)
            ka_ref[rows, :] = bias_lanes(ak_ref[rows, :])
            return carry

        lax.fori_loop(0, nk, build, 0)

    def probs(j, online, diag=None, dst=None):
        rows = pl.ds(pl.multiple_of(j * tk, tk), tk)
        kf = jnp.concatenate([k_ref[rows, :], ka_ref[rows, :]], axis=1)
        q_lo = 0 if diag is None else diag * tk
        ps, alphas = [], []
        for e in range(2):
            s = lax.dot_general(kf, qm_ref[e, q_lo:tq, :], nt, preferred_element_type=F32)
            if diag is not None:
                kpos = lax.broadcasted_iota(jnp.int32, (tk, tk), 0)
                qpos = lax.broadcasted_iota(jnp.int32, (tk, tk), 1)
                blk = jnp.where(kpos <= qpos, s[:, 0:tk], -jnp.inf)
                s = blk if q_lo + tk == tq else jnp.concatenate([blk, s[:, tk:]], axis=1)
            if online:
                m_prev = m_ref[e, :, q_lo:tq]
                m_new = jnp.maximum(m_prev, jnp.max(s, axis=0, keepdims=True))
                alphas.append(jnp.exp2(m_prev - m_new))
                m_ref[e, :, q_lo:tq] = m_new
                s = s - m_new
            p = jnp.exp2(s).astype(BF16)
            if dst is None:
                ps.append(p)
            else:
                dst[e] = p
        return ps, alphas

    def accumulate(ps, j, alphas=None, diag=None):
        q_lo = 0 if diag is None else diag * tk
        for e in range(2):
            pv = jnp.dot(vt_ref[j, e], ps[e], preferred_element_type=F32)
            if alphas:
                acc_ref[e, :, q_lo:tq] = acc_ref[e, :, q_lo:tq] * alphas[e] + pv
            else:
                acc_ref[e, :, q_lo:tq] += pv

    def start():
        qf = jnp.concatenate([q_ref[...], bias_lanes(aq_ref[...])], axis=1)
        zero = jnp.zeros_like(qf)
        qm_ref[0] = jnp.where(head0, qf, zero)
        qm_ref[1] = jnp.where(head1, qf, zero)
        acc_ref[...] = jnp.zeros_like(acc_ref)

    def finish():
        outs = []
        for e in range(2):
            acc = acc_ref[e]
            outs.append(acc[0:FOX_HEAD, :] / acc[FOX_HEAD:FOX_HEAD + 1, :])
        o_ref[...] = (jnp.concatenate(outs, axis=0).T * g_ref[...].astype(F32)).astype(BF16)

    n_full = qi * sub

    @pl.when(flag_ref[0] == 1)
    def _():
        start()
        held = None
        for a in range(sub - 1, 0, -1):
            nxt = probs(n_full + a, False, diag=a)[0]
            if held is not None:
                accumulate(held[0], n_full + held[1], diag=held[1])
            held = (nxt, a)
        probs(n_full, False, diag=0, dst=pa_ref)
        if held is not None:
            accumulate(held[0], n_full + held[1], diag=held[1])

        h0 = bi * heads + 2 * pair
        fq0 = fs_ref[h0 * nk + n_full]
        fq1 = fs_ref[(h0 + 1) * nk + n_full]

        def dead(j, cnt):
            gone0 = fq0 - fe_ref[h0 * nk + j] < -FOX_SKIP
            gone1 = fq1 - fe_ref[(h0 + 1) * nk + j] < -FOX_SKIP
            return cnt + jnp.logical_and(gone0, gone1).astype(jnp.int32)

        first = lax.fori_loop(0, n_full, dead, 0)
        n_live = n_full - first

        def two_tiles(i, carry):
            j = first + 2 * i
            probs(j, False, dst=pb_ref)
            accumulate(pa_ref, jnp.where(i == 0, n_full, j - 1))
            probs(j + 1, False, dst=pa_ref)
            accumulate(pb_ref, j)
            return carry

        lax.fori_loop(0, n_live // 2, two_tiles, 0)

        @pl.when(n_live % 2 == 1)
        def _():
            ps, _ = probs(n_full - 1, False)
            accumulate(pa_ref, jnp.where(n_live == 1, n_full, n_full - 2))
            accumulate(ps, n_full - 1)

        @pl.when(n_live % 2 == 0)
        def _():
            accumulate(pa_ref, jnp.where(n_live == 0, n_full, n_full - 1))

        finish()

    @pl.when(flag_ref[0] != 1)
    def _():
        start()
        m_ref[...] = jnp.full_like(m_ref, -jnp.inf)

        def body(j, carry):
            ps, alphas = probs(j, True)
            accumulate(ps, j, alphas)
            return carry

        lax.fori_loop(0, n_full, body, 0)
        for a in range(sub):
            ps, alphas = probs(n_full + a, True, diag=a)
            accumulate(ps, n_full + a, alphas, diag=a)
        finish()


def _fox_attn_call(flag, f_first, f_last, q, aq, k, ak, v, g, *, batch, seq, tq, tk):
    t, d = q.shape
    pairs = d // LANES
    nq = seq // tq
    q_spec = pl.BlockSpec((tq, LANES), lambda b, p, i, *_: (b * nq + i, p))
    kv_spec = pl.BlockSpec((seq, LANES), lambda b, p, i, *_: (b, p))
    grid_spec = pltpu.PrefetchScalarGridSpec(
        num_scalar_prefetch=3,
        grid=(batch, pairs, nq),
        in_specs=[q_spec, pl.BlockSpec((tq, LANES), lambda b, p, i, *_: (b * nq + i, 0)),
                  kv_spec, pl.BlockSpec((seq, LANES), lambda b, p, i, *_: (b, 0)),
                  kv_spec, q_spec],
        out_specs=q_spec,
        scratch_shapes=[
            pltpu.VMEM((seq // tk, 2, FOX_VROWS, tk), BF16),
            pltpu.VMEM((seq, LANES), BF16),
            pltpu.VMEM((2, tq, 2 * LANES), BF16),
            pltpu.VMEM((2, FOX_VROWS, tq), F32),
            pltpu.VMEM((2, 1, tq), F32),
            pltpu.VMEM((2, tk, tq), BF16),
            pltpu.VMEM((2, tk, tq), BF16),
        ],
    )
    return pl.pallas_call(
        functools.partial(_fox_attn_kernel, tq=tq, tk=tk, heads=d // FOX_HEAD),
        grid_spec=grid_spec,
        out_shape=jax.ShapeDtypeStruct((t, d), BF16),
        compiler_params=_cparams(("arbitrary", "arbitrary", "arbitrary")),
        name="fox_attn",
    )(flag, f_first, f_last, q, aq, k, ak, v, g)


def _post_kernel(x_ref, y_ref, mod_ref, nw_ref, fw_ref, wo_ref, w1_ref, w2_ref, o_ref, *, d, ff_blk, final):
    mod = mod_ref[0]
    g1 = mod[:, 2 * d:3 * d]
    sh2, sc2, g2 = mod[:, 3 * d:4 * d], mod[:, 4 * d:5 * d], mod[:, 5 * d:6 * d]
    x1 = x_ref[...] + g1 * jnp.dot(y_ref[...], wo_ref[...], preferred_element_type=F32)
    h = _modulate(x1, nw_ref[...], sh2, sc2).astype(BF16)
    acc = jnp.zeros_like(x1)
    for j in range(w1_ref.shape[2] // ff_blk):
        a = jnp.maximum(jnp.dot(h, w1_ref[0, :, j * ff_blk:(j + 1) * ff_blk], preferred_element_type=F32), 0.0)
        acc = acc + jnp.dot((a * a).astype(BF16), w2_ref[0, j * ff_blk:(j + 1) * ff_blk, :],
                            preferred_element_type=F32)
    x2 = x1 + g2 * acc
    if final:
        x2 = x2 * lax.rsqrt(jnp.mean(x2 * x2, axis=-1, keepdims=True) + EPS) * fw_ref[...]
    o_ref[...] = x2


def _post_call(x2, y, mod3, nw, fw, w_out, w1, w2, *, layer, batch, tm, final):
    t, d = x2.shape
    tiles_per_batch = (t // batch) // tm
    row = pl.BlockSpec((tm, d), lambda i: (i, 0))
    return pl.pallas_call(
        functools.partial(_post_kernel, d=d, ff_blk=1024, final=final),
        grid=(t // tm,),
        in_specs=[
            row, row,
            pl.BlockSpec((1, 1, N_MOD * d), lambda i: (layer * batch + i // tiles_per_batch, 0, 0)),
            _resident((1, d)), _resident((1, d)),
            _resident(w_out.shape),
            pl.BlockSpec((1,) + w1.shape[1:], lambda i: (layer, 0, 0), pipeline_mode=pl.Buffered(1)),
            pl.BlockSpec((1,) + w2.shape[1:], lambda i: (layer, 0, 0), pipeline_mode=pl.Buffered(1)),
        ],
        out_specs=row,
        out_shape=jax.ShapeDtypeStruct((t, d), F32),
        compiler_params=_cparams(("arbitrary",)),
        name="post",
    )(x2, y, mod3, nw, fw, w_out, w1, w2)


def kernel(x, c, w_mod, b_mod, norm1_w, norm2_w, hg_w_in, hg_w_out, hg_lb, hg_gn_w, fox_w_in, fox_b_f,
           fox_qn_w, fox_kn_w, fox_w_out, mlp_w1, mlp_w2, final_w):
    batch, seq, d = x.shape
    depth = w_mod.shape[0]
    t = batch * seq
    fox_heads = d // FOX_HEAD

    mod3 = _mod_call(c, w_mod, b_mod).reshape(depth * batch, 1, N_MOD * d)
    xs = x.reshape(t, d)
    fw = final_w.reshape(1, d)
    w1_all, w2_all = mlp_w1.astype(BF16), mlp_w2.astype(BF16)

    for i in range(depth):
        j = i // 2
        n1 = norm1_w[i].reshape(1, d)
        if i % 2 == 0:
            q, k, lf, v, g = _hg_proj_call(xs, mod3, n1, hg_lb, hg_w_in[j].astype(BF16),
                                           layer=i, batch=batch, tm=512)
            y = _hg_rec_call(q, k, lf, v, g, hg_gn_w[j].reshape(1, HG_HEAD), batch=batch, tc=2048,
                             unroll_bounded=32, unroll_exact=4)
            w_out = hg_w_out[j]
        else:
            pad = ((0, 0), (0, LANES - fox_heads))
            q, k, v, g, lf = _fox_proj_call(
                xs, mod3, n1,
                jnp.tile(fox_qn_w[j], fox_heads).reshape(1, d),
                jnp.tile(fox_kn_w[j], fox_heads).reshape(1, d),
                jnp.pad(fox_b_f[j].reshape(1, fox_heads), pad),
                fox_w_in[j].astype(BF16), jnp.pad(fox_w_in[j, :, 4 * d:], pad).astype(BF16),
                layer=i, batch=batch, tm=512)
            ta = 512
            aq, ak, edge, flag = _fox_bias_call(lf, fox_qn_w[j].reshape(1, FOX_HEAD),
                                                fox_kn_w[j].reshape(1, FOX_HEAD), batch=batch, d=d, tp=ta)
            edge = edge.reshape(batch, seq // ta, 2, LANES)[..., :fox_heads].transpose(2, 0, 3, 1)
            y = _fox_attn_call(flag[0, :1], edge[0].reshape(-1), edge[1].reshape(-1), q, aq, k, ak, v, g,
                               batch=batch, seq=seq, tq=2 * ta, tk=ta)
            w_out = fox_w_out[j]
        xs = _post_call(xs, y, mod3, norm2_w[i].reshape(1, d), fw, w_out.astype(BF16), w1_all, w2_all,
                        layer=i, batch=batch, tm=512, final=(i == depth - 1))
    return xs.reshape(batch, seq, d)
```

```python
import functools

import numpy as np
import jax
import jax.numpy as jnp
from jax import lax
from jax.experimental import pallas as pl
from jax.experimental.pallas import tpu as pltpu

F32 = jnp.float32
BF16 = jnp.bfloat16
EPS = 1e-6
N_MOD = 6
HG_HEAD = 128
HG_CHUNK = 64
HG_SUB = 16
HG_SAFE_DECAY = 56.0
FOX_HEAD = 64
LANES = 128
SUBLANES = 8
LOG2E = float(np.log2(np.e))
FOX_STAB_MAX = 30.0
FOX_BOUND_SLACK = 1.01
FOX_SKIP = 160.0
ROW_SLAB = 256
VMEM_LIMIT = 56 * 1024 * 1024

PROJ_TM = 512
POST_FF_BLK = 1024
MOD_TN = 1024
HG_BLOCK = 2048
HG_GROUP = 32
HG_GROUP_EXACT = 4
ATTN_TK = 512
ATTN_TQ = 2 * ATTN_TK

_HI = lax.Precision.HIGHEST


def _cparams(sem):
    return pltpu.CompilerParams(dimension_semantics=sem, vmem_limit_bytes=VMEM_LIMIT)


def _resident(shape):
    nd = len(shape)
    return pl.BlockSpec(shape, lambda *_: (0,) * nd, pipeline_mode=pl.Buffered(1))


def _sigmoid(x):
    return 0.5 * jnp.tanh(0.5 * x) + 0.5


def _modulate(x, nw, shift, scale):
    ms = jnp.mean(x * x, axis=-1, keepdims=True)
    y = x * lax.rsqrt(ms + EPS)
    return (y * nw) * (1.0 + scale) + shift


def _mod_kernel(c_ref, w_ref, b_ref, o_ref):
    c = c_ref[...]
    ca = c * _sigmoid(c)
    o_ref[0] = jnp.dot(ca, w_ref[0], precision=_HI, preferred_element_type=F32) + b_ref[0]


def _mod_call(c, w_mod, b_mod):
    depth, d, n = w_mod.shape
    b = c.shape[0]
    tn = MOD_TN
    return pl.pallas_call(
        _mod_kernel,
        grid=(depth, n // tn),
        in_specs=[
            pl.BlockSpec((b, d), lambda l, j: (0, 0)),
            pl.BlockSpec((1, d, tn), lambda l, j: (l, 0, j)),
            pl.BlockSpec((1, 1, tn), lambda l, j: (l, 0, j)),
        ],
        out_specs=pl.BlockSpec((1, b, tn), lambda l, j: (l, 0, j)),
        out_shape=jax.ShapeDtypeStruct((depth, b, n), F32),
        compiler_params=_cparams(("arbitrary", "arbitrary")),
        name="mod",
    )(c, w_mod, b_mod.reshape(depth, 1, n))


def _hg_proj_kernel(x_ref, mod_ref, nw_ref, lb_ref, w_ref, q_ref, k_ref, lf_ref, v_ref, g_ref, wmin_ref, *,
                    layer, d):
    mod = mod_ref[0]
    lbp = lb_ref[...]
    e = jnp.exp(lbp - jnp.max(lbp, axis=0, keepdims=True))
    lb = jnp.sum(e[0:layer + 1], axis=0, keepdims=True) / jnp.sum(e, axis=0, keepdims=True)

    w_r = lax.broadcasted_iota(jnp.int32, (ROW_SLAB // HG_SUB, ROW_SLAB), 0)
    w_c = lax.broadcasted_iota(jnp.int32, (ROW_SLAB // HG_SUB, ROW_SLAB), 1)
    win = (lax.shift_right_logical(w_c, HG_SUB.bit_length() - 1) == w_r).astype(BF16)
    wmin = None

    for r0 in range(0, x_ref.shape[0], ROW_SLAB):
        rows = slice(r0, r0 + ROW_SLAB)
        h = _modulate(x_ref[rows, :], nw_ref[...], mod[:, 0:d], mod[:, d:2 * d]).astype(BF16)

        pq = jnp.dot(h, w_ref[:, 0:d], preferred_element_type=F32)
        q_ref[rows, :] = (pq * _sigmoid(pq)).astype(BF16)

        z = jnp.dot(h, w_ref[:, d:2 * d], preferred_element_type=F32)
        th = 0.5 * jnp.tanh(0.5 * z)
        kk = (1.0 - lb) * (0.5 - th)
        f = lb + (1.0 - lb) * (0.5 + th)
        lf = jnp.log(jnp.where(kk < 0.5, 1.0 - kk, f))
        lf_ref[rows, :] = lf
        k_ref[rows, :] = kk.astype(BF16)
        wsum = jnp.min(jnp.dot(win, lf.astype(BF16), preferred_element_type=F32), axis=0, keepdims=True)
        wmin = wsum if wmin is None else jnp.minimum(wmin, wsum)

        v_ref[rows, :] = jnp.dot(h, w_ref[:, 2 * d:3 * d], preferred_element_type=F32).astype(BF16)
        pg = jnp.dot(h, w_ref[:, 3 * d:4 * d], preferred_element_type=F32)
        g_ref[rows, :] = (pg * _sigmoid(pg)).astype(BF16)
    wmin_ref[0] = wmin


def _hg_proj_call(x2, mod3, nw, hg_lb, w_in, *, layer, batch, tm):
    t, d = x2.shape
    tiles_per_batch = (t // batch) // tm
    row = pl.BlockSpec((tm, d), lambda i: (i, 0))
    out_bf = jax.ShapeDtypeStruct((t, d), BF16)
    return pl.pallas_call(
        functools.partial(_hg_proj_kernel, layer=layer, d=d),
        grid=(t // tm,),
        in_specs=[
            row,
            pl.BlockSpec((1, 1, N_MOD * d), lambda i: (layer * batch + i // tiles_per_batch, 0, 0)),
            _resident((1, d)),
            _resident(hg_lb.shape),
            _resident(w_in.shape),
        ],
        out_specs=[row, row, row, row, row, pl.BlockSpec((1, 1, d), lambda i: (i, 0, 0))],
        out_shape=[out_bf, out_bf, jax.ShapeDtypeStruct((t, d), F32), out_bf, out_bf,
                   jax.ShapeDtypeStruct((t // tm, 1, d), F32)],
        compiler_params=_cparams(("arbitrary",)),
        name="hg_proj",
    )(x2, mod3, nw, hg_lb, w_in)


def _hg_rec_kernel(wmin_ref, q_ref, k_ref, lf_ref, v_ref, g_ref, gnw_ref, o_ref, st_ref, gs_ref, ks_ref, *,
                   n_chunks, unroll_bounded, unroll_exact):
    c, sub = HG_CHUNK, HG_SUB
    n_sub = c // sub
    half = sub // 2

    @pl.when(pl.program_id(2) == 0)
    def _():
        st_ref[...] = jnp.zeros_like(st_ref)

    r_i = lax.broadcasted_iota(jnp.int32, (c, c), 0)
    c_i = lax.broadcasted_iota(jnp.int32, (c, c), 1)
    tri = (c_i <= r_i).astype(BF16)
    ones_w = jnp.ones((HG_HEAD, c), BF16)
    half_row = lax.broadcasted_iota(jnp.int32, (half, HG_HEAD), 0)
    a_lane = lax.broadcasted_iota(jnp.int32, (half, c), 1)
    gnw = gnw_ref[...]
    nt = (((1,), (1,)), ((), ()))
    tn = (((0,), (0,)), ((), ()))

    def offdiag_operands(q, k, gcum, j, bounded):
        lo = j * sub
        hi = lo + sub if bounded else lo
        g_b = gcum[lo - 1:lo, :] if j > 0 else jnp.zeros((1, HG_HEAD), F32)
        qh = (q[lo:lo + sub, :] * jnp.exp2(gcum[lo:lo + sub, :] - g_b)).astype(BF16)
        kh = (k[0:hi, :] * jnp.exp2(g_b - gcum[0:hi, :])).astype(BF16)
        if hi < c:
            kh = jnp.concatenate([kh, jnp.zeros((c - hi, HG_HEAD), BF16)], axis=0)
        return qh, kh

    def diag_products(q, k, gcum, slot, j):
        lo = j * sub
        gt = (gcum[lo:lo + half, :], gcum[lo + half:lo + sub, :])
        qt = (q[lo:lo + half, :], q[lo + half:lo + sub, :])
        ps = []
        for s in range(sub):
            g_s = gs_ref[slot, pl.ds(lo + s, 1), :]
            k_s = ks_ref[slot, pl.ds(lo + s, 1), :]
            for hf in range(2):
                if s >= half and hf == 0:
                    continue
                dlt = gt[hf] - g_s
                if (s >= half) == (hf == 1):
                    dlt = jnp.where(half_row >= s % half, dlt, -jnp.inf)
                ps.append((qt[hf] * k_s) * jnp.exp2(dlt))
        return jnp.concatenate(ps, axis=0).astype(BF16)

    def diag_scatter(r, j):
        lo = j * sub
        a_top = jnp.zeros((half, c), F32)
        a_bot = jnp.zeros((half, c), F32)
        for s in range(sub):
            if s < half:
                a_top = jnp.where(a_lane == lo + s, r[2 * s * half:(2 * s + 1) * half, :], a_top)
                a_bot = jnp.where(a_lane == lo + s, r[(2 * s + 1) * half:(2 * s + 2) * half, :], a_bot)
            else:
                a_bot = jnp.where(a_lane == lo + s, r[(half + s) * half:(half + s + 1) * half, :], a_bot)
        return jnp.concatenate([a_top, a_bot], axis=0)

    def run(bounded, unroll):
        def group(i, st):
            us = range(unroll)
            rows = [pl.ds(pl.multiple_of((i * unroll + u) * c, c), c) for u in us]
            g3 = [jnp.dot(tri, _split3(lf_ref[r, :]), preferred_element_type=F32) for r in rows]
            q = [q_ref[r, :].astype(F32) for r in rows]
            k = [k_ref[r, :].astype(F32) for r in rows]
            gcum = [(g[:, 0:HG_HEAD] + g[:, HG_HEAD:2 * HG_HEAD] + g[:, 2 * HG_HEAD:]) * LOG2E for g in g3]
            g_last = [g[c - 1:c, :] for g in gcum]
            qg = [(q[u] * jnp.exp2(gcum[u])).astype(BF16) for u in us]
            kd = [(k[u] * jnp.exp2(g_last[u] - gcum[u])).astype(BF16) for u in us]
            first = 0 if bounded else 1
            off = [[offdiag_operands(q[u], k[u], gcum[u], j, bounded) for j in range(first, n_sub)] for u in us]
            if not bounded:
                for u in us:
                    gs_ref[u] = gcum[u]
                    ks_ref[u] = k[u]
                pst = [[diag_products(q[u], k[u], gcum[u], u, j) for j in range(n_sub)] for u in us]
            kv = [lax.dot_general(v_ref[rows[u], :], kd[u], tn, preferred_element_type=F32) for u in us]
            blocks = [[lax.dot_general(qh, kh, nt, preferred_element_type=F32) for qh, kh in off[u]] for u in us]
            if bounded:
                a = [jnp.where(c_i <= r_i, jnp.concatenate(blocks[u], axis=0), 0.0) for u in us]
            else:
                sums = [[jnp.dot(p, ones_w, preferred_element_type=F32) for p in pst[u]] for u in us]
                a = []
                for u in us:
                    a_rows = [diag_scatter(sums[u][j], j) for j in range(n_sub)]
                    for j in range(1, n_sub):
                        a_rows[j] = a_rows[j] + blocks[u][j - 1]
                    a.append(jnp.concatenate(a_rows, axis=0))
            o_intra = [jnp.dot(a[u].astype(BF16), v_ref[rows[u], :], preferred_element_type=F32) for u in us]
            for u in us:
                o = o_intra[u] + lax.dot_general(qg[u], st.astype(BF16), nt, preferred_element_type=F32)
                st = st * jnp.exp2(g_last[u]) + kv[u]
                on = o * lax.rsqrt(jnp.mean(o * o, axis=-1, keepdims=True) + EPS) * gnw
                o_ref[rows[u], :] = (on * g_ref[rows[u], :].astype(F32)).astype(BF16)
            return st

        st_ref[...] = lax.fori_loop(0, n_chunks // unroll, group, st_ref[...])

    blk_id = (pl.program_id(0) * pl.num_programs(1) + pl.program_id(1)) * pl.num_programs(2) + pl.program_id(2)
    bounded = wmin_ref[blk_id] * LOG2E >= -HG_SAFE_DECAY

    @pl.when(bounded)
    def _():
        run(True, unroll_bounded)

    @pl.when(jnp.logical_not(bounded))
    def _():
        run(False, unroll_exact)


def _hg_rec_call(wmin, q, k, lf, v, g, gnw, *, batch, tc, unroll_bounded, unroll_exact):
    t, d = q.shape
    heads = d // HG_HEAD
    steps = (t // batch) // tc
    unroll = unroll_exact
    blk = pl.BlockSpec((tc, HG_HEAD), lambda b, h, s, *_: (b * steps + s, h))
    grid_spec = pltpu.PrefetchScalarGridSpec(
        num_scalar_prefetch=1,
        grid=(batch, heads, steps),
        in_specs=[blk, blk, blk, blk, blk, pl.BlockSpec((1, HG_HEAD), lambda b, h, s, *_: (0, 0))],
        out_specs=blk,
        scratch_shapes=[
            pltpu.VMEM((HG_HEAD, HG_HEAD), F32),
            pltpu.VMEM((unroll, HG_CHUNK, HG_HEAD), F32),
            pltpu.VMEM((unroll, HG_CHUNK, HG_HEAD), F32),
        ],
    )
    return pl.pallas_call(
        functools.partial(_hg_rec_kernel, n_chunks=tc // HG_CHUNK, unroll_bounded=unroll_bounded,
                          unroll_exact=unroll_exact),
        grid_spec=grid_spec,
        out_shape=jax.ShapeDtypeStruct((t, d), BF16),
        compiler_params=_cparams(("arbitrary", "arbitrary", "arbitrary")),
        name="hg_rec",
    )(wmin, q, k, lf, v, g, gnw)


def _head_rmsnorm(p, w_row):
    low = lax.broadcasted_iota(jnp.int32, (1, LANES), 1) < FOX_HEAD
    outs = []
    for j in range(p.shape[1] // LANES):
        pj = p[:, j * LANES:(j + 1) * LANES]
        ss = pj * pj
        s_lo = jnp.sum(jnp.where(low, ss, 0.0), axis=-1, keepdims=True)
        s_hi = jnp.sum(jnp.where(low, 0.0, ss), axis=-1, keepdims=True)
        inv = lax.rsqrt(jnp.where(low, s_lo, s_hi) * (1.0 / FOX_HEAD) + EPS)
        outs.append(pj * inv)
    return jnp.concatenate(outs, axis=1) * w_row


def _fox_proj_kernel(x_ref, mod_ref, nw_ref, qw_ref, kw_ref, bf_ref, w_ref, wf_ref,
                     q_ref, k_ref, v_ref, g_ref, lf_ref, *, d):
    mod = mod_ref[0]
    scale = LOG2E / np.sqrt(FOX_HEAD)
    for r0 in range(0, x_ref.shape[0], ROW_SLAB):
        rows = slice(r0, r0 + ROW_SLAB)
        h = _modulate(x_ref[rows, :], nw_ref[...], mod[:, 0:d], mod[:, d:2 * d]).astype(BF16)
        pq = jnp.dot(h, w_ref[:, 0:d], preferred_element_type=F32)
        q_ref[rows, :] = (_head_rmsnorm(pq, qw_ref[...]) * scale).astype(BF16)
        pk = jnp.dot(h, w_ref[:, d:2 * d], preferred_element_type=F32)
        k_ref[rows, :] = _head_rmsnorm(pk, kw_ref[...]).astype(BF16)
        v_ref[rows, :] = jnp.dot(h, w_ref[:, 2 * d:3 * d], preferred_element_type=F32).astype(BF16)
        pg = jnp.dot(h, w_ref[:, 3 * d:4 * d], preferred_element_type=F32)
        g_ref[rows, :] = _sigmoid(pg).astype(BF16)
        u = jnp.dot(h, wf_ref[...], preferred_element_type=F32) + bf_ref[...]
        lf_ref[rows, :] = jnp.minimum(u, 0.0) - jnp.log1p(jnp.exp(-jnp.abs(u)))


def _fox_proj_call(x2, mod3, nw, qw, kw, bf, w_main, w_f, *, layer, batch, tm):
    t, d = x2.shape
    nh = w_f.shape[1]
    tiles_per_batch = (t // batch) // tm
    row = pl.BlockSpec((tm, d), lambda i: (i, 0))
    out_bf = jax.ShapeDtypeStruct((t, d), BF16)
    return pl.pallas_call(
        functools.partial(_fox_proj_kernel, d=d),
        grid=(t // tm,),
        in_specs=[
            row,
            pl.BlockSpec((1, 1, N_MOD * d), lambda i: (layer * batch + i // tiles_per_batch, 0, 0)),
            _resident((1, d)), _resident((1, d)), _resident((1, d)), _resident((1, nh)),
            _resident((d, 4 * d)),
            _resident(w_f.shape),
        ],
        out_specs=[row, row, row, row, pl.BlockSpec((tm, nh), lambda i: (i, 0))],
        out_shape=[out_bf, out_bf, out_bf, out_bf, jax.ShapeDtypeStruct((t, nh), F32)],
        compiler_params=_cparams(("arbitrary",)),
        name="fox_proj",
    )(x2, mod3, nw, qw, kw, bf, w_main, w_f)


FOX_AUG = 6


def _split3(x):
    hi = x.astype(BF16)
    r1 = x - hi.astype(F32)
    mid = r1.astype(BF16)
    lo = (r1 - mid.astype(F32)).astype(BF16)
    return jnp.concatenate([hi, mid, lo], axis=1)


def _fox_bias_kernel(lf_ref, qw_ref, kw_ref, pq_ref, pk_ref, oq_ref, ok_ref,
                     aq_ref, ak_ref, edge_ref, flag_ref, carry_ref):
    tp = lf_ref.shape[0]

    @pl.when(pl.program_id(1) == 0)
    def _():
        carry_ref[...] = jnp.zeros_like(carry_ref)

    bound = FOX_BOUND_SLACK * LOG2E * np.sqrt(FOX_HEAD) * jnp.max(jnp.abs(qw_ref[...] * kw_ref[...]), axis=-1, keepdims=True)
    fast = bound <= FOX_STAB_MAX
    flag_ref[...] = jnp.broadcast_to(fast.astype(jnp.int32), flag_ref.shape)
    stab = jnp.where(fast, bound, 0.0)

    r_i = lax.broadcasted_iota(jnp.int32, (tp, tp), 0)
    c_i = lax.broadcasted_iota(jnp.int32, (tp, tp), 1)
    tri = (c_i <= r_i).astype(BF16)
    f3 = jnp.dot(tri, _split3(lf_ref[...]), preferred_element_type=F32)
    f = f3[:, 0:LANES] + f3[:, LANES:2 * LANES] + f3[:, 2 * LANES:] + carry_ref[...]
    carry_ref[...] = f[tp - 1:tp, :]
    f2 = f * LOG2E
    edge_ref[0] = jnp.concatenate([f2[0:1, :], f2[tp - 1:tp, :]], axis=0)
    aq_ref[...] = jnp.dot(_split3(f2 - stab), pq_ref[...], preferred_element_type=F32) + oq_ref[...]
    ak_ref[...] = jnp.dot(_split3(f2), pk_ref[...], preferred_element_type=F32) + ok_ref[...]


def _fox_bias_call(lf, qw, kw, *, batch, d, tp):
    t = lf.shape[0]
    heads = d // FOX_HEAD
    steps = (t // batch) // tp
    pq = np.zeros((3 * LANES, LANES), np.float32)
    pk = np.zeros((3 * LANES, LANES), np.float32)
    oq = np.zeros((1, LANES), np.float32)
    ok = np.zeros((1, LANES), np.float32)
    for h in range(heads):
        base = FOX_AUG * h
        for i in range(3):
            pq[i * LANES + h, base + i] = 1.0
            ok[0, base + i] = 1.0
            oq[0, base + 3 + i] = 1.0
            pk[i * LANES + h, base + 3 + i] = -1.0
    row = pl.BlockSpec((tp, LANES), lambda b, s: (b * steps + s, 0))
    return pl.pallas_call(
        _fox_bias_kernel,
        grid=(batch, steps),
        in_specs=[
            row,
            _resident((1, FOX_HEAD)), _resident((1, FOX_HEAD)),
            _resident(pq.shape), _resident(pk.shape), _resident(oq.shape), _resident(ok.shape),
        ],
        out_specs=[row, row,
                   pl.BlockSpec((1, 2, LANES), lambda b, s: (b * steps + s, 0, 0)),
                   pl.BlockSpec((SUBLANES, LANES), lambda b, s: (0, 0))],
        out_shape=[jax.ShapeDtypeStruct((t, LANES), F32), jax.ShapeDtypeStruct((t, LANES), F32),
                   jax.ShapeDtypeStruct((batch * steps, 2, LANES), F32),
                   jax.ShapeDtypeStruct((SUBLANES, LANES), jnp.int32)],
        scratch_shapes=[pltpu.VMEM((1, LANES), F32)],
        compiler_params=_cparams(("arbitrary", "arbitrary")),
        name="fox_bias",
    )(lf, qw, kw, jnp.asarray(pq, BF16), jnp.asarray(pk, BF16), jnp.asarray(oq), jnp.asarray(ok))


FOX_VROWS = 80


def _fox_attn_kernel(flag_ref, fs_ref, fe_ref, q_ref, aq_ref, k_ref, ak_ref, v_ref, g_ref, o_ref,
                     v1_ref, ka_ref, qm_ref, acc_ref, m_ref, pa_ref, pb_ref, *, tq, tk, heads):
    bi, pair, qi = pl.program_id(0), pl.program_id(1), pl.program_id(2)
    nk = k_ref.shape[0] // tk
    sub = tq // tk
    pad_rows = lax.broadcasted_iota(jnp.int32, (FOX_VROWS - FOX_HEAD, tk), 0)
    ones_blk = jnp.where(pad_rows == 0, 1.0, 0.0).astype(BF16)
    lane2 = lax.broadcasted_iota(jnp.int32, (1, 2 * LANES), 1)
    head0 = (lane2 < FOX_HEAD) | ((lane2 >= LANES) & (lane2 < LANES + FOX_AUG))
    head1 = ((lane2 >= FOX_HEAD) & (lane2 < LANES)) | ((lane2 >= LANES + FOX_AUG) & (lane2 < LANES + 2 * FOX_AUG))
    lane1 = lax.broadcasted_iota(jnp.int32, (1, LANES), 1)
    low = lane1 < FOX_HEAD
    nt = (((1,), (1,)), ((), ()))

    def bias_lanes(a):
        shifted = pltpu.roll(a, (LANES - 2 * FOX_AUG * pair) % LANES, axis=1)
        return jnp.where(lane1 < 2 * FOX_AUG, shifted, 0.0).astype(BF16)

    @pl.when(qi == 0)
    def _():
        def build(j, carry):
            rows = pl.ds(pl.multiple_of(j * tk, tk), tk)
            ka_ref[rows, :] = bias_lanes(ak_ref[rows, :])
            vt = v_ref[rows, :].astype(F32).T
            for e in range(2):
                v1_ref[j, e] = jnp.concatenate([vt[e * FOX_HEAD:(e + 1) * FOX_HEAD, :].astype(BF16), ones_blk],
                                               axis=0)
            return carry

        lax.fori_loop(0, nk, build, 0)

    def probs(j, online, diag=None, dst=None):
        rows = pl.ds(pl.multiple_of(j * tk, tk), tk)
        kf = jnp.concatenate([k_ref[rows, :], ka_ref[rows, :]], axis=1)
        q_lo = 0 if diag is None else diag * tk
        ps, alphas = [], []
        for e in range(2):
            s = lax.dot_general(kf, qm_ref[e, q_lo:tq, :], nt, preferred_element_type=F32)
            if diag is not None:
                kpos = lax.broadcasted_iota(jnp.int32, (tk, tk), 0)
                qpos = lax.broadcasted_iota(jnp.int32, (tk, tk), 1)
                blk = jnp.where(kpos <= qpos, s[:, 0:tk], -jnp.inf)
                s = blk if q_lo + tk == tq else jnp.concatenate([blk, s[:, tk:]], axis=1)
            if online:
                m_prev = m_ref[e, :, q_lo:tq]
                m_new = jnp.maximum(m_prev, jnp.max(s, axis=0, keepdims=True))
                alphas.append(jnp.exp2(m_prev - m_new))
                m_ref[e, :, q_lo:tq] = m_new
                s = s - m_new
            p = jnp.exp2(s).astype(BF16)
            if dst is None:
                ps.append(p)
            else:
                dst[e] = p
        return ps, alphas

    def accumulate(ps, j, alphas=None, diag=None):
        q_lo = 0 if diag is None else diag * tk
        for e in range(2):
            pv = jnp.dot(v1_ref[j, e], ps[e], preferred_element_type=F32)
            if alphas:
                acc_ref[e, :, q_lo:tq] = acc_ref[e, :, q_lo:tq] * alphas[e] + pv
            else:
                acc_ref[e, :, q_lo:tq] += pv

    def start():
        qf = jnp.concatenate([q_ref[...], bias_lanes(aq_ref[...])], axis=1)
        zero = jnp.zeros_like(qf)
        qm_ref[0] = jnp.where(head0, qf, zero)
        qm_ref[1] = jnp.where(head1, qf, zero)
        acc_ref[...] = jnp.zeros_like(acc_ref)

    def finish():
        outs = []
        for e in range(2):
            acc = acc_ref[e]
            outs.append(acc[0:FOX_HEAD, :] / acc[FOX_HEAD:FOX_HEAD + 1, :])
        o_ref[...] = (jnp.concatenate(outs, axis=0).T * g_ref[...].astype(F32)).astype(BF16)

    n_full = qi * sub

    @pl.when(flag_ref[0] == 1)
    def _():
        start()
        held = None
        for a in range(sub - 1, 0, -1):
            nxt = probs(n_full + a, False, diag=a)[0]
            if held is not None:
                accumulate(held[0], n_full + held[1], diag=held[1])
            held = (nxt, a)
        probs(n_full, False, diag=0, dst=pa_ref)
        if held is not None:
            accumulate(held[0], n_full + held[1], diag=held[1])

        h0 = bi * heads + 2 * pair
        fq0 = fs_ref[h0 * nk + n_full]
        fq1 = fs_ref[(h0 + 1) * nk + n_full]

        def dead(j, cnt):
            gone0 = fq0 - fe_ref[h0 * nk + j] < -FOX_SKIP
            gone1 = fq1 - fe_ref[(h0 + 1) * nk + j] < -FOX_SKIP
            return cnt + jnp.logical_and(gone0, gone1).astype(jnp.int32)

        first = lax.fori_loop(0, n_full, dead, 0)
        n_live = n_full - first

        def two_tiles(i, carry):
            j = first + 2 * i
            probs(j, False, dst=pb_ref)
            accumulate(pa_ref, jnp.where(i == 0, n_full, j - 1))
            probs(j + 1, False, dst=pa_ref)
            accumulate(pb_ref, j)
            return carry

        lax.fori_loop(0, n_live // 2, two_tiles, 0)

        @pl.when(n_live % 2 == 1)
        def _():
            ps, _ = probs(n_full - 1, False)
            accumulate(pa_ref, jnp.where(n_live == 1, n_full, n_full - 2))
            accumulate(ps, n_full - 1)

        @pl.when(n_live % 2 == 0)
        def _():
            accumulate(pa_ref, jnp.where(n_live == 0, n_full, n_full - 1))

        finish()

    @pl.when(flag_ref[0] != 1)
    def _():
        start()
        m_ref[...] = jnp.full_like(m_ref, -jnp.inf)

        def body(j, carry):
            ps, alphas = probs(j, True)
            accumulate(ps, j, alphas)
            return carry

        lax.fori_loop(0, n_full, body, 0)
        for a in range(sub):
            ps, alphas = probs(n_full + a, True, diag=a)
            accumulate(ps, n_full + a, alphas, diag=a)
        finish()


def _fox_attn_call(flag, f_first, f_last, q, aq, k, ak, v, g, *, batch, seq, tq, tk):
    t, d = q.shape
    pairs = d // LANES
    nq = seq // tq
    q_spec = pl.BlockSpec((tq, LANES), lambda b, p, i, *_: (b * nq + i, p))
    kv_spec = pl.BlockSpec((seq, LANES), lambda b, p, i, *_: (b, p))
    grid_spec = pltpu.PrefetchScalarGridSpec(
        num_scalar_prefetch=3,
        grid=(batch, pairs, nq),
        in_specs=[q_spec, pl.BlockSpec((tq, LANES), lambda b, p, i, *_: (b * nq + i, 0)),
                  kv_spec, pl.BlockSpec((seq, LANES), lambda b, p, i, *_: (b, 0)),
                  kv_spec, q_spec],
        out_specs=q_spec,
        scratch_shapes=[
            pltpu.VMEM((seq // tk, 2, FOX_VROWS, tk), BF16),
            pltpu.VMEM((seq, LANES), BF16),
            pltpu.VMEM((2, tq, 2 * LANES), BF16),
            pltpu.VMEM((2, FOX_VROWS, tq), F32),
            pltpu.VMEM((2, 1, tq), F32),
            pltpu.VMEM((2, tk, tq), BF16),
            pltpu.VMEM((2, tk, tq), BF16),
        ],
    )
    return pl.pallas_call(
        functools.partial(_fox_attn_kernel, tq=tq, tk=tk, heads=d // FOX_HEAD),
        grid_spec=grid_spec,
        out_shape=jax.ShapeDtypeStruct((t, d), BF16),
        compiler_params=_cparams(("arbitrary", "arbitrary", "arbitrary")),
        name="fox_attn",
    )(flag, f_first, f_last, q, aq, k, ak, v, g)


def _post_kernel(x_ref, y_ref, mod_ref, nw_ref, fw_ref, wo_ref, w1_ref, w2_ref, o_ref, *, d, ff_blk, final):
    mod = mod_ref[0]
    g1 = mod[:, 2 * d:3 * d]
    sh2, sc2, g2 = mod[:, 3 * d:4 * d], mod[:, 4 * d:5 * d], mod[:, 5 * d:6 * d]
    x1 = x_ref[...] + g1 * jnp.dot(y_ref[...], wo_ref[...], preferred_element_type=F32)
    h = _modulate(x1, nw_ref[...], sh2, sc2).astype(BF16)
    acc = jnp.zeros_like(x1)
    for j in range(w1_ref.shape[2] // ff_blk):
        a = jnp.maximum(jnp.dot(h, w1_ref[0, :, j * ff_blk:(j + 1) * ff_blk], preferred_element_type=F32), 0.0)
        acc = acc + jnp.dot((a * a).astype(BF16), w2_ref[0, j * ff_blk:(j + 1) * ff_blk, :],
                            preferred_element_type=F32)
    x2 = x1 + g2 * acc
    if final:
        x2 = x2 * lax.rsqrt(jnp.mean(x2 * x2, axis=-1, keepdims=True) + EPS) * fw_ref[...]
    o_ref[...] = x2


def _post_call(x2, y, mod3, nw, fw, w_out, w1, w2, *, layer, batch, tm, final):
    t, d = x2.shape
    tiles_per_batch = (t // batch) // tm
    row = pl.BlockSpec((tm, d), lambda i: (i, 0))
    return pl.pallas_call(
        functools.partial(_post_kernel, d=d, ff_blk=POST_FF_BLK, final=final),
        grid=(t // tm,),
        in_specs=[
            row, row,
            pl.BlockSpec((1, 1, N_MOD * d), lambda i: (layer * batch + i // tiles_per_batch, 0, 0)),
            _resident((1, d)), _resident((1, d)),
            _resident(w_out.shape),
            pl.BlockSpec((1,) + w1.shape[1:], lambda i: (layer, 0, 0), pipeline_mode=pl.Buffered(1)),
            pl.BlockSpec((1,) + w2.shape[1:], lambda i: (layer, 0, 0), pipeline_mode=pl.Buffered(1)),
        ],
        out_specs=row,
        out_shape=jax.ShapeDtypeStruct((t, d), F32),
        compiler_params=_cparams(("arbitrary",)),
        name="post",
    )(x2, y, mod3, nw, fw, w_out, w1, w2)


def kernel(x, c, w_mod, b_mod, norm1_w, norm2_w, hg_w_in, hg_w_out, hg_lb, hg_gn_w, fox_w_in, fox_b_f,
           fox_qn_w, fox_kn_w, fox_w_out, mlp_w1, mlp_w2, final_w):
    batch, seq, d = x.shape
    depth = w_mod.shape[0]
    t = batch * seq
    fox_heads = d // FOX_HEAD
    assert seq % HG_BLOCK == 0 and seq % ATTN_TQ == 0 and HG_BLOCK % PROJ_TM == 0 and PROJ_TM % ROW_SLAB == 0
    assert d % LANES == 0 and fox_heads * FOX_AUG <= LANES and (N_MOD * d) % MOD_TN == 0
    assert mlp_w1.shape[2] % POST_FF_BLK == 0 and x.dtype == F32

    mod3 = _mod_call(c, w_mod, b_mod).reshape(depth * batch, 1, N_MOD * d)
    xs = x.reshape(t, d)
    fw = final_w.reshape(1, d)
    w1_all, w2_all = mlp_w1.astype(BF16), mlp_w2.astype(BF16)

    for i in range(depth):
        j = i // 2
        n1 = norm1_w[i].reshape(1, d)
        if i % 2 == 0:
            q, k, lf, v, g, wmin = _hg_proj_call(xs, mod3, n1, hg_lb, hg_w_in[j].astype(BF16),
                                                 layer=i, batch=batch, tm=PROJ_TM)
            wmin = wmin.reshape(batch, seq // HG_BLOCK, HG_BLOCK // PROJ_TM, d // HG_HEAD, HG_HEAD).min(axis=(2, 4))
            y = _hg_rec_call(wmin.transpose(0, 2, 1).reshape(-1), q, k, lf, v, g,
                             hg_gn_w[j].reshape(1, HG_HEAD), batch=batch, tc=HG_BLOCK,
                             unroll_bounded=HG_GROUP, unroll_exact=HG_GROUP_EXACT)
            w_out = hg_w_out[j]
        else:
            pad = ((0, 0), (0, LANES - fox_heads))
            q, k, v, g, lf = _fox_proj_call(
                xs, mod3, n1,
                jnp.tile(fox_qn_w[j], fox_heads).reshape(1, d),
                jnp.tile(fox_kn_w[j], fox_heads).reshape(1, d),
                jnp.pad(fox_b_f[j].reshape(1, fox_heads), pad),
                fox_w_in[j].astype(BF16), jnp.pad(fox_w_in[j, :, 4 * d:], pad).astype(BF16),
                layer=i, batch=batch, tm=PROJ_TM)
            aq, ak, edge, flag = _fox_bias_call(lf, fox_qn_w[j].reshape(1, FOX_HEAD),
                                                fox_kn_w[j].reshape(1, FOX_HEAD), batch=batch, d=d, tp=ATTN_TK)
            edge = edge.reshape(batch, seq // ATTN_TK, 2, LANES)[..., :fox_heads].transpose(2, 0, 3, 1)
            y = _fox_attn_call(flag[0, :1], edge[0].reshape(-1), edge[1].reshape(-1), q, aq, k, ak, v, g,
                               batch=batch, seq=seq, tq=ATTN_TQ, tk=ATTN_TK)
            w_out = fox_w_out[j]
        xs = _post_call(xs, y, mod3, norm2_w[i].reshape(1, d), fw, w_out.astype(BF16), w1_all, w2_all,
                        layer=i, batch=batch, tm=PROJ_TM, final=(i == depth - 1))
    return xs.reshape(batch, seq, d)
```

```python
import functools

import numpy as np
import jax
import jax.numpy as jnp
from jax import lax
from jax.experimental import pallas as pl
from jax.experimental.pallas import tpu as pltpu

F32 = jnp.float32
BF16 = jnp.bfloat16
EPS = 1e-6
N_MOD = 6
HG_HEAD = 128
HG_CHUNK = 64
HG_SUB = 16
HG_SAFE_DECAY = 56.0
FOX_HEAD = 64
LANES = 128
SUBLANES = 8
LOG2E = float(np.log2(np.e))
FOX_STAB_MAX = 30.0
FOX_BOUND_SLACK = 1.01
FOX_SKIP = 160.0
ROW_SLAB = 256
VMEM_LIMIT = 56 * 1024 * 1024

PROJ_TM = 512
POST_FF_BLK = 1024
MOD_TN = 1024
HG_BLOCK = 2048
HG_GROUP = 32
HG_GROUP_EXACT = 4
ATTN_TK = 512
ATTN_TQ = 2 * ATTN_TK

_HI = lax.Precision.HIGHEST


def _cparams(sem):
    return pltpu.CompilerParams(dimension_semantics=sem, vmem_limit_bytes=VMEM_LIMIT)


def _resident(shape):
    nd = len(shape)
    return pl.BlockSpec(shape, lambda *_: (0,) * nd, pipeline_mode=pl.Buffered(1))


def _sigmoid(x):
    return 0.5 * jnp.tanh(0.5 * x) + 0.5


def _modulate(x, nw, shift, scale):
    ms = jnp.mean(x * x, axis=-1, keepdims=True)
    y = x * lax.rsqrt(ms + EPS)
    return (y * nw) * (1.0 + scale) + shift


def _mod_kernel(c_ref, w_ref, b_ref, o_ref):
    c = c_ref[...]
    ca = c * _sigmoid(c)
    o_ref[0] = jnp.dot(ca, w_ref[0], precision=_HI, preferred_element_type=F32) + b_ref[0]


def _mod_call(c, w_mod, b_mod):
    depth, d, n = w_mod.shape
    b = c.shape[0]
    tn = MOD_TN
    return pl.pallas_call(
        _mod_kernel,
        grid=(depth, n // tn),
        in_specs=[
            pl.BlockSpec((b, d), lambda l, j: (0, 0)),
            pl.BlockSpec((1, d, tn), lambda l, j: (l, 0, j)),
            pl.BlockSpec((1, 1, tn), lambda l, j: (l, 0, j)),
        ],
        out_specs=pl.BlockSpec((1, b, tn), lambda l, j: (l, 0, j)),
        out_shape=jax.ShapeDtypeStruct((depth, b, n), F32),
        compiler_params=_cparams(("arbitrary", "arbitrary")),
        name="mod",
    )(c, w_mod, b_mod.reshape(depth, 1, n))


def _hg_proj_kernel(x_ref, mod_ref, nw_ref, lb_ref, w_ref, q_ref, k_ref, lf_ref, v_ref, g_ref, wmin_ref, *,
                    layer, d):
    mod = mod_ref[0]
    lbp = lb_ref[...]
    e = jnp.exp(lbp - jnp.max(lbp, axis=0, keepdims=True))
    lb = jnp.sum(e[0:layer + 1], axis=0, keepdims=True) / jnp.sum(e, axis=0, keepdims=True)

    w_r = lax.broadcasted_iota(jnp.int32, (ROW_SLAB // HG_SUB, ROW_SLAB), 0)
    w_c = lax.broadcasted_iota(jnp.int32, (ROW_SLAB // HG_SUB, ROW_SLAB), 1)
    win = (lax.shift_right_logical(w_c, HG_SUB.bit_length() - 1) == w_r).astype(BF16)
    wmin = None

    for r0 in range(0, x_ref.shape[0], ROW_SLAB):
        rows = slice(r0, r0 + ROW_SLAB)
        h = _modulate(x_ref[rows, :], nw_ref[...], mod[:, 0:d], mod[:, d:2 * d]).astype(BF16)

        pq = jnp.dot(h, w_ref[:, 0:d], preferred_element_type=F32)
        q_ref[rows, :] = (pq * _sigmoid(pq)).astype(BF16)

        z = jnp.dot(h, w_ref[:, d:2 * d], preferred_element_type=F32)
        th = 0.5 * jnp.tanh(0.5 * z)
        kk = (1.0 - lb) * (0.5 - th)
        f = lb + (1.0 - lb) * (0.5 + th)
        lf = jnp.log(jnp.where(kk < 0.5, 1.0 - kk, f))
        lf_ref[rows, :] = lf
        k_ref[rows, :] = kk.astype(BF16)
        wsum = jnp.min(jnp.dot(win, lf.astype(BF16), preferred_element_type=F32), axis=0, keepdims=True)
        wmin = wsum if wmin is None else jnp.minimum(wmin, wsum)

        v_ref[rows, :] = jnp.dot(h, w_ref[:, 2 * d:3 * d], preferred_element_type=F32).astype(BF16)
        pg = jnp.dot(h, w_ref[:, 3 * d:4 * d], preferred_element_type=F32)
        g_ref[rows, :] = (pg * _sigmoid(pg)).astype(BF16)
    wmin_ref[0] = wmin


def _hg_proj_call(x2, mod3, nw, hg_lb, w_in, *, layer, batch, tm):
    t, d = x2.shape
    tiles_per_batch = (t // batch) // tm
    row = pl.BlockSpec((tm, d), lambda i: (i, 0))
    out_bf = jax.ShapeDtypeStruct((t, d), BF16)
    return pl.pallas_call(
        functools.partial(_hg_proj_kernel, layer=layer, d=d),
        grid=(t // tm,),
        in_specs=[
            row,
            pl.BlockSpec((1, 1, N_MOD * d), lambda i: (layer * batch + i // tiles_per_batch, 0, 0)),
            _resident((1, d)),
            _resident(hg_lb.shape),
            _resident(w_in.shape),
        ],
        out_specs=[row, row, row, row, row, pl.BlockSpec((1, 1, d), lambda i: (i, 0, 0))],
        out_shape=[out_bf, out_bf, jax.ShapeDtypeStruct((t, d), F32), out_bf, out_bf,
                   jax.ShapeDtypeStruct((t // tm, 1, d), F32)],
        compiler_params=_cparams(("arbitrary",)),
        name="hg_proj",
    )(x2, mod3, nw, hg_lb, w_in)


def _hg_rec_kernel(wmin_ref, q_ref, k_ref, lf_ref, v_ref, gnw_ref, o_ref, st_ref, gs_ref, ks_ref, *,
                   n_chunks, unroll_bounded, unroll_exact):
    c, sub = HG_CHUNK, HG_SUB
    n_sub = c // sub
    half = sub // 2

    @pl.when(pl.program_id(2) == 0)
    def _():
        st_ref[...] = jnp.zeros_like(st_ref)

    r_i = lax.broadcasted_iota(jnp.int32, (c, c), 0)
    c_i = lax.broadcasted_iota(jnp.int32, (c, c), 1)
    tri = (c_i <= r_i).astype(BF16)
    ones_w = jnp.ones((HG_HEAD, c), BF16)
    half_row = lax.broadcasted_iota(jnp.int32, (half, HG_HEAD), 0)
    a_lane = lax.broadcasted_iota(jnp.int32, (half, c), 1)
    gnw = gnw_ref[...]
    nt = (((1,), (1,)), ((), ()))
    tn = (((0,), (0,)), ((), ()))

    def offdiag_operands(q, k, gcum, j, bounded):
        lo = j * sub
        hi = lo + sub if bounded else lo
        g_b = gcum[lo - 1:lo, :] if j > 0 else jnp.zeros((1, HG_HEAD), F32)
        qh = (q[lo:lo + sub, :] * jnp.exp2(gcum[lo:lo + sub, :] - g_b)).astype(BF16)
        kh = (k[0:hi, :] * jnp.exp2(g_b - gcum[0:hi, :])).astype(BF16)
        if hi < c:
            kh = jnp.concatenate([kh, jnp.zeros((c - hi, HG_HEAD), BF16)], axis=0)
        return qh, kh

    def diag_products(q, k, gcum, slot, j):
        lo = j * sub
        gt = (gcum[lo:lo + half, :], gcum[lo + half:lo + sub, :])
        qt = (q[lo:lo + half, :], q[lo + half:lo + sub, :])
        ps = []
        for s in range(sub):
            g_s = gs_ref[slot, pl.ds(lo + s, 1), :]
            k_s = ks_ref[slot, pl.ds(lo + s, 1), :]
            for hf in range(2):
                if s >= half and hf == 0:
                    continue
                dlt = gt[hf] - g_s
                if (s >= half) == (hf == 1):
                    dlt = jnp.where(half_row >= s % half, dlt, -jnp.inf)
                ps.append((qt[hf] * k_s) * jnp.exp2(dlt))
        return jnp.concatenate(ps, axis=0).astype(BF16)

    def diag_scatter(r, j):
        lo = j * sub
        a_top = jnp.zeros((half, c), F32)
        a_bot = jnp.zeros((half, c), F32)
        for s in range(sub):
            if s < half:
                a_top = jnp.where(a_lane == lo + s, r[2 * s * half:(2 * s + 1) * half, :], a_top)
                a_bot = jnp.where(a_lane == lo + s, r[(2 * s + 1) * half:(2 * s + 2) * half, :], a_bot)
            else:
                a_bot = jnp.where(a_lane == lo + s, r[(half + s) * half:(half + s + 1) * half, :], a_bot)
        return jnp.concatenate([a_top, a_bot], axis=0)

    def run(bounded, unroll):
        def group(i, st):
            us = range(unroll)
            rows = [pl.ds(pl.multiple_of((i * unroll + u) * c, c), c) for u in us]
            g3 = [jnp.dot(tri, _split3(lf_ref[r, :]), preferred_element_type=F32) for r in rows]
            q = [q_ref[r, :].astype(F32) for r in rows]
            k = [k_ref[r, :].astype(F32) for r in rows]
            gcum = [(g[:, 0:HG_HEAD] + g[:, HG_HEAD:2 * HG_HEAD] + g[:, 2 * HG_HEAD:]) * LOG2E for g in g3]
            g_last = [g[c - 1:c, :] for g in gcum]
            qg = [(q[u] * jnp.exp2(gcum[u])).astype(BF16) for u in us]
            kd = [(k[u] * jnp.exp2(g_last[u] - gcum[u])).astype(BF16) for u in us]
            first = 0 if bounded else 1
            off = [[offdiag_operands(q[u], k[u], gcum[u], j, bounded) for j in range(first, n_sub)] for u in us]
            if not bounded:
                for u in us:
                    gs_ref[u] = gcum[u]
                    ks_ref[u] = k[u]
                pst = [[diag_products(q[u], k[u], gcum[u], u, j) for j in range(n_sub)] for u in us]
            kv = [lax.dot_general(v_ref[rows[u], :], kd[u], tn, preferred_element_type=F32) for u in us]
            blocks = [[lax.dot_general(qh, kh, nt, preferred_element_type=F32) for qh, kh in off[u]] for u in us]
            if bounded:
                a = [jnp.where(c_i <= r_i, jnp.concatenate(blocks[u], axis=0), 0.0) for u in us]
            else:
                sums = [[jnp.dot(p, ones_w, preferred_element_type=F32) for p in pst[u]] for u in us]
                a = []
                for u in us:
                    a_rows = [diag_scatter(sums[u][j], j) for j in range(n_sub)]
                    for j in range(1, n_sub):
                        a_rows[j] = a_rows[j] + blocks[u][j - 1]
                    a.append(jnp.concatenate(a_rows, axis=0))
            o_intra = [jnp.dot(a[u].astype(BF16), v_ref[rows[u], :], preferred_element_type=F32) for u in us]
            for u in us:
                o = o_intra[u] + lax.dot_general(qg[u], st.astype(BF16), nt, preferred_element_type=F32)
                st = st * jnp.exp2(g_last[u]) + kv[u]
                on = o * lax.rsqrt(jnp.mean(o * o, axis=-1, keepdims=True) + EPS) * gnw
                o_ref[rows[u], :] = on.astype(BF16)
            return st

        st_ref[...] = lax.fori_loop(0, n_chunks // unroll, group, st_ref[...])

    blk_id = (pl.program_id(0) * pl.num_programs(1) + pl.program_id(1)) * pl.num_programs(2) + pl.program_id(2)
    bounded = wmin_ref[blk_id] * LOG2E >= -HG_SAFE_DECAY

    @pl.when(bounded)
    def _():
        run(True, unroll_bounded)

    @pl.when(jnp.logical_not(bounded))
    def _():
        run(False, unroll_exact)


def _hg_rec_call(wmin, q, k, lf, v, gnw, *, batch, tc, unroll_bounded, unroll_exact):
    t, d = q.shape
    heads = d // HG_HEAD
    steps = (t // batch) // tc
    unroll = unroll_exact
    blk = pl.BlockSpec((tc, HG_HEAD), lambda b, h, s, *_: (b * steps + s, h))
    grid_spec = pltpu.PrefetchScalarGridSpec(
        num_scalar_prefetch=1,
        grid=(batch, heads, steps),
        in_specs=[blk, blk, blk, blk, pl.BlockSpec((1, HG_HEAD), lambda b, h, s, *_: (0, 0))],
        out_specs=blk,
        scratch_shapes=[
            pltpu.VMEM((HG_HEAD, HG_HEAD), F32),
            pltpu.VMEM((unroll, HG_CHUNK, HG_HEAD), F32),
            pltpu.VMEM((unroll, HG_CHUNK, HG_HEAD), F32),
        ],
    )
    return pl.pallas_call(
        functools.partial(_hg_rec_kernel, n_chunks=tc // HG_CHUNK, unroll_bounded=unroll_bounded,
                          unroll_exact=unroll_exact),
        grid_spec=grid_spec,
        out_shape=jax.ShapeDtypeStruct((t, d), BF16),
        compiler_params=_cparams(("arbitrary", "arbitrary", "arbitrary")),
        name="hg_rec",
    )(wmin, q, k, lf, v, gnw)


def _head_rmsnorm(p, w_row):
    low = lax.broadcasted_iota(jnp.int32, (1, LANES), 1) < FOX_HEAD
    outs = []
    for j in range(p.shape[1] // LANES):
        pj = p[:, j * LANES:(j + 1) * LANES]
        ss = pj * pj
        s_lo = jnp.sum(jnp.where(low, ss, 0.0), axis=-1, keepdims=True)
        s_hi = jnp.sum(jnp.where(low, 0.0, ss), axis=-1, keepdims=True)
        inv = lax.rsqrt(jnp.where(low, s_lo, s_hi) * (1.0 / FOX_HEAD) + EPS)
        outs.append(pj * inv)
    return jnp.concatenate(outs, axis=1) * w_row


def _fox_proj_kernel(x_ref, mod_ref, nw_ref, qw_ref, kw_ref, bf_ref, w_ref, wf_ref,
                     q_ref, k_ref, v_ref, g_ref, lf_ref, *, d):
    mod = mod_ref[0]
    scale = LOG2E / np.sqrt(FOX_HEAD)
    for r0 in range(0, x_ref.shape[0], ROW_SLAB):
        rows = slice(r0, r0 + ROW_SLAB)
        h = _modulate(x_ref[rows, :], nw_ref[...], mod[:, 0:d], mod[:, d:2 * d]).astype(BF16)
        pq = jnp.dot(h, w_ref[:, 0:d], preferred_element_type=F32)
        q_ref[rows, :] = (_head_rmsnorm(pq, qw_ref[...]) * scale).astype(BF16)
        pk = jnp.dot(h, w_ref[:, d:2 * d], preferred_element_type=F32)
        k_ref[rows, :] = _head_rmsnorm(pk, kw_ref[...]).astype(BF16)
        v_ref[rows, :] = jnp.dot(h, w_ref[:, 2 * d:3 * d], preferred_element_type=F32).astype(BF16)
        pg = jnp.dot(h, w_ref[:, 3 * d:4 * d], preferred_element_type=F32)
        g_ref[rows, :] = _sigmoid(pg).astype(BF16)
        u = jnp.dot(h, wf_ref[...], preferred_element_type=F32) + bf_ref[...]
        lf_ref[rows, :] = jnp.minimum(u, 0.0) - jnp.log1p(jnp.exp(-jnp.abs(u)))


def _fox_proj_call(x2, mod3, nw, qw, kw, bf, w_main, w_f, *, layer, batch, tm):
    t, d = x2.shape
    nh = w_f.shape[1]
    tiles_per_batch = (t // batch) // tm
    row = pl.BlockSpec((tm, d), lambda i: (i, 0))
    out_bf = jax.ShapeDtypeStruct((t, d), BF16)
    return pl.pallas_call(
        functools.partial(_fox_proj_kernel, d=d),
        grid=(t // tm,),
        in_specs=[
            row,
            pl.BlockSpec((1, 1, N_MOD * d), lambda i: (layer * batch + i // tiles_per_batch, 0, 0)),
            _resident((1, d)), _resident((1, d)), _resident((1, d)), _resident((1, nh)),
            _resident((d, 4 * d)),
            _resident(w_f.shape),
        ],
        out_specs=[row, row, row, row, pl.BlockSpec((tm, nh), lambda i: (i, 0))],
        out_shape=[out_bf, out_bf, out_bf, out_bf, jax.ShapeDtypeStruct((t, nh), F32)],
        compiler_params=_cparams(("arbitrary",)),
        name="fox_proj",
    )(x2, mod3, nw, qw, kw, bf, w_main, w_f)


FOX_AUG = 6


def _split3(x):
    hi = x.astype(BF16)
    r1 = x - hi.astype(F32)
    mid = r1.astype(BF16)
    lo = (r1 - mid.astype(F32)).astype(BF16)
    return jnp.concatenate([hi, mid, lo], axis=1)


def _fox_bias_kernel(lf_ref, qw_ref, kw_ref, pq_ref, pk_ref, oq_ref, ok_ref,
                     aq_ref, ak_ref, edge_ref, flag_ref, carry_ref):
    tp = lf_ref.shape[0]

    @pl.when(pl.program_id(1) == 0)
    def _():
        carry_ref[...] = jnp.zeros_like(carry_ref)

    bound = FOX_BOUND_SLACK * LOG2E * np.sqrt(FOX_HEAD) * jnp.max(jnp.abs(qw_ref[...] * kw_ref[...]), axis=-1, keepdims=True)
    fast = bound <= FOX_STAB_MAX
    flag_ref[...] = jnp.broadcast_to(fast.astype(jnp.int32), flag_ref.shape)
    stab = jnp.where(fast, bound, 0.0)

    r_i = lax.broadcasted_iota(jnp.int32, (tp, tp), 0)
    c_i = lax.broadcasted_iota(jnp.int32, (tp, tp), 1)
    tri = (c_i <= r_i).astype(BF16)
    f3 = jnp.dot(tri, _split3(lf_ref[...]), preferred_element_type=F32)
    f = f3[:, 0:LANES] + f3[:, LANES:2 * LANES] + f3[:, 2 * LANES:] + carry_ref[...]
    carry_ref[...] = f[tp - 1:tp, :]
    f2 = f * LOG2E
    edge_ref[0] = jnp.concatenate([f2[0:1, :], f2[tp - 1:tp, :]], axis=0)
    aq_ref[...] = jnp.dot(_split3(f2 - stab), pq_ref[...], preferred_element_type=F32) + oq_ref[...]
    ak_ref[...] = jnp.dot(_split3(f2), pk_ref[...], preferred_element_type=F32) + ok_ref[...]


def _fox_bias_call(lf, qw, kw, *, batch, d, tp):
    t = lf.shape[0]
    heads = d // FOX_HEAD
    steps = (t // batch) // tp
    pq = np.zeros((3 * LANES, LANES), np.float32)
    pk = np.zeros((3 * LANES, LANES), np.float32)
    oq = np.zeros((1, LANES), np.float32)
    ok = np.zeros((1, LANES), np.float32)
    for h in range(heads):
        base = FOX_AUG * h
        for i in range(3):
            pq[i * LANES + h, base + i] = 1.0
            ok[0, base + i] = 1.0
            oq[0, base + 3 + i] = 1.0
            pk[i * LANES + h, base + 3 + i] = -1.0
    row = pl.BlockSpec((tp, LANES), lambda b, s: (b * steps + s, 0))
    return pl.pallas_call(
        _fox_bias_kernel,
        grid=(batch, steps),
        in_specs=[
            row,
            _resident((1, FOX_HEAD)), _resident((1, FOX_HEAD)),
            _resident(pq.shape), _resident(pk.shape), _resident(oq.shape), _resident(ok.shape),
        ],
        out_specs=[row, row,
                   pl.BlockSpec((1, 2, LANES), lambda b, s: (b * steps + s, 0, 0)),
                   pl.BlockSpec((SUBLANES, LANES), lambda b, s: (0, 0))],
        out_shape=[jax.ShapeDtypeStruct((t, LANES), F32), jax.ShapeDtypeStruct((t, LANES), F32),
                   jax.ShapeDtypeStruct((batch * steps, 2, LANES), F32),
                   jax.ShapeDtypeStruct((SUBLANES, LANES), jnp.int32)],
        scratch_shapes=[pltpu.VMEM((1, LANES), F32)],
        compiler_params=_cparams(("arbitrary", "arbitrary")),
        name="fox_bias",
    )(lf, qw, kw, jnp.asarray(pq, BF16), jnp.asarray(pk, BF16), jnp.asarray(oq), jnp.asarray(ok))


FOX_VROWS = 80


def _fox_attn_kernel(flag_ref, fs_ref, fe_ref, q_ref, aq_ref, k_ref, ak_ref, v_ref, o_ref,
                     v1_ref, ka_ref, qm_ref, acc_ref, m_ref, pa_ref, pb_ref, *, tq, tk, heads):
    bi, pair, qi = pl.program_id(0), pl.program_id(1), pl.program_id(2)
    nk = k_ref.shape[0] // tk
    sub = tq // tk
    pad_rows = lax.broadcasted_iota(jnp.int32, (FOX_VROWS - FOX_HEAD, tk), 0)
    ones_blk = jnp.where(pad_rows == 0, 1.0, 0.0).astype(BF16)
    lane2 = lax.broadcasted_iota(jnp.int32, (1, 2 * LANES), 1)
    head0 = (lane2 < FOX_HEAD) | ((lane2 >= LANES) & (lane2 < LANES + FOX_AUG))
    head1 = ((lane2 >= FOX_HEAD) & (lane2 < LANES)) | ((lane2 >= LANES + FOX_AUG) & (lane2 < LANES + 2 * FOX_AUG))
    lane1 = lax.broadcasted_iota(jnp.int32, (1, LANES), 1)
    low = lane1 < FOX_HEAD
    nt = (((1,), (1,)), ((), ()))

    def bias_lanes(a):
        shifted = pltpu.roll(a, (LANES - 2 * FOX_AUG * pair) % LANES, axis=1)
        return jnp.where(lane1 < 2 * FOX_AUG, shifted, 0.0).astype(BF16)

    @pl.when(qi == 0)
    def _():
        def build(j, carry):
            rows = pl.ds(pl.multiple_of(j * tk, tk), tk)
            ka_ref[rows, :] = bias_lanes(ak_ref[rows, :])
            vt = v_ref[rows, :].astype(F32).T
            for e in range(2):
                v1_ref[j, e] = jnp.concatenate([vt[e * FOX_HEAD:(e + 1) * FOX_HEAD, :].astype(BF16), ones_blk],
                                               axis=0)
            return carry

        lax.fori_loop(0, nk, build, 0)

    def probs(j, online, diag=None, dst=None):
        rows = pl.ds(pl.multiple_of(j * tk, tk), tk)
        kf = jnp.concatenate([k_ref[rows, :], ka_ref[rows, :]], axis=1)
        q_lo = 0 if diag is None else diag * tk
        ps, alphas = [], []
        for e in range(2):
            s = lax.dot_general(kf, qm_ref[e, q_lo:tq, :], nt, preferred_element_type=F32)
            if diag is not None:
                kpos = lax.broadcasted_iota(jnp.int32, (tk, tk), 0)
                qpos = lax.broadcasted_iota(jnp.int32, (tk, tk), 1)
                blk = jnp.where(kpos <= qpos, s[:, 0:tk], -jnp.inf)
                s = blk if q_lo + tk == tq else jnp.concatenate([blk, s[:, tk:]], axis=1)
            if online:
                m_prev = m_ref[e, :, q_lo:tq]
                m_new = jnp.maximum(m_prev, jnp.max(s, axis=0, keepdims=True))
                alphas.append(jnp.exp2(m_prev - m_new))
                m_ref[e, :, q_lo:tq] = m_new
                s = s - m_new
            p = jnp.exp2(s).astype(BF16)
            if dst is None:
                ps.append(p)
            else:
                dst[e] = p
        return ps, alphas

    def accumulate(ps, j, alphas=None, diag=None):
        q_lo = 0 if diag is None else diag * tk
        for e in range(2):
            pv = jnp.dot(v1_ref[j, e], ps[e], preferred_element_type=F32)
            if alphas:
                acc_ref[e, :, q_lo:tq] = acc_ref[e, :, q_lo:tq] * alphas[e] + pv
            else:
                acc_ref[e, :, q_lo:tq] += pv

    def start():
        qf = jnp.concatenate([q_ref[...], bias_lanes(aq_ref[...])], axis=1)
        zero = jnp.zeros_like(qf)
        qm_ref[0] = jnp.where(head0, qf, zero)
        qm_ref[1] = jnp.where(head1, qf, zero)
        acc_ref[...] = jnp.zeros_like(acc_ref)

    def finish():
        outs = []
        for e in range(2):
            acc = acc_ref[e]
            outs.append(acc[0:FOX_HEAD, :] / acc[FOX_HEAD:FOX_HEAD + 1, :])
        o_ref[...] = jnp.concatenate(outs, axis=0).astype(BF16)

    n_full = qi * sub

    @pl.when(flag_ref[0] == 1)
    def _():
        start()
        held = None
        for a in range(sub - 1, 0, -1):
            nxt = probs(n_full + a, False, diag=a)[0]
            if held is not None:
                accumulate(held[0], n_full + held[1], diag=held[1])
            held = (nxt, a)
        probs(n_full, False, diag=0, dst=pa_ref)
        if held is not None:
            accumulate(held[0], n_full + held[1], diag=held[1])

        h0 = bi * heads + 2 * pair
        fq0 = fs_ref[h0 * nk + n_full]
        fq1 = fs_ref[(h0 + 1) * nk + n_full]

        def dead(j, cnt):
            gone0 = fq0 - fe_ref[h0 * nk + j] < -FOX_SKIP
            gone1 = fq1 - fe_ref[(h0 + 1) * nk + j] < -FOX_SKIP
            return cnt + jnp.logical_and(gone0, gone1).astype(jnp.int32)

        first = lax.fori_loop(0, n_full, dead, 0)
        n_live = n_full - first

        def two_tiles(i, carry):
            j = first + 2 * i
            probs(j, False, dst=pb_ref)
            accumulate(pa_ref, jnp.where(i == 0, n_full, j - 1))
            probs(j + 1, False, dst=pa_ref)
            accumulate(pb_ref, j)
            return carry

        lax.fori_loop(0, n_live // 2, two_tiles, 0)

        @pl.when(n_live % 2 == 1)
        def _():
            ps, _ = probs(n_full - 1, False)
            accumulate(pa_ref, jnp.where(n_live == 1, n_full, n_full - 2))
            accumulate(ps, n_full - 1)

        @pl.when(n_live % 2 == 0)
        def _():
            accumulate(pa_ref, jnp.where(n_live == 0, n_full, n_full - 1))

        finish()

    @pl.when(flag_ref[0] != 1)
    def _():
        start()
        m_ref[...] = jnp.full_like(m_ref, -jnp.inf)

        def body(j, carry):
            ps, alphas = probs(j, True)
            accumulate(ps, j, alphas)
            return carry

        lax.fori_loop(0, n_full, body, 0)
        for a in range(sub):
            ps, alphas = probs(n_full + a, True, diag=a)
            accumulate(ps, n_full + a, alphas, diag=a)
        finish()


def _fox_attn_call(flag, f_first, f_last, q, aq, k, ak, v, *, batch, seq, tq, tk):
    t, d = q.shape
    pairs = d // LANES
    nq = seq // tq
    q_spec = pl.BlockSpec((tq, LANES), lambda b, p, i, *_: (b * nq + i, p))
    kv_spec = pl.BlockSpec((seq, LANES), lambda b, p, i, *_: (b, p))
    grid_spec = pltpu.PrefetchScalarGridSpec(
        num_scalar_prefetch=3,
        grid=(batch, pairs, nq),
        in_specs=[q_spec, pl.BlockSpec((tq, LANES), lambda b, p, i, *_: (b * nq + i, 0)),
                  kv_spec, pl.BlockSpec((seq, LANES), lambda b, p, i, *_: (b, 0)),
                  kv_spec],
        out_specs=pl.BlockSpec((LANES, tq), lambda b, p, i, *_: (p, b * nq + i)),
        scratch_shapes=[
            pltpu.VMEM((seq // tk, 2, FOX_VROWS, tk), BF16),
            pltpu.VMEM((seq, LANES), BF16),
            pltpu.VMEM((2, tq, 2 * LANES), BF16),
            pltpu.VMEM((2, FOX_VROWS, tq), F32),
            pltpu.VMEM((2, 1, tq), F32),
            pltpu.VMEM((2, tk, tq), BF16),
            pltpu.VMEM((2, tk, tq), BF16),
        ],
    )
    return pl.pallas_call(
        functools.partial(_fox_attn_kernel, tq=tq, tk=tk, heads=d // FOX_HEAD),
        grid_spec=grid_spec,
        out_shape=jax.ShapeDtypeStruct((d, t), BF16),
        compiler_params=_cparams(("arbitrary", "arbitrary", "arbitrary")),
        name="fox_attn",
    )(flag, f_first, f_last, q, aq, k, ak, v)


def _post_kernel(x_ref, y_ref, gate_ref, mod_ref, nw_ref, fw_ref, wo_ref, w1_ref, w2_ref, o_ref, *,
                 d, ff_blk, final, y_features):
    mod = mod_ref[0]
    g1 = mod[:, 2 * d:3 * d]
    sh2, sc2, g2 = mod[:, 3 * d:4 * d], mod[:, 4 * d:5 * d], mod[:, 5 * d:6 * d]
    y = y_ref[...].astype(F32)
    if y_features:
        y = y.T
    yg = (y * gate_ref[...].astype(F32)).astype(BF16)
    x1 = x_ref[...] + g1 * jnp.dot(yg, wo_ref[...], preferred_element_type=F32)
    h = _modulate(x1, nw_ref[...], sh2, sc2).astype(BF16)
    acc = jnp.zeros_like(x1)
    for j in range(w1_ref.shape[2] // ff_blk):
        a = jnp.maximum(jnp.dot(h, w1_ref[0, :, j * ff_blk:(j + 1) * ff_blk], preferred_element_type=F32), 0.0)
        acc = acc + jnp.dot((a * a).astype(BF16), w2_ref[0, j * ff_blk:(j + 1) * ff_blk, :],
                            preferred_element_type=F32)
    x2 = x1 + g2 * acc
    if final:
        x2 = x2 * lax.rsqrt(jnp.mean(x2 * x2, axis=-1, keepdims=True) + EPS) * fw_ref[...]
    o_ref[...] = x2


def _post_call(x2, y, gate, mod3, nw, fw, w_out, w1, w2, *, layer, batch, tm, final, y_features):
    t, d = x2.shape
    tiles_per_batch = (t // batch) // tm
    row = pl.BlockSpec((tm, d), lambda i: (i, 0))
    assert y.shape == ((d, t) if y_features else (t, d))
    return pl.pallas_call(
        functools.partial(_post_kernel, d=d, ff_blk=POST_FF_BLK, final=final, y_features=y_features),
        grid=(t // tm,),
        in_specs=[
            row,
            pl.BlockSpec((d, tm), lambda i: (0, i)) if y_features else row,
            row,
            pl.BlockSpec((1, 1, N_MOD * d), lambda i: (layer * batch + i // tiles_per_batch, 0, 0)),
            _resident((1, d)), _resident((1, d)),
            _resident(w_out.shape),
            pl.BlockSpec((1,) + w1.shape[1:], lambda i: (layer, 0, 0), pipeline_mode=pl.Buffered(1)),
            pl.BlockSpec((1,) + w2.shape[1:], lambda i: (layer, 0, 0), pipeline_mode=pl.Buffered(1)),
        ],
        out_specs=row,
        out_shape=jax.ShapeDtypeStruct((t, d), F32),
        compiler_params=_cparams(("arbitrary",)),
        name="post",
    )(x2, y, gate, mod3, nw, fw, w_out, w1, w2)


def kernel(x, c, w_mod, b_mod, norm1_w, norm2_w, hg_w_in, hg_w_out, hg_lb, hg_gn_w, fox_w_in, fox_b_f,
           fox_qn_w, fox_kn_w, fox_w_out, mlp_w1, mlp_w2, final_w):
    batch, seq, d = x.shape
    depth = w_mod.shape[0]
    t = batch * seq
    fox_heads = d // FOX_HEAD
    assert seq % HG_BLOCK == 0 and seq % ATTN_TQ == 0 and HG_BLOCK % PROJ_TM == 0 and PROJ_TM % ROW_SLAB == 0
    assert d % LANES == 0 and fox_heads * FOX_AUG <= LANES and (N_MOD * d) % MOD_TN == 0
    assert mlp_w1.shape[2] % POST_FF_BLK == 0 and x.dtype == F32

    mod3 = _mod_call(c, w_mod, b_mod).reshape(depth * batch, 1, N_MOD * d)
    xs = x.reshape(t, d)
    fw = final_w.reshape(1, d)
    w1_all, w2_all = mlp_w1.astype(BF16), mlp_w2.astype(BF16)

    for i in range(depth):
        j = i // 2
        n1 = norm1_w[i].reshape(1, d)
        if i % 2 == 0:
            q, k, lf, v, g, wmin = _hg_proj_call(xs, mod3, n1, hg_lb, hg_w_in[j].astype(BF16),
                                                 layer=i, batch=batch, tm=PROJ_TM)
            wmin = wmin.reshape(batch, seq // HG_BLOCK, HG_BLOCK // PROJ_TM, d // HG_HEAD, HG_HEAD).min(axis=(2, 4))
            y = _hg_rec_call(wmin.transpose(0, 2, 1).reshape(-1), q, k, lf, v,
                             hg_gn_w[j].reshape(1, HG_HEAD), batch=batch, tc=HG_BLOCK,
                             unroll_bounded=HG_GROUP, unroll_exact=HG_GROUP_EXACT)
            w_out = hg_w_out[j]
        else:
            pad = ((0, 0), (0, LANES - fox_heads))
            q, k, v, g, lf = _fox_proj_call(
                xs, mod3, n1,
                jnp.tile(fox_qn_w[j], fox_heads).reshape(1, d),
                jnp.tile(fox_kn_w[j], fox_heads).reshape(1, d),
                jnp.pad(fox_b_f[j].reshape(1, fox_heads), pad),
                fox_w_in[j].astype(BF16), jnp.pad(fox_w_in[j, :, 4 * d:], pad).astype(BF16),
                layer=i, batch=batch, tm=PROJ_TM)
            aq, ak, edge, flag = _fox_bias_call(lf, fox_qn_w[j].reshape(1, FOX_HEAD),
                                                fox_kn_w[j].reshape(1, FOX_HEAD), batch=batch, d=d, tp=ATTN_TK)
            edge = edge.reshape(batch, seq // ATTN_TK, 2, LANES)[..., :fox_heads].transpose(2, 0, 3, 1)
            y = _fox_attn_call(flag[0, :1], edge[0].reshape(-1), edge[1].reshape(-1), q, aq, k, ak, v,
                               batch=batch, seq=seq, tq=ATTN_TQ, tk=ATTN_TK)
            w_out = fox_w_out[j]
        xs = _post_call(xs, y, g, mod3, norm2_w[i].reshape(1, d), fw, w_out.astype(BF16), w1_all, w2_all,
                        layer=i, batch=batch, tm=PROJ_TM, final=(i == depth - 1), y_features=(i % 2 == 1))
    return xs.reshape(batch, seq, d)
```

```python
import functools

import numpy as np
import jax
import jax.numpy as jnp
from jax import lax
from jax.experimental import pallas as pl
from jax.experimental.pallas import tpu as pltpu

F32 = jnp.float32
BF16 = jnp.bfloat16
EPS = 1e-6
N_MOD = 6
HG_HEAD = 128
HG_CHUNK = 64
HG_SUB = 16
HG_SAFE_DECAY = 56.0
FOX_HEAD = 64
LANES = 128
SUBLANES = 8
LOG2E = float(np.log2(np.e))
FOX_STAB_MAX = 30.0
FOX_BOUND_SLACK = 1.01
FOX_SKIP = 160.0
ROW_SLAB = 256
VMEM_LIMIT = 56 * 1024 * 1024

PROJ_TM = 512
POST_FF_BLK = 1024
MOD_TN = 1024
HG_BLOCK = 2048
HG_GROUP = 32
HG_GROUP_EXACT = 4
ATTN_TK = 512
ATTN_TQ = 2 * ATTN_TK

_HI = lax.Precision.HIGHEST


def _cparams(sem):
    return pltpu.CompilerParams(dimension_semantics=sem, vmem_limit_bytes=VMEM_LIMIT)


def _resident(shape):
    nd = len(shape)
    return pl.BlockSpec(shape, lambda *_: (0,) * nd, pipeline_mode=pl.Buffered(1))


def _sigmoid(x):
    return 0.5 * jnp.tanh(0.5 * x) + 0.5


def _modulate(x, nw, shift, scale):
    ms = jnp.mean(x * x, axis=-1, keepdims=True)
    y = x * lax.rsqrt(ms + EPS)
    return (y * nw) * (1.0 + scale) + shift


def _mod_kernel(ct_ref, w_ref, b_ref, o_ref):
    ct = ct_ref[...]
    cat = ct * _sigmoid(ct)
    w = w_ref[0]
    rows = []
    for b in range(ct.shape[1]):
        col = jnp.broadcast_to(cat[:, b:b + 1], (ct.shape[0], LANES))
        rows.append(jnp.concatenate(
            [jnp.sum(w[:, j * LANES:(j + 1) * LANES] * col, axis=0, keepdims=True)
             for j in range(w.shape[1] // LANES)], axis=1))
    o_ref[0] = jnp.concatenate(rows, axis=0) + b_ref[0]


def _mod_call(c, w_mod, b_mod):
    depth, d, n = w_mod.shape
    b = c.shape[0]
    tn = MOD_TN
    return pl.pallas_call(
        _mod_kernel,
        grid=(depth, n // tn),
        in_specs=[
            pl.BlockSpec((d, b), lambda l, j: (0, 0)),
            pl.BlockSpec((1, d, tn), lambda l, j: (l, 0, j)),
            pl.BlockSpec((1, 1, tn), lambda l, j: (l, 0, j)),
        ],
        out_specs=pl.BlockSpec((1, b, tn), lambda l, j: (l, 0, j)),
        out_shape=jax.ShapeDtypeStruct((depth, b, n), F32),
        compiler_params=_cparams(("arbitrary", "arbitrary")),
        name="mod",
    )(c.T, w_mod, b_mod.reshape(depth, 1, n))


def _hg_proj_kernel(x_ref, mod_ref, nw_ref, lb_ref, w_ref, q_ref, k_ref, lf_ref, v_ref, g_ref, wmin_ref, *,
                    layer, d):
    mod = mod_ref[0]
    lbp = lb_ref[...]
    e = jnp.exp(lbp - jnp.max(lbp, axis=0, keepdims=True))
    lb = jnp.sum(e[0:layer + 1], axis=0, keepdims=True) / jnp.sum(e, axis=0, keepdims=True)

    w_r = lax.broadcasted_iota(jnp.int32, (ROW_SLAB // HG_SUB, ROW_SLAB), 0)
    w_c = lax.broadcasted_iota(jnp.int32, (ROW_SLAB // HG_SUB, ROW_SLAB), 1)
    win = (lax.shift_right_logical(w_c, HG_SUB.bit_length() - 1) == w_r).astype(BF16)
    wmin = None

    for r0 in range(0, x_ref.shape[0], ROW_SLAB):
        rows = slice(r0, r0 + ROW_SLAB)
        h = _modulate(x_ref[rows, :], nw_ref[...], mod[:, 0:d], mod[:, d:2 * d]).astype(BF16)

        pq = jnp.dot(h, w_ref[:, 0:d], preferred_element_type=F32)
        q_ref[rows, :] = (pq * _sigmoid(pq)).astype(BF16)

        z = jnp.dot(h, w_ref[:, d:2 * d], preferred_element_type=F32)
        th = 0.5 * jnp.tanh(0.5 * z)
        kk = (1.0 - lb) * (0.5 - th)
        f = lb + (1.0 - lb) * (0.5 + th)
        lf = jnp.log(jnp.where(kk < 0.5, 1.0 - kk, f))
        lf_ref[rows, :] = lf
        k_ref[rows, :] = kk.astype(BF16)
        wsum = jnp.min(jnp.dot(win, lf.astype(BF16), preferred_element_type=F32), axis=0, keepdims=True)
        wmin = wsum if wmin is None else jnp.minimum(wmin, wsum)

        v_ref[rows, :] = jnp.dot(h, w_ref[:, 2 * d:3 * d], preferred_element_type=F32).astype(BF16)
        pg = jnp.dot(h, w_ref[:, 3 * d:4 * d], preferred_element_type=F32)
        g_ref[rows, :] = (pg * _sigmoid(pg)).astype(BF16)
    wmin_ref[0] = wmin


def _hg_proj_call(x2, mod3, nw, hg_lb, w_in, *, layer, batch, tm):
    t, d = x2.shape
    tiles_per_batch = (t // batch) // tm
    row = pl.BlockSpec((tm, d), lambda i: (i, 0))
    out_bf = jax.ShapeDtypeStruct((t, d), BF16)
    return pl.pallas_call(
        functools.partial(_hg_proj_kernel, layer=layer, d=d),
        grid=(t // tm,),
        in_specs=[
            row,
            pl.BlockSpec((1, 1, N_MOD * d), lambda i: (layer * batch + i // tiles_per_batch, 0, 0)),
            _resident((1, d)),
            _resident(hg_lb.shape),
            _resident(w_in.shape),
        ],
        out_specs=[row, row, row, row, row, pl.BlockSpec((1, 1, d), lambda i: (i, 0, 0))],
        out_shape=[out_bf, out_bf, jax.ShapeDtypeStruct((t, d), F32), out_bf, out_bf,
                   jax.ShapeDtypeStruct((t // tm, 1, d), F32)],
        compiler_params=_cparams(("arbitrary",)),
        name="hg_proj",
    )(x2, mod3, nw, hg_lb, w_in)


def _hg_rec_kernel(wmin_ref, q_ref, k_ref, lf_ref, v_ref, gnw_ref, o_ref, st_ref, gs_ref, ks_ref, *,
                   n_chunks, unroll_bounded, unroll_exact):
    c, sub = HG_CHUNK, HG_SUB
    n_sub = c // sub
    half = sub // 2

    @pl.when(pl.program_id(2) == 0)
    def _():
        st_ref[...] = jnp.zeros_like(st_ref)

    r_i = lax.broadcasted_iota(jnp.int32, (c, c), 0)
    c_i = lax.broadcasted_iota(jnp.int32, (c, c), 1)
    tri = (c_i <= r_i).astype(BF16)
    ones_w = jnp.ones((HG_HEAD, c), BF16)
    half_row = lax.broadcasted_iota(jnp.int32, (half, HG_HEAD), 0)
    a_lane = lax.broadcasted_iota(jnp.int32, (half, c), 1)
    gnw = gnw_ref[...]
    nt = (((1,), (1,)), ((), ()))
    tn = (((0,), (0,)), ((), ()))

    def offdiag_operands(q, k, gcum, j, bounded):
        lo = j * sub
        hi = lo + sub if bounded else lo
        g_b = gcum[lo - 1:lo, :] if j > 0 else jnp.zeros((1, HG_HEAD), F32)
        qh = (q[lo:lo + sub, :] * jnp.exp2(gcum[lo:lo + sub, :] - g_b)).astype(BF16)
        kh = (k[0:hi, :] * jnp.exp2(g_b - gcum[0:hi, :])).astype(BF16)
        if hi < c:
            kh = jnp.concatenate([kh, jnp.zeros((c - hi, HG_HEAD), BF16)], axis=0)
        return qh, kh

    def diag_products(q, k, gcum, slot, j):
        lo = j * sub
        gt = (gcum[lo:lo + half, :], gcum[lo + half:lo + sub, :])
        qt = (q[lo:lo + half, :], q[lo + half:lo + sub, :])
        ps = []
        for s in range(sub):
            g_s = gs_ref[slot, pl.ds(lo + s, 1), :]
            k_s = ks_ref[slot, pl.ds(lo + s, 1), :]
            for hf in range(2):
                if s >= half and hf == 0:
                    continue
                dlt = gt[hf] - g_s
                if (s >= half) == (hf == 1):
                    dlt = jnp.where(half_row >= s % half, dlt, -jnp.inf)
                ps.append((qt[hf] * k_s) * jnp.exp2(dlt))
        return jnp.concatenate(ps, axis=0).astype(BF16)

    def diag_scatter(r, j):
        lo = j * sub
        a_top = jnp.zeros((half, c), F32)
        a_bot = jnp.zeros((half, c), F32)
        for s in range(sub):
            if s < half:
                a_top = jnp.where(a_lane == lo + s, r[2 * s * half:(2 * s + 1) * half, :], a_top)
                a_bot = jnp.where(a_lane == lo + s, r[(2 * s + 1) * half:(2 * s + 2) * half, :], a_bot)
            else:
                a_bot = jnp.where(a_lane == lo + s, r[(half + s) * half:(half + s + 1) * half, :], a_bot)
        return jnp.concatenate([a_top, a_bot], axis=0)

    def run(bounded, unroll):
        def group(i, st):
            us = range(unroll)
            rows = [pl.ds(pl.multiple_of((i * unroll + u) * c, c), c) for u in us]
            g3 = [jnp.dot(tri, _split3(lf_ref[r, :]), preferred_element_type=F32) for r in rows]
            q = [q_ref[r, :].astype(F32) for r in rows]
            k = [k_ref[r, :].astype(F32) for r in rows]
            gcum = [(g[:, 0:HG_HEAD] + g[:, HG_HEAD:2 * HG_HEAD] + g[:, 2 * HG_HEAD:]) * LOG2E for g in g3]
            g_last = [g[c - 1:c, :] for g in gcum]
            qg = [(q[u] * jnp.exp2(gcum[u])).astype(BF16) for u in us]
            kd = [(k[u] * jnp.exp2(g_last[u] - gcum[u])).astype(BF16) for u in us]
            first = 0 if bounded else 1
            off = [[offdiag_operands(q[u], k[u], gcum[u], j, bounded) for j in range(first, n_sub)] for u in us]
            if not bounded:
                for u in us:
                    gs_ref[u] = gcum[u]
                    ks_ref[u] = k[u]
                pst = [[diag_products(q[u], k[u], gcum[u], u, j) for j in range(n_sub)] for u in us]
            kv = [lax.dot_general(v_ref[rows[u], :], kd[u], tn, preferred_element_type=F32) for u in us]
            blocks = [[lax.dot_general(qh, kh, nt, preferred_element_type=F32) for qh, kh in off[u]] for u in us]
            if bounded:
                a = [jnp.where(c_i <= r_i, jnp.concatenate(blocks[u], axis=0), 0.0) for u in us]
            else:
                sums = [[jnp.dot(p, ones_w, preferred_element_type=F32) for p in pst[u]] for u in us]
                a = []
                for u in us:
                    a_rows = [diag_scatter(sums[u][j], j) for j in range(n_sub)]
                    for j in range(1, n_sub):
                        a_rows[j] = a_rows[j] + blocks[u][j - 1]
                    a.append(jnp.concatenate(a_rows, axis=0))
            o_intra = [jnp.dot(a[u].astype(BF16), v_ref[rows[u], :], preferred_element_type=F32) for u in us]
            for u in us:
                o = o_intra[u] + lax.dot_general(qg[u], st.astype(BF16), nt, preferred_element_type=F32)
                st = st * jnp.exp2(g_last[u]) + kv[u]
                on = o * lax.rsqrt(jnp.mean(o * o, axis=-1, keepdims=True) + EPS) * gnw
                o_ref[rows[u], :] = on.astype(BF16)
            return st

        st_ref[...] = lax.fori_loop(0, n_chunks // unroll, group, st_ref[...])

    blk_id = (pl.program_id(0) * pl.num_programs(1) + pl.program_id(1)) * pl.num_programs(2) + pl.program_id(2)
    bounded = wmin_ref[blk_id] * LOG2E >= -HG_SAFE_DECAY

    @pl.when(bounded)
    def _():
        run(True, unroll_bounded)

    @pl.when(jnp.logical_not(bounded))
    def _():
        run(False, unroll_exact)


def _hg_rec_call(wmin, q, k, lf, v, gnw, *, batch, tc, unroll_bounded, unroll_exact):
    t, d = q.shape
    heads = d // HG_HEAD
    steps = (t // batch) // tc
    unroll = unroll_exact
    blk = pl.BlockSpec((tc, HG_HEAD), lambda b, h, s, *_: (b * steps + s, h))
    grid_spec = pltpu.PrefetchScalarGridSpec(
        num_scalar_prefetch=1,
        grid=(batch, heads, steps),
        in_specs=[blk, blk, blk, blk, pl.BlockSpec((1, HG_HEAD), lambda b, h, s, *_: (0, 0))],
        out_specs=blk,
        scratch_shapes=[
            pltpu.VMEM((HG_HEAD, HG_HEAD), F32),
            pltpu.VMEM((unroll, HG_CHUNK, HG_HEAD), F32),
            pltpu.VMEM((unroll, HG_CHUNK, HG_HEAD), F32),
        ],
    )
    return pl.pallas_call(
        functools.partial(_hg_rec_kernel, n_chunks=tc // HG_CHUNK, unroll_bounded=unroll_bounded,
                          unroll_exact=unroll_exact),
        grid_spec=grid_spec,
        out_shape=jax.ShapeDtypeStruct((t, d), BF16),
        compiler_params=_cparams(("arbitrary", "arbitrary", "arbitrary")),
        name="hg_rec",
    )(wmin, q, k, lf, v, gnw)


def _head_rmsnorm(p, w_row):
    low = lax.broadcasted_iota(jnp.int32, (1, LANES), 1) < FOX_HEAD
    outs = []
    for j in range(p.shape[1] // LANES):
        pj = p[:, j * LANES:(j + 1) * LANES]
        ss = pj * pj
        s_lo = jnp.sum(jnp.where(low, ss, 0.0), axis=-1, keepdims=True)
        s_hi = jnp.sum(jnp.where(low, 0.0, ss), axis=-1, keepdims=True)
        inv = lax.rsqrt(jnp.where(low, s_lo, s_hi) * (1.0 / FOX_HEAD) + EPS)
        outs.append(pj * inv)
    return jnp.concatenate(outs, axis=1) * w_row


def _fox_proj_kernel(x_ref, mod_ref, nw_ref, qw_ref, kw_ref, bf_ref, w_ref, wf_ref,
                     q_ref, k_ref, v_ref, g_ref, lf_ref, *, d):
    mod = mod_ref[0]
    scale = LOG2E / np.sqrt(FOX_HEAD)
    for r0 in range(0, x_ref.shape[0], ROW_SLAB):
        rows = slice(r0, r0 + ROW_SLAB)
        h = _modulate(x_ref[rows, :], nw_ref[...], mod[:, 0:d], mod[:, d:2 * d]).astype(BF16)
        pq = jnp.dot(h, w_ref[:, 0:d], preferred_element_type=F32)
        q_ref[rows, :] = (_head_rmsnorm(pq, qw_ref[...]) * scale).astype(BF16)
        pk = jnp.dot(h, w_ref[:, d:2 * d], preferred_element_type=F32)
        k_ref[rows, :] = _head_rmsnorm(pk, kw_ref[...]).astype(BF16)
        v_ref[rows, :] = jnp.dot(h, w_ref[:, 2 * d:3 * d], preferred_element_type=F32).astype(BF16)
        pg = jnp.dot(h, w_ref[:, 3 * d:4 * d], preferred_element_type=F32)
        g_ref[rows, :] = _sigmoid(pg).astype(BF16)
        u = jnp.dot(h, wf_ref[...], preferred_element_type=F32) + bf_ref[...]
        lf_ref[rows, :] = jnp.minimum(u, 0.0) - jnp.log1p(jnp.exp(-jnp.abs(u)))


def _fox_proj_call(x2, mod3, nw, qw, kw, bf, w_main, w_f, *, layer, batch, tm):
    t, d = x2.shape
    nh = w_f.shape[1]
    tiles_per_batch = (t // batch) // tm
    row = pl.BlockSpec((tm, d), lambda i: (i, 0))
    out_bf = jax.ShapeDtypeStruct((t, d), BF16)
    return pl.pallas_call(
        functools.partial(_fox_proj_kernel, d=d),
        grid=(t // tm,),
        in_specs=[
            row,
            pl.BlockSpec((1, 1, N_MOD * d), lambda i: (layer * batch + i // tiles_per_batch, 0, 0)),
            _resident((1, d)), _resident((1, d)), _resident((1, d)), _resident((1, nh)),
            _resident((d, 4 * d)),
            _resident(w_f.shape),
        ],
        out_specs=[row, row, row, row, pl.BlockSpec((tm, nh), lambda i: (i, 0))],
        out_shape=[out_bf, out_bf, out_bf, out_bf, jax.ShapeDtypeStruct((t, nh), F32)],
        compiler_params=_cparams(("arbitrary",)),
        name="fox_proj",
    )(x2, mod3, nw, qw, kw, bf, w_main, w_f)


FOX_AUG = 6


def _split3(x):
    hi = x.astype(BF16)
    r1 = x - hi.astype(F32)
    mid = r1.astype(BF16)
    lo = (r1 - mid.astype(F32)).astype(BF16)
    return jnp.concatenate([hi, mid, lo], axis=1)


def _fox_bias_kernel(lf_ref, qw_ref, kw_ref, pq_ref, pk_ref, oq_ref, ok_ref,
                     aq_ref, ak_ref, edge_ref, flag_ref, *, tp):
    bound = FOX_BOUND_SLACK * LOG2E * np.sqrt(FOX_HEAD) * jnp.max(jnp.abs(qw_ref[...] * kw_ref[...]), axis=-1, keepdims=True)
    fast = bound <= FOX_STAB_MAX
    flag_ref[...] = jnp.broadcast_to(fast.astype(jnp.int32), flag_ref.shape)
    stab = jnp.where(fast, bound, 0.0)

    r_i = lax.broadcasted_iota(jnp.int32, (tp, tp), 0)
    c_i = lax.broadcasted_iota(jnp.int32, (tp, tp), 1)
    tri = (c_i <= r_i).astype(BF16)

    def tile(j, carry):
        rows = pl.ds(pl.multiple_of(j * tp, tp), tp)
        f3 = jnp.dot(tri, _split3(lf_ref[rows, :]), preferred_element_type=F32)
        f = f3[:, 0:LANES] + f3[:, LANES:2 * LANES] + f3[:, 2 * LANES:] + carry
        f2 = f * LOG2E
        edge_ref[j] = jnp.concatenate([f2[0:1, :], f2[tp - 1:tp, :]], axis=0)
        aq_ref[rows, :] = (jnp.dot(_split3(f2 - stab), pq_ref[...], preferred_element_type=F32)
                           + oq_ref[...]).astype(BF16)
        ak_ref[rows, :] = (jnp.dot(_split3(f2), pk_ref[...], preferred_element_type=F32)
                           + ok_ref[...]).astype(BF16)
        return f[tp - 1:tp, :]

    lax.fori_loop(0, lf_ref.shape[0] // tp, tile, jnp.zeros((1, LANES), F32))


def _fox_bias_call(lf, qw, kw, *, batch, d, tp):
    t = lf.shape[0]
    heads = d // FOX_HEAD
    steps = (t // batch) // tp
    pq = np.zeros((3 * LANES, LANES), np.float32)
    pk = np.zeros((3 * LANES, LANES), np.float32)
    oq = np.zeros((1, LANES), np.float32)
    ok = np.zeros((1, LANES), np.float32)
    for h in range(heads):
        base = FOX_AUG * h
        for i in range(3):
            pq[i * LANES + h, base + i] = 1.0
            ok[0, base + i] = 1.0
            oq[0, base + 3 + i] = 1.0
            pk[i * LANES + h, base + 3 + i] = -1.0
    whole = pl.BlockSpec((t // batch, LANES), lambda b: (b, 0))
    return pl.pallas_call(
        functools.partial(_fox_bias_kernel, tp=tp),
        grid=(batch,),
        in_specs=[
            whole,
            _resident((1, FOX_HEAD)), _resident((1, FOX_HEAD)),
            _resident(pq.shape), _resident(pk.shape), _resident(oq.shape), _resident(ok.shape),
        ],
        out_specs=[whole, whole,
                   pl.BlockSpec((steps, 2, LANES), lambda b: (b, 0, 0)),
                   pl.BlockSpec((SUBLANES, LANES), lambda b: (0, 0))],
        out_shape=[jax.ShapeDtypeStruct((t, LANES), BF16), jax.ShapeDtypeStruct((t, LANES), BF16),
                   jax.ShapeDtypeStruct((batch * steps, 2, LANES), F32),
                   jax.ShapeDtypeStruct((SUBLANES, LANES), jnp.int32)],
        compiler_params=_cparams(("arbitrary",)),
        name="fox_bias",
    )(lf, qw, kw, jnp.asarray(pq, BF16), jnp.asarray(pk, BF16), jnp.asarray(oq), jnp.asarray(ok))


FOX_VROWS = 80


def _fox_attn_kernel(flag_ref, fs_ref, fe_ref, q_ref, aq_ref, k_ref, ak_ref, v_ref, o_ref,
                     v1_ref, qm_ref, acc_ref, m_ref, pa_ref, pb_ref, *, tq, tk, heads):
    bi, pair, qi = pl.program_id(0), pl.program_id(1), pl.program_id(2)
    nk = k_ref.shape[0] // tk
    sub = tq // tk
    pad_rows = lax.broadcasted_iota(jnp.int32, (FOX_VROWS - FOX_HEAD, tk), 0)
    ones_blk = jnp.where(pad_rows == 0, 1.0, 0.0).astype(BF16)
    lane2 = lax.broadcasted_iota(jnp.int32, (1, 2 * LANES), 1)
    bias0 = LANES + 2 * FOX_AUG * pair
    head0 = (lane2 < FOX_HEAD) | ((lane2 >= bias0) & (lane2 < bias0 + FOX_AUG))
    head1 = ((lane2 >= FOX_HEAD) & (lane2 < LANES)) | ((lane2 >= bias0 + FOX_AUG) & (lane2 < bias0 + 2 * FOX_AUG))
    nt = (((1,), (1,)), ((), ()))

    @pl.when(qi == 0)
    def _():
        def build(j, carry):
            rows = pl.ds(pl.multiple_of(j * tk, tk), tk)
            vt = v_ref[rows, :].astype(F32).T
            for e in range(2):
                v1_ref[j, e] = jnp.concatenate([vt[e * FOX_HEAD:(e + 1) * FOX_HEAD, :].astype(BF16), ones_blk],
                                               axis=0)
            return carry

        lax.fori_loop(0, nk, build, 0)

    def probs(j, online, diag=None, dst=None):
        rows = pl.ds(pl.multiple_of(j * tk, tk), tk)
        kf = jnp.concatenate([k_ref[rows, :], ak_ref[rows, :]], axis=1)
        q_lo = 0 if diag is None else diag * tk
        ps, alphas = [], []
        for e in range(2):
            s = lax.dot_general(kf, qm_ref[e, q_lo:tq, :], nt, preferred_element_type=F32)
            if diag is not None:
                kpos = lax.broadcasted_iota(jnp.int32, (tk, tk), 0)
                qpos = lax.broadcasted_iota(jnp.int32, (tk, tk), 1)
                blk = jnp.where(kpos <= qpos, s[:, 0:tk], -jnp.inf)
                s = blk if q_lo + tk == tq else jnp.concatenate([blk, s[:, tk:]], axis=1)
            if online:
                m_prev = m_ref[e, :, q_lo:tq]
                m_new = jnp.maximum(m_prev, jnp.max(s, axis=0, keepdims=True))
                alphas.append(jnp.exp2(m_prev - m_new))
                m_ref[e, :, q_lo:tq] = m_new
                s = s - m_new
            p = jnp.exp2(s).astype(BF16)
            if dst is None:
                ps.append(p)
            else:
                dst[e] = p
        return ps, alphas

    def accumulate(ps, j, alphas=None, diag=None):
        q_lo = 0 if diag is None else diag * tk
        for e in range(2):
            pv = jnp.dot(v1_ref[j, e], ps[e], preferred_element_type=F32)
            if alphas:
                acc_ref[e, :, q_lo:tq] = acc_ref[e, :, q_lo:tq] * alphas[e] + pv
            else:
                acc_ref[e, :, q_lo:tq] += pv

    def start():
        qf = jnp.concatenate([q_ref[...], aq_ref[...]], axis=1)
        zero = jnp.zeros_like(qf)
        qm_ref[0] = jnp.where(head0, qf, zero)
        qm_ref[1] = jnp.where(head1, qf, zero)
        acc_ref[...] = jnp.zeros_like(acc_ref)

    def finish():
        outs = []
        for e in range(2):
            acc = acc_ref[e]
            outs.append(acc[0:FOX_HEAD, :] / acc[FOX_HEAD:FOX_HEAD + 1, :])
        o_ref[...] = jnp.concatenate(outs, axis=0).astype(BF16)

    n_full = qi * sub

    @pl.when(flag_ref[0] == 1)
    def _():
        start()
        held = None
        for a in range(sub - 1, 0, -1):
            nxt = probs(n_full + a, False, diag=a)[0]
            if held is not None:
                accumulate(held[0], n_full + held[1], diag=held[1])
            held = (nxt, a)
        probs(n_full, False, diag=0, dst=pa_ref)
        if held is not None:
            accumulate(held[0], n_full + held[1], diag=held[1])

        h0 = bi * heads + 2 * pair
        fq0 = fs_ref[h0 * nk + n_full]
        fq1 = fs_ref[(h0 + 1) * nk + n_full]

        def dead(j, cnt):
            gone0 = fq0 - fe_ref[h0 * nk + j] < -FOX_SKIP
            gone1 = fq1 - fe_ref[(h0 + 1) * nk + j] < -FOX_SKIP
            return cnt + jnp.logical_and(gone0, gone1).astype(jnp.int32)

        first = lax.fori_loop(0, n_full, dead, 0)
        n_live = n_full - first

        def two_tiles(i, carry):
            j = first + 2 * i
            probs(j, False, dst=pb_ref)
            accumulate(pa_ref, jnp.where(i == 0, n_full, j - 1))
            probs(j + 1, False, dst=pa_ref)
            accumulate(pb_ref, j)
            return carry

        lax.fori_loop(0, n_live // 2, two_tiles, 0)

        @pl.when(n_live % 2 == 1)
        def _():
            ps, _ = probs(n_full - 1, False)
            accumulate(pa_ref, jnp.where(n_live == 1, n_full, n_full - 2))
            accumulate(ps, n_full - 1)

        @pl.when(n_live % 2 == 0)
        def _():
            accumulate(pa_ref, jnp.where(n_live == 0, n_full, n_full - 1))

        finish()

    @pl.when(flag_ref[0] != 1)
    def _():
        start()
        m_ref[...] = jnp.full_like(m_ref, -jnp.inf)

        def body(j, carry):
            ps, alphas = probs(j, True)
            accumulate(ps, j, alphas)
            return carry

        lax.fori_loop(0, n_full, body, 0)
        for a in range(sub):
            ps, alphas = probs(n_full + a, True, diag=a)
            accumulate(ps, n_full + a, alphas, diag=a)
        finish()


def _fox_attn_call(flag, f_first, f_last, q, aq, k, ak, v, *, batch, seq, tq, tk):
    t, d = q.shape
    pairs = d // LANES
    nq = seq // tq
    q_spec = pl.BlockSpec((tq, LANES), lambda b, p, i, *_: (b * nq + i, p))
    kv_spec = pl.BlockSpec((seq, LANES), lambda b, p, i, *_: (b, p))
    grid_spec = pltpu.PrefetchScalarGridSpec(
        num_scalar_prefetch=3,
        grid=(batch, pairs, nq),
        in_specs=[q_spec, pl.BlockSpec((tq, LANES), lambda b, p, i, *_: (b * nq + i, 0)),
                  kv_spec, pl.BlockSpec((seq, LANES), lambda b, p, i, *_: (b, 0)),
                  kv_spec],
        out_specs=pl.BlockSpec((LANES, tq), lambda b, p, i, *_: (p, b * nq + i)),
        scratch_shapes=[
            pltpu.VMEM((seq // tk, 2, FOX_VROWS, tk), BF16),
            pltpu.VMEM((2, tq, 2 * LANES), BF16),
            pltpu.VMEM((2, FOX_VROWS, tq), F32),
            pltpu.VMEM((2, 1, tq), F32),
            pltpu.VMEM((2, tk, tq), BF16),
            pltpu.VMEM((2, tk, tq), BF16),
        ],
    )
    return pl.pallas_call(
        functools.partial(_fox_attn_kernel, tq=tq, tk=tk, heads=d // FOX_HEAD),
        grid_spec=grid_spec,
        out_shape=jax.ShapeDtypeStruct((d, t), BF16),
        compiler_params=_cparams(("arbitrary", "arbitrary", "arbitrary")),
        name="fox_attn",
    )(flag, f_first, f_last, q, aq, k, ak, v)


def _post_kernel(x_ref, y_ref, gate_ref, mod_ref, nw_ref, fw_ref, wo_ref, w1_ref, w2_ref, o_ref, *,
                 d, ff_blk, final, y_features):
    mod = mod_ref[0]
    g1 = mod[:, 2 * d:3 * d]
    sh2, sc2, g2 = mod[:, 3 * d:4 * d], mod[:, 4 * d:5 * d], mod[:, 5 * d:6 * d]
    y = y_ref[...].astype(F32)
    if y_features:
        y = y.T
    yg = (y * gate_ref[...].astype(F32)).astype(BF16)
    x1 = x_ref[...] + g1 * jnp.dot(yg, wo_ref[...], preferred_element_type=F32)
    h = _modulate(x1, nw_ref[...], sh2, sc2).astype(BF16)
    acc = jnp.zeros_like(x1)
    for j in range(w1_ref.shape[2] // ff_blk):
        a = jnp.maximum(jnp.dot(h, w1_ref[0, :, j * ff_blk:(j + 1) * ff_blk], preferred_element_type=F32), 0.0)
        acc = acc + jnp.dot((a * a).astype(BF16), w2_ref[0, j * ff_blk:(j + 1) * ff_blk, :],
                            preferred_element_type=F32)
    x2 = x1 + g2 * acc
    if final:
        x2 = x2 * lax.rsqrt(jnp.mean(x2 * x2, axis=-1, keepdims=True) + EPS) * fw_ref[...]
    o_ref[...] = x2


def _post_call(x2, y, gate, mod3, nw, fw, w_out, w1, w2, *, layer, batch, tm, final, y_features):
    t, d = x2.shape
    tiles_per_batch = (t // batch) // tm
    row = pl.BlockSpec((tm, d), lambda i: (i, 0))
    assert y.shape == ((d, t) if y_features else (t, d))
    return pl.pallas_call(
        functools.partial(_post_kernel, d=d, ff_blk=POST_FF_BLK, final=final, y_features=y_features),
        grid=(t // tm,),
        in_specs=[
            row,
            pl.BlockSpec((d, tm), lambda i: (0, i)) if y_features else row,
            row,
            pl.BlockSpec((1, 1, N_MOD * d), lambda i: (layer * batch + i // tiles_per_batch, 0, 0)),
            _resident((1, d)), _resident((1, d)),
            _resident(w_out.shape),
            pl.BlockSpec((1,) + w1.shape[1:], lambda i: (layer, 0, 0), pipeline_mode=pl.Buffered(1)),
            pl.BlockSpec((1,) + w2.shape[1:], lambda i: (layer, 0, 0), pipeline_mode=pl.Buffered(1)),
        ],
        out_specs=row,
        out_shape=jax.ShapeDtypeStruct((t, d), F32),
        compiler_params=_cparams(("arbitrary",)),
        name="post",
    )(x2, y, gate, mod3, nw, fw, w_out, w1, w2)


def kernel(x, c, w_mod, b_mod, norm1_w, norm2_w, hg_w_in, hg_w_out, hg_lb, hg_gn_w, fox_w_in, fox_b_f,
           fox_qn_w, fox_kn_w, fox_w_out, mlp_w1, mlp_w2, final_w):
    batch, seq, d = x.shape
    depth = w_mod.shape[0]
    t = batch * seq
    fox_heads = d // FOX_HEAD
    assert seq % HG_BLOCK == 0 and seq % ATTN_TQ == 0 and HG_BLOCK % PROJ_TM == 0 and PROJ_TM % ROW_SLAB == 0
    assert d % LANES == 0 and fox_heads * FOX_AUG <= LANES and (N_MOD * d) % MOD_TN == 0
    assert mlp_w1.shape[2] % POST_FF_BLK == 0 and x.dtype == F32

    mod3 = _mod_call(c, w_mod, b_mod).reshape(depth * batch, 1, N_MOD * d)
    xs = x.reshape(t, d)
    fw = final_w.reshape(1, d)
    w1_all, w2_all = mlp_w1.astype(BF16), mlp_w2.astype(BF16)

    for i in range(depth):
        j = i // 2
        n1 = norm1_w[i].reshape(1, d)
        if i % 2 == 0:
            q, k, lf, v, g, wmin = _hg_proj_call(xs, mod3, n1, hg_lb, hg_w_in[j].astype(BF16),
                                                 layer=i, batch=batch, tm=PROJ_TM)
            wmin = wmin.reshape(batch, seq // HG_BLOCK, HG_BLOCK // PROJ_TM, d // HG_HEAD, HG_HEAD).min(axis=(2, 4))
            y = _hg_rec_call(wmin.transpose(0, 2, 1).reshape(-1), q, k, lf, v,
                             hg_gn_w[j].reshape(1, HG_HEAD), batch=batch, tc=HG_BLOCK,
                             unroll_bounded=HG_GROUP, unroll_exact=HG_GROUP_EXACT)
            w_out = hg_w_out[j]
        else:
            pad = ((0, 0), (0, LANES - fox_heads))
            q, k, v, g, lf = _fox_proj_call(
                xs, mod3, n1,
                jnp.tile(fox_qn_w[j], fox_heads).reshape(1, d),
                jnp.tile(fox_kn_w[j], fox_heads).reshape(1, d),
                jnp.pad(fox_b_f[j].reshape(1, fox_heads), pad),
                fox_w_in[j].astype(BF16), jnp.pad(fox_w_in[j, :, 4 * d:], pad).astype(BF16),
                layer=i, batch=batch, tm=PROJ_TM)
            aq, ak, edge, flag = _fox_bias_call(lf, fox_qn_w[j].reshape(1, FOX_HEAD),
                                                fox_kn_w[j].reshape(1, FOX_HEAD), batch=batch, d=d, tp=ATTN_TK)
            edge = edge.reshape(batch, seq // ATTN_TK, 2, LANES)[..., :fox_heads].transpose(2, 0, 3, 1)
            y = _fox_attn_call(flag[0, :1], edge[0].reshape(-1), edge[1].reshape(-1), q, aq, k, ak, v,
                               batch=batch, seq=seq, tq=ATTN_TQ, tk=ATTN_TK)
            w_out = fox_w_out[j]
        xs = _post_call(xs, y, g, mod3, norm2_w[i].reshape(1, d), fw, w_out.astype(BF16), w1_all, w2_all,
                        layer=i, batch=batch, tm=PROJ_TM, final=(i == depth - 1), y_features=(i % 2 == 1))
    return xs.reshape(batch, seq, d)
```

```python
import functools

import numpy as np
import jax
import jax.numpy as jnp
from jax import lax
from jax.experimental import pallas as pl
from jax.experimental.pallas import tpu as pltpu

F32 = jnp.float32
BF16 = jnp.bfloat16
EPS = 1e-6
N_MOD = 6
HG_HEAD = 128
HG_CHUNK = 64
HG_SUB = 16
HG_SAFE_DECAY = 56.0
FOX_HEAD = 64
LANES = 128
SUBLANES = 8
LOG2E = float(np.log2(np.e))
FOX_STAB_MAX = 30.0
FOX_BOUND_SLACK = 1.01
FOX_SKIP = 160.0
ROW_SLAB = 256
VMEM_LIMIT = 56 * 1024 * 1024

PROJ_TM = 512
POST_FF_BLK = 1024
MOD_TN = 1024
HG_BLOCK = 2048
HG_GROUP = 32
HG_GROUP_EXACT = 4
ATTN_TK = 512
ATTN_TQ = 2 * ATTN_TK

_HI = lax.Precision.HIGHEST


def _cparams(sem):
    return pltpu.CompilerParams(dimension_semantics=sem, vmem_limit_bytes=VMEM_LIMIT)


def _resident(shape):
    nd = len(shape)
    return pl.BlockSpec(shape, lambda *_: (0,) * nd, pipeline_mode=pl.Buffered(1))


def _sigmoid(x):
    return 0.5 * jnp.tanh(0.5 * x) + 0.5


def _modulate(x, nw, shift, scale):
    ms = jnp.mean(x * x, axis=-1, keepdims=True)
    y = x * lax.rsqrt(ms + EPS)
    return (y * nw) * (1.0 + scale) + shift


def _mod_kernel(ct_ref, w_ref, b_ref, o_ref):
    ct = ct_ref[...]
    cat = ct * _sigmoid(ct)
    w = w_ref[0]
    rows = []
    for b in range(ct.shape[1]):
        col = jnp.broadcast_to(cat[:, b:b + 1], (ct.shape[0], LANES))
        rows.append(jnp.concatenate(
            [jnp.sum(w[:, j * LANES:(j + 1) * LANES] * col, axis=0, keepdims=True)
             for j in range(w.shape[1] // LANES)], axis=1))
    o_ref[0] = jnp.concatenate(rows, axis=0) + b_ref[0]


def _mod_call(c, w_mod, b_mod):
    depth, d, n = w_mod.shape
    b = c.shape[0]
    tn = MOD_TN
    return pl.pallas_call(
        _mod_kernel,
        grid=(depth, n // tn),
        in_specs=[
            pl.BlockSpec((d, b), lambda l, j: (0, 0)),
            pl.BlockSpec((1, d, tn), lambda l, j: (l, 0, j)),
            pl.BlockSpec((1, 1, tn), lambda l, j: (l, 0, j)),
        ],
        out_specs=pl.BlockSpec((1, b, tn), lambda l, j: (l, 0, j)),
        out_shape=jax.ShapeDtypeStruct((depth, b, n), F32),
        compiler_params=_cparams(("arbitrary", "arbitrary")),
        name="mod",
    )(c.T, w_mod, b_mod.reshape(depth, 1, n))


def _hg_proj_kernel(x_ref, mod_ref, nw_ref, lb_ref, w_ref, q_ref, k_ref, lf_ref, v_ref, g_ref, wmin_ref, *,
                    layer, d):
    mod = mod_ref[0]
    lbp = lb_ref[...]
    e = jnp.exp(lbp - jnp.max(lbp, axis=0, keepdims=True))
    lb = jnp.sum(e[0:layer + 1], axis=0, keepdims=True) / jnp.sum(e, axis=0, keepdims=True)

    w_r = lax.broadcasted_iota(jnp.int32, (ROW_SLAB // HG_SUB, ROW_SLAB), 0)
    w_c = lax.broadcasted_iota(jnp.int32, (ROW_SLAB // HG_SUB, ROW_SLAB), 1)
    win = (lax.shift_right_logical(w_c, HG_SUB.bit_length() - 1) == w_r).astype(BF16)
    wmin = None

    for r0 in range(0, x_ref.shape[0], ROW_SLAB):
        rows = slice(r0, r0 + ROW_SLAB)
        h = _modulate(x_ref[rows, :], nw_ref[...], mod[:, 0:d], mod[:, d:2 * d]).astype(BF16)

        pq = jnp.dot(h, w_ref[:, 0:d], preferred_element_type=F32)
        q_ref[rows, :] = (pq * _sigmoid(pq)).astype(BF16)

        z = jnp.dot(h, w_ref[:, d:2 * d], preferred_element_type=F32)
        th = 0.5 * jnp.tanh(0.5 * z)
        kk = (1.0 - lb) * (0.5 - th)
        f = lb + (1.0 - lb) * (0.5 + th)
        lf = jnp.log(jnp.where(kk < 0.5, 1.0 - kk, f))
        lf_ref[rows, :] = lf
        k_ref[rows, :] = kk.astype(BF16)
        wsum = jnp.min(jnp.dot(win, lf.astype(BF16), preferred_element_type=F32), axis=0, keepdims=True)
        wmin = wsum if wmin is None else jnp.minimum(wmin, wsum)

        v_ref[rows, :] = jnp.dot(h, w_ref[:, 2 * d:3 * d], preferred_element_type=F32).astype(BF16)
        pg = jnp.dot(h, w_ref[:, 3 * d:4 * d], preferred_element_type=F32)
        g_ref[rows, :] = (pg * _sigmoid(pg)).astype(BF16)
    wmin_ref[0] = wmin


def _hg_proj_call(x2, mod3, nw, hg_lb, w_in, *, layer, batch, tm):
    t, d = x2.shape
    tiles_per_batch = (t // batch) // tm
    row = pl.BlockSpec((tm, d), lambda i: (i, 0))
    out_bf = jax.ShapeDtypeStruct((t, d), BF16)
    return pl.pallas_call(
        functools.partial(_hg_proj_kernel, layer=layer, d=d),
        grid=(t // tm,),
        in_specs=[
            row,
            pl.BlockSpec((1, 1, N_MOD * d), lambda i: (layer * batch + i // tiles_per_batch, 0, 0)),
            _resident((1, d)),
            _resident(hg_lb.shape),
            _resident(w_in.shape),
        ],
        out_specs=[row, row, row, row, row, pl.BlockSpec((1, 1, d), lambda i: (i, 0, 0))],
        out_shape=[out_bf, out_bf, jax.ShapeDtypeStruct((t, d), F32), out_bf, out_bf,
                   jax.ShapeDtypeStruct((t // tm, 1, d), F32)],
        compiler_params=_cparams(("arbitrary",)),
        name="hg_proj",
    )(x2, mod3, nw, hg_lb, w_in)


def _hg_rec_kernel(wmin_ref, q_ref, k_ref, lf_ref, v_ref, gnw_ref, o_ref, st_ref, gs_ref, ks_ref, *,
                   n_chunks, unroll_bounded, unroll_exact):
    c, sub = HG_CHUNK, HG_SUB
    n_sub = c // sub
    half = sub // 2

    @pl.when(pl.program_id(2) == 0)
    def _():
        st_ref[...] = jnp.zeros_like(st_ref)

    r_i = lax.broadcasted_iota(jnp.int32, (c, c), 0)
    c_i = lax.broadcasted_iota(jnp.int32, (c, c), 1)
    tri = (c_i <= r_i).astype(BF16)
    ones_w = jnp.ones((HG_HEAD, c), BF16)
    half_row = lax.broadcasted_iota(jnp.int32, (half, HG_HEAD), 0)
    a_lane = lax.broadcasted_iota(jnp.int32, (half, c), 1)
    gnw = gnw_ref[...]
    nt = (((1,), (1,)), ((), ()))
    tn = (((0,), (0,)), ((), ()))

    def offdiag_operands(q, k, gcum, j, bounded):
        lo = j * sub
        hi = lo + sub if bounded else lo
        g_b = gcum[lo - 1:lo, :] if j > 0 else jnp.zeros((1, HG_HEAD), F32)
        qh = (q[lo:lo + sub, :] * jnp.exp2(gcum[lo:lo + sub, :] - g_b)).astype(BF16)
        kh = (k[0:hi, :] * jnp.exp2(g_b - gcum[0:hi, :])).astype(BF16)
        if hi < c:
            kh = jnp.concatenate([kh, jnp.zeros((c - hi, HG_HEAD), BF16)], axis=0)
        return qh, kh

    def diag_products(q, k, gcum, slot, j):
        lo = j * sub
        gt = (gcum[lo:lo + half, :], gcum[lo + half:lo + sub, :])
        qt = (q[lo:lo + half, :], q[lo + half:lo + sub, :])
        ps = []
        for s in range(sub):
            g_s = gs_ref[slot, pl.ds(lo + s, 1), :]
            k_s = ks_ref[slot, pl.ds(lo + s, 1), :]
            for hf in range(2):
                if s >= half and hf == 0:
                    continue
                dlt = gt[hf] - g_s
                if (s >= half) == (hf == 1):
                    dlt = jnp.where(half_row >= s % half, dlt, -jnp.inf)
                ps.append((qt[hf] * k_s) * jnp.exp2(dlt))
        return jnp.concatenate(ps, axis=0).astype(BF16)

    def diag_scatter(r, j):
        lo = j * sub
        a_top = jnp.zeros((half, c), F32)
        a_bot = jnp.zeros((half, c), F32)
        for s in range(sub):
            if s < half:
                a_top = jnp.where(a_lane == lo + s, r[2 * s * half:(2 * s + 1) * half, :], a_top)
                a_bot = jnp.where(a_lane == lo + s, r[(2 * s + 1) * half:(2 * s + 2) * half, :], a_bot)
            else:
                a_bot = jnp.where(a_lane == lo + s, r[(half + s) * half:(half + s + 1) * half, :], a_bot)
        return jnp.concatenate([a_top, a_bot], axis=0)

    def run(bounded, unroll):
        def group(i, st):
            us = range(unroll)
            rows = [pl.ds(pl.multiple_of((i * unroll + u) * c, c), c) for u in us]
            g3 = [jnp.dot(tri, _split3(lf_ref[r, :]), preferred_element_type=F32) for r in rows]
            q = [q_ref[r, :].astype(F32) for r in rows]
            k = [k_ref[r, :].astype(F32) for r in rows]
            gcum = [(g[:, 0:HG_HEAD] + g[:, HG_HEAD:2 * HG_HEAD] + g[:, 2 * HG_HEAD:]) * LOG2E for g in g3]
            g_last = [g[c - 1:c, :] for g in gcum]
            qg = [(q[u] * jnp.exp2(gcum[u])).astype(BF16) for u in us]
            kd = [(k[u] * jnp.exp2(g_last[u] - gcum[u])).astype(BF16) for u in us]
            first = 0 if bounded else 1
            off = [[offdiag_operands(q[u], k[u], gcum[u], j, bounded) for j in range(first, n_sub)] for u in us]
            if not bounded:
                for u in us:
                    gs_ref[u] = gcum[u]
                    ks_ref[u] = k[u]
                pst = [[diag_products(q[u], k[u], gcum[u], u, j) for j in range(n_sub)] for u in us]
            kv = [lax.dot_general(v_ref[rows[u], :], kd[u], tn, preferred_element_type=F32) for u in us]
            blocks = [[lax.dot_general(qh, kh, nt, preferred_element_type=F32) for qh, kh in off[u]] for u in us]
            if bounded:
                a = [jnp.where(c_i <= r_i, jnp.concatenate(blocks[u], axis=0), 0.0) for u in us]
            else:
                sums = [[jnp.dot(p, ones_w, preferred_element_type=F32) for p in pst[u]] for u in us]
                a = []
                for u in us:
                    a_rows = [diag_scatter(sums[u][j], j) for j in range(n_sub)]
                    for j in range(1, n_sub):
                        a_rows[j] = a_rows[j] + blocks[u][j - 1]
                    a.append(jnp.concatenate(a_rows, axis=0))
            o_intra = [jnp.dot(a[u].astype(BF16), v_ref[rows[u], :], preferred_element_type=F32) for u in us]
            for u in us:
                o = o_intra[u] + lax.dot_general(qg[u], st.astype(BF16), nt, preferred_element_type=F32)
                st = st * jnp.exp2(g_last[u]) + kv[u]
                on = o * lax.rsqrt(jnp.mean(o * o, axis=-1, keepdims=True) + EPS) * gnw
                o_ref[rows[u], :] = on.astype(BF16)
            return st

        st_ref[...] = lax.fori_loop(0, n_chunks // unroll, group, st_ref[...])

    blk_id = (pl.program_id(0) * pl.num_programs(1) + pl.program_id(1)) * pl.num_programs(2) + pl.program_id(2)
    bounded = wmin_ref[blk_id] * LOG2E >= -HG_SAFE_DECAY

    @pl.when(bounded)
    def _():
        run(True, unroll_bounded)

    @pl.when(jnp.logical_not(bounded))
    def _():
        run(False, unroll_exact)


def _hg_rec_call(wmin, q, k, lf, v, gnw, *, batch, tc, unroll_bounded, unroll_exact):
    t, d = q.shape
    heads = d // HG_HEAD
    steps = (t // batch) // tc
    unroll = unroll_exact
    blk = pl.BlockSpec((tc, HG_HEAD), lambda b, h, s, *_: (b * steps + s, h))
    grid_spec = pltpu.PrefetchScalarGridSpec(
        num_scalar_prefetch=1,
        grid=(batch, heads, steps),
        in_specs=[blk, blk, blk, blk, pl.BlockSpec((1, HG_HEAD), lambda b, h, s, *_: (0, 0))],
        out_specs=blk,
        scratch_shapes=[
            pltpu.VMEM((HG_HEAD, HG_HEAD), F32),
            pltpu.VMEM((unroll, HG_CHUNK, HG_HEAD), F32),
            pltpu.VMEM((unroll, HG_CHUNK, HG_HEAD), F32),
        ],
    )
    return pl.pallas_call(
        functools.partial(_hg_rec_kernel, n_chunks=tc // HG_CHUNK, unroll_bounded=unroll_bounded,
                          unroll_exact=unroll_exact),
        grid_spec=grid_spec,
        out_shape=jax.ShapeDtypeStruct((t, d), BF16),
        compiler_params=_cparams(("arbitrary", "arbitrary", "arbitrary")),
        name="hg_rec",
    )(wmin, q, k, lf, v, gnw)


def _head_rmsnorm(p, w_row):
    low = lax.broadcasted_iota(jnp.int32, (1, LANES), 1) < FOX_HEAD
    outs = []
    for j in range(p.shape[1] // LANES):
        pj = p[:, j * LANES:(j + 1) * LANES]
        ss = pj * pj
        s_lo = jnp.sum(jnp.where(low, ss, 0.0), axis=-1, keepdims=True)
        s_hi = jnp.sum(jnp.where(low, 0.0, ss), axis=-1, keepdims=True)
        inv = lax.rsqrt(jnp.where(low, s_lo, s_hi) * (1.0 / FOX_HEAD) + EPS)
        outs.append(pj * inv)
    return jnp.concatenate(outs, axis=1) * w_row


def _fox_proj_kernel(x_ref, mod_ref, nw_ref, qw_ref, kw_ref, bf_ref, w_ref, wf_ref,
                     q_ref, k_ref, v_ref, g_ref, lf_ref, vs_ref, *, d):
    mod = mod_ref[0]
    scale = LOG2E / np.sqrt(FOX_HEAD)
    for r0 in range(0, x_ref.shape[0], ROW_SLAB):
        rows = slice(r0, r0 + ROW_SLAB)
        h = _modulate(x_ref[rows, :], nw_ref[...], mod[:, 0:d], mod[:, d:2 * d]).astype(BF16)
        pq = jnp.dot(h, w_ref[:, 0:d], preferred_element_type=F32)
        q_ref[rows, :] = (_head_rmsnorm(pq, qw_ref[...]) * scale).astype(BF16)
        pk = jnp.dot(h, w_ref[:, d:2 * d], preferred_element_type=F32)
        k_ref[rows, :] = _head_rmsnorm(pk, kw_ref[...]).astype(BF16)
        vs_ref[...] = jnp.dot(h, w_ref[:, 2 * d:3 * d], preferred_element_type=F32)
        v_ref[:, rows] = vs_ref[...].T.astype(BF16)
        pg = jnp.dot(h, w_ref[:, 3 * d:4 * d], preferred_element_type=F32)
        g_ref[rows, :] = _sigmoid(pg).astype(BF16)
        u = jnp.dot(h, wf_ref[...], preferred_element_type=F32) + bf_ref[...]
        lf_ref[rows, :] = jnp.minimum(u, 0.0) - jnp.log1p(jnp.exp(-jnp.abs(u)))


def _fox_proj_call(x2, mod3, nw, qw, kw, bf, w_main, w_f, *, layer, batch, tm):
    t, d = x2.shape
    nh = w_f.shape[1]
    tiles_per_batch = (t // batch) // tm
    row = pl.BlockSpec((tm, d), lambda i: (i, 0))
    out_bf = jax.ShapeDtypeStruct((t, d), BF16)
    return pl.pallas_call(
        functools.partial(_fox_proj_kernel, d=d),
        grid=(t // tm,),
        in_specs=[
            row,
            pl.BlockSpec((1, 1, N_MOD * d), lambda i: (layer * batch + i // tiles_per_batch, 0, 0)),
            _resident((1, d)), _resident((1, d)), _resident((1, d)), _resident((1, nh)),
            _resident((d, 4 * d)),
            _resident(w_f.shape),
        ],
        out_specs=[row, row, pl.BlockSpec((d, tm), lambda i: (0, i)), row, pl.BlockSpec((tm, nh), lambda i: (i, 0))],
        out_shape=[out_bf, out_bf, jax.ShapeDtypeStruct((d, t), BF16), out_bf, jax.ShapeDtypeStruct((t, nh), F32)],
        scratch_shapes=[pltpu.VMEM((ROW_SLAB, d), F32)],
        compiler_params=_cparams(("arbitrary",)),
        name="fox_proj",
    )(x2, mod3, nw, qw, kw, bf, w_main, w_f)


FOX_AUG = 6


def _split3(x):
    hi = x.astype(BF16)
    r1 = x - hi.astype(F32)
    mid = r1.astype(BF16)
    lo = (r1 - mid.astype(F32)).astype(BF16)
    return jnp.concatenate([hi, mid, lo], axis=1)


def _fox_bias_kernel(lf_ref, qw_ref, kw_ref, pq_ref, pk_ref, oq_ref, ok_ref,
                     aq_ref, ak_ref, edge_ref, flag_ref, *, tp):
    bound = FOX_BOUND_SLACK * LOG2E * np.sqrt(FOX_HEAD) * jnp.max(jnp.abs(qw_ref[...] * kw_ref[...]), axis=-1, keepdims=True)
    fast = bound <= FOX_STAB_MAX
    flag_ref[...] = jnp.broadcast_to(fast.astype(jnp.int32), flag_ref.shape)
    stab = jnp.where(fast, bound, 0.0)

    r_i = lax.broadcasted_iota(jnp.int32, (tp, tp), 0)
    c_i = lax.broadcasted_iota(jnp.int32, (tp, tp), 1)
    tri = (c_i <= r_i).astype(BF16)

    def tile(j, carry):
        rows = pl.ds(pl.multiple_of(j * tp, tp), tp)
        f3 = jnp.dot(tri, _split3(lf_ref[rows, :]), preferred_element_type=F32)
        f = f3[:, 0:LANES] + f3[:, LANES:2 * LANES] + f3[:, 2 * LANES:] + carry
        f2 = f * LOG2E
        edge_ref[j] = jnp.concatenate([f2[0:1, :], f2[tp - 1:tp, :]], axis=0)
        aq_ref[rows, :] = (jnp.dot(_split3(f2 - stab), pq_ref[...], preferred_element_type=F32)
                           + oq_ref[...]).astype(BF16)
        ak_ref[rows, :] = (jnp.dot(_split3(f2), pk_ref[...], preferred_element_type=F32)
                           + ok_ref[...]).astype(BF16)
        return f[tp - 1:tp, :]

    lax.fori_loop(0, lf_ref.shape[0] // tp, tile, jnp.zeros((1, LANES), F32))


def _fox_bias_call(lf, qw, kw, *, batch, d, tp):
    t = lf.shape[0]
    heads = d // FOX_HEAD
    steps = (t // batch) // tp
    pq = np.zeros((3 * LANES, LANES), np.float32)
    pk = np.zeros((3 * LANES, LANES), np.float32)
    oq = np.zeros((1, LANES), np.float32)
    ok = np.zeros((1, LANES), np.float32)
    for h in range(heads):
        base = FOX_AUG * h
        for i in range(3):
            pq[i * LANES + h, base + i] = 1.0
            ok[0, base + i] = 1.0
            oq[0, base + 3 + i] = 1.0
            pk[i * LANES + h, base + 3 + i] = -1.0
    whole = pl.BlockSpec((t // batch, LANES), lambda b: (b, 0))
    return pl.pallas_call(
        functools.partial(_fox_bias_kernel, tp=tp),
        grid=(batch,),
        in_specs=[
            whole,
            _resident((1, FOX_HEAD)), _resident((1, FOX_HEAD)),
            _resident(pq.shape), _resident(pk.shape), _resident(oq.shape), _resident(ok.shape),
        ],
        out_specs=[whole, whole,
                   pl.BlockSpec((steps, 2, LANES), lambda b: (b, 0, 0)),
                   pl.BlockSpec((SUBLANES, LANES), lambda b: (0, 0))],
        out_shape=[jax.ShapeDtypeStruct((t, LANES), BF16), jax.ShapeDtypeStruct((t, LANES), BF16),
                   jax.ShapeDtypeStruct((batch * steps, 2, LANES), F32),
                   jax.ShapeDtypeStruct((SUBLANES, LANES), jnp.int32)],
        compiler_params=_cparams(("arbitrary",)),
        name="fox_bias",
    )(lf, qw, kw, jnp.asarray(pq, BF16), jnp.asarray(pk, BF16), jnp.asarray(oq), jnp.asarray(ok))


FOX_VROWS = 80


def _fox_attn_kernel(flag_ref, fs_ref, fe_ref, q_ref, aq_ref, k_ref, ak_ref, v_ref, o_ref,
                     v1_ref, qm_ref, acc_ref, m_ref, pa_ref, pb_ref, *, tq, tk, heads):
    bi, pair, qi = pl.program_id(0), pl.program_id(1), pl.program_id(2)
    nk = k_ref.shape[0] // tk
    sub = tq // tk
    pad_rows = lax.broadcasted_iota(jnp.int32, (FOX_VROWS - FOX_HEAD, tk), 0)
    ones_blk = jnp.where(pad_rows == 0, 1.0, 0.0).astype(BF16)
    lane2 = lax.broadcasted_iota(jnp.int32, (1, 2 * LANES), 1)
    bias0 = LANES + 2 * FOX_AUG * pair
    head0 = (lane2 < FOX_HEAD) | ((lane2 >= bias0) & (lane2 < bias0 + FOX_AUG))
    head1 = ((lane2 >= FOX_HEAD) & (lane2 < LANES)) | ((lane2 >= bias0 + FOX_AUG) & (lane2 < bias0 + 2 * FOX_AUG))
    nt = (((1,), (1,)), ((), ()))

    @pl.when(qi == 0)
    def _():
        def build(j, carry):
            cols = pl.ds(pl.multiple_of(j * tk, tk), tk)
            for e in range(2):
                v1_ref[j, e] = jnp.concatenate([v_ref[e * FOX_HEAD:(e + 1) * FOX_HEAD, cols], ones_blk], axis=0)
            return carry

        lax.fori_loop(0, nk, build, 0)

    def probs(j, online, diag=None, dst=None):
        rows = pl.ds(pl.multiple_of(j * tk, tk), tk)
        kf = jnp.concatenate([k_ref[rows, :], ak_ref[rows, :]], axis=1)
        q_lo = 0 if diag is None else diag * tk
        ps, alphas = [], []
        for e in range(2):
            s = lax.dot_general(kf, qm_ref[e, q_lo:tq, :], nt, preferred_element_type=F32)
            if diag is not None:
                kpos = lax.broadcasted_iota(jnp.int32, (tk, tk), 0)
                qpos = lax.broadcasted_iota(jnp.int32, (tk, tk), 1)
                blk = jnp.where(kpos <= qpos, s[:, 0:tk], -jnp.inf)
                s = blk if q_lo + tk == tq else jnp.concatenate([blk, s[:, tk:]], axis=1)
            if online:
                m_prev = m_ref[e, :, q_lo:tq]
                m_new = jnp.maximum(m_prev, jnp.max(s, axis=0, keepdims=True))
                alphas.append(jnp.exp2(m_prev - m_new))
                m_ref[e, :, q_lo:tq] = m_new
                s = s - m_new
            p = jnp.exp2(s).astype(BF16)
            if dst is None:
                ps.append(p)
            else:
                dst[e] = p
        return ps, alphas

    def accumulate(ps, j, alphas=None, diag=None):
        q_lo = 0 if diag is None else diag * tk
        for e in range(2):
            pv = jnp.dot(v1_ref[j, e], ps[e], preferred_element_type=F32)
            if alphas:
                acc_ref[e, :, q_lo:tq] = acc_ref[e, :, q_lo:tq] * alphas[e] + pv
            else:
                acc_ref[e, :, q_lo:tq] += pv

    def start():
        qf = jnp.concatenate([q_ref[...], aq_ref[...]], axis=1)
        zero = jnp.zeros_like(qf)
        qm_ref[0] = jnp.where(head0, qf, zero)
        qm_ref[1] = jnp.where(head1, qf, zero)
        acc_ref[...] = jnp.zeros_like(acc_ref)

    def finish():
        outs = []
        for e in range(2):
            acc = acc_ref[e]
            outs.append(acc[0:FOX_HEAD, :] / acc[FOX_HEAD:FOX_HEAD + 1, :])
        o_ref[...] = jnp.concatenate(outs, axis=0).astype(BF16)

    n_full = qi * sub

    @pl.when(flag_ref[0] == 1)
    def _():
        start()
        held = None
        for a in range(sub - 1, 0, -1):
            nxt = probs(n_full + a, False, diag=a)[0]
            if held is not None:
                accumulate(held[0], n_full + held[1], diag=held[1])
            held = (nxt, a)
        probs(n_full, False, diag=0, dst=pa_ref)
        if held is not None:
            accumulate(held[0], n_full + held[1], diag=held[1])

        h0 = bi * heads + 2 * pair
        fq0 = fs_ref[h0 * nk + n_full]
        fq1 = fs_ref[(h0 + 1) * nk + n_full]

        def dead(j, cnt):
            gone0 = fq0 - fe_ref[h0 * nk + j] < -FOX_SKIP
            gone1 = fq1 - fe_ref[(h0 + 1) * nk + j] < -FOX_SKIP
            return cnt + jnp.logical_and(gone0, gone1).astype(jnp.int32)

        first = lax.fori_loop(0, n_full, dead, 0)
        n_live = n_full - first

        def two_tiles(i, carry):
            j = first + 2 * i
            probs(j, False, dst=pb_ref)
            accumulate(pa_ref, jnp.where(i == 0, n_full, j - 1))
            probs(j + 1, False, dst=pa_ref)
            accumulate(pb_ref, j)
            return carry

        lax.fori_loop(0, n_live // 2, two_tiles, 0)

        @pl.when(n_live % 2 == 1)
        def _():
            ps, _ = probs(n_full - 1, False)
            accumulate(pa_ref, jnp.where(n_live == 1, n_full, n_full - 2))
            accumulate(ps, n_full - 1)

        @pl.when(n_live % 2 == 0)
        def _():
            accumulate(pa_ref, jnp.where(n_live == 0, n_full, n_full - 1))

        finish()

    @pl.when(flag_ref[0] != 1)
    def _():
        start()
        m_ref[...] = jnp.full_like(m_ref, -jnp.inf)

        def body(j, carry):
            ps, alphas = probs(j, True)
            accumulate(ps, j, alphas)
            return carry

        lax.fori_loop(0, n_full, body, 0)
        for a in range(sub):
            ps, alphas = probs(n_full + a, True, diag=a)
            accumulate(ps, n_full + a, alphas, diag=a)
        finish()


def _fox_attn_call(flag, f_first, f_last, q, aq, k, ak, v, *, batch, seq, tq, tk):
    t, d = q.shape
    pairs = d // LANES
    nq = seq // tq
    q_spec = pl.BlockSpec((tq, LANES), lambda b, p, i, *_: (b * nq + i, p))
    kv_spec = pl.BlockSpec((seq, LANES), lambda b, p, i, *_: (b, p))
    grid_spec = pltpu.PrefetchScalarGridSpec(
        num_scalar_prefetch=3,
        grid=(batch, pairs, nq),
        in_specs=[q_spec, pl.BlockSpec((tq, LANES), lambda b, p, i, *_: (b * nq + i, 0)),
                  kv_spec, pl.BlockSpec((seq, LANES), lambda b, p, i, *_: (b, 0)),
                  pl.BlockSpec((LANES, seq), lambda b, p, i, *_: (p, b))],
        out_specs=pl.BlockSpec((LANES, tq), lambda b, p, i, *_: (p, b * nq + i)),
        scratch_shapes=[
            pltpu.VMEM((seq // tk, 2, FOX_VROWS, tk), BF16),
            pltpu.VMEM((2, tq, 2 * LANES), BF16),
            pltpu.VMEM((2, FOX_VROWS, tq), F32),
            pltpu.VMEM((2, 1, tq), F32),
            pltpu.VMEM((2, tk, tq), BF16),
            pltpu.VMEM((2, tk, tq), BF16),
        ],
    )
    return pl.pallas_call(
        functools.partial(_fox_attn_kernel, tq=tq, tk=tk, heads=d // FOX_HEAD),
        grid_spec=grid_spec,
        out_shape=jax.ShapeDtypeStruct((d, t), BF16),
        compiler_params=_cparams(("arbitrary", "arbitrary", "arbitrary")),
        name="fox_attn",
    )(flag, f_first, f_last, q, aq, k, ak, v)


def _post_kernel(x_ref, y_ref, gate_ref, mod_ref, nw_ref, fw_ref, wo_ref, w1_ref, w2_ref, o_ref, *,
                 d, ff_blk, final, y_features):
    mod = mod_ref[0]
    g1 = mod[:, 2 * d:3 * d]
    sh2, sc2, g2 = mod[:, 3 * d:4 * d], mod[:, 4 * d:5 * d], mod[:, 5 * d:6 * d]
    y = y_ref[...].astype(F32)
    if y_features:
        y = y.T
    yg = (y * gate_ref[...].astype(F32)).astype(BF16)
    x1 = x_ref[...] + g1 * jnp.dot(yg, wo_ref[...], preferred_element_type=F32)
    h = _modulate(x1, nw_ref[...], sh2, sc2).astype(BF16)
    acc = jnp.zeros_like(x1)
    for j in range(w1_ref.shape[2] // ff_blk):
        a = jnp.maximum(jnp.dot(h, w1_ref[0, :, j * ff_blk:(j + 1) * ff_blk], preferred_element_type=F32), 0.0)
        acc = acc + jnp.dot((a * a).astype(BF16), w2_ref[0, j * ff_blk:(j + 1) * ff_blk, :],
                            preferred_element_type=F32)
    x2 = x1 + g2 * acc
    if final:
        x2 = x2 * lax.rsqrt(jnp.mean(x2 * x2, axis=-1, keepdims=True) + EPS) * fw_ref[...]
    o_ref[...] = x2


def _post_call(x2, y, gate, mod3, nw, fw, w_out, w1, w2, *, layer, batch, tm, final, y_features):
    t, d = x2.shape
    tiles_per_batch = (t // batch) // tm
    row = pl.BlockSpec((tm, d), lambda i: (i, 0))
    assert y.shape == ((d, t) if y_features else (t, d))
    return pl.pallas_call(
        functools.partial(_post_kernel, d=d, ff_blk=POST_FF_BLK, final=final, y_features=y_features),
        grid=(t // tm,),
        in_specs=[
            row,
            pl.BlockSpec((d, tm), lambda i: (0, i)) if y_features else row,
            row,
            pl.BlockSpec((1, 1, N_MOD * d), lambda i: (layer * batch + i // tiles_per_batch, 0, 0)),
            _resident((1, d)), _resident((1, d)),
            _resident(w_out.shape),
            pl.BlockSpec((1,) + w1.shape[1:], lambda i: (layer, 0, 0), pipeline_mode=pl.Buffered(1)),
            pl.BlockSpec((1,) + w2.shape[1:], lambda i: (layer, 0, 0), pipeline_mode=pl.Buffered(1)),
        ],
        out_specs=row,
        out_shape=jax.ShapeDtypeStruct((t, d), F32),
        compiler_params=_cparams(("arbitrary",)),
        name="post",
    )(x2, y, gate, mod3, nw, fw, w_out, w1, w2)


def kernel(x, c, w_mod, b_mod, norm1_w, norm2_w, hg_w_in, hg_w_out, hg_lb, hg_gn_w, fox_w_in, fox_b_f,
           fox_qn_w, fox_kn_w, fox_w_out, mlp_w1, mlp_w2, final_w):
    batch, seq, d = x.shape
    depth = w_mod.shape[0]
    t = batch * seq
    fox_heads = d // FOX_HEAD
    assert seq % HG_BLOCK == 0 and seq % ATTN_TQ == 0 and HG_BLOCK % PROJ_TM == 0 and PROJ_TM % ROW_SLAB == 0
    assert d % LANES == 0 and fox_heads * FOX_AUG <= LANES and (N_MOD * d) % MOD_TN == 0
    assert mlp_w1.shape[2] % POST_FF_BLK == 0 and x.dtype == F32

    mod3 = _mod_call(c, w_mod, b_mod).reshape(depth * batch, 1, N_MOD * d)
    xs = x.reshape(t, d)
    fw = final_w.reshape(1, d)
    w1_all, w2_all = mlp_w1.astype(BF16), mlp_w2.astype(BF16)

    for i in range(depth):
        j = i // 2
        n1 = norm1_w[i].reshape(1, d)
        if i % 2 == 0:
            q, k, lf, v, g, wmin = _hg_proj_call(xs, mod3, n1, hg_lb, hg_w_in[j].astype(BF16),
                                                 layer=i, batch=batch, tm=PROJ_TM)
            wmin = wmin.reshape(batch, seq // HG_BLOCK, HG_BLOCK // PROJ_TM, d // HG_HEAD, HG_HEAD).min(axis=(2, 4))
            y = _hg_rec_call(wmin.transpose(0, 2, 1).reshape(-1), q, k, lf, v,
                             hg_gn_w[j].reshape(1, HG_HEAD), batch=batch, tc=HG_BLOCK,
                             unroll_bounded=HG_GROUP, unroll_exact=HG_GROUP_EXACT)
            w_out = hg_w_out[j]
        else:
            pad = ((0, 0), (0, LANES - fox_heads))
            q, k, v, g, lf = _fox_proj_call(
                xs, mod3, n1,
                jnp.tile(fox_qn_w[j], fox_heads).reshape(1, d),
                jnp.tile(fox_kn_w[j], fox_heads).reshape(1, d),
                jnp.pad(fox_b_f[j].reshape(1, fox_heads), pad),
                fox_w_in[j].astype(BF16), jnp.pad(fox_w_in[j, :, 4 * d:], pad).astype(BF16),
                layer=i, batch=batch, tm=PROJ_TM)
            aq, ak, edge, flag = _fox_bias_call(lf, fox_qn_w[j].reshape(1, FOX_HEAD),
                                                fox_kn_w[j].reshape(1, FOX_HEAD), batch=batch, d=d, tp=ATTN_TK)
            edge = edge.reshape(batch, seq // ATTN_TK, 2, LANES)[..., :fox_heads].transpose(2, 0, 3, 1)
            y = _fox_attn_call(flag[0, :1], edge[0].reshape(-1), edge[1].reshape(-1), q, aq, k, ak, v,
                               batch=batch, seq=seq, tq=ATTN_TQ, tk=ATTN_TK)
            w_out = fox_w_out[j]
        xs = _post_call(xs, y, g, mod3, norm2_w[i].reshape(1, d), fw, w_out.astype(BF16), w1_all, w2_all,
                        layer=i, batch=batch, tm=PROJ_TM, final=(i == depth - 1), y_features=(i % 2 == 1))
    return xs.reshape(batch, seq, d)
```

```python
import functools

import numpy as np
import jax
import jax.numpy as jnp
from jax import lax
from jax.experimental import pallas as pl
from jax.experimental.pallas import tpu as pltpu

F32 = jnp.float32
BF16 = jnp.bfloat16
EPS = 1e-6
N_MOD = 6
HG_HEAD = 128
HG_CHUNK = 64
HG_SUB = 16
HG_SAFE_DECAY = 56.0
FOX_HEAD = 64
LANES = 128
SUBLANES = 8
LOG2E = float(np.log2(np.e))
FOX_STAB_MAX = 30.0
FOX_BOUND_SLACK = 1.01
FOX_SKIP = 160.0
ROW_SLAB = 256
VMEM_LIMIT = 56 * 1024 * 1024

PROJ_TM = 512
POST_FF_BLK = 1024
MOD_TN = 1024
HG_BLOCK = 2048
HG_GROUP = 32
HG_GROUP_EXACT = 4
ATTN_TK = 512
ATTN_TQ = 2 * ATTN_TK

_HI = lax.Precision.HIGHEST


def _cparams(sem):
    return pltpu.CompilerParams(dimension_semantics=sem, vmem_limit_bytes=VMEM_LIMIT)


def _resident(shape):
    nd = len(shape)
    return pl.BlockSpec(shape, lambda *_: (0,) * nd, pipeline_mode=pl.Buffered(1))


def _sigmoid(x):
    return 0.5 * jnp.tanh(0.5 * x) + 0.5


def _cast_weight_once(w_ref, wb_ref, cols):
    @pl.when(pl.program_id(0) == 0)
    def _():
        for c0 in range(0, wb_ref.shape[1], cols):
            wb_ref[:, c0:c0 + cols] = w_ref[:, c0:c0 + cols].astype(BF16)


def _modulate(x, nw, shift, scale):
    ms = jnp.mean(x * x, axis=-1, keepdims=True)
    y = x * lax.rsqrt(ms + EPS)
    return (y * nw) * (1.0 + scale) + shift


def _mod_kernel(ct_ref, w_ref, b_ref, o_ref):
    ct = ct_ref[...]
    cat = ct * _sigmoid(ct)
    w = w_ref[0]
    rows = []
    for b in range(ct.shape[1]):
        col = jnp.broadcast_to(cat[:, b:b + 1], (ct.shape[0], LANES))
        rows.append(jnp.concatenate(
            [jnp.sum(w[:, j * LANES:(j + 1) * LANES] * col, axis=0, keepdims=True)
             for j in range(w.shape[1] // LANES)], axis=1))
    o_ref[0] = jnp.concatenate(rows, axis=0) + b_ref[0]


def _mod_call(c, w_mod, b_mod):
    depth, d, n = w_mod.shape
    b = c.shape[0]
    tn = MOD_TN
    return pl.pallas_call(
        _mod_kernel,
        grid=(depth, n // tn),
        in_specs=[
            pl.BlockSpec((d, b), lambda l, j: (0, 0)),
            pl.BlockSpec((1, d, tn), lambda l, j: (l, 0, j)),
            pl.BlockSpec((1, 1, tn), lambda l, j: (l, 0, j)),
        ],
        out_specs=pl.BlockSpec((1, b, tn), lambda l, j: (l, 0, j)),
        out_shape=jax.ShapeDtypeStruct((depth, b, n), F32),
        compiler_params=_cparams(("arbitrary", "arbitrary")),
        name="mod",
    )(c.T, w_mod, b_mod.reshape(depth, 1, n))


def _hg_proj_kernel(x_ref, mod_ref, nw_ref, lb_ref, wf32_ref, q_ref, k_ref, lf_ref, v_ref, g_ref, wmin_ref,
                    w_ref, *, layer, d):
    _cast_weight_once(wf32_ref, w_ref, d)
    mod = mod_ref[0]
    lbp = lb_ref[...]
    e = jnp.exp(lbp - jnp.max(lbp, axis=0, keepdims=True))
    lb = jnp.sum(e[0:layer + 1], axis=0, keepdims=True) / jnp.sum(e, axis=0, keepdims=True)

    w_r = lax.broadcasted_iota(jnp.int32, (ROW_SLAB // HG_SUB, ROW_SLAB), 0)
    w_c = lax.broadcasted_iota(jnp.int32, (ROW_SLAB // HG_SUB, ROW_SLAB), 1)
    win = (lax.shift_right_logical(w_c, HG_SUB.bit_length() - 1) == w_r).astype(BF16)
    lfs = []

    for r0 in range(0, x_ref.shape[0], ROW_SLAB):
        rows = slice(r0, r0 + ROW_SLAB)
        h = _modulate(x_ref[rows, :], nw_ref[...], mod[:, 0:d], mod[:, d:2 * d]).astype(BF16)

        pq = jnp.dot(h, w_ref[:, 0:d], preferred_element_type=F32)
        q_ref[rows, :] = (pq * _sigmoid(pq)).astype(BF16)

        z = jnp.dot(h, w_ref[:, d:2 * d], preferred_element_type=F32)
        th = 0.5 * jnp.tanh(0.5 * z)
        kk = (1.0 - lb) * (0.5 - th)
        f = lb + (1.0 - lb) * (0.5 + th)
        lf = jnp.log(jnp.where(kk < 0.5, 1.0 - kk, f))
        lf_ref[rows, :] = lf
        k_ref[rows, :] = kk.astype(BF16)
        lfs.append(lf.astype(BF16))

        v_ref[rows, :] = jnp.dot(h, w_ref[:, 2 * d:3 * d], preferred_element_type=F32).astype(BF16)
        pg = jnp.dot(h, w_ref[:, 3 * d:4 * d], preferred_element_type=F32)
        g_ref[rows, :] = (pg * _sigmoid(pg)).astype(BF16)

    wmin = None
    for lf16 in lfs:
        wsum = jnp.min(jnp.dot(win, lf16, preferred_element_type=F32), axis=0, keepdims=True)
        wmin = wsum if wmin is None else jnp.minimum(wmin, wsum)
    wmin_ref[0] = wmin


def _hg_proj_call(x2, mod3, nw, hg_lb, w_in, *, layer, batch, tm):
    t, d = x2.shape
    tiles_per_batch = (t // batch) // tm
    row = pl.BlockSpec((tm, d), lambda i: (i, 0))
    out_bf = jax.ShapeDtypeStruct((t, d), BF16)
    return pl.pallas_call(
        functools.partial(_hg_proj_kernel, layer=layer, d=d),
        grid=(t // tm,),
        in_specs=[
            row,
            pl.BlockSpec((1, 1, N_MOD * d), lambda i: (layer * batch + i // tiles_per_batch, 0, 0)),
            _resident((1, d)),
            _resident(hg_lb.shape),
            _resident(w_in.shape),
        ],
        out_specs=[row, row, row, row, row, pl.BlockSpec((1, 1, d), lambda i: (i, 0, 0))],
        out_shape=[out_bf, out_bf, jax.ShapeDtypeStruct((t, d), F32), out_bf, out_bf,
                   jax.ShapeDtypeStruct((t // tm, 1, d), F32)],
        scratch_shapes=[pltpu.VMEM(w_in.shape, BF16)],
        compiler_params=_cparams(("arbitrary",)),
        name="hg_proj",
    )(x2, mod3, nw, hg_lb, w_in)


def _hg_rec_kernel(wmin_ref, q_ref, k_ref, lf_ref, v_ref, gnw_ref, o_ref, st_ref, gs_ref, ks_ref, *,
                   n_chunks, unroll_bounded, unroll_exact):
    c, sub = HG_CHUNK, HG_SUB
    n_sub = c // sub
    half = sub // 2

    @pl.when(pl.program_id(2) == 0)
    def _():
        st_ref[...] = jnp.zeros_like(st_ref)

    r_i = lax.broadcasted_iota(jnp.int32, (c, c), 0)
    c_i = lax.broadcasted_iota(jnp.int32, (c, c), 1)
    tri = (c_i <= r_i).astype(BF16)
    ones_w = jnp.ones((HG_HEAD, c), BF16)
    half_row = lax.broadcasted_iota(jnp.int32, (half, HG_HEAD), 0)
    a_lane = lax.broadcasted_iota(jnp.int32, (half, c), 1)
    gnw = gnw_ref[...]
    nt = (((1,), (1,)), ((), ()))
    tn = (((0,), (0,)), ((), ()))

    def offdiag_operands(q, k, gcum, j, bounded):
        lo = j * sub
        hi = lo + sub if bounded else lo
        g_b = gcum[lo - 1:lo, :] if j > 0 else jnp.zeros((1, HG_HEAD), F32)
        qh = (q[lo:lo + sub, :] * jnp.exp2(gcum[lo:lo + sub, :] - g_b)).astype(BF16)
        kh = (k[0:hi, :] * jnp.exp2(g_b - gcum[0:hi, :])).astype(BF16)
        if hi < c:
            kh = jnp.concatenate([kh, jnp.zeros((c - hi, HG_HEAD), BF16)], axis=0)
        return qh, kh

    def diag_products(q, k, gcum, slot, j):
        lo = j * sub
        gt = (gcum[lo:lo + half, :], gcum[lo + half:lo + sub, :])
        qt = (q[lo:lo + half, :], q[lo + half:lo + sub, :])
        ps = []
        for s in range(sub):
            g_s = gs_ref[slot, pl.ds(lo + s, 1), :]
            k_s = ks_ref[slot, pl.ds(lo + s, 1), :]
            for hf in range(2):
                if s >= half and hf == 0:
                    continue
                dlt = gt[hf] - g_s
                if (s >= half) == (hf == 1):
                    dlt = jnp.where(half_row >= s % half, dlt, -jnp.inf)
                ps.append((qt[hf] * k_s) * jnp.exp2(dlt))
        return jnp.concatenate(ps, axis=0).astype(BF16)

    def diag_scatter(r, j):
        lo = j * sub
        a_top = jnp.zeros((half, c), F32)
        a_bot = jnp.zeros((half, c), F32)
        for s in range(sub):
            if s < half:
                a_top = jnp.where(a_lane == lo + s, r[2 * s * half:(2 * s + 1) * half, :], a_top)
                a_bot = jnp.where(a_lane == lo + s, r[(2 * s + 1) * half:(2 * s + 2) * half, :], a_bot)
            else:
                a_bot = jnp.where(a_lane == lo + s, r[(half + s) * half:(half + s + 1) * half, :], a_bot)
        return jnp.concatenate([a_top, a_bot], axis=0)

    def run(bounded, unroll):
        def group(i, st):
            us = range(unroll)
            rows = [pl.ds(pl.multiple_of((i * unroll + u) * c, c), c) for u in us]
            g3 = [jnp.dot(tri, _split3(lf_ref[r, :]), preferred_element_type=F32) for r in rows]
            q = [q_ref[r, :].astype(F32) for r in rows]
            k = [k_ref[r, :].astype(F32) for r in rows]
            gcum = [(g[:, 0:HG_HEAD] + g[:, HG_HEAD:2 * HG_HEAD] + g[:, 2 * HG_HEAD:]) * LOG2E for g in g3]
            g_last = [g[c - 1:c, :] for g in gcum]
            qg = [(q[u] * jnp.exp2(gcum[u])).astype(BF16) for u in us]
            kd = [(k[u] * jnp.exp2(g_last[u] - gcum[u])).astype(BF16) for u in us]
            first = 0 if bounded else 1
            off = [[offdiag_operands(q[u], k[u], gcum[u], j, bounded) for j in range(first, n_sub)] for u in us]
            if not bounded:
                for u in us:
                    gs_ref[u] = gcum[u]
                    ks_ref[u] = k[u]
                pst = [[diag_products(q[u], k[u], gcum[u], u, j) for j in range(n_sub)] for u in us]
            kv = [lax.dot_general(v_ref[rows[u], :], kd[u], tn, preferred_element_type=F32) for u in us]
            blocks = [[lax.dot_general(qh, kh, nt, preferred_element_type=F32) for qh, kh in off[u]] for u in us]
            if bounded:
                a = [jnp.where(c_i <= r_i, jnp.concatenate(blocks[u], axis=0), 0.0) for u in us]
            else:
                sums = [[jnp.dot(p, ones_w, preferred_element_type=F32) for p in pst[u]] for u in us]
                a = []
                for u in us:
                    a_rows = [diag_scatter(sums[u][j], j) for j in range(n_sub)]
                    for j in range(1, n_sub):
                        a_rows[j] = a_rows[j] + blocks[u][j - 1]
                    a.append(jnp.concatenate(a_rows, axis=0))
            o_intra = [jnp.dot(a[u].astype(BF16), v_ref[rows[u], :], preferred_element_type=F32) for u in us]
            for u in us:
                o = o_intra[u] + lax.dot_general(qg[u], st.astype(BF16), nt, preferred_element_type=F32)
                st = st * jnp.exp2(g_last[u]) + kv[u]
                on = o * lax.rsqrt(jnp.mean(o * o, axis=-1, keepdims=True) + EPS) * gnw
                o_ref[rows[u], :] = on.astype(BF16)
            return st

        st_ref[...] = lax.fori_loop(0, n_chunks // unroll, group, st_ref[...])

    blk_id = (pl.program_id(0) * pl.num_programs(1) + pl.program_id(1)) * pl.num_programs(2) + pl.program_id(2)
    bounded = wmin_ref[blk_id] * LOG2E >= -HG_SAFE_DECAY

    @pl.when(bounded)
    def _():
        run(True, unroll_bounded)

    @pl.when(jnp.logical_not(bounded))
    def _():
        run(False, unroll_exact)


def _hg_rec_call(wmin, q, k, lf, v, gnw, *, batch, tc, unroll_bounded, unroll_exact):
    t, d = q.shape
    heads = d // HG_HEAD
    steps = (t // batch) // tc
    unroll = unroll_exact
    blk = pl.BlockSpec((tc, HG_HEAD), lambda b, h, s, *_: (b * steps + s, h))
    grid_spec = pltpu.PrefetchScalarGridSpec(
        num_scalar_prefetch=1,
        grid=(batch, heads, steps),
        in_specs=[blk, blk, blk, blk, pl.BlockSpec((1, HG_HEAD), lambda b, h, s, *_: (0, 0))],
        out_specs=blk,
        scratch_shapes=[
            pltpu.VMEM((HG_HEAD, HG_HEAD), F32),
            pltpu.VMEM((unroll, HG_CHUNK, HG_HEAD), F32),
            pltpu.VMEM((unroll, HG_CHUNK, HG_HEAD), F32),
        ],
    )
    return pl.pallas_call(
        functools.partial(_hg_rec_kernel, n_chunks=tc // HG_CHUNK, unroll_bounded=unroll_bounded,
                          unroll_exact=unroll_exact),
        grid_spec=grid_spec,
        out_shape=jax.ShapeDtypeStruct((t, d), BF16),
        compiler_params=_cparams(("arbitrary", "arbitrary", "arbitrary")),
        name="hg_rec",
    )(wmin, q, k, lf, v, gnw)


def _head_rmsnorm(p, w_row):
    low = lax.broadcasted_iota(jnp.int32, (1, LANES), 1) < FOX_HEAD
    outs = []
    for j in range(p.shape[1] // LANES):
        pj = p[:, j * LANES:(j + 1) * LANES]
        ss = pj * pj
        s_lo = jnp.sum(jnp.where(low, ss, 0.0), axis=-1, keepdims=True)
        s_hi = jnp.sum(jnp.where(low, 0.0, ss), axis=-1, keepdims=True)
        inv = lax.rsqrt(jnp.where(low, s_lo, s_hi) * (1.0 / FOX_HEAD) + EPS)
        outs.append(pj * inv)
    return jnp.concatenate(outs, axis=1) * w_row


def _fox_proj_kernel(x_ref, mod_ref, nw_ref, qw_ref, kw_ref, bf_ref, wf32_ref, wf_ref,
                     q_ref, k_ref, v_ref, g_ref, lf_ref, vs_ref, w_ref, *, d):
    _cast_weight_once(wf32_ref, w_ref, d)
    mod = mod_ref[0]
    scale = LOG2E / np.sqrt(FOX_HEAD)
    for r0 in range(0, x_ref.shape[0], ROW_SLAB):
        rows = slice(r0, r0 + ROW_SLAB)
        h = _modulate(x_ref[rows, :], nw_ref[...], mod[:, 0:d], mod[:, d:2 * d]).astype(BF16)
        pq = jnp.dot(h, w_ref[:, 0:d], preferred_element_type=F32)
        q_ref[rows, :] = (_head_rmsnorm(pq, qw_ref[...]) * scale).astype(BF16)
        pk = jnp.dot(h, w_ref[:, d:2 * d], preferred_element_type=F32)
        k_ref[rows, :] = _head_rmsnorm(pk, kw_ref[...]).astype(BF16)
        vs_ref[...] = jnp.dot(h, w_ref[:, 2 * d:3 * d], preferred_element_type=F32)
        v_ref[:, rows] = vs_ref[...].T.astype(BF16)
        pg = jnp.dot(h, w_ref[:, 3 * d:4 * d], preferred_element_type=F32)
        g_ref[rows, :] = _sigmoid(pg).astype(BF16)
        u = jnp.dot(h, wf_ref[...], preferred_element_type=F32) + bf_ref[...]
        lf_ref[rows, :] = jnp.minimum(u, 0.0) - jnp.log1p(jnp.exp(-jnp.abs(u)))


def _fox_proj_call(x2, mod3, nw, qw, kw, bf, w_main, w_f, *, layer, batch, tm):
    t, d = x2.shape
    nh = w_f.shape[1]
    tiles_per_batch = (t // batch) // tm
    row = pl.BlockSpec((tm, d), lambda i: (i, 0))
    out_bf = jax.ShapeDtypeStruct((t, d), BF16)
    return pl.pallas_call(
        functools.partial(_fox_proj_kernel, d=d),
        grid=(t // tm,),
        in_specs=[
            row,
            pl.BlockSpec((1, 1, N_MOD * d), lambda i: (layer * batch + i // tiles_per_batch, 0, 0)),
            _resident((1, d)), _resident((1, d)), _resident((1, d)), _resident((1, nh)),
            _resident((d, 4 * d)),
            _resident(w_f.shape),
        ],
        out_specs=[row, row, pl.BlockSpec((d, tm), lambda i: (0, i)), row, pl.BlockSpec((tm, nh), lambda i: (i, 0))],
        out_shape=[out_bf, out_bf, jax.ShapeDtypeStruct((d, t), BF16), out_bf, jax.ShapeDtypeStruct((t, nh), F32)],
        scratch_shapes=[pltpu.VMEM((ROW_SLAB, d), F32),
                        pltpu.VMEM((d, 4 * d), BF16)],
        compiler_params=_cparams(("arbitrary",)),
        name="fox_proj",
    )(x2, mod3, nw, qw, kw, bf, w_main, w_f)


FOX_AUG = 6


def _split3(x):
    hi = x.astype(BF16)
    r1 = x - hi.astype(F32)
    mid = r1.astype(BF16)
    lo = (r1 - mid.astype(F32)).astype(BF16)
    return jnp.concatenate([hi, mid, lo], axis=1)


def _fox_bias_kernel(lf_ref, qw_ref, kw_ref, pq_ref, pk_ref, oq_ref, ok_ref,
                     aq_ref, ak_ref, edge_ref, flag_ref, *, tp):
    bound = FOX_BOUND_SLACK * LOG2E * np.sqrt(FOX_HEAD) * jnp.max(jnp.abs(qw_ref[...] * kw_ref[...]), axis=-1, keepdims=True)
    fast = bound <= FOX_STAB_MAX
    flag_ref[...] = jnp.broadcast_to(fast.astype(jnp.int32), flag_ref.shape)
    stab = jnp.where(fast, bound, 0.0)

    r_i = lax.broadcasted_iota(jnp.int32, (tp, tp), 0)
    c_i = lax.broadcasted_iota(jnp.int32, (tp, tp), 1)
    tri = (c_i <= r_i).astype(BF16)

    def tile(j, carry):
        rows = pl.ds(pl.multiple_of(j * tp, tp), tp)
        f3 = jnp.dot(tri, _split3(lf_ref[rows, :]), preferred_element_type=F32)
        f = f3[:, 0:LANES] + f3[:, LANES:2 * LANES] + f3[:, 2 * LANES:] + carry
        f2 = f * LOG2E
        edge_ref[j] = jnp.concatenate([f2[0:1, :], f2[tp - 1:tp, :]], axis=0)
        aq_ref[rows, :] = (jnp.dot(_split3(f2 - stab), pq_ref[...], preferred_element_type=F32)
                           + oq_ref[...]).astype(BF16)
        ak_ref[rows, :] = (jnp.dot(_split3(f2), pk_ref[...], preferred_element_type=F32)
                           + ok_ref[...]).astype(BF16)
        return f[tp - 1:tp, :]

    lax.fori_loop(0, lf_ref.shape[0] // tp, tile, jnp.zeros((1, LANES), F32))


def _fox_bias_call(lf, qw, kw, *, batch, d, tp):
    t = lf.shape[0]
    heads = d // FOX_HEAD
    steps = (t // batch) // tp
    pq = np.zeros((3 * LANES, LANES), np.float32)
    pk = np.zeros((3 * LANES, LANES), np.float32)
    oq = np.zeros((1, LANES), np.float32)
    ok = np.zeros((1, LANES), np.float32)
    for h in range(heads):
        base = FOX_AUG * h
        for i in range(3):
            pq[i * LANES + h, base + i] = 1.0
            ok[0, base + i] = 1.0
            oq[0, base + 3 + i] = 1.0
            pk[i * LANES + h, base + 3 + i] = -1.0
    whole = pl.BlockSpec((t // batch, LANES), lambda b: (b, 0))
    return pl.pallas_call(
        functools.partial(_fox_bias_kernel, tp=tp),
        grid=(batch,),
        in_specs=[
            whole,
            _resident((1, FOX_HEAD)), _resident((1, FOX_HEAD)),
            _resident(pq.shape), _resident(pk.shape), _resident(oq.shape), _resident(ok.shape),
        ],
        out_specs=[whole, whole,
                   pl.BlockSpec((steps, 2, LANES), lambda b: (b, 0, 0)),
                   pl.BlockSpec((SUBLANES, LANES), lambda b: (0, 0))],
        out_shape=[jax.ShapeDtypeStruct((t, LANES), BF16), jax.ShapeDtypeStruct((t, LANES), BF16),
                   jax.ShapeDtypeStruct((batch * steps, 2, LANES), F32),
                   jax.ShapeDtypeStruct((SUBLANES, LANES), jnp.int32)],
        compiler_params=_cparams(("arbitrary",)),
        name="fox_bias",
    )(lf, qw, kw, jnp.asarray(pq, BF16), jnp.asarray(pk, BF16), jnp.asarray(oq), jnp.asarray(ok))


FOX_VROWS = 80


def _fox_attn_kernel(flag_ref, fs_ref, fe_ref, q_ref, aq_ref, k_ref, ak_ref, v_ref, o_ref,
                     v1_ref, qm_ref, acc_ref, m_ref, pa_ref, pb_ref, *, tq, tk, heads):
    bi, pair, qi = pl.program_id(0), pl.program_id(1), pl.program_id(2)
    nk = k_ref.shape[0] // tk
    sub = tq // tk
    pad_rows = lax.broadcasted_iota(jnp.int32, (FOX_VROWS - FOX_HEAD, tk), 0)
    ones_blk = jnp.where(pad_rows == 0, 1.0, 0.0).astype(BF16)
    lane2 = lax.broadcasted_iota(jnp.int32, (1, 2 * LANES), 1)
    bias0 = LANES + 2 * FOX_AUG * pair
    head0 = (lane2 < FOX_HEAD) | ((lane2 >= bias0) & (lane2 < bias0 + FOX_AUG))
    head1 = ((lane2 >= FOX_HEAD) & (lane2 < LANES)) | ((lane2 >= bias0 + FOX_AUG) & (lane2 < bias0 + 2 * FOX_AUG))
    nt = (((1,), (1,)), ((), ()))

    @pl.when(qi == 0)
    def _():
        def build(j, carry):
            cols = pl.ds(pl.multiple_of(j * tk, tk), tk)
            for e in range(2):
                v1_ref[j, e] = jnp.concatenate([v_ref[e * FOX_HEAD:(e + 1) * FOX_HEAD, cols], ones_blk], axis=0)
            return carry

        lax.fori_loop(0, nk, build, 0)

    def probs(j, online, diag=None, dst=None):
        rows = pl.ds(pl.multiple_of(j * tk, tk), tk)
        kf = jnp.concatenate([k_ref[rows, :], ak_ref[rows, :]], axis=1)
        q_lo = 0 if diag is None else diag * tk
        ps, alphas = [], []
        for e in range(2):
            s = lax.dot_general(kf, qm_ref[e, q_lo:tq, :], nt, preferred_element_type=F32)
            if diag is not None:
                kpos = lax.broadcasted_iota(jnp.int32, (tk, tk), 0)
                qpos = lax.broadcasted_iota(jnp.int32, (tk, tk), 1)
                blk = jnp.where(kpos <= qpos, s[:, 0:tk], -jnp.inf)
                s = blk if q_lo + tk == tq else jnp.concatenate([blk, s[:, tk:]], axis=1)
            if online:
                m_prev = m_ref[e, :, q_lo:tq]
                m_new = jnp.maximum(m_prev, jnp.max(s, axis=0, keepdims=True))
                alphas.append(jnp.exp2(m_prev - m_new))
                m_ref[e, :, q_lo:tq] = m_new
                s = s - m_new
            p = jnp.exp2(s).astype(BF16)
            if dst is None:
                ps.append(p)
            else:
                dst[e] = p
        return ps, alphas

    def accumulate(ps, j, alphas=None, diag=None):
        q_lo = 0 if diag is None else diag * tk
        for e in range(2):
            pv = jnp.dot(v1_ref[j, e], ps[e], preferred_element_type=F32)
            if alphas:
                acc_ref[e, :, q_lo:tq] = acc_ref[e, :, q_lo:tq] * alphas[e] + pv
            else:
                acc_ref[e, :, q_lo:tq] += pv

    def start():
        qf = jnp.concatenate([q_ref[...], aq_ref[...]], axis=1)
        zero = jnp.zeros_like(qf)
        qm_ref[0] = jnp.where(head0, qf, zero)
        qm_ref[1] = jnp.where(head1, qf, zero)
        acc_ref[...] = jnp.zeros_like(acc_ref)

    def finish():
        outs = []
        for e in range(2):
            acc = acc_ref[e]
            outs.append(acc[0:FOX_HEAD, :] / acc[FOX_HEAD:FOX_HEAD + 1, :])
        o_ref[...] = jnp.concatenate(outs, axis=0).astype(BF16)

    n_full = qi * sub

    @pl.when(flag_ref[0] == 1)
    def _():
        start()
        held = None
        for a in range(sub - 1, 0, -1):
            nxt = probs(n_full + a, False, diag=a)[0]
            if held is not None:
                accumulate(held[0], n_full + held[1], diag=held[1])
            held = (nxt, a)
        probs(n_full, False, diag=0, dst=pa_ref)
        if held is not None:
            accumulate(held[0], n_full + held[1], diag=held[1])

        h0 = bi * heads + 2 * pair
        fq0 = fs_ref[h0 * nk + n_full]
        fq1 = fs_ref[(h0 + 1) * nk + n_full]

        def dead(j, cnt):
            gone0 = fq0 - fe_ref[h0 * nk + j] < -FOX_SKIP
            gone1 = fq1 - fe_ref[(h0 + 1) * nk + j] < -FOX_SKIP
            return cnt + jnp.logical_and(gone0, gone1).astype(jnp.int32)

        first = lax.fori_loop(0, n_full, dead, 0)
        n_live = n_full - first

        def two_tiles(i, carry):
            j = first + 2 * i
            probs(j, False, dst=pb_ref)
            accumulate(pa_ref, jnp.where(i == 0, n_full, j - 1))
            probs(j + 1, False, dst=pa_ref)
            accumulate(pb_ref, j)
            return carry

        lax.fori_loop(0, n_live // 2, two_tiles, 0)

        @pl.when(n_live % 2 == 1)
        def _():
            ps, _ = probs(n_full - 1, False)
            accumulate(pa_ref, jnp.where(n_live == 1, n_full, n_full - 2))
            accumulate(ps, n_full - 1)

        @pl.when(n_live % 2 == 0)
        def _():
            accumulate(pa_ref, jnp.where(n_live == 0, n_full, n_full - 1))

        finish()

    @pl.when(flag_ref[0] != 1)
    def _():
        start()
        m_ref[...] = jnp.full_like(m_ref, -jnp.inf)

        def body(j, carry):
            ps, alphas = probs(j, True)
            accumulate(ps, j, alphas)
            return carry

        lax.fori_loop(0, n_full, body, 0)
        for a in range(sub):
            ps, alphas = probs(n_full + a, True, diag=a)
            accumulate(ps, n_full + a, alphas, diag=a)
        finish()


def _fox_attn_call(flag, f_first, f_last, q, aq, k, ak, v, *, batch, seq, tq, tk):
    t, d = q.shape
    pairs = d // LANES
    nq = seq // tq
    q_spec = pl.BlockSpec((tq, LANES), lambda b, p, i, *_: (b * nq + i, p))
    kv_spec = pl.BlockSpec((seq, LANES), lambda b, p, i, *_: (b, p))
    grid_spec = pltpu.PrefetchScalarGridSpec(
        num_scalar_prefetch=3,
        grid=(batch, pairs, nq),
        in_specs=[q_spec, pl.BlockSpec((tq, LANES), lambda b, p, i, *_: (b * nq + i, 0)),
                  kv_spec, pl.BlockSpec((seq, LANES), lambda b, p, i, *_: (b, 0)),
                  pl.BlockSpec((LANES, seq), lambda b, p, i, *_: (p, b))],
        out_specs=pl.BlockSpec((LANES, tq), lambda b, p, i, *_: (p, b * nq + i)),
        scratch_shapes=[
            pltpu.VMEM((seq // tk, 2, FOX_VROWS, tk), BF16),
            pltpu.VMEM((2, tq, 2 * LANES), BF16),
            pltpu.VMEM((2, FOX_VROWS, tq), F32),
            pltpu.VMEM((2, 1, tq), F32),
            pltpu.VMEM((2, tk, tq), BF16),
            pltpu.VMEM((2, tk, tq), BF16),
        ],
    )
    return pl.pallas_call(
        functools.partial(_fox_attn_kernel, tq=tq, tk=tk, heads=d // FOX_HEAD),
        grid_spec=grid_spec,
        out_shape=jax.ShapeDtypeStruct((d, t), BF16),
        compiler_params=_cparams(("arbitrary", "arbitrary", "arbitrary")),
        name="fox_attn",
    )(flag, f_first, f_last, q, aq, k, ak, v)


def _post_kernel(x_ref, y_ref, gate_ref, mod_ref, nw_ref, fw_ref, wo_ref, w1_ref, w2_ref, o_ref, *,
                 d, ff_blk, final, y_features):
    mod = mod_ref[0]
    g1 = mod[:, 2 * d:3 * d]
    sh2, sc2, g2 = mod[:, 3 * d:4 * d], mod[:, 4 * d:5 * d], mod[:, 5 * d:6 * d]
    y = y_ref[...].astype(F32)
    if y_features:
        y = y.T
    yg = (y * gate_ref[...].astype(F32)).astype(BF16)
    x1 = x_ref[...] + g1 * jnp.dot(yg, wo_ref[...], preferred_element_type=F32)
    h = _modulate(x1, nw_ref[...], sh2, sc2).astype(BF16)
    acc = jnp.zeros_like(x1)
    for j in range(w1_ref.shape[2] // ff_blk):
        a = jnp.maximum(jnp.dot(h, w1_ref[0, :, j * ff_blk:(j + 1) * ff_blk], preferred_element_type=F32), 0.0)
        acc = acc + jnp.dot((a * a).astype(BF16), w2_ref[0, j * ff_blk:(j + 1) * ff_blk, :],
                            preferred_element_type=F32)
    x2 = x1 + g2 * acc
    if final:
        x2 = x2 * lax.rsqrt(jnp.mean(x2 * x2, axis=-1, keepdims=True) + EPS) * fw_ref[...]
    o_ref[...] = x2


def _post_call(x2, y, gate, mod3, nw, fw, w_out, w1, w2, *, layer, batch, tm, final, y_features):
    t, d = x2.shape
    tiles_per_batch = (t // batch) // tm
    row = pl.BlockSpec((tm, d), lambda i: (i, 0))
    assert y.shape == ((d, t) if y_features else (t, d))
    return pl.pallas_call(
        functools.partial(_post_kernel, d=d, ff_blk=POST_FF_BLK, final=final, y_features=y_features),
        grid=(t // tm,),
        in_specs=[
            row,
            pl.BlockSpec((d, tm), lambda i: (0, i)) if y_features else row,
            row,
            pl.BlockSpec((1, 1, N_MOD * d), lambda i: (layer * batch + i // tiles_per_batch, 0, 0)),
            _resident((1, d)), _resident((1, d)),
            _resident(w_out.shape),
            pl.BlockSpec((1,) + w1.shape[1:], lambda i: (layer, 0, 0), pipeline_mode=pl.Buffered(1)),
            pl.BlockSpec((1,) + w2.shape[1:], lambda i: (layer, 0, 0), pipeline_mode=pl.Buffered(1)),
        ],
        out_specs=row,
        out_shape=jax.ShapeDtypeStruct((t, d), F32),
        compiler_params=_cparams(("arbitrary",)),
        name="post",
    )(x2, y, gate, mod3, nw, fw, w_out, w1, w2)


def kernel(x, c, w_mod, b_mod, norm1_w, norm2_w, hg_w_in, hg_w_out, hg_lb, hg_gn_w, fox_w_in, fox_b_f,
           fox_qn_w, fox_kn_w, fox_w_out, mlp_w1, mlp_w2, final_w):
    batch, seq, d = x.shape
    depth = w_mod.shape[0]
    t = batch * seq
    fox_heads = d // FOX_HEAD
    assert seq % HG_BLOCK == 0 and seq % ATTN_TQ == 0 and HG_BLOCK % PROJ_TM == 0 and PROJ_TM % ROW_SLAB == 0
    assert d % LANES == 0 and fox_heads * FOX_AUG <= LANES and (N_MOD * d) % MOD_TN == 0
    assert mlp_w1.shape[2] % POST_FF_BLK == 0 and x.dtype == F32

    mod3 = _mod_call(c, w_mod, b_mod).reshape(depth * batch, 1, N_MOD * d)
    xs = x.reshape(t, d)
    fw = final_w.reshape(1, d)
    w1_all, w2_all = mlp_w1.astype(BF16), mlp_w2.astype(BF16)

    for i in range(depth):
        j = i // 2
        n1 = norm1_w[i].reshape(1, d)
        if i % 2 == 0:
            q, k, lf, v, g, wmin = _hg_proj_call(xs, mod3, n1, hg_lb, hg_w_in[j],
                                                 layer=i, batch=batch, tm=PROJ_TM)
            wmin = wmin.reshape(batch, seq // HG_BLOCK, HG_BLOCK // PROJ_TM, d // HG_HEAD, HG_HEAD).min(axis=(2, 4))
            y = _hg_rec_call(wmin.transpose(0, 2, 1).reshape(-1), q, k, lf, v,
                             hg_gn_w[j].reshape(1, HG_HEAD), batch=batch, tc=HG_BLOCK,
                             unroll_bounded=HG_GROUP, unroll_exact=HG_GROUP_EXACT)
            w_out = hg_w_out[j]
        else:
            pad = ((0, 0), (0, LANES - fox_heads))
            q, k, v, g, lf = _fox_proj_call(
                xs, mod3, n1,
                jnp.tile(fox_qn_w[j], fox_heads).reshape(1, d),
                jnp.tile(fox_kn_w[j], fox_heads).reshape(1, d),
                jnp.pad(fox_b_f[j].reshape(1, fox_heads), pad),
                fox_w_in[j], jnp.pad(fox_w_in[j, :, 4 * d:], pad).astype(BF16),
                layer=i, batch=batch, tm=PROJ_TM)
            aq, ak, edge, flag = _fox_bias_call(lf, fox_qn_w[j].reshape(1, FOX_HEAD),
                                                fox_kn_w[j].reshape(1, FOX_HEAD), batch=batch, d=d, tp=ATTN_TK)
            edge = edge.reshape(batch, seq // ATTN_TK, 2, LANES)[..., :fox_heads].transpose(2, 0, 3, 1)
            y = _fox_attn_call(flag[0, :1], edge[0].reshape(-1), edge[1].reshape(-1), q, aq, k, ak, v,
                               batch=batch, seq=seq, tq=ATTN_TQ, tk=ATTN_TK)
            w_out = fox_w_out[j]
        xs = _post_call(xs, y, g, mod3, norm2_w[i].reshape(1, d), fw, w_out.astype(BF16), w1_all, w2_all,
                        layer=i, batch=batch, tm=PROJ_TM, final=(i == depth - 1), y_features=(i % 2 == 1))
    return xs.reshape(batch, seq, d)
```

```python
import functools

import numpy as np
import jax
import jax.numpy as jnp
from jax import lax
from jax.experimental import pallas as pl
from jax.experimental.pallas import tpu as pltpu

F32 = jnp.float32
BF16 = jnp.bfloat16
EPS = 1e-6
N_MOD = 6
HG_HEAD = 128
HG_CHUNK = 64
HG_SUB = 16
HG_SAFE_DECAY = 56.0
FOX_HEAD = 64
LANES = 128
SUBLANES = 8
LOG2E = float(np.log2(np.e))
FOX_STAB_MAX = 30.0
FOX_BOUND_SLACK = 1.01
FOX_SKIP = 160.0
ROW_SLAB = 256
VMEM_LIMIT = 56 * 1024 * 1024

PROJ_TM = 512
POST_FF_BLK = 1024
MOD_TN = 1024
HG_BLOCK = 2048
HG_GROUP = 32
HG_GROUP_EXACT = 4
ATTN_TK = 512
ATTN_TQ = 2 * ATTN_TK

_HI = lax.Precision.HIGHEST


def _cparams(sem):
    return pltpu.CompilerParams(dimension_semantics=sem, vmem_limit_bytes=VMEM_LIMIT)


def _resident(shape):
    nd = len(shape)
    return pl.BlockSpec(shape, lambda *_: (0,) * nd, pipeline_mode=pl.Buffered(1))


def _sigmoid(x):
    return 0.5 * jnp.tanh(0.5 * x) + 0.5


def _cast_weight_once(w_ref, wb_ref, cols):
    @pl.when(pl.program_id(0) == 0)
    def _():
        for c0 in range(0, wb_ref.shape[1], cols):
            wb_ref[:, c0:c0 + cols] = w_ref[:, c0:c0 + cols].astype(BF16)


def _modulate(x, nw, shift, scale):
    ms = jnp.mean(x * x, axis=-1, keepdims=True)
    y = x * lax.rsqrt(ms + EPS)
    return (y * nw) * (1.0 + scale) + shift


def _mod_kernel(ct_ref, w_ref, b_ref, o_ref):
    ct = ct_ref[...]
    cat = ct * _sigmoid(ct)
    w = w_ref[0]
    rows = []
    for b in range(ct.shape[1]):
        col = jnp.broadcast_to(cat[:, b:b + 1], (ct.shape[0], LANES))
        rows.append(jnp.concatenate(
            [jnp.sum(w[:, j * LANES:(j + 1) * LANES] * col, axis=0, keepdims=True)
             for j in range(w.shape[1] // LANES)], axis=1))
    o_ref[0] = jnp.concatenate(rows, axis=0) + b_ref[0]


def _mod_call(c, w_mod, b_mod):
    depth, d, n = w_mod.shape
    b = c.shape[0]
    tn = MOD_TN
    return pl.pallas_call(
        _mod_kernel,
        grid=(depth, n // tn),
        in_specs=[
            pl.BlockSpec((d, b), lambda l, j: (0, 0)),
            pl.BlockSpec((1, d, tn), lambda l, j: (l, 0, j)),
            pl.BlockSpec((1, 1, tn), lambda l, j: (l, 0, j)),
        ],
        out_specs=pl.BlockSpec((1, b, tn), lambda l, j: (l, 0, j)),
        out_shape=jax.ShapeDtypeStruct((depth, b, n), F32),
        compiler_params=_cparams(("arbitrary", "arbitrary")),
        name="mod",
    )(c.T, w_mod, b_mod.reshape(depth, 1, n))


def _hg_proj_kernel(x_ref, mod_ref, nw_ref, lb_ref, wf32_ref, q_ref, k_ref, lf_ref, v_ref, g_ref, wmin_ref,
                    w_ref, *, layer, d):
    _cast_weight_once(wf32_ref, w_ref, d)
    mod = mod_ref[0]
    lbp = lb_ref[...]
    e = jnp.exp(lbp - jnp.max(lbp, axis=0, keepdims=True))
    lb = jnp.sum(e[0:layer + 1], axis=0, keepdims=True) / jnp.sum(e, axis=0, keepdims=True)

    w_r = lax.broadcasted_iota(jnp.int32, (ROW_SLAB // HG_SUB, ROW_SLAB), 0)
    w_c = lax.broadcasted_iota(jnp.int32, (ROW_SLAB // HG_SUB, ROW_SLAB), 1)
    win = (lax.shift_right_logical(w_c, HG_SUB.bit_length() - 1) == w_r).astype(BF16)
    lfs = []

    for r0 in range(0, x_ref.shape[0], ROW_SLAB):
        rows = slice(r0, r0 + ROW_SLAB)
        h = _modulate(x_ref[rows, :], nw_ref[...], mod[:, 0:d], mod[:, d:2 * d]).astype(BF16)

        pq = jnp.dot(h, w_ref[:, 0:d], preferred_element_type=F32)
        q_ref[rows, :] = (pq * _sigmoid(pq)).astype(BF16)

        z = jnp.dot(h, w_ref[:, d:2 * d], preferred_element_type=F32)
        th = 0.5 * jnp.tanh(0.5 * z)
        kk = (1.0 - lb) * (0.5 - th)
        f = lb + (1.0 - lb) * (0.5 + th)
        lf = jnp.log(jnp.where(kk < 0.5, 1.0 - kk, f))
        lf_ref[rows, :] = lf
        k_ref[rows, :] = kk.astype(BF16)
        lfs.append(lf.astype(BF16))

        v_ref[rows, :] = jnp.dot(h, w_ref[:, 2 * d:3 * d], preferred_element_type=F32).astype(BF16)
        pg = jnp.dot(h, w_ref[:, 3 * d:4 * d], preferred_element_type=F32)
        g_ref[rows, :] = (pg * _sigmoid(pg)).astype(BF16)

    wmin = None
    for lf16 in lfs:
        wsum = jnp.min(jnp.dot(win, lf16, preferred_element_type=F32), axis=0, keepdims=True)
        wmin = wsum if wmin is None else jnp.minimum(wmin, wsum)
    wmin_ref[0] = wmin


def _hg_proj_call(x2, mod3, nw, hg_lb, w_in, *, layer, batch, tm):
    t, d = x2.shape
    tiles_per_batch = (t // batch) // tm
    row = pl.BlockSpec((tm, d), lambda i: (i, 0))
    out_bf = jax.ShapeDtypeStruct((t, d), BF16)
    return pl.pallas_call(
        functools.partial(_hg_proj_kernel, layer=layer, d=d),
        grid=(t // tm,),
        in_specs=[
            row,
            pl.BlockSpec((1, 1, N_MOD * d), lambda i: (layer * batch + i // tiles_per_batch, 0, 0)),
            _resident((1, d)),
            _resident(hg_lb.shape),
            _resident(w_in.shape),
        ],
        out_specs=[row, row, row, row, row, pl.BlockSpec((1, 1, d), lambda i: (i, 0, 0))],
        out_shape=[out_bf, out_bf, jax.ShapeDtypeStruct((t, d), F32), out_bf, out_bf,
                   jax.ShapeDtypeStruct((t // tm, 1, d), F32)],
        scratch_shapes=[pltpu.VMEM(w_in.shape, BF16)],
        compiler_params=_cparams(("arbitrary",)),
        name="hg_proj",
    )(x2, mod3, nw, hg_lb, w_in)


def _hg_rec_kernel(wmin_ref, q_ref, k_ref, lf_ref, v_ref, gnw_ref, ca_ref, cb_ref, o_ref, cao_ref, cbo_ref,
                   st_ref, gs_ref, ks_ref, *, n_chunks, unroll_bounded, unroll_exact):
    cao_ref[...] = ca_ref[...].astype(BF16)
    cbo_ref[...] = cb_ref[...].astype(BF16)
    c, sub = HG_CHUNK, HG_SUB
    n_sub = c // sub
    half = sub // 2

    @pl.when(pl.program_id(2) == 0)
    def _():
        st_ref[...] = jnp.zeros_like(st_ref)

    r_i = lax.broadcasted_iota(jnp.int32, (c, c), 0)
    c_i = lax.broadcasted_iota(jnp.int32, (c, c), 1)
    tri = (c_i <= r_i).astype(BF16)
    ones_w = jnp.ones((HG_HEAD, c), BF16)
    half_row = lax.broadcasted_iota(jnp.int32, (half, HG_HEAD), 0)
    a_lane = lax.broadcasted_iota(jnp.int32, (half, c), 1)
    gnw = gnw_ref[...]
    nt = (((1,), (1,)), ((), ()))
    tn = (((0,), (0,)), ((), ()))

    def offdiag_operands(q, k, gcum, j, bounded):
        lo = j * sub
        hi = lo + sub if bounded else lo
        g_b = gcum[lo - 1:lo, :] if j > 0 else jnp.zeros((1, HG_HEAD), F32)
        qh = (q[lo:lo + sub, :] * jnp.exp2(gcum[lo:lo + sub, :] - g_b)).astype(BF16)
        kh = (k[0:hi, :] * jnp.exp2(g_b - gcum[0:hi, :])).astype(BF16)
        if hi < c:
            kh = jnp.concatenate([kh, jnp.zeros((c - hi, HG_HEAD), BF16)], axis=0)
        return qh, kh

    def diag_products(q, k, gcum, slot, j):
        lo = j * sub
        gt = (gcum[lo:lo + half, :], gcum[lo + half:lo + sub, :])
        qt = (q[lo:lo + half, :], q[lo + half:lo + sub, :])
        ps = []
        for s in range(sub):
            g_s = gs_ref[slot, pl.ds(lo + s, 1), :]
            k_s = ks_ref[slot, pl.ds(lo + s, 1), :]
            for hf in range(2):
                if s >= half and hf == 0:
                    continue
                dlt = gt[hf] - g_s
                if (s >= half) == (hf == 1):
                    dlt = jnp.where(half_row >= s % half, dlt, -jnp.inf)
                ps.append((qt[hf] * k_s) * jnp.exp2(dlt))
        return jnp.concatenate(ps, axis=0).astype(BF16)

    def diag_scatter(r, j):
        lo = j * sub
        a_top = jnp.zeros((half, c), F32)
        a_bot = jnp.zeros((half, c), F32)
        for s in range(sub):
            if s < half:
                a_top = jnp.where(a_lane == lo + s, r[2 * s * half:(2 * s + 1) * half, :], a_top)
                a_bot = jnp.where(a_lane == lo + s, r[(2 * s + 1) * half:(2 * s + 2) * half, :], a_bot)
            else:
                a_bot = jnp.where(a_lane == lo + s, r[(half + s) * half:(half + s + 1) * half, :], a_bot)
        return jnp.concatenate([a_top, a_bot], axis=0)

    def run(bounded, unroll):
        def group(i, st):
            us = range(unroll)
            rows = [pl.ds(pl.multiple_of((i * unroll + u) * c, c), c) for u in us]
            g3 = [jnp.dot(tri, _split3(lf_ref[r, :]), preferred_element_type=F32) for r in rows]
            q = [q_ref[r, :].astype(F32) for r in rows]
            k = [k_ref[r, :].astype(F32) for r in rows]
            gcum = [(g[:, 0:HG_HEAD] + g[:, HG_HEAD:2 * HG_HEAD] + g[:, 2 * HG_HEAD:]) * LOG2E for g in g3]
            g_last = [g[c - 1:c, :] for g in gcum]
            qg = [(q[u] * jnp.exp2(gcum[u])).astype(BF16) for u in us]
            kd = [(k[u] * jnp.exp2(g_last[u] - gcum[u])).astype(BF16) for u in us]
            first = 0 if bounded else 1
            off = [[offdiag_operands(q[u], k[u], gcum[u], j, bounded) for j in range(first, n_sub)] for u in us]
            if not bounded:
                for u in us:
                    gs_ref[u] = gcum[u]
                    ks_ref[u] = k[u]
                pst = [[diag_products(q[u], k[u], gcum[u], u, j) for j in range(n_sub)] for u in us]
            kv = [lax.dot_general(v_ref[rows[u], :], kd[u], tn, preferred_element_type=F32) for u in us]
            blocks = [[lax.dot_general(qh, kh, nt, preferred_element_type=F32) for qh, kh in off[u]] for u in us]
            if bounded:
                a = [jnp.where(c_i <= r_i, jnp.concatenate(blocks[u], axis=0), 0.0) for u in us]
            else:
                sums = [[jnp.dot(p, ones_w, preferred_element_type=F32) for p in pst[u]] for u in us]
                a = []
                for u in us:
                    a_rows = [diag_scatter(sums[u][j], j) for j in range(n_sub)]
                    for j in range(1, n_sub):
                        a_rows[j] = a_rows[j] + blocks[u][j - 1]
                    a.append(jnp.concatenate(a_rows, axis=0))
            o_intra = [jnp.dot(a[u].astype(BF16), v_ref[rows[u], :], preferred_element_type=F32) for u in us]
            for u in us:
                o = o_intra[u] + lax.dot_general(qg[u], st.astype(BF16), nt, preferred_element_type=F32)
                st = st * jnp.exp2(g_last[u]) + kv[u]
                on = o * lax.rsqrt(jnp.mean(o * o, axis=-1, keepdims=True) + EPS) * gnw
                o_ref[rows[u], :] = on.astype(BF16)
            return st

        st_ref[...] = lax.fori_loop(0, n_chunks // unroll, group, st_ref[...])

    blk_id = (pl.program_id(0) * pl.num_programs(1) + pl.program_id(1)) * pl.num_programs(2) + pl.program_id(2)
    bounded = wmin_ref[blk_id] * LOG2E >= -HG_SAFE_DECAY

    @pl.when(bounded)
    def _():
        run(True, unroll_bounded)

    @pl.when(jnp.logical_not(bounded))
    def _():
        run(False, unroll_exact)


def _hg_rec_call(wmin, q, k, lf, v, gnw, cast_a, cast_b, *, batch, tc, unroll_bounded, unroll_exact):
    t, d = q.shape
    heads = d // HG_HEAD
    steps = (t // batch) // tc
    n_steps = batch * heads * steps
    unroll = unroll_exact
    blk = pl.BlockSpec((tc, HG_HEAD), lambda b, h, s, *_: (b * steps + s, h))

    def cast_spec(a):
        assert a.shape[0] % (n_steps * 2 * SUBLANES) == 0
        return pl.BlockSpec((a.shape[0] // n_steps, a.shape[1]), lambda b, h, s, *_: ((b * heads + h) * steps + s, 0))

    grid_spec = pltpu.PrefetchScalarGridSpec(
        num_scalar_prefetch=1,
        grid=(batch, heads, steps),
        in_specs=[blk, blk, blk, blk, pl.BlockSpec((1, HG_HEAD), lambda b, h, s, *_: (0, 0)),
                  cast_spec(cast_a), cast_spec(cast_b)],
        out_specs=[blk, cast_spec(cast_a), cast_spec(cast_b)],
        scratch_shapes=[
            pltpu.VMEM((HG_HEAD, HG_HEAD), F32),
            pltpu.VMEM((unroll, HG_CHUNK, HG_HEAD), F32),
            pltpu.VMEM((unroll, HG_CHUNK, HG_HEAD), F32),
        ],
    )
    return pl.pallas_call(
        functools.partial(_hg_rec_kernel, n_chunks=tc // HG_CHUNK, unroll_bounded=unroll_bounded,
                          unroll_exact=unroll_exact),
        grid_spec=grid_spec,
        out_shape=[jax.ShapeDtypeStruct((t, d), BF16), jax.ShapeDtypeStruct(cast_a.shape, BF16),
                   jax.ShapeDtypeStruct(cast_b.shape, BF16)],
        compiler_params=_cparams(("arbitrary", "arbitrary", "arbitrary")),
        name="hg_rec",
    )(wmin, q, k, lf, v, gnw, cast_a, cast_b)


def _head_rmsnorm(p, w_row):
    low = lax.broadcasted_iota(jnp.int32, (1, LANES), 1) < FOX_HEAD
    outs = []
    for j in range(p.shape[1] // LANES):
        pj = p[:, j * LANES:(j + 1) * LANES]
        ss = pj * pj
        s_lo = jnp.sum(jnp.where(low, ss, 0.0), axis=-1, keepdims=True)
        s_hi = jnp.sum(jnp.where(low, 0.0, ss), axis=-1, keepdims=True)
        inv = lax.rsqrt(jnp.where(low, s_lo, s_hi) * (1.0 / FOX_HEAD) + EPS)
        outs.append(pj * inv)
    return jnp.concatenate(outs, axis=1) * w_row


def _fox_proj_kernel(x_ref, mod_ref, nw_ref, qw_ref, kw_ref, bf_ref, wf32_ref, wf_ref,
                     q_ref, k_ref, v_ref, g_ref, lf_ref, vs_ref, w_ref, *, d):
    _cast_weight_once(wf32_ref.at[0], w_ref, d)
    mod = mod_ref[0]
    scale = LOG2E / np.sqrt(FOX_HEAD)
    for r0 in range(0, x_ref.shape[0], ROW_SLAB):
        rows = slice(r0, r0 + ROW_SLAB)
        h = _modulate(x_ref[rows, :], nw_ref[...], mod[:, 0:d], mod[:, d:2 * d]).astype(BF16)
        pq = jnp.dot(h, w_ref[:, 0:d], preferred_element_type=F32)
        q_ref[rows, :] = (_head_rmsnorm(pq, qw_ref[...]) * scale).astype(BF16)
        pk = jnp.dot(h, w_ref[:, d:2 * d], preferred_element_type=F32)
        k_ref[rows, :] = _head_rmsnorm(pk, kw_ref[...]).astype(BF16)
        vs_ref[...] = jnp.dot(h, w_ref[:, 2 * d:3 * d], preferred_element_type=F32)
        v_ref[:, rows] = vs_ref[...].T.astype(BF16)
        pg = jnp.dot(h, w_ref[:, 3 * d:4 * d], preferred_element_type=F32)
        g_ref[rows, :] = _sigmoid(pg).astype(BF16)
        u = jnp.dot(h, wf_ref[...], preferred_element_type=F32) + bf_ref[...]
        lf_ref[rows, :] = jnp.minimum(u, 0.0) - jnp.log1p(jnp.exp(-jnp.abs(u)))


def _fox_proj_call(x2, mod3, nw, qw, kw, bf, w_main, w_f, *, layer, w_index, batch, tm):
    t, d = x2.shape
    nh = w_f.shape[1]
    tiles_per_batch = (t // batch) // tm
    row = pl.BlockSpec((tm, d), lambda i: (i, 0))
    out_bf = jax.ShapeDtypeStruct((t, d), BF16)
    return pl.pallas_call(
        functools.partial(_fox_proj_kernel, d=d),
        grid=(t // tm,),
        in_specs=[
            row,
            pl.BlockSpec((1, 1, N_MOD * d), lambda i: (layer * batch + i // tiles_per_batch, 0, 0)),
            _resident((1, d)), _resident((1, d)), _resident((1, d)), _resident((1, nh)),
            pl.BlockSpec((1, d, 4 * d), lambda i: (w_index, 0, 0), pipeline_mode=pl.Buffered(1)),
            _resident(w_f.shape),
        ],
        out_specs=[row, row, pl.BlockSpec((d, tm), lambda i: (0, i)), row, pl.BlockSpec((tm, nh), lambda i: (i, 0))],
        out_shape=[out_bf, out_bf, jax.ShapeDtypeStruct((d, t), BF16), out_bf, jax.ShapeDtypeStruct((t, nh), F32)],
        scratch_shapes=[pltpu.VMEM((ROW_SLAB, d), F32),
                        pltpu.VMEM((d, 4 * d), BF16)],
        compiler_params=_cparams(("arbitrary",)),
        name="fox_proj",
    )(x2, mod3, nw, qw, kw, bf, w_main, w_f)


FOX_AUG = 6


def _split3(x):
    hi = x.astype(BF16)
    r1 = x - hi.astype(F32)
    mid = r1.astype(BF16)
    lo = (r1 - mid.astype(F32)).astype(BF16)
    return jnp.concatenate([hi, mid, lo], axis=1)


def _fox_bias_kernel(lf_ref, qw_ref, kw_ref, pq_ref, pk_ref, oq_ref, ok_ref,
                     aq_ref, ak_ref, edge_ref, flag_ref, *, tp):
    bound = FOX_BOUND_SLACK * LOG2E * np.sqrt(FOX_HEAD) * jnp.max(jnp.abs(qw_ref[...] * kw_ref[...]), axis=-1, keepdims=True)
    fast = bound <= FOX_STAB_MAX
    flag_ref[...] = jnp.broadcast_to(fast.astype(jnp.int32), flag_ref.shape)
    stab = jnp.where(fast, bound, 0.0)

    r_i = lax.broadcasted_iota(jnp.int32, (tp, tp), 0)
    c_i = lax.broadcasted_iota(jnp.int32, (tp, tp), 1)
    tri = (c_i <= r_i).astype(BF16)

    def tile(j, carry):
        rows = pl.ds(pl.multiple_of(j * tp, tp), tp)
        f3 = jnp.dot(tri, _split3(lf_ref[rows, :]), preferred_element_type=F32)
        f = f3[:, 0:LANES] + f3[:, LANES:2 * LANES] + f3[:, 2 * LANES:] + carry
        f2 = f * LOG2E
        edge_ref[j] = jnp.concatenate([f2[0:1, :], f2[tp - 1:tp, :]], axis=0)
        aq_ref[rows, :] = (jnp.dot(_split3(f2 - stab), pq_ref[...], preferred_element_type=F32)
                           + oq_ref[...]).astype(BF16)
        ak_ref[rows, :] = (jnp.dot(_split3(f2), pk_ref[...], preferred_element_type=F32)
                           + ok_ref[...]).astype(BF16)
        return f[tp - 1:tp, :]

    lax.fori_loop(0, lf_ref.shape[0] // tp, tile, jnp.zeros((1, LANES), F32))


def _fox_bias_call(lf, qw, kw, *, batch, d, tp):
    t = lf.shape[0]
    heads = d // FOX_HEAD
    steps = (t // batch) // tp
    pq = np.zeros((3 * LANES, LANES), np.float32)
    pk = np.zeros((3 * LANES, LANES), np.float32)
    oq = np.zeros((1, LANES), np.float32)
    ok = np.zeros((1, LANES), np.float32)
    for h in range(heads):
        base = FOX_AUG * h
        for i in range(3):
            pq[i * LANES + h, base + i] = 1.0
            ok[0, base + i] = 1.0
            oq[0, base + 3 + i] = 1.0
            pk[i * LANES + h, base + 3 + i] = -1.0
    whole = pl.BlockSpec((t // batch, LANES), lambda b: (b, 0))
    return pl.pallas_call(
        functools.partial(_fox_bias_kernel, tp=tp),
        grid=(batch,),
        in_specs=[
            whole,
            _resident((1, FOX_HEAD)), _resident((1, FOX_HEAD)),
            _resident(pq.shape), _resident(pk.shape), _resident(oq.shape), _resident(ok.shape),
        ],
        out_specs=[whole, whole,
                   pl.BlockSpec((steps, 2, LANES), lambda b: (b, 0, 0)),
                   pl.BlockSpec((SUBLANES, LANES), lambda b: (0, 0))],
        out_shape=[jax.ShapeDtypeStruct((t, LANES), BF16), jax.ShapeDtypeStruct((t, LANES), BF16),
                   jax.ShapeDtypeStruct((batch * steps, 2, LANES), F32),
                   jax.ShapeDtypeStruct((SUBLANES, LANES), jnp.int32)],
        compiler_params=_cparams(("arbitrary",)),
        name="fox_bias",
    )(lf, qw, kw, jnp.asarray(pq, BF16), jnp.asarray(pk, BF16), jnp.asarray(oq), jnp.asarray(ok))


FOX_VROWS = 80


def _fox_attn_kernel(flag_ref, fs_ref, fe_ref, q_ref, aq_ref, k_ref, ak_ref, v_ref, o_ref,
                     v1_ref, qm_ref, acc_ref, m_ref, pa_ref, pb_ref, *, tq, tk, heads):
    bi, pair, qi = pl.program_id(0), pl.program_id(1), pl.program_id(2)
    nk = k_ref.shape[0] // tk
    sub = tq // tk
    pad_rows = lax.broadcasted_iota(jnp.int32, (FOX_VROWS - FOX_HEAD, tk), 0)
    ones_blk = jnp.where(pad_rows == 0, 1.0, 0.0).astype(BF16)
    lane2 = lax.broadcasted_iota(jnp.int32, (1, 2 * LANES), 1)
    bias0 = LANES + 2 * FOX_AUG * pair
    head0 = (lane2 < FOX_HEAD) | ((lane2 >= bias0) & (lane2 < bias0 + FOX_AUG))
    head1 = ((lane2 >= FOX_HEAD) & (lane2 < LANES)) | ((lane2 >= bias0 + FOX_AUG) & (lane2 < bias0 + 2 * FOX_AUG))
    nt = (((1,), (1,)), ((), ()))

    @pl.when(qi == 0)
    def _():
        def build(j, carry):
            cols = pl.ds(pl.multiple_of(j * tk, tk), tk)
            for e in range(2):
                v1_ref[j, e] = jnp.concatenate([v_ref[e * FOX_HEAD:(e + 1) * FOX_HEAD, cols], ones_blk], axis=0)
            return carry

        lax.fori_loop(0, nk, build, 0)

    def probs(j, online, diag=None, dst=None):
        rows = pl.ds(pl.multiple_of(j * tk, tk), tk)
        kf = jnp.concatenate([k_ref[rows, :], ak_ref[rows, :]], axis=1)
        q_lo = 0 if diag is None else diag * tk
        ps, alphas = [], []
        for e in range(2):
            s = lax.dot_general(kf, qm_ref[e, q_lo:tq, :], nt, preferred_element_type=F32)
            if diag is not None:
                kpos = lax.broadcasted_iota(jnp.int32, (tk, tk), 0)
                qpos = lax.broadcasted_iota(jnp.int32, (tk, tk), 1)
                blk = jnp.where(kpos <= qpos, s[:, 0:tk], -jnp.inf)
                s = blk if q_lo + tk == tq else jnp.concatenate([blk, s[:, tk:]], axis=1)
            if online:
                m_prev = m_ref[e, :, q_lo:tq]
                m_new = jnp.maximum(m_prev, jnp.max(s, axis=0, keepdims=True))
                alphas.append(jnp.exp2(m_prev - m_new))
                m_ref[e, :, q_lo:tq] = m_new
                s = s - m_new
            p = jnp.exp2(s).astype(BF16)
            if dst is None:
                ps.append(p)
            else:
                dst[e] = p
        return ps, alphas

    def accumulate(ps, j, alphas=None, diag=None):
        q_lo = 0 if diag is None else diag * tk
        for e in range(2):
            pv = jnp.dot(v1_ref[j, e], ps[e], preferred_element_type=F32)
            if alphas:
                acc_ref[e, :, q_lo:tq] = acc_ref[e, :, q_lo:tq] * alphas[e] + pv
            else:
                acc_ref[e, :, q_lo:tq] += pv

    def start():
        qf = jnp.concatenate([q_ref[...], aq_ref[...]], axis=1)
        zero = jnp.zeros_like(qf)
        qm_ref[0] = jnp.where(head0, qf, zero)
        qm_ref[1] = jnp.where(head1, qf, zero)
        acc_ref[...] = jnp.zeros_like(acc_ref)

    def finish():
        outs = []
        for e in range(2):
            acc = acc_ref[e]
            outs.append(acc[0:FOX_HEAD, :] / acc[FOX_HEAD:FOX_HEAD + 1, :])
        o_ref[...] = jnp.concatenate(outs, axis=0).astype(BF16)

    n_full = qi * sub

    @pl.when(flag_ref[0] == 1)
    def _():
        start()
        held = None
        for a in range(sub - 1, 0, -1):
            nxt = probs(n_full + a, False, diag=a)[0]
            if held is not None:
                accumulate(held[0], n_full + held[1], diag=held[1])
            held = (nxt, a)
        probs(n_full, False, diag=0, dst=pa_ref)
        if held is not None:
            accumulate(held[0], n_full + held[1], diag=held[1])

        h0 = bi * heads + 2 * pair
        fq0 = fs_ref[h0 * nk + n_full]
        fq1 = fs_ref[(h0 + 1) * nk + n_full]

        def dead(j, cnt):
            gone0 = fq0 - fe_ref[h0 * nk + j] < -FOX_SKIP
            gone1 = fq1 - fe_ref[(h0 + 1) * nk + j] < -FOX_SKIP
            return cnt + jnp.logical_and(gone0, gone1).astype(jnp.int32)

        first = lax.fori_loop(0, n_full, dead, 0)
        n_live = n_full - first

        def two_tiles(i, carry):
            j = first + 2 * i
            probs(j, False, dst=pb_ref)
            accumulate(pa_ref, jnp.where(i == 0, n_full, j - 1))
            probs(j + 1, False, dst=pa_ref)
            accumulate(pb_ref, j)
            return carry

        lax.fori_loop(0, n_live // 2, two_tiles, 0)

        @pl.when(n_live % 2 == 1)
        def _():
            ps, _ = probs(n_full - 1, False)
            accumulate(pa_ref, jnp.where(n_live == 1, n_full, n_full - 2))
            accumulate(ps, n_full - 1)

        @pl.when(n_live % 2 == 0)
        def _():
            accumulate(pa_ref, jnp.where(n_live == 0, n_full, n_full - 1))

        finish()

    @pl.when(flag_ref[0] != 1)
    def _():
        start()
        m_ref[...] = jnp.full_like(m_ref, -jnp.inf)

        def body(j, carry):
            ps, alphas = probs(j, True)
            accumulate(ps, j, alphas)
            return carry

        lax.fori_loop(0, n_full, body, 0)
        for a in range(sub):
            ps, alphas = probs(n_full + a, True, diag=a)
            accumulate(ps, n_full + a, alphas, diag=a)
        finish()


def _fox_attn_call(flag, f_first, f_last, q, aq, k, ak, v, *, batch, seq, tq, tk):
    t, d = q.shape
    pairs = d // LANES
    nq = seq // tq
    q_spec = pl.BlockSpec((tq, LANES), lambda b, p, i, *_: (b * nq + i, p))
    kv_spec = pl.BlockSpec((seq, LANES), lambda b, p, i, *_: (b, p))
    grid_spec = pltpu.PrefetchScalarGridSpec(
        num_scalar_prefetch=3,
        grid=(batch, pairs, nq),
        in_specs=[q_spec, pl.BlockSpec((tq, LANES), lambda b, p, i, *_: (b * nq + i, 0)),
                  kv_spec, pl.BlockSpec((seq, LANES), lambda b, p, i, *_: (b, 0)),
                  pl.BlockSpec((LANES, seq), lambda b, p, i, *_: (p, b))],
        out_specs=pl.BlockSpec((LANES, tq), lambda b, p, i, *_: (p, b * nq + i)),
        scratch_shapes=[
            pltpu.VMEM((seq // tk, 2, FOX_VROWS, tk), BF16),
            pltpu.VMEM((2, tq, 2 * LANES), BF16),
            pltpu.VMEM((2, FOX_VROWS, tq), F32),
            pltpu.VMEM((2, 1, tq), F32),
            pltpu.VMEM((2, tk, tq), BF16),
            pltpu.VMEM((2, tk, tq), BF16),
        ],
    )
    return pl.pallas_call(
        functools.partial(_fox_attn_kernel, tq=tq, tk=tk, heads=d // FOX_HEAD),
        grid_spec=grid_spec,
        out_shape=jax.ShapeDtypeStruct((d, t), BF16),
        compiler_params=_cparams(("arbitrary", "arbitrary", "arbitrary")),
        name="fox_attn",
    )(flag, f_first, f_last, q, aq, k, ak, v)


def _post_kernel(x_ref, y_ref, gate_ref, mod_ref, nw_ref, fw_ref, wo_ref, w1_ref, w2_ref, o_ref, *,
                 d, ff_blk, final, y_features):
    mod = mod_ref[0]
    g1 = mod[:, 2 * d:3 * d]
    sh2, sc2, g2 = mod[:, 3 * d:4 * d], mod[:, 4 * d:5 * d], mod[:, 5 * d:6 * d]
    y = y_ref[...].astype(F32)
    if y_features:
        y = y.T
    yg = (y * gate_ref[...].astype(F32)).astype(BF16)
    x1 = x_ref[...] + g1 * jnp.dot(yg, wo_ref[...], preferred_element_type=F32)
    h = _modulate(x1, nw_ref[...], sh2, sc2).astype(BF16)
    acc = jnp.zeros_like(x1)
    for j in range(w1_ref.shape[2] // ff_blk):
        a = jnp.maximum(jnp.dot(h, w1_ref[0, :, j * ff_blk:(j + 1) * ff_blk], preferred_element_type=F32), 0.0)
        acc = acc + jnp.dot((a * a).astype(BF16), w2_ref[0, j * ff_blk:(j + 1) * ff_blk, :],
                            preferred_element_type=F32)
    x2 = x1 + g2 * acc
    if final:
        x2 = x2 * lax.rsqrt(jnp.mean(x2 * x2, axis=-1, keepdims=True) + EPS) * fw_ref[...]
    o_ref[...] = x2


def _post_call(x2, y, gate, mod3, nw, fw, w_out, w1, w2, *, layer, batch, tm, final, y_features):
    t, d = x2.shape
    tiles_per_batch = (t // batch) // tm
    row = pl.BlockSpec((tm, d), lambda i: (i, 0))
    assert y.shape == ((d, t) if y_features else (t, d))
    return pl.pallas_call(
        functools.partial(_post_kernel, d=d, ff_blk=POST_FF_BLK, final=final, y_features=y_features),
        grid=(t // tm,),
        in_specs=[
            row,
            pl.BlockSpec((d, tm), lambda i: (0, i)) if y_features else row,
            row,
            pl.BlockSpec((1, 1, N_MOD * d), lambda i: (layer * batch + i // tiles_per_batch, 0, 0)),
            _resident((1, d)), _resident((1, d)),
            _resident(w_out.shape),
            pl.BlockSpec((1,) + w1.shape[1:], lambda i: (layer, 0, 0), pipeline_mode=pl.Buffered(1)),
            pl.BlockSpec((1,) + w2.shape[1:], lambda i: (layer, 0, 0), pipeline_mode=pl.Buffered(1)),
        ],
        out_specs=row,
        out_shape=jax.ShapeDtypeStruct((t, d), F32),
        compiler_params=_cparams(("arbitrary",)),
        name="post",
    )(x2, y, gate, mod3, nw, fw, w_out, w1, w2)


def kernel(x, c, w_mod, b_mod, norm1_w, norm2_w, hg_w_in, hg_w_out, hg_lb, hg_gn_w, fox_w_in, fox_b_f,
           fox_qn_w, fox_kn_w, fox_w_out, mlp_w1, mlp_w2, final_w):
    batch, seq, d = x.shape
    depth = w_mod.shape[0]
    t = batch * seq
    fox_heads = d // FOX_HEAD
    assert seq % HG_BLOCK == 0 and seq % ATTN_TQ == 0 and HG_BLOCK % PROJ_TM == 0 and PROJ_TM % ROW_SLAB == 0
    assert d % LANES == 0 and fox_heads * FOX_AUG <= LANES and (N_MOD * d) % MOD_TN == 0
    assert mlp_w1.shape[2] % POST_FF_BLK == 0 and x.dtype == F32

    mod3 = _mod_call(c, w_mod, b_mod).reshape(depth * batch, 1, N_MOD * d)
    xs = x.reshape(t, d)
    fw = final_w.reshape(1, d)
    ff = mlp_w1.shape[2]

    for i in range(depth):
        j = i // 2
        n1 = norm1_w[i].reshape(1, d)
        if i % 2 == 0:
            q, k, lf, v, g, wmin = _hg_proj_call(xs, mod3, n1, hg_lb, hg_w_in[j],
                                                 layer=i, batch=batch, tm=PROJ_TM)
            wmin = wmin.reshape(batch, seq // HG_BLOCK, HG_BLOCK // PROJ_TM, d // HG_HEAD, HG_HEAD).min(axis=(2, 4))
            y, w1_all, w2_all = _hg_rec_call(wmin.transpose(0, 2, 1).reshape(-1), q, k, lf, v,
                                             hg_gn_w[j].reshape(1, HG_HEAD),
                                             mlp_w1.reshape(depth * d, ff), mlp_w2.reshape(depth * ff, d),
                                             batch=batch, tc=HG_BLOCK,
                                             unroll_bounded=HG_GROUP, unroll_exact=HG_GROUP_EXACT)
            w1_all, w2_all = w1_all.reshape(depth, d, ff), w2_all.reshape(depth, ff, d)
            w_out = hg_w_out[j]
        else:
            pad = ((0, 0), (0, LANES - fox_heads))
            q, k, v, g, lf = _fox_proj_call(
                xs, mod3, n1,
                jnp.tile(fox_qn_w[j], fox_heads).reshape(1, d),
                jnp.tile(fox_kn_w[j], fox_heads).reshape(1, d),
                jnp.pad(fox_b_f[j].reshape(1, fox_heads), pad),
                fox_w_in, jnp.pad(fox_w_in[j, :, 4 * d:], pad).astype(BF16),
                layer=i, w_index=j, batch=batch, tm=PROJ_TM)
            aq, ak, edge, flag = _fox_bias_call(lf, fox_qn_w[j].reshape(1, FOX_HEAD),
                                                fox_kn_w[j].reshape(1, FOX_HEAD), batch=batch, d=d, tp=ATTN_TK)
            edge = edge.reshape(batch, seq // ATTN_TK, 2, LANES)[..., :fox_heads].transpose(2, 0, 3, 1)
            y = _fox_attn_call(flag[0, :1], edge[0].reshape(-1), edge[1].reshape(-1), q, aq, k, ak, v,
                               batch=batch, seq=seq, tq=ATTN_TQ, tk=ATTN_TK)
            w_out = fox_w_out[j]
        xs = _post_call(xs, y, g, mod3, norm2_w[i].reshape(1, d), fw, w_out.astype(BF16), w1_all, w2_all,
                        layer=i, batch=batch, tm=PROJ_TM, final=(i == depth - 1), y_features=(i % 2 == 1))
    return xs.reshape(batch, seq, d)
```

```python
import functools

import numpy as np
import jax
import jax.numpy as jnp
from jax import lax
from jax.experimental import pallas as pl
from jax.experimental.pallas import tpu as pltpu

F32 = jnp.float32
BF16 = jnp.bfloat16
EPS = 1e-6
N_MOD = 6
HG_HEAD = 128
HG_CHUNK = 64
HG_SUB = 16
HG_SAFE_DECAY = 56.0
FOX_HEAD = 64
LANES = 128
SUBLANES = 8
LOG2E = float(np.log2(np.e))
FOX_STAB_MAX = 30.0
FOX_BOUND_SLACK = 1.01
FOX_SKIP = 160.0
ROW_SLAB = 256
VMEM_LIMIT = 56 * 1024 * 1024

PROJ_TM = 512
POST_FF_BLK = 1024
MOD_TN = 1024
HG_BLOCK = 2048
HG_GROUP = 32
HG_GROUP_EXACT = 4
ATTN_TK = 512
ATTN_TQ = 2 * ATTN_TK

_HI = lax.Precision.HIGHEST


def _cparams(sem):
    return pltpu.CompilerParams(dimension_semantics=sem, vmem_limit_bytes=VMEM_LIMIT)


def _resident(shape):
    nd = len(shape)
    return pl.BlockSpec(shape, lambda *_: (0,) * nd, pipeline_mode=pl.Buffered(1))


def _sigmoid(x):
    return 0.5 * jnp.tanh(0.5 * x) + 0.5


def _cast_weight_once(w_ref, wb_ref, cols, transposed=False):
    @pl.when(pl.program_id(0) == 0)
    def _():
        for c0 in range(0, wb_ref.shape[1], cols):
            src = w_ref[c0:c0 + cols, :].T if transposed else w_ref[:, c0:c0 + cols]
            wb_ref[:, c0:c0 + cols] = src.astype(BF16)


def _modulate(x, nw, shift, scale):
    ms = jnp.mean(x * x, axis=-1, keepdims=True)
    y = x * lax.rsqrt(ms + EPS)
    return (y * nw) * (1.0 + scale) + shift


def _mod_kernel(ct_ref, w_ref, b_ref, o_ref):
    ct = ct_ref[...]
    cat = ct * _sigmoid(ct)
    w = w_ref[0]
    rows = []
    for b in range(ct.shape[1]):
        col = jnp.broadcast_to(cat[:, b:b + 1], (ct.shape[0], LANES))
        rows.append(jnp.concatenate(
            [jnp.sum(w[:, j * LANES:(j + 1) * LANES] * col, axis=0, keepdims=True)
             for j in range(w.shape[1] // LANES)], axis=1))
    o_ref[0] = jnp.concatenate(rows, axis=0) + b_ref[0]


def _mod_call(c, w_mod, b_mod):
    depth, d, n = w_mod.shape
    b = c.shape[0]
    tn = MOD_TN
    return pl.pallas_call(
        _mod_kernel,
        grid=(depth, n // tn),
        in_specs=[
            pl.BlockSpec((d, b), lambda l, j: (0, 0)),
            pl.BlockSpec((1, d, tn), lambda l, j: (l, 0, j)),
            pl.BlockSpec((1, 1, tn), lambda l, j: (l, 0, j)),
        ],
        out_specs=pl.BlockSpec((1, b, tn), lambda l, j: (l, 0, j)),
        out_shape=jax.ShapeDtypeStruct((depth, b, n), F32),
        compiler_params=_cparams(("arbitrary", "arbitrary")),
        name="mod",
    )(c.T, w_mod, b_mod.reshape(depth, 1, n))


def _hg_proj_kernel(x_ref, mod_ref, nw_ref, lb_ref, wf32_ref, q_ref, k_ref, lf_ref, v_ref, g_ref, wmin_ref,
                    w_ref, *, layer, d):
    _cast_weight_once(wf32_ref, w_ref, d)
    mod = mod_ref[0]
    lbp = lb_ref[...]
    e = jnp.exp(lbp - jnp.max(lbp, axis=0, keepdims=True))
    lb = jnp.sum(e[0:layer + 1], axis=0, keepdims=True) / jnp.sum(e, axis=0, keepdims=True)

    w_r = lax.broadcasted_iota(jnp.int32, (ROW_SLAB // HG_SUB, ROW_SLAB), 0)
    w_c = lax.broadcasted_iota(jnp.int32, (ROW_SLAB // HG_SUB, ROW_SLAB), 1)
    win = (lax.shift_right_logical(w_c, HG_SUB.bit_length() - 1) == w_r).astype(BF16)
    lfs = []

    for r0 in range(0, x_ref.shape[0], ROW_SLAB):
        rows = slice(r0, r0 + ROW_SLAB)
        h = _modulate(x_ref[rows, :], nw_ref[...], mod[:, 0:d], mod[:, d:2 * d]).astype(BF16)

        pq = jnp.dot(h, w_ref[:, 0:d], preferred_element_type=F32)
        q_ref[rows, :] = (pq * _sigmoid(pq)).astype(BF16)

        z = jnp.dot(h, w_ref[:, d:2 * d], preferred_element_type=F32)
        th = 0.5 * jnp.tanh(0.5 * z)
        kk = (1.0 - lb) * (0.5 - th)
        f = lb + (1.0 - lb) * (0.5 + th)
        lf = jnp.log(jnp.where(kk < 0.5, 1.0 - kk, f))
        lf_ref[rows, :] = lf
        k_ref[rows, :] = kk.astype(BF16)
        lfs.append(lf.astype(BF16))

        v_ref[rows, :] = jnp.dot(h, w_ref[:, 2 * d:3 * d], preferred_element_type=F32).astype(BF16)
        pg = jnp.dot(h, w_ref[:, 3 * d:4 * d], preferred_element_type=F32)
        g_ref[rows, :] = (pg * _sigmoid(pg)).astype(BF16)

    wmin = None
    for lf16 in lfs:
        wsum = jnp.min(jnp.dot(win, lf16, preferred_element_type=F32), axis=0, keepdims=True)
        wmin = wsum if wmin is None else jnp.minimum(wmin, wsum)
    wmin_ref[0] = wmin


def _hg_proj_call(x2, mod3, nw, hg_lb, w_in, *, layer, batch, tm):
    t, d = x2.shape
    tiles_per_batch = (t // batch) // tm
    row = pl.BlockSpec((tm, d), lambda i: (i, 0))
    out_bf = jax.ShapeDtypeStruct((t, d), BF16)
    return pl.pallas_call(
        functools.partial(_hg_proj_kernel, layer=layer, d=d),
        grid=(t // tm,),
        in_specs=[
            row,
            pl.BlockSpec((1, 1, N_MOD * d), lambda i: (layer * batch + i // tiles_per_batch, 0, 0)),
            _resident((1, d)),
            _resident(hg_lb.shape),
            _resident(w_in.shape),
        ],
        out_specs=[row, row, row, row, row, pl.BlockSpec((1, 1, d), lambda i: (i, 0, 0))],
        out_shape=[out_bf, out_bf, jax.ShapeDtypeStruct((t, d), F32), out_bf, out_bf,
                   jax.ShapeDtypeStruct((t // tm, 1, d), F32)],
        scratch_shapes=[pltpu.VMEM(w_in.shape, BF16)],
        compiler_params=_cparams(("arbitrary",)),
        name="hg_proj",
    )(x2, mod3, nw, hg_lb, w_in)


def _hg_rec_kernel(wmin_ref, q_ref, k_ref, lf_ref, v_ref, gnw_ref, ca_ref, cb_ref, o_ref, cao_ref, cbo_ref,
                   st_ref, gs_ref, ks_ref, *, n_chunks, unroll_bounded, unroll_exact):
    cao_ref[...] = ca_ref[...].astype(BF16)
    cbo_ref[...] = cb_ref[...].astype(BF16)
    c, sub = HG_CHUNK, HG_SUB
    n_sub = c // sub
    half = sub // 2

    @pl.when(pl.program_id(2) == 0)
    def _():
        st_ref[...] = jnp.zeros_like(st_ref)

    r_i = lax.broadcasted_iota(jnp.int32, (c, c), 0)
    c_i = lax.broadcasted_iota(jnp.int32, (c, c), 1)
    tri = (c_i <= r_i).astype(BF16)
    ones_w = jnp.ones((HG_HEAD, c), BF16)
    half_row = lax.broadcasted_iota(jnp.int32, (half, HG_HEAD), 0)
    a_lane = lax.broadcasted_iota(jnp.int32, (half, c), 1)
    gnw = gnw_ref[...]
    nt = (((1,), (1,)), ((), ()))
    tn = (((0,), (0,)), ((), ()))

    def offdiag_operands(q, k, gcum, j, bounded):
        lo = j * sub
        hi = lo + sub if bounded else lo
        g_b = gcum[lo - 1:lo, :] if j > 0 else jnp.zeros((1, HG_HEAD), F32)
        qh = (q[lo:lo + sub, :] * jnp.exp2(gcum[lo:lo + sub, :] - g_b)).astype(BF16)
        kh = (k[0:hi, :] * jnp.exp2(g_b - gcum[0:hi, :])).astype(BF16)
        if hi < c:
            kh = jnp.concatenate([kh, jnp.zeros((c - hi, HG_HEAD), BF16)], axis=0)
        return qh, kh

    def diag_products(q, k, gcum, slot, j):
        lo = j * sub
        gt = (gcum[lo:lo + half, :], gcum[lo + half:lo + sub, :])
        qt = (q[lo:lo + half, :], q[lo + half:lo + sub, :])
        ps = []
        for s in range(sub):
            g_s = gs_ref[slot, pl.ds(lo + s, 1), :]
            k_s = ks_ref[slot, pl.ds(lo + s, 1), :]
            for hf in range(2):
                if s >= half and hf == 0:
                    continue
                dlt = gt[hf] - g_s
                if (s >= half) == (hf == 1):
                    dlt = jnp.where(half_row >= s % half, dlt, -jnp.inf)
                ps.append((qt[hf] * k_s) * jnp.exp2(dlt))
        return jnp.concatenate(ps, axis=0).astype(BF16)

    def diag_scatter(r, j):
        lo = j * sub
        a_top = jnp.zeros((half, c), F32)
        a_bot = jnp.zeros((half, c), F32)
        for s in range(sub):
            if s < half:
                a_top = jnp.where(a_lane == lo + s, r[2 * s * half:(2 * s + 1) * half, :], a_top)
                a_bot = jnp.where(a_lane == lo + s, r[(2 * s + 1) * half:(2 * s + 2) * half, :], a_bot)
            else:
                a_bot = jnp.where(a_lane == lo + s, r[(half + s) * half:(half + s + 1) * half, :], a_bot)
        return jnp.concatenate([a_top, a_bot], axis=0)

    def run(bounded, unroll):
        def group(i, st):
            us = range(unroll)
            rows = [pl.ds(pl.multiple_of((i * unroll + u) * c, c), c) for u in us]
            g3 = [jnp.dot(tri, _split3(lf_ref[r, :]), preferred_element_type=F32) for r in rows]
            q = [q_ref[r, :].astype(F32) for r in rows]
            k = [k_ref[r, :].astype(F32) for r in rows]
            gcum = [(g[:, 0:HG_HEAD] + g[:, HG_HEAD:2 * HG_HEAD] + g[:, 2 * HG_HEAD:]) * LOG2E for g in g3]
            g_last = [g[c - 1:c, :] for g in gcum]
            qg = [(q[u] * jnp.exp2(gcum[u])).astype(BF16) for u in us]
            kd = [(k[u] * jnp.exp2(g_last[u] - gcum[u])).astype(BF16) for u in us]
            first = 0 if bounded else 1
            off = [[offdiag_operands(q[u], k[u], gcum[u], j, bounded) for j in range(first, n_sub)] for u in us]
            if not bounded:
                for u in us:
                    gs_ref[u] = gcum[u]
                    ks_ref[u] = k[u]
                pst = [[diag_products(q[u], k[u], gcum[u], u, j) for j in range(n_sub)] for u in us]
            kv = [lax.dot_general(v_ref[rows[u], :], kd[u], tn, preferred_element_type=F32) for u in us]
            blocks = [[lax.dot_general(qh, kh, nt, preferred_element_type=F32) for qh, kh in off[u]] for u in us]
            if bounded:
                a = [jnp.where(c_i <= r_i, jnp.concatenate(blocks[u], axis=0), 0.0) for u in us]
            else:
                sums = [[jnp.dot(p, ones_w, preferred_element_type=F32) for p in pst[u]] for u in us]
                a = []
                for u in us:
                    a_rows = [diag_scatter(sums[u][j], j) for j in range(n_sub)]
                    for j in range(1, n_sub):
                        a_rows[j] = a_rows[j] + blocks[u][j - 1]
                    a.append(jnp.concatenate(a_rows, axis=0))
            o_intra = [jnp.dot(a[u].astype(BF16), v_ref[rows[u], :], preferred_element_type=F32) for u in us]
            for u in us:
                o = o_intra[u] + lax.dot_general(qg[u], st.astype(BF16), nt, preferred_element_type=F32)
                st = st * jnp.exp2(g_last[u]) + kv[u]
                on = o * lax.rsqrt(jnp.mean(o * o, axis=-1, keepdims=True) + EPS) * gnw
                o_ref[rows[u], :] = on.astype(BF16)
            return st

        st_ref[...] = lax.fori_loop(0, n_chunks // unroll, group, st_ref[...])

    blk_id = (pl.program_id(0) * pl.num_programs(1) + pl.program_id(1)) * pl.num_programs(2) + pl.program_id(2)
    bounded = wmin_ref[blk_id] * LOG2E >= -HG_SAFE_DECAY

    @pl.when(bounded)
    def _():
        run(True, unroll_bounded)

    @pl.when(jnp.logical_not(bounded))
    def _():
        run(False, unroll_exact)


def _hg_rec_call(wmin, q, k, lf, v, gnw, cast_a, cast_b, *, batch, tc, unroll_bounded, unroll_exact):
    t, d = q.shape
    heads = d // HG_HEAD
    steps = (t // batch) // tc
    n_steps = batch * heads * steps
    unroll = unroll_exact
    blk = pl.BlockSpec((tc, HG_HEAD), lambda b, h, s, *_: (b * steps + s, h))

    def cast_spec(a):
        assert a.shape[0] % (n_steps * 2 * SUBLANES) == 0
        return pl.BlockSpec((a.shape[0] // n_steps, a.shape[1]), lambda b, h, s, *_: ((b * heads + h) * steps + s, 0))

    grid_spec = pltpu.PrefetchScalarGridSpec(
        num_scalar_prefetch=1,
        grid=(batch, heads, steps),
        in_specs=[blk, blk, blk, blk, pl.BlockSpec((1, HG_HEAD), lambda b, h, s, *_: (0, 0)),
                  cast_spec(cast_a), cast_spec(cast_b)],
        out_specs=[blk, cast_spec(cast_a), cast_spec(cast_b)],
        scratch_shapes=[
            pltpu.VMEM((HG_HEAD, HG_HEAD), F32),
            pltpu.VMEM((unroll, HG_CHUNK, HG_HEAD), F32),
            pltpu.VMEM((unroll, HG_CHUNK, HG_HEAD), F32),
        ],
    )
    return pl.pallas_call(
        functools.partial(_hg_rec_kernel, n_chunks=tc // HG_CHUNK, unroll_bounded=unroll_bounded,
                          unroll_exact=unroll_exact),
        grid_spec=grid_spec,
        out_shape=[jax.ShapeDtypeStruct((t, d), BF16), jax.ShapeDtypeStruct(cast_a.shape, BF16),
                   jax.ShapeDtypeStruct(cast_b.shape, BF16)],
        compiler_params=_cparams(("arbitrary", "arbitrary", "arbitrary")),
        name="hg_rec",
    )(wmin, q, k, lf, v, gnw, cast_a, cast_b)


def _head_rmsnorm(p, w_row):
    low = lax.broadcasted_iota(jnp.int32, (1, LANES), 1) < FOX_HEAD
    outs = []
    for j in range(p.shape[1] // LANES):
        pj = p[:, j * LANES:(j + 1) * LANES]
        ss = pj * pj
        s_lo = jnp.sum(jnp.where(low, ss, 0.0), axis=-1, keepdims=True)
        s_hi = jnp.sum(jnp.where(low, 0.0, ss), axis=-1, keepdims=True)
        inv = lax.rsqrt(jnp.where(low, s_lo, s_hi) * (1.0 / FOX_HEAD) + EPS)
        outs.append(pj * inv)
    return jnp.concatenate(outs, axis=1) * w_row


def _fox_proj_kernel(x_ref, mod_ref, nw_ref, qw_ref, kw_ref, bf_ref, wf32_ref, wf_ref,
                     q_ref, k_ref, v_ref, g_ref, lf_ref, vs_ref, w_ref, *, d):
    _cast_weight_once(wf32_ref.at[0], w_ref, d, transposed=True)
    mod = mod_ref[0]
    scale = LOG2E / np.sqrt(FOX_HEAD)
    for r0 in range(0, x_ref.shape[0], ROW_SLAB):
        rows = slice(r0, r0 + ROW_SLAB)
        h = _modulate(x_ref[rows, :], nw_ref[...], mod[:, 0:d], mod[:, d:2 * d]).astype(BF16)
        pq = jnp.dot(h, w_ref[:, 0:d], preferred_element_type=F32)
        q_ref[rows, :] = (_head_rmsnorm(pq, qw_ref[...]) * scale).astype(BF16)
        pk = jnp.dot(h, w_ref[:, d:2 * d], preferred_element_type=F32)
        k_ref[rows, :] = _head_rmsnorm(pk, kw_ref[...]).astype(BF16)
        vs_ref[...] = jnp.dot(h, w_ref[:, 2 * d:3 * d], preferred_element_type=F32)
        v_ref[:, rows] = vs_ref[...].T.astype(BF16)
        pg = jnp.dot(h, w_ref[:, 3 * d:4 * d], preferred_element_type=F32)
        g_ref[rows, :] = _sigmoid(pg).astype(BF16)
        u = jnp.dot(h, wf_ref[...], preferred_element_type=F32) + bf_ref[...]
        lf_ref[rows, :] = jnp.minimum(u, 0.0) - jnp.log1p(jnp.exp(-jnp.abs(u)))


def _fox_proj_call(x2, mod3, nw, qw, kw, bf, w_main, w_f, *, layer, w_index, batch, tm):
    t, d = x2.shape
    nh = w_f.shape[1]
    tiles_per_batch = (t // batch) // tm
    row = pl.BlockSpec((tm, d), lambda i: (i, 0))
    out_bf = jax.ShapeDtypeStruct((t, d), BF16)
    return pl.pallas_call(
        functools.partial(_fox_proj_kernel, d=d),
        grid=(t // tm,),
        in_specs=[
            row,
            pl.BlockSpec((1, 1, N_MOD * d), lambda i: (layer * batch + i // tiles_per_batch, 0, 0)),
            _resident((1, d)), _resident((1, d)), _resident((1, d)), _resident((1, nh)),
            pl.BlockSpec((1, 4 * d, d), lambda i: (w_index, 0, 0), pipeline_mode=pl.Buffered(1)),
            _resident(w_f.shape),
        ],
        out_specs=[row, row, pl.BlockSpec((d, tm), lambda i: (0, i)), row, pl.BlockSpec((tm, nh), lambda i: (i, 0))],
        out_shape=[out_bf, out_bf, jax.ShapeDtypeStruct((d, t), BF16), out_bf, jax.ShapeDtypeStruct((t, nh), F32)],
        scratch_shapes=[pltpu.VMEM((ROW_SLAB, d), F32),
                        pltpu.VMEM((d, 4 * d), BF16)],
        compiler_params=_cparams(("arbitrary",)),
        name="fox_proj",
    )(x2, mod3, nw, qw, kw, bf, w_main, w_f)


FOX_AUG = 6


def _split3(x):
    hi = x.astype(BF16)
    r1 = x - hi.astype(F32)
    mid = r1.astype(BF16)
    lo = (r1 - mid.astype(F32)).astype(BF16)
    return jnp.concatenate([hi, mid, lo], axis=1)


def _fox_bias_kernel(lf_ref, qw_ref, kw_ref, pq_ref, pk_ref, oq_ref, ok_ref,
                     aq_ref, ak_ref, edge_ref, flag_ref, *, tp):
    bound = FOX_BOUND_SLACK * LOG2E * np.sqrt(FOX_HEAD) * jnp.max(jnp.abs(qw_ref[...] * kw_ref[...]), axis=-1, keepdims=True)
    fast = bound <= FOX_STAB_MAX
    flag_ref[...] = jnp.broadcast_to(fast.astype(jnp.int32), flag_ref.shape)
    stab = jnp.where(fast, bound, 0.0)

    r_i = lax.broadcasted_iota(jnp.int32, (tp, tp), 0)
    c_i = lax.broadcasted_iota(jnp.int32, (tp, tp), 1)
    tri = (c_i <= r_i).astype(BF16)

    def tile(j, carry):
        rows = pl.ds(pl.multiple_of(j * tp, tp), tp)
        f3 = jnp.dot(tri, _split3(lf_ref[rows, :]), preferred_element_type=F32)
        f = f3[:, 0:LANES] + f3[:, LANES:2 * LANES] + f3[:, 2 * LANES:] + carry
        f2 = f * LOG2E
        edge_ref[j] = jnp.concatenate([f2[0:1, :], f2[tp - 1:tp, :]], axis=0)
        aq_ref[rows, :] = (jnp.dot(_split3(f2 - stab), pq_ref[...], preferred_element_type=F32)
                           + oq_ref[...]).astype(BF16)
        ak_ref[rows, :] = (jnp.dot(_split3(f2), pk_ref[...], preferred_element_type=F32)
                           + ok_ref[...]).astype(BF16)
        return f[tp - 1:tp, :]

    lax.fori_loop(0, lf_ref.shape[0] // tp, tile, jnp.zeros((1, LANES), F32))


def _fox_bias_call(lf, qw, kw, *, batch, d, tp):
    t = lf.shape[0]
    heads = d // FOX_HEAD
    steps = (t // batch) // tp
    pq = np.zeros((3 * LANES, LANES), np.float32)
    pk = np.zeros((3 * LANES, LANES), np.float32)
    oq = np.zeros((1, LANES), np.float32)
    ok = np.zeros((1, LANES), np.float32)
    for h in range(heads):
        base = FOX_AUG * h
        for i in range(3):
            pq[i * LANES + h, base + i] = 1.0
            ok[0, base + i] = 1.0
            oq[0, base + 3 + i] = 1.0
            pk[i * LANES + h, base + 3 + i] = -1.0
    whole = pl.BlockSpec((t // batch, LANES), lambda b: (b, 0))
    return pl.pallas_call(
        functools.partial(_fox_bias_kernel, tp=tp),
        grid=(batch,),
        in_specs=[
            whole,
            _resident((1, FOX_HEAD)), _resident((1, FOX_HEAD)),
            _resident(pq.shape), _resident(pk.shape), _resident(oq.shape), _resident(ok.shape),
        ],
        out_specs=[whole, whole,
                   pl.BlockSpec((steps, 2, LANES), lambda b: (b, 0, 0)),
                   pl.BlockSpec((SUBLANES, LANES), lambda b: (0, 0))],
        out_shape=[jax.ShapeDtypeStruct((t, LANES), BF16), jax.ShapeDtypeStruct((t, LANES), BF16),
                   jax.ShapeDtypeStruct((batch * steps, 2, LANES), F32),
                   jax.ShapeDtypeStruct((SUBLANES, LANES), jnp.int32)],
        compiler_params=_cparams(("arbitrary",)),
        name="fox_bias",
    )(lf, qw, kw, jnp.asarray(pq, BF16), jnp.asarray(pk, BF16), jnp.asarray(oq), jnp.asarray(ok))


FOX_VROWS = 80


def _fox_attn_kernel(flag_ref, fs_ref, fe_ref, q_ref, aq_ref, k_ref, ak_ref, v_ref, o_ref,
                     v1_ref, qm_ref, acc_ref, m_ref, pa_ref, pb_ref, *, tq, tk, heads):
    bi, pair, qi = pl.program_id(0), pl.program_id(1), pl.program_id(2)
    nk = k_ref.shape[0] // tk
    sub = tq // tk
    pad_rows = lax.broadcasted_iota(jnp.int32, (FOX_VROWS - FOX_HEAD, tk), 0)
    ones_blk = jnp.where(pad_rows == 0, 1.0, 0.0).astype(BF16)
    lane2 = lax.broadcasted_iota(jnp.int32, (1, 2 * LANES), 1)
    bias0 = LANES + 2 * FOX_AUG * pair
    head0 = (lane2 < FOX_HEAD) | ((lane2 >= bias0) & (lane2 < bias0 + FOX_AUG))
    head1 = ((lane2 >= FOX_HEAD) & (lane2 < LANES)) | ((lane2 >= bias0 + FOX_AUG) & (lane2 < bias0 + 2 * FOX_AUG))
    nt = (((1,), (1,)), ((), ()))

    @pl.when(qi == 0)
    def _():
        def build(j, carry):
            cols = pl.ds(pl.multiple_of(j * tk, tk), tk)
            for e in range(2):
                v1_ref[j, e] = jnp.concatenate([v_ref[e * FOX_HEAD:(e + 1) * FOX_HEAD, cols], ones_blk], axis=0)
            return carry

        lax.fori_loop(0, nk, build, 0)

    def probs(j, online, diag=None, dst=None):
        rows = pl.ds(pl.multiple_of(j * tk, tk), tk)
        kf = jnp.concatenate([k_ref[rows, :], ak_ref[rows, :]], axis=1)
        q_lo = 0 if diag is None else diag * tk
        ps, alphas = [], []
        for e in range(2):
            s = lax.dot_general(kf, qm_ref[e, q_lo:tq, :], nt, preferred_element_type=F32)
            if diag is not None:
                kpos = lax.broadcasted_iota(jnp.int32, (tk, tk), 0)
                qpos = lax.broadcasted_iota(jnp.int32, (tk, tk), 1)
                blk = jnp.where(kpos <= qpos, s[:, 0:tk], -jnp.inf)
                s = blk if q_lo + tk == tq else jnp.concatenate([blk, s[:, tk:]], axis=1)
            if online:
                m_prev = m_ref[e, :, q_lo:tq]
                m_new = jnp.maximum(m_prev, jnp.max(s, axis=0, keepdims=True))
                alphas.append(jnp.exp2(m_prev - m_new))
                m_ref[e, :, q_lo:tq] = m_new
                s = s - m_new
            p = jnp.exp2(s).astype(BF16)
            if dst is None:
                ps.append(p)
            else:
                dst[e] = p
        return ps, alphas

    def accumulate(ps, j, alphas=None, diag=None):
        q_lo = 0 if diag is None else diag * tk
        for e in range(2):
            pv = jnp.dot(v1_ref[j, e], ps[e], preferred_element_type=F32)
            if alphas:
                acc_ref[e, :, q_lo:tq] = acc_ref[e, :, q_lo:tq] * alphas[e] + pv
            else:
                acc_ref[e, :, q_lo:tq] += pv

    def start():
        qf = jnp.concatenate([q_ref[...], aq_ref[...]], axis=1)
        zero = jnp.zeros_like(qf)
        qm_ref[0] = jnp.where(head0, qf, zero)
        qm_ref[1] = jnp.where(head1, qf, zero)
        acc_ref[...] = jnp.zeros_like(acc_ref)

    def finish():
        outs = []
        for e in range(2):
            acc = acc_ref[e]
            outs.append(acc[0:FOX_HEAD, :] / acc[FOX_HEAD:FOX_HEAD + 1, :])
        o_ref[...] = jnp.concatenate(outs, axis=0).astype(BF16)

    n_full = qi * sub

    @pl.when(flag_ref[0] == 1)
    def _():
        start()
        held = None
        for a in range(sub - 1, 0, -1):
            nxt = probs(n_full + a, False, diag=a)[0]
            if held is not None:
                accumulate(held[0], n_full + held[1], diag=held[1])
            held = (nxt, a)
        probs(n_full, False, diag=0, dst=pa_ref)
        if held is not None:
            accumulate(held[0], n_full + held[1], diag=held[1])

        h0 = bi * heads + 2 * pair
        fq0 = fs_ref[h0 * nk + n_full]
        fq1 = fs_ref[(h0 + 1) * nk + n_full]

        def dead(j, cnt):
            gone0 = fq0 - fe_ref[h0 * nk + j] < -FOX_SKIP
            gone1 = fq1 - fe_ref[(h0 + 1) * nk + j] < -FOX_SKIP
            return cnt + jnp.logical_and(gone0, gone1).astype(jnp.int32)

        first = lax.fori_loop(0, n_full, dead, 0)
        n_live = n_full - first

        def two_tiles(i, carry):
            j = first + 2 * i
            probs(j, False, dst=pb_ref)
            accumulate(pa_ref, jnp.where(i == 0, n_full, j - 1))
            probs(j + 1, False, dst=pa_ref)
            accumulate(pb_ref, j)
            return carry

        lax.fori_loop(0, n_live // 2, two_tiles, 0)

        @pl.when(n_live % 2 == 1)
        def _():
            ps, _ = probs(n_full - 1, False)
            accumulate(pa_ref, jnp.where(n_live == 1, n_full, n_full - 2))
            accumulate(ps, n_full - 1)

        @pl.when(n_live % 2 == 0)
        def _():
            accumulate(pa_ref, jnp.where(n_live == 0, n_full, n_full - 1))

        finish()

    @pl.when(flag_ref[0] != 1)
    def _():
        start()
        m_ref[...] = jnp.full_like(m_ref, -jnp.inf)

        def body(j, carry):
            ps, alphas = probs(j, True)
            accumulate(ps, j, alphas)
            return carry

        lax.fori_loop(0, n_full, body, 0)
        for a in range(sub):
            ps, alphas = probs(n_full + a, True, diag=a)
            accumulate(ps, n_full + a, alphas, diag=a)
        finish()


def _fox_attn_call(flag, f_first, f_last, q, aq, k, ak, v, *, batch, seq, tq, tk):
    t, d = q.shape
    pairs = d // LANES
    nq = seq // tq
    q_spec = pl.BlockSpec((tq, LANES), lambda b, p, i, *_: (b * nq + i, p))
    kv_spec = pl.BlockSpec((seq, LANES), lambda b, p, i, *_: (b, p))
    grid_spec = pltpu.PrefetchScalarGridSpec(
        num_scalar_prefetch=3,
        grid=(batch, pairs, nq),
        in_specs=[q_spec, pl.BlockSpec((tq, LANES), lambda b, p, i, *_: (b * nq + i, 0)),
                  kv_spec, pl.BlockSpec((seq, LANES), lambda b, p, i, *_: (b, 0)),
                  pl.BlockSpec((LANES, seq), lambda b, p, i, *_: (p, b))],
        out_specs=pl.BlockSpec((LANES, tq), lambda b, p, i, *_: (p, b * nq + i)),
        scratch_shapes=[
            pltpu.VMEM((seq // tk, 2, FOX_VROWS, tk), BF16),
            pltpu.VMEM((2, tq, 2 * LANES), BF16),
            pltpu.VMEM((2, FOX_VROWS, tq), F32),
            pltpu.VMEM((2, 1, tq), F32),
            pltpu.VMEM((2, tk, tq), BF16),
            pltpu.VMEM((2, tk, tq), BF16),
        ],
    )
    return pl.pallas_call(
        functools.partial(_fox_attn_kernel, tq=tq, tk=tk, heads=d // FOX_HEAD),
        grid_spec=grid_spec,
        out_shape=jax.ShapeDtypeStruct((d, t), BF16),
        compiler_params=_cparams(("arbitrary", "arbitrary", "arbitrary")),
        name="fox_attn",
    )(flag, f_first, f_last, q, aq, k, ak, v)


def _post_kernel(x_ref, y_ref, gate_ref, mod_ref, nw_ref, fw_ref, wo_ref, w1_ref, w2_ref, o_ref, *,
                 d, ff_blk, final, y_features):
    mod = mod_ref[0]
    g1 = mod[:, 2 * d:3 * d]
    sh2, sc2, g2 = mod[:, 3 * d:4 * d], mod[:, 4 * d:5 * d], mod[:, 5 * d:6 * d]
    y = y_ref[...].astype(F32)
    if y_features:
        y = y.T
    yg = (y * gate_ref[...].astype(F32)).astype(BF16)
    x1 = x_ref[...] + g1 * jnp.dot(yg, wo_ref[...], preferred_element_type=F32)
    h = _modulate(x1, nw_ref[...], sh2, sc2).astype(BF16)
    acc = jnp.zeros_like(x1)
    for j in range(w1_ref.shape[2] // ff_blk):
        a = jnp.maximum(jnp.dot(h, w1_ref[0, :, j * ff_blk:(j + 1) * ff_blk], preferred_element_type=F32), 0.0)
        acc = acc + jnp.dot((a * a).astype(BF16), w2_ref[0, j * ff_blk:(j + 1) * ff_blk, :],
                            preferred_element_type=F32)
    x2 = x1 + g2 * acc
    if final:
        x2 = x2 * lax.rsqrt(jnp.mean(x2 * x2, axis=-1, keepdims=True) + EPS) * fw_ref[...]
    o_ref[...] = x2


def _post_call(x2, y, gate, mod3, nw, fw, w_out, w1, w2, *, layer, batch, tm, final, y_features):
    t, d = x2.shape
    tiles_per_batch = (t // batch) // tm
    row = pl.BlockSpec((tm, d), lambda i: (i, 0))
    assert y.shape == ((d, t) if y_features else (t, d))
    return pl.pallas_call(
        functools.partial(_post_kernel, d=d, ff_blk=POST_FF_BLK, final=final, y_features=y_features),
        grid=(t // tm,),
        in_specs=[
            row,
            pl.BlockSpec((d, tm), lambda i: (0, i)) if y_features else row,
            row,
            pl.BlockSpec((1, 1, N_MOD * d), lambda i: (layer * batch + i // tiles_per_batch, 0, 0)),
            _resident((1, d)), _resident((1, d)),
            _resident(w_out.shape),
            pl.BlockSpec((1,) + w1.shape[1:], lambda i: (layer, 0, 0), pipeline_mode=pl.Buffered(1)),
            pl.BlockSpec((1,) + w2.shape[1:], lambda i: (layer, 0, 0), pipeline_mode=pl.Buffered(1)),
        ],
        out_specs=row,
        out_shape=jax.ShapeDtypeStruct((t, d), F32),
        compiler_params=_cparams(("arbitrary",)),
        name="post",
    )(x2, y, gate, mod3, nw, fw, w_out, w1, w2)


def kernel(x, c, w_mod, b_mod, norm1_w, norm2_w, hg_w_in, hg_w_out, hg_lb, hg_gn_w, fox_w_in, fox_b_f,
           fox_qn_w, fox_kn_w, fox_w_out, mlp_w1, mlp_w2, final_w):
    batch, seq, d = x.shape
    depth = w_mod.shape[0]
    t = batch * seq
    fox_heads = d // FOX_HEAD
    assert seq % HG_BLOCK == 0 and seq % ATTN_TQ == 0 and HG_BLOCK % PROJ_TM == 0 and PROJ_TM % ROW_SLAB == 0
    assert d % LANES == 0 and fox_heads * FOX_AUG <= LANES and (N_MOD * d) % MOD_TN == 0
    assert mlp_w1.shape[2] % POST_FF_BLK == 0 and x.dtype == F32

    mod3 = _mod_call(c, w_mod, b_mod).reshape(depth * batch, 1, N_MOD * d)
    xs = x.reshape(t, d)
    fw = final_w.reshape(1, d)
    ff = mlp_w1.shape[2]

    for i in range(depth):
        j = i // 2
        n1 = norm1_w[i].reshape(1, d)
        if i % 2 == 0:
            q, k, lf, v, g, wmin = _hg_proj_call(xs, mod3, n1, hg_lb, hg_w_in[j],
                                                 layer=i, batch=batch, tm=PROJ_TM)
            wmin = wmin.reshape(batch, seq // HG_BLOCK, HG_BLOCK // PROJ_TM, d // HG_HEAD, HG_HEAD).min(axis=(2, 4))
            y, w1_all, w2_all = _hg_rec_call(wmin.transpose(0, 2, 1).reshape(-1), q, k, lf, v,
                                             hg_gn_w[j].reshape(1, HG_HEAD),
                                             mlp_w1.reshape(depth * d, ff), mlp_w2.reshape(depth * ff, d),
                                             batch=batch, tc=HG_BLOCK,
                                             unroll_bounded=HG_GROUP, unroll_exact=HG_GROUP_EXACT)
            w1_all, w2_all = w1_all.reshape(depth, d, ff), w2_all.reshape(depth, ff, d)
            w_out = hg_w_out[j]
        else:
            pad = ((0, 0), (0, LANES - fox_heads))
            q, k, v, g, lf = _fox_proj_call(
                xs, mod3, n1,
                jnp.tile(fox_qn_w[j], fox_heads).reshape(1, d),
                jnp.tile(fox_kn_w[j], fox_heads).reshape(1, d),
                jnp.pad(fox_b_f[j].reshape(1, fox_heads), pad),
                jnp.swapaxes(fox_w_in, 1, 2), jnp.pad(fox_w_in[j, :, 4 * d:], pad).astype(BF16),
                layer=i, w_index=j, batch=batch, tm=PROJ_TM)
            aq, ak, edge, flag = _fox_bias_call(lf, fox_qn_w[j].reshape(1, FOX_HEAD),
                                                fox_kn_w[j].reshape(1, FOX_HEAD), batch=batch, d=d, tp=ATTN_TK)
            edge = edge.reshape(batch, seq // ATTN_TK, 2, LANES)[..., :fox_heads].transpose(2, 0, 3, 1)
            y = _fox_attn_call(flag[0, :1], edge[0].reshape(-1), edge[1].reshape(-1), q, aq, k, ak, v,
                               batch=batch, seq=seq, tq=ATTN_TQ, tk=ATTN_TK)
            w_out = fox_w_out[j]
        xs = _post_call(xs, y, g, mod3, norm2_w[i].reshape(1, d), fw, w_out.astype(BF16), w1_all, w2_all,
                        layer=i, batch=batch, tm=PROJ_TM, final=(i == depth - 1), y_features=(i % 2 == 1))
    return xs.reshape(batch, seq, d)
```

```python
import functools

import numpy as np
import jax
import jax.numpy as jnp
from jax import lax
from jax.experimental import pallas as pl
from jax.experimental.pallas import tpu as pltpu

F32 = jnp.float32
BF16 = jnp.bfloat16
EPS = 1e-6
N_MOD = 6
HG_HEAD = 128
HG_CHUNK = 64
HG_SUB = 16
HG_SAFE_DECAY = 56.0
FOX_HEAD = 64
LANES = 128
SUBLANES = 8
LOG2E = float(np.log2(np.e))
FOX_STAB_MAX = 30.0
FOX_BOUND_SLACK = 1.01
FOX_SKIP = 160.0
ROW_SLAB = 256
VMEM_LIMIT = 56 * 1024 * 1024

PROJ_TM = 512
POST_FF_BLK = 1024
MOD_TN = 2048
HG_BLOCK = 2048
HG_GROUP = 32
HG_GROUP_EXACT = 4
ATTN_TK = 512
ATTN_TQ = 2 * ATTN_TK


def _cparams(sem):
    return pltpu.CompilerParams(dimension_semantics=sem, vmem_limit_bytes=VMEM_LIMIT)


def _resident(shape):
    nd = len(shape)
    return pl.BlockSpec(shape, lambda *_: (0,) * nd, pipeline_mode=pl.Buffered(1))


def _sigmoid(x):
    return 0.5 * jnp.tanh(0.5 * x) + 0.5


def _cast_weight_once(w_ref, wb_ref, cols, transposed=False):
    @pl.when(pl.program_id(0) == 0)
    def _():
        for c0 in range(0, wb_ref.shape[1], cols):
            src = w_ref[c0:c0 + cols, :].T if transposed else w_ref[:, c0:c0 + cols]
            wb_ref[:, c0:c0 + cols] = src.astype(BF16)


def _modulate(x, nw, shift, scale):
    ms = jnp.mean(x * x, axis=-1, keepdims=True)
    y = x * lax.rsqrt(ms + EPS)
    return (y * nw) * (1.0 + scale) + shift


def _mod_kernel(ct_ref, w_ref, b_ref, o_ref):
    ct = ct_ref[...]
    cat = ct * _sigmoid(ct)
    w = w_ref[0]
    rows = []
    for b in range(ct.shape[1]):
        col = jnp.broadcast_to(cat[:, b:b + 1], (ct.shape[0], LANES))
        rows.append(jnp.concatenate(
            [jnp.sum(w[:, j * LANES:(j + 1) * LANES] * col, axis=0, keepdims=True)
             for j in range(w.shape[1] // LANES)], axis=1))
    o_ref[0] = jnp.concatenate(rows, axis=0) + b_ref[0]


def _mod_call(c, w_mod, b_mod):
    depth, d, n = w_mod.shape
    b = c.shape[0]
    tn = MOD_TN
    return pl.pallas_call(
        _mod_kernel,
        grid=(depth, n // tn),
        in_specs=[
            pl.BlockSpec((d, b), lambda l, j: (0, 0)),
            pl.BlockSpec((1, d, tn), lambda l, j: (l, 0, j)),
            pl.BlockSpec((1, 1, tn), lambda l, j: (l, 0, j)),
        ],
        out_specs=pl.BlockSpec((1, b, tn), lambda l, j: (l, 0, j)),
        out_shape=jax.ShapeDtypeStruct((depth, b, n), F32),
        compiler_params=_cparams(("arbitrary", "arbitrary")),
        name="mod",
    )(c.T, w_mod, b_mod.reshape(depth, 1, n))


def _hg_proj_kernel(x_ref, mod_ref, nw_ref, lb_ref, wf32_ref, q_ref, k_ref, lf_ref, v_ref, g_ref, wmin_ref,
                    w_ref, *, layer, d):
    _cast_weight_once(wf32_ref, w_ref, d)
    mod = mod_ref[0]
    lbp = lb_ref[...]
    e = jnp.exp(lbp - jnp.max(lbp, axis=0, keepdims=True))
    lb = jnp.sum(e[0:layer + 1], axis=0, keepdims=True) / jnp.sum(e, axis=0, keepdims=True)

    w_r = lax.broadcasted_iota(jnp.int32, (ROW_SLAB // HG_SUB, ROW_SLAB), 0)
    w_c = lax.broadcasted_iota(jnp.int32, (ROW_SLAB // HG_SUB, ROW_SLAB), 1)
    win = (lax.shift_right_logical(w_c, HG_SUB.bit_length() - 1) == w_r).astype(BF16)
    lfs = []

    for r0 in range(0, x_ref.shape[0], ROW_SLAB):
        rows = slice(r0, r0 + ROW_SLAB)
        h = _modulate(x_ref[rows, :], nw_ref[...], mod[:, 0:d], mod[:, d:2 * d]).astype(BF16)

        pq = jnp.dot(h, w_ref[:, 0:d], preferred_element_type=F32)
        q_ref[rows, :] = (pq * _sigmoid(pq)).astype(BF16)

        z = jnp.dot(h, w_ref[:, d:2 * d], preferred_element_type=F32)
        th = 0.5 * jnp.tanh(0.5 * z)
        kk = (1.0 - lb) * (0.5 - th)
        f = lb + (1.0 - lb) * (0.5 + th)
        lf = jnp.log(jnp.where(kk < 0.5, 1.0 - kk, f))
        lf_ref[rows, :] = lf
        k_ref[rows, :] = kk.astype(BF16)
        lfs.append(lf.astype(BF16))

        v_ref[rows, :] = jnp.dot(h, w_ref[:, 2 * d:3 * d], preferred_element_type=F32).astype(BF16)
        pg = jnp.dot(h, w_ref[:, 3 * d:4 * d], preferred_element_type=F32)
        g_ref[rows, :] = (pg * _sigmoid(pg)).astype(BF16)

    wmin = None
    for lf16 in lfs:
        wsum = jnp.min(jnp.dot(win, lf16, preferred_element_type=F32), axis=0, keepdims=True)
        wmin = wsum if wmin is None else jnp.minimum(wmin, wsum)
    wmin_ref[0] = wmin


def _hg_proj_call(x2, mod3, nw, hg_lb, w_in, *, layer, batch, tm):
    t, d = x2.shape
    tiles_per_batch = (t // batch) // tm
    row = pl.BlockSpec((tm, d), lambda i: (i, 0))
    out_bf = jax.ShapeDtypeStruct((t, d), BF16)
    return pl.pallas_call(
        functools.partial(_hg_proj_kernel, layer=layer, d=d),
        grid=(t // tm,),
        in_specs=[
            row,
            pl.BlockSpec((1, 1, N_MOD * d), lambda i: (layer * batch + i // tiles_per_batch, 0, 0)),
            _resident((1, d)),
            _resident(hg_lb.shape),
            _resident(w_in.shape),
        ],
        out_specs=[row, row, row, row, row, pl.BlockSpec((1, 1, d), lambda i: (i, 0, 0))],
        out_shape=[out_bf, out_bf, jax.ShapeDtypeStruct((t, d), F32), out_bf, out_bf,
                   jax.ShapeDtypeStruct((t // tm, 1, d), F32)],
        scratch_shapes=[pltpu.VMEM(w_in.shape, BF16)],
        compiler_params=_cparams(("arbitrary",)),
        name="hg_proj",
    )(x2, mod3, nw, hg_lb, w_in)


def _hg_rec_kernel(wmin_ref, q_ref, k_ref, lf_ref, v_ref, gnw_ref, ca_ref, cb_ref, o_ref, cao_ref, cbo_ref,
                   st_ref, gs_ref, ks_ref, *, n_chunks, unroll_bounded, unroll_exact):
    cao_ref[...] = ca_ref[...].astype(BF16)
    cbo_ref[...] = cb_ref[...].astype(BF16)
    c, sub = HG_CHUNK, HG_SUB
    n_sub = c // sub
    half = sub // 2

    @pl.when(pl.program_id(2) == 0)
    def _():
        st_ref[...] = jnp.zeros_like(st_ref)

    r_i = lax.broadcasted_iota(jnp.int32, (c, c), 0)
    c_i = lax.broadcasted_iota(jnp.int32, (c, c), 1)
    tri = (c_i <= r_i).astype(BF16)
    ones_w = jnp.ones((HG_HEAD, c), BF16)
    half_row = lax.broadcasted_iota(jnp.int32, (half, HG_HEAD), 0)
    a_lane = lax.broadcasted_iota(jnp.int32, (half, c), 1)
    gnw = gnw_ref[...]
    nt = (((1,), (1,)), ((), ()))
    tn = (((0,), (0,)), ((), ()))

    def offdiag_operands(q, k, gcum, j, bounded):
        lo = j * sub
        hi = lo + sub if bounded else lo
        g_b = gcum[lo - 1:lo, :] if j > 0 else jnp.zeros((1, HG_HEAD), F32)
        qh = (q[lo:lo + sub, :] * jnp.exp2(gcum[lo:lo + sub, :] - g_b)).astype(BF16)
        kh = (k[0:hi, :] * jnp.exp2(g_b - gcum[0:hi, :])).astype(BF16)
        if hi < c:
            kh = jnp.concatenate([kh, jnp.zeros((c - hi, HG_HEAD), BF16)], axis=0)
        return qh, kh

    def diag_products(q, k, gcum, slot, j):
        lo = j * sub
        gt = (gcum[lo:lo + half, :], gcum[lo + half:lo + sub, :])
        qt = (q[lo:lo + half, :], q[lo + half:lo + sub, :])
        ps = []
        for s in range(sub):
            g_s = gs_ref[slot, pl.ds(lo + s, 1), :]
            k_s = ks_ref[slot, pl.ds(lo + s, 1), :]
            for hf in range(2):
                if s >= half and hf == 0:
                    continue
                dlt = gt[hf] - g_s
                if (s >= half) == (hf == 1):
                    dlt = jnp.where(half_row >= s % half, dlt, -jnp.inf)
                ps.append((qt[hf] * k_s) * jnp.exp2(dlt))
        return jnp.concatenate(ps, axis=0).astype(BF16)

    def diag_scatter(r, j):
        lo = j * sub
        a_top = jnp.zeros((half, c), F32)
        a_bot = jnp.zeros((half, c), F32)
        for s in range(sub):
            if s < half:
                a_top = jnp.where(a_lane == lo + s, r[2 * s * half:(2 * s + 1) * half, :], a_top)
                a_bot = jnp.where(a_lane == lo + s, r[(2 * s + 1) * half:(2 * s + 2) * half, :], a_bot)
            else:
                a_bot = jnp.where(a_lane == lo + s, r[(half + s) * half:(half + s + 1) * half, :], a_bot)
        return jnp.concatenate([a_top, a_bot], axis=0)

    def run(bounded, unroll):
        def group(i, st):
            us = range(unroll)
            rows = [pl.ds(pl.multiple_of((i * unroll + u) * c, c), c) for u in us]
            g3 = [jnp.dot(tri, _split3(lf_ref[r, :]), preferred_element_type=F32) for r in rows]
            q = [q_ref[r, :].astype(F32) for r in rows]
            k = [k_ref[r, :].astype(F32) for r in rows]
            gcum = [(g[:, 0:HG_HEAD] + g[:, HG_HEAD:2 * HG_HEAD] + g[:, 2 * HG_HEAD:]) * LOG2E for g in g3]
            g_last = [g[c - 1:c, :] for g in gcum]
            qg = [(q[u] * jnp.exp2(gcum[u])).astype(BF16) for u in us]
            kd = [(k[u] * jnp.exp2(g_last[u] - gcum[u])).astype(BF16) for u in us]
            first = 0 if bounded else 1
            off = [[offdiag_operands(q[u], k[u], gcum[u], j, bounded) for j in range(first, n_sub)] for u in us]
            if not bounded:
                for u in us:
                    gs_ref[u] = gcum[u]
                    ks_ref[u] = k[u]
                pst = [[diag_products(q[u], k[u], gcum[u], u, j) for j in range(n_sub)] for u in us]
            kv = [lax.dot_general(v_ref[rows[u], :], kd[u], tn, preferred_element_type=F32) for u in us]
            blocks = [[lax.dot_general(qh, kh, nt, preferred_element_type=F32) for qh, kh in off[u]] for u in us]
            if bounded:
                a = [jnp.where(c_i <= r_i, jnp.concatenate(blocks[u], axis=0), 0.0) for u in us]
            else:
                sums = [[jnp.dot(p, ones_w, preferred_element_type=F32) for p in pst[u]] for u in us]
                a = []
                for u in us:
                    a_rows = [diag_scatter(sums[u][j], j) for j in range(n_sub)]
                    for j in range(1, n_sub):
                        a_rows[j] = a_rows[j] + blocks[u][j - 1]
                    a.append(jnp.concatenate(a_rows, axis=0))
            o_intra = [jnp.dot(a[u].astype(BF16), v_ref[rows[u], :], preferred_element_type=F32) for u in us]
            for u in us:
                o = o_intra[u] + lax.dot_general(qg[u], st.astype(BF16), nt, preferred_element_type=F32)
                st = st * jnp.exp2(g_last[u]) + kv[u]
                on = o * lax.rsqrt(jnp.mean(o * o, axis=-1, keepdims=True) + EPS) * gnw
                o_ref[rows[u], :] = on.astype(BF16)
            return st

        st_ref[...] = lax.fori_loop(0, n_chunks // unroll, group, st_ref[...])

    blk_id = (pl.program_id(0) * pl.num_programs(1) + pl.program_id(1)) * pl.num_programs(2) + pl.program_id(2)
    bounded = wmin_ref[blk_id] * LOG2E >= -HG_SAFE_DECAY

    @pl.when(bounded)
    def _():
        run(True, unroll_bounded)

    @pl.when(jnp.logical_not(bounded))
    def _():
        run(False, unroll_exact)


def _hg_rec_call(wmin, q, k, lf, v, gnw, cast_a, cast_b, *, batch, tc, unroll_bounded, unroll_exact):
    t, d = q.shape
    heads = d // HG_HEAD
    steps = (t // batch) // tc
    n_steps = batch * heads * steps
    unroll = unroll_exact
    blk = pl.BlockSpec((tc, HG_HEAD), lambda b, h, s, *_: (b * steps + s, h))

    def cast_spec(a):
        assert a.shape[0] % (n_steps * 2 * SUBLANES) == 0
        return pl.BlockSpec((a.shape[0] // n_steps, a.shape[1]), lambda b, h, s, *_: ((b * heads + h) * steps + s, 0))

    grid_spec = pltpu.PrefetchScalarGridSpec(
        num_scalar_prefetch=1,
        grid=(batch, heads, steps),
        in_specs=[blk, blk, blk, blk, pl.BlockSpec((1, HG_HEAD), lambda b, h, s, *_: (0, 0)),
                  cast_spec(cast_a), cast_spec(cast_b)],
        out_specs=[blk, cast_spec(cast_a), cast_spec(cast_b)],
        scratch_shapes=[
            pltpu.VMEM((HG_HEAD, HG_HEAD), F32),
            pltpu.VMEM((unroll, HG_CHUNK, HG_HEAD), F32),
            pltpu.VMEM((unroll, HG_CHUNK, HG_HEAD), F32),
        ],
    )
    return pl.pallas_call(
        functools.partial(_hg_rec_kernel, n_chunks=tc // HG_CHUNK, unroll_bounded=unroll_bounded,
                          unroll_exact=unroll_exact),
        grid_spec=grid_spec,
        out_shape=[jax.ShapeDtypeStruct((t, d), BF16), jax.ShapeDtypeStruct(cast_a.shape, BF16),
                   jax.ShapeDtypeStruct(cast_b.shape, BF16)],
        compiler_params=_cparams(("arbitrary", "arbitrary", "arbitrary")),
        name="hg_rec",
    )(wmin, q, k, lf, v, gnw, cast_a, cast_b)


def _head_rmsnorm(p, w_row):
    low = lax.broadcasted_iota(jnp.int32, (1, LANES), 1) < FOX_HEAD
    outs = []
    for j in range(p.shape[1] // LANES):
        pj = p[:, j * LANES:(j + 1) * LANES]
        ss = pj * pj
        s_lo = jnp.sum(jnp.where(low, ss, 0.0), axis=-1, keepdims=True)
        s_hi = jnp.sum(jnp.where(low, 0.0, ss), axis=-1, keepdims=True)
        inv = lax.rsqrt(jnp.where(low, s_lo, s_hi) * (1.0 / FOX_HEAD) + EPS)
        outs.append(pj * inv)
    return jnp.concatenate(outs, axis=1) * w_row


def _fox_proj_kernel(x_ref, mod_ref, nw_ref, qw_ref, kw_ref, bf_ref, wf32_ref, wf_ref,
                     q_ref, k_ref, v_ref, g_ref, lf_ref, vs_ref, w_ref, *, d):
    _cast_weight_once(wf32_ref.at[0], w_ref, d, transposed=True)
    mod = mod_ref[0]
    scale = LOG2E / np.sqrt(FOX_HEAD)
    for r0 in range(0, x_ref.shape[0], ROW_SLAB):
        rows = slice(r0, r0 + ROW_SLAB)
        h = _modulate(x_ref[rows, :], nw_ref[...], mod[:, 0:d], mod[:, d:2 * d]).astype(BF16)
        pq = jnp.dot(h, w_ref[:, 0:d], preferred_element_type=F32)
        q_ref[rows, :] = (_head_rmsnorm(pq, qw_ref[...]) * scale).astype(BF16)
        pk = jnp.dot(h, w_ref[:, d:2 * d], preferred_element_type=F32)
        k_ref[rows, :] = _head_rmsnorm(pk, kw_ref[...]).astype(BF16)
        vs_ref[...] = jnp.dot(h, w_ref[:, 2 * d:3 * d], preferred_element_type=F32)
        v_ref[:, rows] = vs_ref[...].T.astype(BF16)
        pg = jnp.dot(h, w_ref[:, 3 * d:4 * d], preferred_element_type=F32)
        g_ref[rows, :] = _sigmoid(pg).astype(BF16)
        u = jnp.dot(h, wf_ref[...], preferred_element_type=F32) + bf_ref[...]
        lf_ref[rows, :] = jnp.minimum(u, 0.0) - jnp.log1p(jnp.exp(-jnp.abs(u)))


def _fox_proj_call(x2, mod3, nw, qw, kw, bf, w_main, w_f, *, layer, w_index, batch, tm):
    t, d = x2.shape
    nh = w_f.shape[1]
    tiles_per_batch = (t // batch) // tm
    row = pl.BlockSpec((tm, d), lambda i: (i, 0))
    out_bf = jax.ShapeDtypeStruct((t, d), BF16)
    return pl.pallas_call(
        functools.partial(_fox_proj_kernel, d=d),
        grid=(t // tm,),
        in_specs=[
            row,
            pl.BlockSpec((1, 1, N_MOD * d), lambda i: (layer * batch + i // tiles_per_batch, 0, 0)),
            _resident((1, d)), _resident((1, d)), _resident((1, d)), _resident((1, nh)),
            pl.BlockSpec((1, 4 * d, d), lambda i: (w_index, 0, 0), pipeline_mode=pl.Buffered(1)),
            _resident(w_f.shape),
        ],
        out_specs=[row, row, pl.BlockSpec((d, tm), lambda i: (0, i)), row, pl.BlockSpec((tm, nh), lambda i: (i, 0))],
        out_shape=[out_bf, out_bf, jax.ShapeDtypeStruct((d, t), BF16), out_bf, jax.ShapeDtypeStruct((t, nh), F32)],
        scratch_shapes=[pltpu.VMEM((ROW_SLAB, d), F32),
                        pltpu.VMEM((d, 4 * d), BF16)],
        compiler_params=_cparams(("arbitrary",)),
        name="fox_proj",
    )(x2, mod3, nw, qw, kw, bf, w_main, w_f)


FOX_AUG = 6


def _split3(x):
    hi = x.astype(BF16)
    r1 = x - hi.astype(F32)
    mid = r1.astype(BF16)
    lo = (r1 - mid.astype(F32)).astype(BF16)
    return jnp.concatenate([hi, mid, lo], axis=1)


def _fox_bias_kernel(lf_ref, qw_ref, kw_ref, pq_ref, pk_ref, oq_ref, ok_ref,
                     aq_ref, ak_ref, edge_ref, flag_ref, *, tp):
    bound = FOX_BOUND_SLACK * LOG2E * np.sqrt(FOX_HEAD) * jnp.max(jnp.abs(qw_ref[...] * kw_ref[...]), axis=-1, keepdims=True)
    fast = bound <= FOX_STAB_MAX
    flag_ref[...] = jnp.broadcast_to(fast.astype(jnp.int32), flag_ref.shape)
    stab = jnp.where(fast, bound, 0.0)

    r_i = lax.broadcasted_iota(jnp.int32, (tp, tp), 0)
    c_i = lax.broadcasted_iota(jnp.int32, (tp, tp), 1)
    tri = (c_i <= r_i).astype(BF16)

    def tile(j, carry):
        rows = pl.ds(pl.multiple_of(j * tp, tp), tp)
        f3 = jnp.dot(tri, _split3(lf_ref[rows, :]), preferred_element_type=F32)
        f = f3[:, 0:LANES] + f3[:, LANES:2 * LANES] + f3[:, 2 * LANES:] + carry
        f2 = f * LOG2E
        edge_ref[j] = jnp.concatenate([f2[0:1, :], f2[tp - 1:tp, :]], axis=0)
        aq_ref[rows, :] = (jnp.dot(_split3(f2 - stab), pq_ref[...], preferred_element_type=F32)
                           + oq_ref[...]).astype(BF16)
        ak_ref[rows, :] = (jnp.dot(_split3(f2), pk_ref[...], preferred_element_type=F32)
                           + ok_ref[...]).astype(BF16)
        return f[tp - 1:tp, :]

    lax.fori_loop(0, lf_ref.shape[0] // tp, tile, jnp.zeros((1, LANES), F32))


def _fox_bias_call(lf, qw, kw, *, batch, d, tp):
    t = lf.shape[0]
    heads = d // FOX_HEAD
    steps = (t // batch) // tp
    pq = np.zeros((3 * LANES, LANES), np.float32)
    pk = np.zeros((3 * LANES, LANES), np.float32)
    oq = np.zeros((1, LANES), np.float32)
    ok = np.zeros((1, LANES), np.float32)
    for h in range(heads):
        base = FOX_AUG * h
        for i in range(3):
            pq[i * LANES + h, base + i] = 1.0
            ok[0, base + i] = 1.0
            oq[0, base + 3 + i] = 1.0
            pk[i * LANES + h, base + 3 + i] = -1.0
    whole = pl.BlockSpec((t // batch, LANES), lambda b: (b, 0))
    return pl.pallas_call(
        functools.partial(_fox_bias_kernel, tp=tp),
        grid=(batch,),
        in_specs=[
            whole,
            _resident((1, FOX_HEAD)), _resident((1, FOX_HEAD)),
            _resident(pq.shape), _resident(pk.shape), _resident(oq.shape), _resident(ok.shape),
        ],
        out_specs=[whole, whole,
                   pl.BlockSpec((steps, 2, LANES), lambda b: (b, 0, 0)),
                   pl.BlockSpec((SUBLANES, LANES), lambda b: (0, 0))],
        out_shape=[jax.ShapeDtypeStruct((t, LANES), BF16), jax.ShapeDtypeStruct((t, LANES), BF16),
                   jax.ShapeDtypeStruct((batch * steps, 2, LANES), F32),
                   jax.ShapeDtypeStruct((SUBLANES, LANES), jnp.int32)],
        compiler_params=_cparams(("arbitrary",)),
        name="fox_bias",
    )(lf, qw, kw, jnp.asarray(pq, BF16), jnp.asarray(pk, BF16), jnp.asarray(oq), jnp.asarray(ok))


FOX_VROWS = 80


def _fox_attn_kernel(flag_ref, fs_ref, fe_ref, q_ref, aq_ref, k_ref, ak_ref, v_ref, o_ref,
                     v1_ref, qm_ref, acc_ref, m_ref, pa_ref, pb_ref, *, tq, tk, heads):
    bi, pair, qi = pl.program_id(0), pl.program_id(1), pl.program_id(2)
    nk = k_ref.shape[0] // tk
    sub = tq // tk
    pad_rows = lax.broadcasted_iota(jnp.int32, (FOX_VROWS - FOX_HEAD, tk), 0)
    ones_blk = jnp.where(pad_rows == 0, 1.0, 0.0).astype(BF16)
    lane2 = lax.broadcasted_iota(jnp.int32, (1, 2 * LANES), 1)
    bias0 = LANES + 2 * FOX_AUG * pair
    head0 = (lane2 < FOX_HEAD) | ((lane2 >= bias0) & (lane2 < bias0 + FOX_AUG))
    head1 = ((lane2 >= FOX_HEAD) & (lane2 < LANES)) | ((lane2 >= bias0 + FOX_AUG) & (lane2 < bias0 + 2 * FOX_AUG))
    nt = (((1,), (1,)), ((), ()))

    @pl.when(qi == 0)
    def _():
        def build(j, carry):
            cols = pl.ds(pl.multiple_of(j * tk, tk), tk)
            for e in range(2):
                v1_ref[j, e] = jnp.concatenate([v_ref[e * FOX_HEAD:(e + 1) * FOX_HEAD, cols], ones_blk], axis=0)
            return carry

        lax.fori_loop(0, nk, build, 0)

    def probs(j, online, diag=None, dst=None):
        rows = pl.ds(pl.multiple_of(j * tk, tk), tk)
        kf = jnp.concatenate([k_ref[rows, :], ak_ref[rows, :]], axis=1)
        q_lo = 0 if diag is None else diag * tk
        ps, alphas = [], []
        for e in range(2):
            s = lax.dot_general(kf, qm_ref[e, q_lo:tq, :], nt, preferred_element_type=F32)
            if diag is not None:
                kpos = lax.broadcasted_iota(jnp.int32, (tk, tk), 0)
                qpos = lax.broadcasted_iota(jnp.int32, (tk, tk), 1)
                blk = jnp.where(kpos <= qpos, s[:, 0:tk], -jnp.inf)
                s = blk if q_lo + tk == tq else jnp.concatenate([blk, s[:, tk:]], axis=1)
            if online:
                m_prev = m_ref[e, :, q_lo:tq]
                m_new = jnp.maximum(m_prev, jnp.max(s, axis=0, keepdims=True))
                alphas.append(jnp.exp2(m_prev - m_new))
                m_ref[e, :, q_lo:tq] = m_new
                s = s - m_new
            p = jnp.exp2(s).astype(BF16)
            if dst is None:
                ps.append(p)
            else:
                dst[e] = p
        return ps, alphas

    def accumulate(ps, j, alphas=None, diag=None):
        q_lo = 0 if diag is None else diag * tk
        for e in range(2):
            pv = jnp.dot(v1_ref[j, e], ps[e], preferred_element_type=F32)
            if alphas:
                acc_ref[e, :, q_lo:tq] = acc_ref[e, :, q_lo:tq] * alphas[e] + pv
            else:
                acc_ref[e, :, q_lo:tq] += pv

    def start():
        qf = jnp.concatenate([q_ref[...], aq_ref[...]], axis=1)
        zero = jnp.zeros_like(qf)
        qm_ref[0] = jnp.where(head0, qf, zero)
        qm_ref[1] = jnp.where(head1, qf, zero)
        acc_ref[...] = jnp.zeros_like(acc_ref)

    def finish():
        outs = []
        for e in range(2):
            acc = acc_ref[e]
            outs.append(acc[0:FOX_HEAD, :] / acc[FOX_HEAD:FOX_HEAD + 1, :])
        o_ref[...] = jnp.concatenate(outs, axis=0).astype(BF16)

    n_full = qi * sub

    @pl.when(flag_ref[0] == 1)
    def _():
        start()
        held = None
        for a in range(sub - 1, 0, -1):
            nxt = probs(n_full + a, False, diag=a)[0]
            if held is not None:
                accumulate(held[0], n_full + held[1], diag=held[1])
            held = (nxt, a)
        probs(n_full, False, diag=0, dst=pa_ref)
        if held is not None:
            accumulate(held[0], n_full + held[1], diag=held[1])

        h0 = bi * heads + 2 * pair
        fq0 = fs_ref[h0 * nk + n_full]
        fq1 = fs_ref[(h0 + 1) * nk + n_full]

        def dead(j, cnt):
            gone0 = fq0 - fe_ref[h0 * nk + j] < -FOX_SKIP
            gone1 = fq1 - fe_ref[(h0 + 1) * nk + j] < -FOX_SKIP
            return cnt + jnp.logical_and(gone0, gone1).astype(jnp.int32)

        first = lax.fori_loop(0, n_full, dead, 0)
        n_live = n_full - first

        def two_tiles(i, carry):
            j = first + 2 * i
            probs(j, False, dst=pb_ref)
            accumulate(pa_ref, jnp.where(i == 0, n_full, j - 1))
            probs(j + 1, False, dst=pa_ref)
            accumulate(pb_ref, j)
            return carry

        lax.fori_loop(0, n_live // 2, two_tiles, 0)

        @pl.when(n_live % 2 == 1)
        def _():
            ps, _ = probs(n_full - 1, False)
            accumulate(pa_ref, jnp.where(n_live == 1, n_full, n_full - 2))
            accumulate(ps, n_full - 1)

        @pl.when(n_live % 2 == 0)
        def _():
            accumulate(pa_ref, jnp.where(n_live == 0, n_full, n_full - 1))

        finish()

    @pl.when(flag_ref[0] != 1)
    def _():
        start()
        m_ref[...] = jnp.full_like(m_ref, -jnp.inf)

        def body(j, carry):
            ps, alphas = probs(j, True)
            accumulate(ps, j, alphas)
            return carry

        lax.fori_loop(0, n_full, body, 0)
        for a in range(sub):
            ps, alphas = probs(n_full + a, True, diag=a)
            accumulate(ps, n_full + a, alphas, diag=a)
        finish()


def _fox_attn_call(flag, f_first, f_last, q, aq, k, ak, v, *, batch, seq, tq, tk):
    t, d = q.shape
    pairs = d // LANES
    nq = seq // tq
    q_spec = pl.BlockSpec((tq, LANES), lambda b, p, i, *_: (b * nq + i, p))
    kv_spec = pl.BlockSpec((seq, LANES), lambda b, p, i, *_: (b, p))
    grid_spec = pltpu.PrefetchScalarGridSpec(
        num_scalar_prefetch=3,
        grid=(batch, pairs, nq),
        in_specs=[q_spec, pl.BlockSpec((tq, LANES), lambda b, p, i, *_: (b * nq + i, 0)),
                  kv_spec, pl.BlockSpec((seq, LANES), lambda b, p, i, *_: (b, 0)),
                  pl.BlockSpec((LANES, seq), lambda b, p, i, *_: (p, b))],
        out_specs=pl.BlockSpec((LANES, tq), lambda b, p, i, *_: (p, b * nq + i)),
        scratch_shapes=[
            pltpu.VMEM((seq // tk, 2, FOX_VROWS, tk), BF16),
            pltpu.VMEM((2, tq, 2 * LANES), BF16),
            pltpu.VMEM((2, FOX_VROWS, tq), F32),
            pltpu.VMEM((2, 1, tq), F32),
            pltpu.VMEM((2, tk, tq), BF16),
            pltpu.VMEM((2, tk, tq), BF16),
        ],
    )
    return pl.pallas_call(
        functools.partial(_fox_attn_kernel, tq=tq, tk=tk, heads=d // FOX_HEAD),
        grid_spec=grid_spec,
        out_shape=jax.ShapeDtypeStruct((d, t), BF16),
        compiler_params=_cparams(("arbitrary", "arbitrary", "arbitrary")),
        name="fox_attn",
    )(flag, f_first, f_last, q, aq, k, ak, v)


def _post_kernel(x_ref, y_ref, gate_ref, mod_ref, nw_ref, fw_ref, wo32_ref, w1_ref, w2_ref, o_ref, wo_ref, *,
                 d, ff_blk, final, y_features):
    _cast_weight_once(wo32_ref, wo_ref, d)
    mod = mod_ref[0]
    g1 = mod[:, 2 * d:3 * d]
    sh2, sc2, g2 = mod[:, 3 * d:4 * d], mod[:, 4 * d:5 * d], mod[:, 5 * d:6 * d]
    y = y_ref[...].astype(F32)
    if y_features:
        y = y.T
    yg = (y * gate_ref[...].astype(F32)).astype(BF16)
    x1 = x_ref[...] + g1 * jnp.dot(yg, wo_ref[...], preferred_element_type=F32)
    h = _modulate(x1, nw_ref[...], sh2, sc2).astype(BF16)
    acc = jnp.zeros_like(x1)
    for j in range(w1_ref.shape[2] // ff_blk):
        a = jnp.maximum(jnp.dot(h, w1_ref[0, :, j * ff_blk:(j + 1) * ff_blk], preferred_element_type=F32), 0.0)
        acc = acc + jnp.dot((a * a).astype(BF16), w2_ref[0, j * ff_blk:(j + 1) * ff_blk, :],
                            preferred_element_type=F32)
    x2 = x1 + g2 * acc
    if final:
        x2 = x2 * lax.rsqrt(jnp.mean(x2 * x2, axis=-1, keepdims=True) + EPS) * fw_ref[...]
    o_ref[...] = x2


def _post_call(x2, y, gate, mod3, nw, fw, w_out, w1, w2, *, layer, batch, tm, final, y_features):
    t, d = x2.shape
    tiles_per_batch = (t // batch) // tm
    row = pl.BlockSpec((tm, d), lambda i: (i, 0))
    assert y.shape == ((d, t) if y_features else (t, d))
    return pl.pallas_call(
        functools.partial(_post_kernel, d=d, ff_blk=POST_FF_BLK, final=final, y_features=y_features),
        grid=(t // tm,),
        in_specs=[
            row,
            pl.BlockSpec((d, tm), lambda i: (0, i)) if y_features else row,
            row,
            pl.BlockSpec((1, 1, N_MOD * d), lambda i: (layer * batch + i // tiles_per_batch, 0, 0)),
            _resident((1, d)), _resident((1, d)),
            _resident(w_out.shape),
            pl.BlockSpec((1,) + w1.shape[1:], lambda i: (layer, 0, 0), pipeline_mode=pl.Buffered(1)),
            pl.BlockSpec((1,) + w2.shape[1:], lambda i: (layer, 0, 0), pipeline_mode=pl.Buffered(1)),
        ],
        out_specs=row,
        out_shape=jax.ShapeDtypeStruct((t, d), F32),
        scratch_shapes=[pltpu.VMEM(w_out.shape, BF16)],
        compiler_params=_cparams(("arbitrary",)),
        name="post",
    )(x2, y, gate, mod3, nw, fw, w_out, w1, w2)


def kernel(x, c, w_mod, b_mod, norm1_w, norm2_w, hg_w_in, hg_w_out, hg_lb, hg_gn_w, fox_w_in, fox_b_f,
           fox_qn_w, fox_kn_w, fox_w_out, mlp_w1, mlp_w2, final_w):
    batch, seq, d = x.shape
    depth = w_mod.shape[0]
    t = batch * seq
    fox_heads = d // FOX_HEAD
    assert seq % HG_BLOCK == 0 and seq % ATTN_TQ == 0 and HG_BLOCK % PROJ_TM == 0 and PROJ_TM % ROW_SLAB == 0
    assert d % LANES == 0 and fox_heads * FOX_AUG <= LANES and (N_MOD * d) % MOD_TN == 0
    assert mlp_w1.shape[2] % POST_FF_BLK == 0 and x.dtype == F32

    mod3 = _mod_call(c, w_mod, b_mod).reshape(depth * batch, 1, N_MOD * d)
    xs = x.reshape(t, d)
    fw = final_w.reshape(1, d)
    ff = mlp_w1.shape[2]

    for i in range(depth):
        j = i // 2
        n1 = norm1_w[i].reshape(1, d)
        if i % 2 == 0:
            q, k, lf, v, g, wmin = _hg_proj_call(xs, mod3, n1, hg_lb, hg_w_in[j],
                                                 layer=i, batch=batch, tm=PROJ_TM)
            wmin = wmin.reshape(batch, seq // HG_BLOCK, HG_BLOCK // PROJ_TM, d // HG_HEAD, HG_HEAD).min(axis=(2, 4))
            y, w1_all, w2_all = _hg_rec_call(wmin.transpose(0, 2, 1).reshape(-1), q, k, lf, v,
                                             hg_gn_w[j].reshape(1, HG_HEAD),
                                             mlp_w1.reshape(depth * d, ff), mlp_w2.reshape(depth * ff, d),
                                             batch=batch, tc=HG_BLOCK,
                                             unroll_bounded=HG_GROUP, unroll_exact=HG_GROUP_EXACT)
            w1_all, w2_all = w1_all.reshape(depth, d, ff), w2_all.reshape(depth, ff, d)
            w_out = hg_w_out[j]
        else:
            pad = ((0, 0), (0, LANES - fox_heads))
            q, k, v, g, lf = _fox_proj_call(
                xs, mod3, n1,
                jnp.tile(fox_qn_w[j], fox_heads).reshape(1, d),
                jnp.tile(fox_kn_w[j], fox_heads).reshape(1, d),
                jnp.pad(fox_b_f[j].reshape(1, fox_heads), pad),
                jnp.swapaxes(fox_w_in, 1, 2), jnp.pad(fox_w_in[j, :, 4 * d:], pad).astype(BF16),
                layer=i, w_index=j, batch=batch, tm=PROJ_TM)
            aq, ak, edge, flag = _fox_bias_call(lf, fox_qn_w[j].reshape(1, FOX_HEAD),
                                                fox_kn_w[j].reshape(1, FOX_HEAD), batch=batch, d=d, tp=ATTN_TK)
            edge = edge.reshape(batch, seq // ATTN_TK, 2, LANES)[..., :fox_heads].transpose(2, 0, 3, 1)
            y = _fox_attn_call(flag[0, :1], edge[0].reshape(-1), edge[1].reshape(-1), q, aq, k, ak, v,
                               batch=batch, seq=seq, tq=ATTN_TQ, tk=ATTN_TK)
            w_out = fox_w_out[j]
        xs = _post_call(xs, y, g, mod3, norm2_w[i].reshape(1, d), fw, w_out, w1_all, w2_all,
                        layer=i, batch=batch, tm=PROJ_TM, final=(i == depth - 1), y_features=(i % 2 == 1))
    return xs.reshape(batch, seq, d)
```

```python
import functools

import numpy as np
import jax
import jax.numpy as jnp
from jax import lax
from jax.experimental import pallas as pl
from jax.experimental.pallas import tpu as pltpu

F32 = jnp.float32
BF16 = jnp.bfloat16
EPS = 1e-6
N_MOD = 6
HG_HEAD = 128
HG_CHUNK = 64
HG_SUB = 16
HG_SAFE_DECAY = 56.0
FOX_HEAD = 64
LANES = 128
SUBLANES = 8
LOG2E = float(np.log2(np.e))
FOX_STAB_MAX = 30.0
FOX_BOUND_SLACK = 1.01
FOX_SKIP = 160.0
ROW_SLAB = 256
VMEM_LIMIT = 56 * 1024 * 1024

PROJ_TM = 512
POST_FF_BLK = 1024
MOD_TN = 1024
HG_BLOCK = 2048
HG_GROUP = 32
HG_GROUP_EXACT = 4
ATTN_TK = 512
ATTN_TQ = 2 * ATTN_TK


def _cparams(sem):
    return pltpu.CompilerParams(dimension_semantics=sem, vmem_limit_bytes=VMEM_LIMIT)


def _resident(shape):
    nd = len(shape)
    return pl.BlockSpec(shape, lambda *_: (0,) * nd, pipeline_mode=pl.Buffered(1))


def _sigmoid(x):
    return 0.5 * jnp.tanh(0.5 * x) + 0.5


def _cast_weight_once(w_ref, wb_ref, cols, transposed=False):
    @pl.when(pl.program_id(0) == 0)
    def _():
        for c0 in range(0, wb_ref.shape[1], cols):
            src = w_ref[c0:c0 + cols, :].T if transposed else w_ref[:, c0:c0 + cols]
            wb_ref[:, c0:c0 + cols] = src.astype(BF16)


def _modulate(x, nw, shift, scale):
    ms = jnp.mean(x * x, axis=-1, keepdims=True)
    y = x * lax.rsqrt(ms + EPS)
    return (y * nw) * (1.0 + scale) + shift


def _mod_kernel(ct_ref, w_ref, b_ref, o_ref):
    ct = ct_ref[...]
    cat = ct * _sigmoid(ct)
    w = w_ref[0]
    rows = []
    for b in range(ct.shape[1]):
        col = jnp.broadcast_to(cat[:, b:b + 1], (ct.shape[0], LANES))
        rows.append(jnp.concatenate(
            [jnp.sum(w[:, j * LANES:(j + 1) * LANES] * col, axis=0, keepdims=True)
             for j in range(w.shape[1] // LANES)], axis=1))
    o_ref[0] = jnp.concatenate(rows, axis=0) + b_ref[0]


def _mod_call(c, w_mod, b_mod):
    depth, d, n = w_mod.shape
    b = c.shape[0]
    tn = MOD_TN
    return pl.pallas_call(
        _mod_kernel,
        grid=(depth, n // tn),
        in_specs=[
            pl.BlockSpec((d, b), lambda l, j: (0, 0)),
            pl.BlockSpec((1, d, tn), lambda l, j: (l, 0, j)),
            pl.BlockSpec((1, 1, tn), lambda l, j: (l, 0, j)),
        ],
        out_specs=pl.BlockSpec((1, b, tn), lambda l, j: (l, 0, j)),
        out_shape=jax.ShapeDtypeStruct((depth, b, n), F32),
        compiler_params=_cparams(("arbitrary", "arbitrary")),
        name="mod",
    )(c.T, w_mod, b_mod.reshape(depth, 1, n))


def _hg_proj_kernel(x_ref, mod_ref, nw_ref, lb_ref, wf32_ref, q_ref, k_ref, lf_ref, v_ref, g_ref, wmin_ref,
                    w_ref, *, layer, d):
    _cast_weight_once(wf32_ref, w_ref, d)
    mod = mod_ref[0]
    lbp = lb_ref[...]
    e = jnp.exp(lbp - jnp.max(lbp, axis=0, keepdims=True))
    lb = jnp.sum(e[0:layer + 1], axis=0, keepdims=True) / jnp.sum(e, axis=0, keepdims=True)

    w_r = lax.broadcasted_iota(jnp.int32, (ROW_SLAB // HG_SUB, ROW_SLAB), 0)
    w_c = lax.broadcasted_iota(jnp.int32, (ROW_SLAB // HG_SUB, ROW_SLAB), 1)
    win = (lax.shift_right_logical(w_c, HG_SUB.bit_length() - 1) == w_r).astype(BF16)
    lfs = []

    for r0 in range(0, x_ref.shape[0], ROW_SLAB):
        rows = slice(r0, r0 + ROW_SLAB)
        h = _modulate(x_ref[rows, :], nw_ref[...], mod[:, 0:d], mod[:, d:2 * d]).astype(BF16)

        pq = jnp.dot(h, w_ref[:, 0:d], preferred_element_type=F32)
        q_ref[rows, :] = (pq * _sigmoid(pq)).astype(BF16)

        z = jnp.dot(h, w_ref[:, d:2 * d], preferred_element_type=F32)
        th = 0.5 * jnp.tanh(0.5 * z)
        kk = (1.0 - lb) * (0.5 - th)
        f = lb + (1.0 - lb) * (0.5 + th)
        lf = jnp.log(jnp.where(kk < 0.5, 1.0 - kk, f))
        lf_ref[rows, :] = lf
        k_ref[rows, :] = kk.astype(BF16)
        lfs.append(lf.astype(BF16))

        v_ref[rows, :] = jnp.dot(h, w_ref[:, 2 * d:3 * d], preferred_element_type=F32).astype(BF16)
        pg = jnp.dot(h, w_ref[:, 3 * d:4 * d], preferred_element_type=F32)
        g_ref[rows, :] = (pg * _sigmoid(pg)).astype(BF16)

    wmin = None
    for lf16 in lfs:
        wsum = jnp.min(jnp.dot(win, lf16, preferred_element_type=F32), axis=0, keepdims=True)
        wmin = wsum if wmin is None else jnp.minimum(wmin, wsum)
    wmin_ref[0] = wmin


def _hg_proj_call(x2, mod3, nw, hg_lb, w_in, *, layer, batch, tm):
    t, d = x2.shape
    tiles_per_batch = (t // batch) // tm
    row = pl.BlockSpec((tm, d), lambda i: (i, 0))
    out_bf = jax.ShapeDtypeStruct((t, d), BF16)
    return pl.pallas_call(
        functools.partial(_hg_proj_kernel, layer=layer, d=d),
        grid=(t // tm,),
        in_specs=[
            row,
            pl.BlockSpec((1, 1, N_MOD * d), lambda i: (layer * batch + i // tiles_per_batch, 0, 0)),
            _resident((1, d)),
            _resident(hg_lb.shape),
            _resident(w_in.shape),
        ],
        out_specs=[row, row, row, row, row, pl.BlockSpec((1, 1, d), lambda i: (i, 0, 0))],
        out_shape=[out_bf, out_bf, jax.ShapeDtypeStruct((t, d), F32), out_bf, out_bf,
                   jax.ShapeDtypeStruct((t // tm, 1, d), F32)],
        scratch_shapes=[pltpu.VMEM(w_in.shape, BF16)],
        compiler_params=_cparams(("arbitrary",)),
        name="hg_proj",
    )(x2, mod3, nw, hg_lb, w_in)


def _hg_rec_kernel(wmin_ref, q_ref, k_ref, lf_ref, v_ref, gnw_ref, ca_ref, cb_ref, o_ref, cao_ref, cbo_ref,
                   st_ref, gs_ref, ks_ref, *, n_chunks, unroll_bounded, unroll_exact):
    cao_ref[...] = ca_ref[...].astype(BF16)
    cbo_ref[...] = cb_ref[...].astype(BF16)
    c, sub = HG_CHUNK, HG_SUB
    n_sub = c // sub
    half = sub // 2

    @pl.when(pl.program_id(2) == 0)
    def _():
        st_ref[...] = jnp.zeros_like(st_ref)

    r_i = lax.broadcasted_iota(jnp.int32, (c, c), 0)
    c_i = lax.broadcasted_iota(jnp.int32, (c, c), 1)
    tri = (c_i <= r_i).astype(BF16)
    ones_w = jnp.ones((HG_HEAD, c), BF16)
    half_row = lax.broadcasted_iota(jnp.int32, (half, HG_HEAD), 0)
    a_lane = lax.broadcasted_iota(jnp.int32, (half, c), 1)
    gnw = gnw_ref[...]
    nt = (((1,), (1,)), ((), ()))
    tn = (((0,), (0,)), ((), ()))

    def offdiag_operands(q, k, gcum, j, bounded):
        lo = j * sub
        hi = lo + sub if bounded else lo
        g_b = gcum[lo - 1:lo, :] if j > 0 else jnp.zeros((1, HG_HEAD), F32)
        qh = (q[lo:lo + sub, :] * jnp.exp2(gcum[lo:lo + sub, :] - g_b)).astype(BF16)
        kh = (k[0:hi, :] * jnp.exp2(g_b - gcum[0:hi, :])).astype(BF16)
        if hi < c:
            kh = jnp.concatenate([kh, jnp.zeros((c - hi, HG_HEAD), BF16)], axis=0)
        return qh, kh

    def diag_products(q, k, gcum, slot, j):
        lo = j * sub
        gt = (gcum[lo:lo + half, :], gcum[lo + half:lo + sub, :])
        qt = (q[lo:lo + half, :], q[lo + half:lo + sub, :])
        ps = []
        for s in range(sub):
            g_s = gs_ref[slot, pl.ds(lo + s, 1), :]
            k_s = ks_ref[slot, pl.ds(lo + s, 1), :]
            for hf in range(2):
                if s >= half and hf == 0:
                    continue
                dlt = gt[hf] - g_s
                if (s >= half) == (hf == 1):
                    dlt = jnp.where(half_row >= s % half, dlt, -jnp.inf)
                ps.append((qt[hf] * k_s) * jnp.exp2(dlt))
        return jnp.concatenate(ps, axis=0).astype(BF16)

    def diag_scatter(r, j):
        lo = j * sub
        a_top = jnp.zeros((half, c), F32)
        a_bot = jnp.zeros((half, c), F32)
        for s in range(sub):
            if s < half:
                a_top = jnp.where(a_lane == lo + s, r[2 * s * half:(2 * s + 1) * half, :], a_top)
                a_bot = jnp.where(a_lane == lo + s, r[(2 * s + 1) * half:(2 * s + 2) * half, :], a_bot)
            else:
                a_bot = jnp.where(a_lane == lo + s, r[(half + s) * half:(half + s + 1) * half, :], a_bot)
        return jnp.concatenate([a_top, a_bot], axis=0)

    def run(bounded, unroll):
        def group(i, st):
            us = range(unroll)
            rows = [pl.ds(pl.multiple_of((i * unroll + u) * c, c), c) for u in us]
            g3 = [jnp.dot(tri, _split3(lf_ref[r, :]), preferred_element_type=F32) for r in rows]
            q = [q_ref[r, :].astype(F32) for r in rows]
            k = [k_ref[r, :].astype(F32) for r in rows]
            gcum = [(g[:, 0:HG_HEAD] + g[:, HG_HEAD:2 * HG_HEAD] + g[:, 2 * HG_HEAD:]) * LOG2E for g in g3]
            g_last = [g[c - 1:c, :] for g in gcum]
            qg = [(q[u] * jnp.exp2(gcum[u])).astype(BF16) for u in us]
            kd = [(k[u] * jnp.exp2(g_last[u] - gcum[u])).astype(BF16) for u in us]
            first = 0 if bounded else 1
            off = [[offdiag_operands(q[u], k[u], gcum[u], j, bounded) for j in range(first, n_sub)] for u in us]
            if not bounded:
                for u in us:
                    gs_ref[u] = gcum[u]
                    ks_ref[u] = k[u]
                pst = [[diag_products(q[u], k[u], gcum[u], u, j) for j in range(n_sub)] for u in us]
            kv = [lax.dot_general(v_ref[rows[u], :], kd[u], tn, preferred_element_type=F32) for u in us]
            blocks = [[lax.dot_general(qh, kh, nt, preferred_element_type=F32) for qh, kh in off[u]] for u in us]
            if bounded:
                a = [jnp.where(c_i <= r_i, jnp.concatenate(blocks[u], axis=0), 0.0) for u in us]
            else:
                sums = [[jnp.dot(p, ones_w, preferred_element_type=F32) for p in pst[u]] for u in us]
                a = []
                for u in us:
                    a_rows = [diag_scatter(sums[u][j], j) for j in range(n_sub)]
                    for j in range(1, n_sub):
                        a_rows[j] = a_rows[j] + blocks[u][j - 1]
                    a.append(jnp.concatenate(a_rows, axis=0))
            o_intra = [jnp.dot(a[u].astype(BF16), v_ref[rows[u], :], preferred_element_type=F32) for u in us]
            for u in us:
                o = o_intra[u] + lax.dot_general(qg[u], st.astype(BF16), nt, preferred_element_type=F32)
                st = st * jnp.exp2(g_last[u]) + kv[u]
                on = o * lax.rsqrt(jnp.mean(o * o, axis=-1, keepdims=True) + EPS) * gnw
                o_ref[rows[u], :] = on.astype(BF16)
            return st

        st_ref[...] = lax.fori_loop(0, n_chunks // unroll, group, st_ref[...])

    blk_id = (pl.program_id(0) * pl.num_programs(1) + pl.program_id(1)) * pl.num_programs(2) + pl.program_id(2)
    bounded = wmin_ref[blk_id] * LOG2E >= -HG_SAFE_DECAY

    @pl.when(bounded)
    def _():
        run(True, unroll_bounded)

    @pl.when(jnp.logical_not(bounded))
    def _():
        run(False, unroll_exact)


def _hg_rec_call(wmin, q, k, lf, v, gnw, cast_a, cast_b, *, batch, tc, unroll_bounded, unroll_exact):
    t, d = q.shape
    heads = d // HG_HEAD
    steps = (t // batch) // tc
    n_steps = batch * heads * steps
    unroll = unroll_exact
    blk = pl.BlockSpec((tc, HG_HEAD), lambda b, h, s, *_: (b * steps + s, h))

    def cast_spec(a):
        assert a.shape[0] % (n_steps * 2 * SUBLANES) == 0
        return pl.BlockSpec((a.shape[0] // n_steps, a.shape[1]), lambda b, h, s, *_: ((b * heads + h) * steps + s, 0))

    grid_spec = pltpu.PrefetchScalarGridSpec(
        num_scalar_prefetch=1,
        grid=(batch, heads, steps),
        in_specs=[blk, blk, blk, blk, pl.BlockSpec((1, HG_HEAD), lambda b, h, s, *_: (0, 0)),
                  cast_spec(cast_a), cast_spec(cast_b)],
        out_specs=[blk, cast_spec(cast_a), cast_spec(cast_b)],
        scratch_shapes=[
            pltpu.VMEM((HG_HEAD, HG_HEAD), F32),
            pltpu.VMEM((unroll, HG_CHUNK, HG_HEAD), F32),
            pltpu.VMEM((unroll, HG_CHUNK, HG_HEAD), F32),
        ],
    )
    return pl.pallas_call(
        functools.partial(_hg_rec_kernel, n_chunks=tc // HG_CHUNK, unroll_bounded=unroll_bounded,
                          unroll_exact=unroll_exact),
        grid_spec=grid_spec,
        out_shape=[jax.ShapeDtypeStruct((t, d), BF16), jax.ShapeDtypeStruct(cast_a.shape, BF16),
                   jax.ShapeDtypeStruct(cast_b.shape, BF16)],
        compiler_params=_cparams(("arbitrary", "arbitrary", "arbitrary")),
        name="hg_rec",
    )(wmin, q, k, lf, v, gnw, cast_a, cast_b)


def _head_rmsnorm(p, w_row):
    low = lax.broadcasted_iota(jnp.int32, (1, LANES), 1) < FOX_HEAD
    outs = []
    for j in range(p.shape[1] // LANES):
        pj = p[:, j * LANES:(j + 1) * LANES]
        ss = pj * pj
        s_lo = jnp.sum(jnp.where(low, ss, 0.0), axis=-1, keepdims=True)
        s_hi = jnp.sum(jnp.where(low, 0.0, ss), axis=-1, keepdims=True)
        inv = lax.rsqrt(jnp.where(low, s_lo, s_hi) * (1.0 / FOX_HEAD) + EPS)
        outs.append(pj * inv)
    return jnp.concatenate(outs, axis=1) * w_row


def _fox_proj_kernel(x_ref, mod_ref, nw_ref, qw_ref, kw_ref, bf_ref, wf32_ref, wf_ref,
                     q_ref, k_ref, v_ref, g_ref, lf_ref, vs_ref, w_ref, *, d):
    _cast_weight_once(wf32_ref.at[0], w_ref, d, transposed=True)
    mod = mod_ref[0]
    scale = LOG2E / np.sqrt(FOX_HEAD)
    for r0 in range(0, x_ref.shape[0], ROW_SLAB):
        rows = slice(r0, r0 + ROW_SLAB)
        h = _modulate(x_ref[rows, :], nw_ref[...], mod[:, 0:d], mod[:, d:2 * d]).astype(BF16)
        pq = jnp.dot(h, w_ref[:, 0:d], preferred_element_type=F32)
        q_ref[rows, :] = (_head_rmsnorm(pq, qw_ref[...]) * scale).astype(BF16)
        pk = jnp.dot(h, w_ref[:, d:2 * d], preferred_element_type=F32)
        k_ref[rows, :] = _head_rmsnorm(pk, kw_ref[...]).astype(BF16)
        vs_ref[...] = jnp.dot(h, w_ref[:, 2 * d:3 * d], preferred_element_type=F32)
        v_ref[:, rows] = vs_ref[...].T.astype(BF16)
        pg = jnp.dot(h, w_ref[:, 3 * d:4 * d], preferred_element_type=F32)
        g_ref[rows, :] = _sigmoid(pg).astype(BF16)
        u = jnp.dot(h, wf_ref[...], preferred_element_type=F32) + bf_ref[...]
        lf_ref[rows, :] = jnp.minimum(u, 0.0) - jnp.log1p(jnp.exp(-jnp.abs(u)))


def _fox_proj_call(x2, mod3, nw, qw, kw, bf, w_main, w_f, *, layer, w_index, batch, tm):
    t, d = x2.shape
    nh = w_f.shape[1]
    tiles_per_batch = (t // batch) // tm
    row = pl.BlockSpec((tm, d), lambda i: (i, 0))
    out_bf = jax.ShapeDtypeStruct((t, d), BF16)
    return pl.pallas_call(
        functools.partial(_fox_proj_kernel, d=d),
        grid=(t // tm,),
        in_specs=[
            row,
            pl.BlockSpec((1, 1, N_MOD * d), lambda i: (layer * batch + i // tiles_per_batch, 0, 0)),
            _resident((1, d)), _resident((1, d)), _resident((1, d)), _resident((1, nh)),
            pl.BlockSpec((1, 4 * d, d), lambda i: (w_index, 0, 0), pipeline_mode=pl.Buffered(1)),
            _resident(w_f.shape),
        ],
        out_specs=[row, row, pl.BlockSpec((d, tm), lambda i: (0, i)), row, pl.BlockSpec((tm, nh), lambda i: (i, 0))],
        out_shape=[out_bf, out_bf, jax.ShapeDtypeStruct((d, t), BF16), out_bf, jax.ShapeDtypeStruct((t, nh), F32)],
        scratch_shapes=[pltpu.VMEM((ROW_SLAB, d), F32),
                        pltpu.VMEM((d, 4 * d), BF16)],
        compiler_params=_cparams(("arbitrary",)),
        name="fox_proj",
    )(x2, mod3, nw, qw, kw, bf, w_main, w_f)


FOX_AUG = 6


def _split3(x):
    hi = x.astype(BF16)
    r1 = x - hi.astype(F32)
    mid = r1.astype(BF16)
    lo = (r1 - mid.astype(F32)).astype(BF16)
    return jnp.concatenate([hi, mid, lo], axis=1)


def _fox_bias_kernel(lf_ref, qw_ref, kw_ref, pq_ref, pk_ref, oq_ref, ok_ref,
                     aq_ref, ak_ref, edge_ref, flag_ref, *, tp):
    bound = FOX_BOUND_SLACK * LOG2E * np.sqrt(FOX_HEAD) * jnp.max(jnp.abs(qw_ref[...] * kw_ref[...]), axis=-1, keepdims=True)
    fast = bound <= FOX_STAB_MAX
    flag_ref[...] = jnp.broadcast_to(fast.astype(jnp.int32), flag_ref.shape)
    stab = jnp.where(fast, bound, 0.0)

    r_i = lax.broadcasted_iota(jnp.int32, (tp, tp), 0)
    c_i = lax.broadcasted_iota(jnp.int32, (tp, tp), 1)
    tri = (c_i <= r_i).astype(BF16)

    def tile(j, carry):
        rows = pl.ds(pl.multiple_of(j * tp, tp), tp)
        f3 = jnp.dot(tri, _split3(lf_ref[rows, :]), preferred_element_type=F32)
        f = f3[:, 0:LANES] + f3[:, LANES:2 * LANES] + f3[:, 2 * LANES:] + carry
        f2 = f * LOG2E
        edge_ref[j] = jnp.concatenate([f2[0:1, :], f2[tp - 1:tp, :]], axis=0)
        aq_ref[rows, :] = (jnp.dot(_split3(f2 - stab), pq_ref[...], preferred_element_type=F32)
                           + oq_ref[...]).astype(BF16)
        ak_ref[rows, :] = (jnp.dot(_split3(f2), pk_ref[...], preferred_element_type=F32)
                           + ok_ref[...]).astype(BF16)
        return f[tp - 1:tp, :]

    lax.fori_loop(0, lf_ref.shape[0] // tp, tile, jnp.zeros((1, LANES), F32))


def _fox_bias_call(lf, qw, kw, *, batch, d, tp):
    t = lf.shape[0]
    heads = d // FOX_HEAD
    steps = (t // batch) // tp
    pq = np.zeros((3 * LANES, LANES), np.float32)
    pk = np.zeros((3 * LANES, LANES), np.float32)
    oq = np.zeros((1, LANES), np.float32)
    ok = np.zeros((1, LANES), np.float32)
    for h in range(heads):
        base = FOX_AUG * h
        for i in range(3):
            pq[i * LANES + h, base + i] = 1.0
            ok[0, base + i] = 1.0
            oq[0, base + 3 + i] = 1.0
            pk[i * LANES + h, base + 3 + i] = -1.0
    whole = pl.BlockSpec((t // batch, LANES), lambda b: (b, 0))
    return pl.pallas_call(
        functools.partial(_fox_bias_kernel, tp=tp),
        grid=(batch,),
        in_specs=[
            whole,
            _resident((1, FOX_HEAD)), _resident((1, FOX_HEAD)),
            _resident(pq.shape), _resident(pk.shape), _resident(oq.shape), _resident(ok.shape),
        ],
        out_specs=[whole, whole,
                   pl.BlockSpec((steps, 2, LANES), lambda b: (b, 0, 0)),
                   pl.BlockSpec((SUBLANES, LANES), lambda b: (0, 0))],
        out_shape=[jax.ShapeDtypeStruct((t, LANES), BF16), jax.ShapeDtypeStruct((t, LANES), BF16),
                   jax.ShapeDtypeStruct((batch * steps, 2, LANES), F32),
                   jax.ShapeDtypeStruct((SUBLANES, LANES), jnp.int32)],
        compiler_params=_cparams(("arbitrary",)),
        name="fox_bias",
    )(lf, qw, kw, jnp.asarray(pq, BF16), jnp.asarray(pk, BF16), jnp.asarray(oq), jnp.asarray(ok))


FOX_VROWS = 80


def _fox_attn_kernel(flag_ref, fs_ref, fe_ref, q_ref, aq_ref, k_ref, ak_ref, v_ref, o_ref,
                     v1_ref, qm_ref, acc_ref, m_ref, pa_ref, pb_ref, *, tq, tk, heads):
    bi, pair, qi = pl.program_id(0), pl.program_id(1), pl.program_id(2)
    nk = k_ref.shape[0] // tk
    sub = tq // tk
    pad_rows = lax.broadcasted_iota(jnp.int32, (FOX_VROWS - FOX_HEAD, tk), 0)
    ones_blk = jnp.where(pad_rows == 0, 1.0, 0.0).astype(BF16)
    lane2 = lax.broadcasted_iota(jnp.int32, (1, 2 * LANES), 1)
    bias0 = LANES + 2 * FOX_AUG * pair
    head0 = (lane2 < FOX_HEAD) | ((lane2 >= bias0) & (lane2 < bias0 + FOX_AUG))
    head1 = ((lane2 >= FOX_HEAD) & (lane2 < LANES)) | ((lane2 >= bias0 + FOX_AUG) & (lane2 < bias0 + 2 * FOX_AUG))
    nt = (((1,), (1,)), ((), ()))

    @pl.when(qi == 0)
    def _():
        def build(j, carry):
            cols = pl.ds(pl.multiple_of(j * tk, tk), tk)
            for e in range(2):
                v1_ref[j, e] = jnp.concatenate([v_ref[e * FOX_HEAD:(e + 1) * FOX_HEAD, cols], ones_blk], axis=0)
            return carry

        lax.fori_loop(0, nk, build, 0)

    def probs(j, online, diag=None, dst=None):
        rows = pl.ds(pl.multiple_of(j * tk, tk), tk)
        kf = jnp.concatenate([k_ref[rows, :], ak_ref[rows, :]], axis=1)
        q_lo = 0 if diag is None else diag * tk
        ps, alphas = [], []
        for e in range(2):
            s = lax.dot_general(kf, qm_ref[e, q_lo:tq, :], nt, preferred_element_type=F32)
            if diag is not None:
                kpos = lax.broadcasted_iota(jnp.int32, (tk, tk), 0)
                qpos = lax.broadcasted_iota(jnp.int32, (tk, tk), 1)
                blk = jnp.where(kpos <= qpos, s[:, 0:tk], -jnp.inf)
                s = blk if q_lo + tk == tq else jnp.concatenate([blk, s[:, tk:]], axis=1)
            if online:
                m_prev = m_ref[e, :, q_lo:tq]
                m_new = jnp.maximum(m_prev, jnp.max(s, axis=0, keepdims=True))
                alphas.append(jnp.exp2(m_prev - m_new))
                m_ref[e, :, q_lo:tq] = m_new
                s = s - m_new
            p = jnp.exp2(s).astype(BF16)
            if dst is None:
                ps.append(p)
            else:
                dst[e] = p
        return ps, alphas

    def accumulate(ps, j, alphas=None, diag=None):
        q_lo = 0 if diag is None else diag * tk
        for e in range(2):
            pv = jnp.dot(v1_ref[j, e], ps[e], preferred_element_type=F32)
            if alphas:
                acc_ref[e, :, q_lo:tq] = acc_ref[e, :, q_lo:tq] * alphas[e] + pv
            else:
                acc_ref[e, :, q_lo:tq] += pv

    def start():
        qf = jnp.concatenate([q_ref[...], aq_ref[...]], axis=1)
        zero = jnp.zeros_like(qf)
        qm_ref[0] = jnp.where(head0, qf, zero)
        qm_ref[1] = jnp.where(head1, qf, zero)
        acc_ref[...] = jnp.zeros_like(acc_ref)

    def finish():
        outs = []
        for e in range(2):
            acc = acc_ref[e]
            outs.append(acc[0:FOX_HEAD, :] / acc[FOX_HEAD:FOX_HEAD + 1, :])
        o_ref[...] = jnp.concatenate(outs, axis=0).astype(BF16)

    n_full = qi * sub

    @pl.when(flag_ref[0] == 1)
    def _():
        start()
        held = None
        for a in range(sub - 1, 0, -1):
            nxt = probs(n_full + a, False, diag=a)[0]
            if held is not None:
                accumulate(held[0], n_full + held[1], diag=held[1])
            held = (nxt, a)
        probs(n_full, False, diag=0, dst=pa_ref)
        if held is not None:
            accumulate(held[0], n_full + held[1], diag=held[1])

        h0 = bi * heads + 2 * pair
        fq0 = fs_ref[h0 * nk + n_full]
        fq1 = fs_ref[(h0 + 1) * nk + n_full]

        def dead(j, cnt):
            gone0 = fq0 - fe_ref[h0 * nk + j] < -FOX_SKIP
            gone1 = fq1 - fe_ref[(h0 + 1) * nk + j] < -FOX_SKIP
            return cnt + jnp.logical_and(gone0, gone1).astype(jnp.int32)

        first = lax.fori_loop(0, n_full, dead, 0)
        n_live = n_full - first

        def two_tiles(i, carry):
            j = first + 2 * i
            probs(j, False, dst=pb_ref)
            accumulate(pa_ref, jnp.where(i == 0, n_full, j - 1))
            probs(j + 1, False, dst=pa_ref)
            accumulate(pb_ref, j)
            return carry

        lax.fori_loop(0, n_live // 2, two_tiles, 0)

        @pl.when(n_live % 2 == 1)
        def _():
            ps, _ = probs(n_full - 1, False)
            accumulate(pa_ref, jnp.where(n_live == 1, n_full, n_full - 2))
            accumulate(ps, n_full - 1)

        @pl.when(n_live % 2 == 0)
        def _():
            accumulate(pa_ref, jnp.where(n_live == 0, n_full, n_full - 1))

        finish()

    @pl.when(flag_ref[0] != 1)
    def _():
        start()
        m_ref[...] = jnp.full_like(m_ref, -jnp.inf)

        def body(j, carry):
            ps, alphas = probs(j, True)
            accumulate(ps, j, alphas)
            return carry

        lax.fori_loop(0, n_full, body, 0)
        for a in range(sub):
            ps, alphas = probs(n_full + a, True, diag=a)
            accumulate(ps, n_full + a, alphas, diag=a)
        finish()


def _fox_attn_call(flag, f_first, f_last, q, aq, k, ak, v, *, batch, seq, tq, tk):
    t, d = q.shape
    pairs = d // LANES
    nq = seq // tq
    q_spec = pl.BlockSpec((tq, LANES), lambda b, p, i, *_: (b * nq + i, p))
    kv_spec = pl.BlockSpec((seq, LANES), lambda b, p, i, *_: (b, p))
    grid_spec = pltpu.PrefetchScalarGridSpec(
        num_scalar_prefetch=3,
        grid=(batch, pairs, nq),
        in_specs=[q_spec, pl.BlockSpec((tq, LANES), lambda b, p, i, *_: (b * nq + i, 0)),
                  kv_spec, pl.BlockSpec((seq, LANES), lambda b, p, i, *_: (b, 0)),
                  pl.BlockSpec((LANES, seq), lambda b, p, i, *_: (p, b))],
        out_specs=pl.BlockSpec((LANES, tq), lambda b, p, i, *_: (p, b * nq + i)),
        scratch_shapes=[
            pltpu.VMEM((seq // tk, 2, FOX_VROWS, tk), BF16),
            pltpu.VMEM((2, tq, 2 * LANES), BF16),
            pltpu.VMEM((2, FOX_VROWS, tq), F32),
            pltpu.VMEM((2, 1, tq), F32),
            pltpu.VMEM((2, tk, tq), BF16),
            pltpu.VMEM((2, tk, tq), BF16),
        ],
    )
    return pl.pallas_call(
        functools.partial(_fox_attn_kernel, tq=tq, tk=tk, heads=d // FOX_HEAD),
        grid_spec=grid_spec,
        out_shape=jax.ShapeDtypeStruct((d, t), BF16),
        compiler_params=_cparams(("arbitrary", "arbitrary", "arbitrary")),
        name="fox_attn",
    )(flag, f_first, f_last, q, aq, k, ak, v)


def _post_kernel(x_ref, y_ref, gate_ref, mod_ref, nw_ref, fw_ref, wo_ref, w1_ref, w2_ref, o_ref, *,
                 d, ff_blk, final, y_features):
    mod = mod_ref[0]
    g1 = mod[:, 2 * d:3 * d]
    sh2, sc2, g2 = mod[:, 3 * d:4 * d], mod[:, 4 * d:5 * d], mod[:, 5 * d:6 * d]
    y = y_ref[...].astype(F32)
    if y_features:
        y = y.T
    yg = (y * gate_ref[...].astype(F32)).astype(BF16)
    x1 = x_ref[...] + g1 * jnp.dot(yg, wo_ref[...], preferred_element_type=F32)
    h = _modulate(x1, nw_ref[...], sh2, sc2).astype(BF16)
    acc = jnp.zeros_like(x1)
    for j in range(w1_ref.shape[2] // ff_blk):
        a = jnp.maximum(jnp.dot(h, w1_ref[0, :, j * ff_blk:(j + 1) * ff_blk], preferred_element_type=F32), 0.0)
        acc = acc + jnp.dot((a * a).astype(BF16), w2_ref[0, j * ff_blk:(j + 1) * ff_blk, :],
                            preferred_element_type=F32)
    x2 = x1 + g2 * acc
    if final:
        x2 = x2 * lax.rsqrt(jnp.mean(x2 * x2, axis=-1, keepdims=True) + EPS) * fw_ref[...]
    o_ref[...] = x2


def _post_call(x2, y, gate, mod3, nw, fw, w_out, w1, w2, *, layer, batch, tm, final, y_features):
    t, d = x2.shape
    tiles_per_batch = (t // batch) // tm
    row = pl.BlockSpec((tm, d), lambda i: (i, 0))
    assert y.shape == ((d, t) if y_features else (t, d))
    return pl.pallas_call(
        functools.partial(_post_kernel, d=d, ff_blk=POST_FF_BLK, final=final, y_features=y_features),
        grid=(t // tm,),
        in_specs=[
            row,
            pl.BlockSpec((d, tm), lambda i: (0, i)) if y_features else row,
            row,
            pl.BlockSpec((1, 1, N_MOD * d), lambda i: (layer * batch + i // tiles_per_batch, 0, 0)),
            _resident((1, d)), _resident((1, d)),
            _resident(w_out.shape),
            pl.BlockSpec((1,) + w1.shape[1:], lambda i: (layer, 0, 0), pipeline_mode=pl.Buffered(1)),
            pl.BlockSpec((1,) + w2.shape[1:], lambda i: (layer, 0, 0), pipeline_mode=pl.Buffered(1)),
        ],
        out_specs=row,
        out_shape=jax.ShapeDtypeStruct((t, d), F32),
        compiler_params=_cparams(("arbitrary",)),
        name="post",
    )(x2, y, gate, mod3, nw, fw, w_out, w1, w2)


def kernel(x, c, w_mod, b_mod, norm1_w, norm2_w, hg_w_in, hg_w_out, hg_lb, hg_gn_w, fox_w_in, fox_b_f,
           fox_qn_w, fox_kn_w, fox_w_out, mlp_w1, mlp_w2, final_w):
    batch, seq, d = x.shape
    depth = w_mod.shape[0]
    t = batch * seq
    fox_heads = d // FOX_HEAD
    assert seq % HG_BLOCK == 0 and seq % ATTN_TQ == 0 and HG_BLOCK % PROJ_TM == 0 and PROJ_TM % ROW_SLAB == 0
    assert d % LANES == 0 and fox_heads * FOX_AUG <= LANES and (N_MOD * d) % MOD_TN == 0
    assert mlp_w1.shape[2] % POST_FF_BLK == 0 and x.dtype == F32

    mod3 = _mod_call(c, w_mod, b_mod).reshape(depth * batch, 1, N_MOD * d)
    xs = x.reshape(t, d)
    fw = final_w.reshape(1, d)
    ff = mlp_w1.shape[2]

    for i in range(depth):
        j = i // 2
        n1 = norm1_w[i].reshape(1, d)
        if i % 2 == 0:
            q, k, lf, v, g, wmin = _hg_proj_call(xs, mod3, n1, hg_lb, hg_w_in[j],
                                                 layer=i, batch=batch, tm=PROJ_TM)
            wmin = wmin.reshape(batch, seq // HG_BLOCK, HG_BLOCK // PROJ_TM, d // HG_HEAD, HG_HEAD).min(axis=(2, 4))
            y, w1_all, w2_all = _hg_rec_call(wmin.transpose(0, 2, 1).reshape(-1), q, k, lf, v,
                                             hg_gn_w[j].reshape(1, HG_HEAD),
                                             mlp_w1.reshape(depth * d, ff), mlp_w2.reshape(depth * ff, d),
                                             batch=batch, tc=HG_BLOCK,
                                             unroll_bounded=HG_GROUP, unroll_exact=HG_GROUP_EXACT)
            w1_all, w2_all = w1_all.reshape(depth, d, ff), w2_all.reshape(depth, ff, d)
            w_out = hg_w_out[j]
        else:
            pad = ((0, 0), (0, LANES - fox_heads))
            q, k, v, g, lf = _fox_proj_call(
                xs, mod3, n1,
                jnp.tile(fox_qn_w[j], fox_heads).reshape(1, d),
                jnp.tile(fox_kn_w[j], fox_heads).reshape(1, d),
                jnp.pad(fox_b_f[j].reshape(1, fox_heads), pad),
                jnp.swapaxes(fox_w_in, 1, 2), jnp.pad(fox_w_in[j, :, 4 * d:], pad).astype(BF16),
                layer=i, w_index=j, batch=batch, tm=PROJ_TM)
            aq, ak, edge, flag = _fox_bias_call(lf, fox_qn_w[j].reshape(1, FOX_HEAD),
                                                fox_kn_w[j].reshape(1, FOX_HEAD), batch=batch, d=d, tp=ATTN_TK)
            edge = edge.reshape(batch, seq // ATTN_TK, 2, LANES)[..., :fox_heads].transpose(2, 0, 3, 1)
            y = _fox_attn_call(flag[0, :1], edge[0].reshape(-1), edge[1].reshape(-1), q, aq, k, ak, v,
                               batch=batch, seq=seq, tq=ATTN_TQ, tk=ATTN_TK)
            w_out = fox_w_out[j]
        xs = _post_call(xs, y, g, mod3, norm2_w[i].reshape(1, d), fw, w_out.astype(BF16), w1_all, w2_all,
                        layer=i, batch=batch, tm=PROJ_TM, final=(i == depth - 1), y_features=(i % 2 == 1))
    return xs.reshape(batch, seq, d)
```

```python
import functools

import numpy as np
import jax
import jax.numpy as jnp
from jax import lax
from jax.experimental import pallas as pl
from jax.experimental.pallas import tpu as pltpu

F32 = jnp.float32
BF16 = jnp.bfloat16
EPS = 1e-6
N_MOD = 6
HG_HEAD = 128
HG_CHUNK = 64
HG_SUB = 16
HG_SAFE_DECAY = 56.0
FOX_HEAD = 64
LANES = 128
SUBLANES = 8
LOG2E = float(np.log2(np.e))
FOX_STAB_MAX = 30.0
FOX_BOUND_SLACK = 1.01
FOX_SKIP = 160.0
ROW_SLAB = 256
VMEM_LIMIT = 56 * 1024 * 1024

PROJ_TM = 512
POST_FF_BLK = 1024
MOD_TN = 1024
HG_BLOCK = 4096
HG_GROUP = 32
HG_GROUP_EXACT = 4
ATTN_TK = 512
ATTN_TQ = 2 * ATTN_TK

_HI = lax.Precision.HIGHEST


def _cparams(sem):
    return pltpu.CompilerParams(dimension_semantics=sem, vmem_limit_bytes=VMEM_LIMIT)


def _resident(shape):
    nd = len(shape)
    return pl.BlockSpec(shape, lambda *_: (0,) * nd, pipeline_mode=pl.Buffered(1))


def _sigmoid(x):
    return 0.5 * jnp.tanh(0.5 * x) + 0.5


def _cast_weight_once(w_ref, wb_ref, cols, transposed=False):
    @pl.when(pl.program_id(0) == 0)
    def _():
        for c0 in range(0, wb_ref.shape[1], cols):
            src = w_ref[c0:c0 + cols, :].T if transposed else w_ref[:, c0:c0 + cols]
            wb_ref[:, c0:c0 + cols] = src.astype(BF16)


def _modulate(x, nw, shift, scale):
    ms = jnp.mean(x * x, axis=-1, keepdims=True)
    y = x * lax.rsqrt(ms + EPS)
    return (y * nw) * (1.0 + scale) + shift


def _mod_kernel(ct_ref, w_ref, b_ref, o_ref):
    ct = ct_ref[...]
    cat = ct * _sigmoid(ct)
    w = w_ref[0]
    rows = []
    for b in range(ct.shape[1]):
        col = jnp.broadcast_to(cat[:, b:b + 1], (ct.shape[0], LANES))
        rows.append(jnp.concatenate(
            [jnp.sum(w[:, j * LANES:(j + 1) * LANES] * col, axis=0, keepdims=True)
             for j in range(w.shape[1] // LANES)], axis=1))
    o_ref[0] = jnp.concatenate(rows, axis=0) + b_ref[0]


def _mod_call(c, w_mod, b_mod):
    depth, d, n = w_mod.shape
    b = c.shape[0]
    tn = MOD_TN
    return pl.pallas_call(
        _mod_kernel,
        grid=(depth, n // tn),
        in_specs=[
            pl.BlockSpec((d, b), lambda l, j: (0, 0)),
            pl.BlockSpec((1, d, tn), lambda l, j: (l, 0, j)),
            pl.BlockSpec((1, 1, tn), lambda l, j: (l, 0, j)),
        ],
        out_specs=pl.BlockSpec((1, b, tn), lambda l, j: (l, 0, j)),
        out_shape=jax.ShapeDtypeStruct((depth, b, n), F32),
        compiler_params=_cparams(("arbitrary", "arbitrary")),
        name="mod",
    )(c.T, w_mod, b_mod.reshape(depth, 1, n))


def _hg_proj_kernel(x_ref, mod_ref, nw_ref, lb_ref, wf32_ref, q_ref, k_ref, lf_ref, v_ref, g_ref, wmin_ref,
                    w_ref, *, layer, d):
    _cast_weight_once(wf32_ref, w_ref, d)
    mod = mod_ref[0]
    lbp = lb_ref[...]
    e = jnp.exp(lbp - jnp.max(lbp, axis=0, keepdims=True))
    lb = jnp.sum(e[0:layer + 1], axis=0, keepdims=True) / jnp.sum(e, axis=0, keepdims=True)

    w_r = lax.broadcasted_iota(jnp.int32, (ROW_SLAB // HG_SUB, ROW_SLAB), 0)
    w_c = lax.broadcasted_iota(jnp.int32, (ROW_SLAB // HG_SUB, ROW_SLAB), 1)
    win = (lax.shift_right_logical(w_c, HG_SUB.bit_length() - 1) == w_r).astype(BF16)
    lfs = []

    for r0 in range(0, x_ref.shape[0], ROW_SLAB):
        rows = slice(r0, r0 + ROW_SLAB)
        h = _modulate(x_ref[rows, :], nw_ref[...], mod[:, 0:d], mod[:, d:2 * d]).astype(BF16)

        pq = jnp.dot(h, w_ref[:, 0:d], preferred_element_type=F32)
        q_ref[rows, :] = (pq * _sigmoid(pq)).astype(BF16)

        z = jnp.dot(h, w_ref[:, d:2 * d], preferred_element_type=F32)
        th = 0.5 * jnp.tanh(0.5 * z)
        kk = (1.0 - lb) * (0.5 - th)
        f = lb + (1.0 - lb) * (0.5 + th)
        lf = jnp.log(jnp.where(kk < 0.5, 1.0 - kk, f))
        lf_ref[rows, :] = lf
        k_ref[rows, :] = kk.astype(BF16)
        lfs.append(lf.astype(BF16))

        v_ref[rows, :] = jnp.dot(h, w_ref[:, 2 * d:3 * d], preferred_element_type=F32).astype(BF16)
        pg = jnp.dot(h, w_ref[:, 3 * d:4 * d], preferred_element_type=F32)
        g_ref[rows, :] = (pg * _sigmoid(pg)).astype(BF16)

    wmin = None
    for lf16 in lfs:
        wsum = jnp.min(jnp.dot(win, lf16, preferred_element_type=F32), axis=0, keepdims=True)
        wmin = wsum if wmin is None else jnp.minimum(wmin, wsum)
    wmin_ref[0] = wmin


def _hg_proj_call(x2, mod3, nw, hg_lb, w_in, *, layer, batch, tm):
    t, d = x2.shape
    tiles_per_batch = (t // batch) // tm
    row = pl.BlockSpec((tm, d), lambda i: (i, 0))
    out_bf = jax.ShapeDtypeStruct((t, d), BF16)
    return pl.pallas_call(
        functools.partial(_hg_proj_kernel, layer=layer, d=d),
        grid=(t // tm,),
        in_specs=[
            row,
            pl.BlockSpec((1, 1, N_MOD * d), lambda i: (layer * batch + i // tiles_per_batch, 0, 0)),
            _resident((1, d)),
            _resident(hg_lb.shape),
            _resident(w_in.shape),
        ],
        out_specs=[row, row, row, row, row, pl.BlockSpec((1, 1, d), lambda i: (i, 0, 0))],
        out_shape=[out_bf, out_bf, jax.ShapeDtypeStruct((t, d), F32), out_bf, out_bf,
                   jax.ShapeDtypeStruct((t // tm, 1, d), F32)],
        scratch_shapes=[pltpu.VMEM(w_in.shape, BF16)],
        compiler_params=_cparams(("arbitrary",)),
        name="hg_proj",
    )(x2, mod3, nw, hg_lb, w_in)


def _hg_rec_kernel(wmin_ref, q_ref, k_ref, lf_ref, v_ref, gnw_ref, ca_ref, cb_ref, o_ref, cao_ref, cbo_ref,
                   st_ref, gs_ref, ks_ref, *, n_chunks, unroll_bounded, unroll_exact):
    cao_ref[...] = ca_ref[...].astype(BF16)
    cbo_ref[...] = cb_ref[...].astype(BF16)
    c, sub = HG_CHUNK, HG_SUB
    n_sub = c // sub
    half = sub // 2

    @pl.when(pl.program_id(2) == 0)
    def _():
        st_ref[...] = jnp.zeros_like(st_ref)

    r_i = lax.broadcasted_iota(jnp.int32, (c, c), 0)
    c_i = lax.broadcasted_iota(jnp.int32, (c, c), 1)
    tri = (c_i <= r_i).astype(BF16)
    ones_w = jnp.ones((HG_HEAD, c), BF16)
    half_row = lax.broadcasted_iota(jnp.int32, (half, HG_HEAD), 0)
    a_lane = lax.broadcasted_iota(jnp.int32, (half, c), 1)
    gnw = gnw_ref[...]
    nt = (((1,), (1,)), ((), ()))
    tn = (((0,), (0,)), ((), ()))

    def offdiag_operands(q, k, gcum, j, bounded):
        lo = j * sub
        hi = lo + sub if bounded else lo
        g_b = gcum[lo - 1:lo, :] if j > 0 else jnp.zeros((1, HG_HEAD), F32)
        qh = (q[lo:lo + sub, :] * jnp.exp2(gcum[lo:lo + sub, :] - g_b)).astype(BF16)
        kh = (k[0:hi, :] * jnp.exp2(g_b - gcum[0:hi, :])).astype(BF16)
        if hi < c:
            kh = jnp.concatenate([kh, jnp.zeros((c - hi, HG_HEAD), BF16)], axis=0)
        return qh, kh

    def diag_products(q, k, gcum, slot, j):
        lo = j * sub
        gt = (gcum[lo:lo + half, :], gcum[lo + half:lo + sub, :])
        qt = (q[lo:lo + half, :], q[lo + half:lo + sub, :])
        ps = []
        for s in range(sub):
            g_s = gs_ref[slot, pl.ds(lo + s, 1), :]
            k_s = ks_ref[slot, pl.ds(lo + s, 1), :]
            for hf in range(2):
                if s >= half and hf == 0:
                    continue
                dlt = gt[hf] - g_s
                if (s >= half) == (hf == 1):
                    dlt = jnp.where(half_row >= s % half, dlt, -jnp.inf)
                ps.append((qt[hf] * k_s) * jnp.exp2(dlt))
        return jnp.concatenate(ps, axis=0).astype(BF16)

    def diag_scatter(r, j):
        lo = j * sub
        a_top = jnp.zeros((half, c), F32)
        a_bot = jnp.zeros((half, c), F32)
        for s in range(sub):
            if s < half:
                a_top = jnp.where(a_lane == lo + s, r[2 * s * half:(2 * s + 1) * half, :], a_top)
                a_bot = jnp.where(a_lane == lo + s, r[(2 * s + 1) * half:(2 * s + 2) * half, :], a_bot)
            else:
                a_bot = jnp.where(a_lane == lo + s, r[(half + s) * half:(half + s + 1) * half, :], a_bot)
        return jnp.concatenate([a_top, a_bot], axis=0)

    def run(bounded, unroll):
        def group(i, st):
            us = range(unroll)
            rows = [pl.ds(pl.multiple_of((i * unroll + u) * c, c), c) for u in us]
            g3 = [jnp.dot(tri, _split3(lf_ref[r, :]), preferred_element_type=F32) for r in rows]
            q = [q_ref[r, :].astype(F32) for r in rows]
            k = [k_ref[r, :].astype(F32) for r in rows]
            gcum = [(g[:, 0:HG_HEAD] + g[:, HG_HEAD:2 * HG_HEAD] + g[:, 2 * HG_HEAD:]) * LOG2E for g in g3]
            g_last = [g[c - 1:c, :] for g in gcum]
            qg = [(q[u] * jnp.exp2(gcum[u])).astype(BF16) for u in us]
            kd = [(k[u] * jnp.exp2(g_last[u] - gcum[u])).astype(BF16) for u in us]
            first = 0 if bounded else 1
            off = [[offdiag_operands(q[u], k[u], gcum[u], j, bounded) for j in range(first, n_sub)] for u in us]
            if not bounded:
                for u in us:
                    gs_ref[u] = gcum[u]
                    ks_ref[u] = k[u]
                pst = [[diag_products(q[u], k[u], gcum[u], u, j) for j in range(n_sub)] for u in us]
            kv = [lax.dot_general(v_ref[rows[u], :], kd[u], tn, preferred_element_type=F32) for u in us]
            blocks = [[lax.dot_general(qh, kh, nt, preferred_element_type=F32) for qh, kh in off[u]] for u in us]
            if bounded:
                a = [jnp.where(c_i <= r_i, jnp.concatenate(blocks[u], axis=0), 0.0) for u in us]
            else:
                sums = [[jnp.dot(p, ones_w, preferred_element_type=F32) for p in pst[u]] for u in us]
                a = []
                for u in us:
                    a_rows = [diag_scatter(sums[u][j], j) for j in range(n_sub)]
                    for j in range(1, n_sub):
                        a_rows[j] = a_rows[j] + blocks[u][j - 1]
                    a.append(jnp.concatenate(a_rows, axis=0))
            o_intra = [jnp.dot(a[u].astype(BF16), v_ref[rows[u], :], preferred_element_type=F32) for u in us]
            for u in us:
                o = o_intra[u] + lax.dot_general(qg[u], st.astype(BF16), nt, preferred_element_type=F32)
                st = st * jnp.exp2(g_last[u]) + kv[u]
                on = o * lax.rsqrt(jnp.mean(o * o, axis=-1, keepdims=True) + EPS) * gnw
                o_ref[rows[u], :] = on.astype(BF16)
            return st

        st_ref[...] = lax.fori_loop(0, n_chunks // unroll, group, st_ref[...])

    blk_id = (pl.program_id(0) * pl.num_programs(1) + pl.program_id(1)) * pl.num_programs(2) + pl.program_id(2)
    bounded = wmin_ref[blk_id] * LOG2E >= -HG_SAFE_DECAY

    @pl.when(bounded)
    def _():
        run(True, unroll_bounded)

    @pl.when(jnp.logical_not(bounded))
    def _():
        run(False, unroll_exact)


def _hg_rec_call(wmin, q, k, lf, v, gnw, cast_a, cast_b, *, batch, tc, unroll_bounded, unroll_exact):
    t, d = q.shape
    heads = d // HG_HEAD
    steps = (t // batch) // tc
    n_steps = batch * heads * steps
    unroll = unroll_exact
    blk = pl.BlockSpec((tc, HG_HEAD), lambda b, h, s, *_: (b * steps + s, h))

    def cast_spec(a):
        assert a.shape[0] % (n_steps * 2 * SUBLANES) == 0
        return pl.BlockSpec((a.shape[0] // n_steps, a.shape[1]), lambda b, h, s, *_: ((b * heads + h) * steps + s, 0))

    grid_spec = pltpu.PrefetchScalarGridSpec(
        num_scalar_prefetch=1,
        grid=(batch, heads, steps),
        in_specs=[blk, blk, blk, blk, pl.BlockSpec((1, HG_HEAD), lambda b, h, s, *_: (0, 0)),
                  cast_spec(cast_a), cast_spec(cast_b)],
        out_specs=[blk, cast_spec(cast_a), cast_spec(cast_b)],
        scratch_shapes=[
            pltpu.VMEM((HG_HEAD, HG_HEAD), F32),
            pltpu.VMEM((unroll, HG_CHUNK, HG_HEAD), F32),
            pltpu.VMEM((unroll, HG_CHUNK, HG_HEAD), F32),
        ],
    )
    return pl.pallas_call(
        functools.partial(_hg_rec_kernel, n_chunks=tc // HG_CHUNK, unroll_bounded=unroll_bounded,
                          unroll_exact=unroll_exact),
        grid_spec=grid_spec,
        out_shape=[jax.ShapeDtypeStruct((t, d), BF16), jax.ShapeDtypeStruct(cast_a.shape, BF16),
                   jax.ShapeDtypeStruct(cast_b.shape, BF16)],
        compiler_params=_cparams(("arbitrary", "arbitrary", "arbitrary")),
        name="hg_rec",
    )(wmin, q, k, lf, v, gnw, cast_a, cast_b)


def _head_rmsnorm(p, w_row):
    low = lax.broadcasted_iota(jnp.int32, (1, LANES), 1) < FOX_HEAD
    outs = []
    for j in range(p.shape[1] // LANES):
        pj = p[:, j * LANES:(j + 1) * LANES]
        ss = pj * pj
        s_lo = jnp.sum(jnp.where(low, ss, 0.0), axis=-1, keepdims=True)
        s_hi = jnp.sum(jnp.where(low, 0.0, ss), axis=-1, keepdims=True)
        inv = lax.rsqrt(jnp.where(low, s_lo, s_hi) * (1.0 / FOX_HEAD) + EPS)
        outs.append(pj * inv)
    return jnp.concatenate(outs, axis=1) * w_row


def _fox_proj_kernel(x_ref, mod_ref, nw_ref, qw_ref, kw_ref, bf_ref, wf32_ref, wf_ref,
                     q_ref, k_ref, v_ref, g_ref, lf_ref, vs_ref, w_ref, *, d):
    _cast_weight_once(wf32_ref.at[0], w_ref, d, transposed=True)
    mod = mod_ref[0]
    scale = LOG2E / np.sqrt(FOX_HEAD)
    for r0 in range(0, x_ref.shape[0], ROW_SLAB):
        rows = slice(r0, r0 + ROW_SLAB)
        h = _modulate(x_ref[rows, :], nw_ref[...], mod[:, 0:d], mod[:, d:2 * d]).astype(BF16)
        pq = jnp.dot(h, w_ref[:, 0:d], preferred_element_type=F32)
        q_ref[rows, :] = (_head_rmsnorm(pq, qw_ref[...]) * scale).astype(BF16)
        pk = jnp.dot(h, w_ref[:, d:2 * d], preferred_element_type=F32)
        k_ref[rows, :] = _head_rmsnorm(pk, kw_ref[...]).astype(BF16)
        vs_ref[...] = jnp.dot(h, w_ref[:, 2 * d:3 * d], preferred_element_type=F32)
        v_ref[:, rows] = vs_ref[...].T.astype(BF16)
        pg = jnp.dot(h, w_ref[:, 3 * d:4 * d], preferred_element_type=F32)
        g_ref[rows, :] = _sigmoid(pg).astype(BF16)
        u = jnp.dot(h, wf_ref[...], preferred_element_type=F32) + bf_ref[...]
        lf_ref[rows, :] = jnp.minimum(u, 0.0) - jnp.log1p(jnp.exp(-jnp.abs(u)))


def _fox_proj_call(x2, mod3, nw, qw, kw, bf, w_main, w_f, *, layer, w_index, batch, tm):
    t, d = x2.shape
    nh = w_f.shape[1]
    tiles_per_batch = (t // batch) // tm
    row = pl.BlockSpec((tm, d), lambda i: (i, 0))
    out_bf = jax.ShapeDtypeStruct((t, d), BF16)
    return pl.pallas_call(
        functools.partial(_fox_proj_kernel, d=d),
        grid=(t // tm,),
        in_specs=[
            row,
            pl.BlockSpec((1, 1, N_MOD * d), lambda i: (layer * batch + i // tiles_per_batch, 0, 0)),
            _resident((1, d)), _resident((1, d)), _resident((1, d)), _resident((1, nh)),
            pl.BlockSpec((1, 4 * d, d), lambda i: (w_index, 0, 0), pipeline_mode=pl.Buffered(1)),
            _resident(w_f.shape),
        ],
        out_specs=[row, row, pl.BlockSpec((d, tm), lambda i: (0, i)), row, pl.BlockSpec((tm, nh), lambda i: (i, 0))],
        out_shape=[out_bf, out_bf, jax.ShapeDtypeStruct((d, t), BF16), out_bf, jax.ShapeDtypeStruct((t, nh), F32)],
        scratch_shapes=[pltpu.VMEM((ROW_SLAB, d), F32),
                        pltpu.VMEM((d, 4 * d), BF16)],
        compiler_params=_cparams(("arbitrary",)),
        name="fox_proj",
    )(x2, mod3, nw, qw, kw, bf, w_main, w_f)


FOX_AUG = 6


def _split3(x):
    hi = x.astype(BF16)
    r1 = x - hi.astype(F32)
    mid = r1.astype(BF16)
    lo = (r1 - mid.astype(F32)).astype(BF16)
    return jnp.concatenate([hi, mid, lo], axis=1)


def _fox_bias_kernel(lf_ref, qw_ref, kw_ref, pq_ref, pk_ref, oq_ref, ok_ref,
                     aq_ref, ak_ref, edge_ref, flag_ref, *, tp):
    bound = FOX_BOUND_SLACK * LOG2E * np.sqrt(FOX_HEAD) * jnp.max(jnp.abs(qw_ref[...] * kw_ref[...]), axis=-1, keepdims=True)
    fast = bound <= FOX_STAB_MAX
    flag_ref[...] = jnp.broadcast_to(fast.astype(jnp.int32), flag_ref.shape)
    stab = jnp.where(fast, bound, 0.0)

    r_i = lax.broadcasted_iota(jnp.int32, (tp, tp), 0)
    c_i = lax.broadcasted_iota(jnp.int32, (tp, tp), 1)
    tri = (c_i <= r_i).astype(BF16)

    def tile(j, carry):
        rows = pl.ds(pl.multiple_of(j * tp, tp), tp)
        f3 = jnp.dot(tri, _split3(lf_ref[rows, :]), preferred_element_type=F32)
        f = f3[:, 0:LANES] + f3[:, LANES:2 * LANES] + f3[:, 2 * LANES:] + carry
        f2 = f * LOG2E
        edge_ref[j] = jnp.concatenate([f2[0:1, :], f2[tp - 1:tp, :]], axis=0)
        aq_ref[rows, :] = (jnp.dot(_split3(f2 - stab), pq_ref[...], preferred_element_type=F32)
                           + oq_ref[...]).astype(BF16)
        ak_ref[rows, :] = (jnp.dot(_split3(f2), pk_ref[...], preferred_element_type=F32)
                           + ok_ref[...]).astype(BF16)
        return f[tp - 1:tp, :]

    lax.fori_loop(0, lf_ref.shape[0] // tp, tile, jnp.zeros((1, LANES), F32))


def _fox_bias_call(lf, qw, kw, *, batch, d, tp):
    t = lf.shape[0]
    heads = d // FOX_HEAD
    steps = (t // batch) // tp
    pq = np.zeros((3 * LANES, LANES), np.float32)
    pk = np.zeros((3 * LANES, LANES), np.float32)
    oq = np.zeros((1, LANES), np.float32)
    ok = np.zeros((1, LANES), np.float32)
    for h in range(heads):
        base = FOX_AUG * h
        for i in range(3):
            pq[i * LANES + h, base + i] = 1.0
            ok[0, base + i] = 1.0
            oq[0, base + 3 + i] = 1.0
            pk[i * LANES + h, base + 3 + i] = -1.0
    whole = pl.BlockSpec((t // batch, LANES), lambda b: (b, 0))
    return pl.pallas_call(
        functools.partial(_fox_bias_kernel, tp=tp),
        grid=(batch,),
        in_specs=[
            whole,
            _resident((1, FOX_HEAD)), _resident((1, FOX_HEAD)),
            _resident(pq.shape), _resident(pk.shape), _resident(oq.shape), _resident(ok.shape),
        ],
        out_specs=[whole, whole,
                   pl.BlockSpec((steps, 2, LANES), lambda b: (b, 0, 0)),
                   pl.BlockSpec((SUBLANES, LANES), lambda b: (0, 0))],
        out_shape=[jax.ShapeDtypeStruct((t, LANES), BF16), jax.ShapeDtypeStruct((t, LANES), BF16),
                   jax.ShapeDtypeStruct((batch * steps, 2, LANES), F32),
                   jax.ShapeDtypeStruct((SUBLANES, LANES), jnp.int32)],
        compiler_params=_cparams(("arbitrary",)),
        name="fox_bias",
    )(lf, qw, kw, jnp.asarray(pq, BF16), jnp.asarray(pk, BF16), jnp.asarray(oq), jnp.asarray(ok))


FOX_VROWS = 80


def _fox_attn_kernel(flag_ref, fs_ref, fe_ref, q_ref, aq_ref, k_ref, ak_ref, v_ref, o_ref,
                     v1_ref, qm_ref, acc_ref, m_ref, pa_ref, pb_ref, *, tq, tk, heads):
    bi, pair, qi = pl.program_id(0), pl.program_id(1), pl.program_id(2)
    nk = k_ref.shape[0] // tk
    sub = tq // tk
    pad_rows = lax.broadcasted_iota(jnp.int32, (FOX_VROWS - FOX_HEAD, tk), 0)
    ones_blk = jnp.where(pad_rows == 0, 1.0, 0.0).astype(BF16)
    lane2 = lax.broadcasted_iota(jnp.int32, (1, 2 * LANES), 1)
    bias0 = LANES + 2 * FOX_AUG * pair
    head0 = (lane2 < FOX_HEAD) | ((lane2 >= bias0) & (lane2 < bias0 + FOX_AUG))
    head1 = ((lane2 >= FOX_HEAD) & (lane2 < LANES)) | ((lane2 >= bias0 + FOX_AUG) & (lane2 < bias0 + 2 * FOX_AUG))
    nt = (((1,), (1,)), ((), ()))

    @pl.when(qi == 0)
    def _():
        def build(j, carry):
            cols = pl.ds(pl.multiple_of(j * tk, tk), tk)
            for e in range(2):
                v1_ref[j, e] = jnp.concatenate([v_ref[e * FOX_HEAD:(e + 1) * FOX_HEAD, cols], ones_blk], axis=0)
            return carry

        lax.fori_loop(0, nk, build, 0)

    def probs(j, online, diag=None, dst=None):
        rows = pl.ds(pl.multiple_of(j * tk, tk), tk)
        kf = jnp.concatenate([k_ref[rows, :], ak_ref[rows, :]], axis=1)
        q_lo = 0 if diag is None else diag * tk
        ps, alphas = [], []
        for e in range(2):
            s = lax.dot_general(kf, qm_ref[e, q_lo:tq, :], nt, preferred_element_type=F32)
            if diag is not None:
                kpos = lax.broadcasted_iota(jnp.int32, (tk, tk), 0)
                qpos = lax.broadcasted_iota(jnp.int32, (tk, tk), 1)
                blk = jnp.where(kpos <= qpos, s[:, 0:tk], -jnp.inf)
                s = blk if q_lo + tk == tq else jnp.concatenate([blk, s[:, tk:]], axis=1)
            if online:
                m_prev = m_ref[e, :, q_lo:tq]
                m_new = jnp.maximum(m_prev, jnp.max(s, axis=0, keepdims=True))
                alphas.append(jnp.exp2(m_prev - m_new))
                m_ref[e, :, q_lo:tq] = m_new
                s = s - m_new
            p = jnp.exp2(s).astype(BF16)
            if dst is None:
                ps.append(p)
            else:
                dst[e] = p
        return ps, alphas

    def accumulate(ps, j, alphas=None, diag=None):
        q_lo = 0 if diag is None else diag * tk
        for e in range(2):
            pv = jnp.dot(v1_ref[j, e], ps[e], preferred_element_type=F32)
            if alphas:
                acc_ref[e, :, q_lo:tq] = acc_ref[e, :, q_lo:tq] * alphas[e] + pv
            else:
                acc_ref[e, :, q_lo:tq] += pv

    def start():
        qf = jnp.concatenate([q_ref[...], aq_ref[...]], axis=1)
        zero = jnp.zeros_like(qf)
        qm_ref[0] = jnp.where(head0, qf, zero)
        qm_ref[1] = jnp.where(head1, qf, zero)
        acc_ref[...] = jnp.zeros_like(acc_ref)

    def finish():
        outs = []
        for e in range(2):
            acc = acc_ref[e]
            outs.append(acc[0:FOX_HEAD, :] / acc[FOX_HEAD:FOX_HEAD + 1, :])
        o_ref[...] = jnp.concatenate(outs, axis=0).astype(BF16)

    n_full = qi * sub

    @pl.when(flag_ref[0] == 1)
    def _():
        start()
        held = None
        for a in range(sub - 1, 0, -1):
            nxt = probs(n_full + a, False, diag=a)[0]
            if held is not None:
                accumulate(held[0], n_full + held[1], diag=held[1])
            held = (nxt, a)
        probs(n_full, False, diag=0, dst=pa_ref)
        if held is not None:
            accumulate(held[0], n_full + held[1], diag=held[1])

        h0 = bi * heads + 2 * pair
        fq0 = fs_ref[h0 * nk + n_full]
        fq1 = fs_ref[(h0 + 1) * nk + n_full]

        def dead(j, cnt):
            gone0 = fq0 - fe_ref[h0 * nk + j] < -FOX_SKIP
            gone1 = fq1 - fe_ref[(h0 + 1) * nk + j] < -FOX_SKIP
            return cnt + jnp.logical_and(gone0, gone1).astype(jnp.int32)

        first = lax.fori_loop(0, n_full, dead, 0)
        n_live = n_full - first

        def two_tiles(i, carry):
            j = first + 2 * i
            probs(j, False, dst=pb_ref)
            accumulate(pa_ref, jnp.where(i == 0, n_full, j - 1))
            probs(j + 1, False, dst=pa_ref)
            accumulate(pb_ref, j)
            return carry

        lax.fori_loop(0, n_live // 2, two_tiles, 0)

        @pl.when(n_live % 2 == 1)
        def _():
            ps, _ = probs(n_full - 1, False)
            accumulate(pa_ref, jnp.where(n_live == 1, n_full, n_full - 2))
            accumulate(ps, n_full - 1)

        @pl.when(n_live % 2 == 0)
        def _():
            accumulate(pa_ref, jnp.where(n_live == 0, n_full, n_full - 1))

        finish()

    @pl.when(flag_ref[0] != 1)
    def _():
        start()
        m_ref[...] = jnp.full_like(m_ref, -jnp.inf)

        def body(j, carry):
            ps, alphas = probs(j, True)
            accumulate(ps, j, alphas)
            return carry

        lax.fori_loop(0, n_full, body, 0)
        for a in range(sub):
            ps, alphas = probs(n_full + a, True, diag=a)
            accumulate(ps, n_full + a, alphas, diag=a)
        finish()


def _fox_attn_call(flag, f_first, f_last, q, aq, k, ak, v, *, batch, seq, tq, tk):
    t, d = q.shape
    pairs = d // LANES
    nq = seq // tq
    q_spec = pl.BlockSpec((tq, LANES), lambda b, p, i, *_: (b * nq + i, p))
    kv_spec = pl.BlockSpec((seq, LANES), lambda b, p, i, *_: (b, p))
    grid_spec = pltpu.PrefetchScalarGridSpec(
        num_scalar_prefetch=3,
        grid=(batch, pairs, nq),
        in_specs=[q_spec, pl.BlockSpec((tq, LANES), lambda b, p, i, *_: (b * nq + i, 0)),
                  kv_spec, pl.BlockSpec((seq, LANES), lambda b, p, i, *_: (b, 0)),
                  pl.BlockSpec((LANES, seq), lambda b, p, i, *_: (p, b))],
        out_specs=pl.BlockSpec((LANES, tq), lambda b, p, i, *_: (p, b * nq + i)),
        scratch_shapes=[
            pltpu.VMEM((seq // tk, 2, FOX_VROWS, tk), BF16),
            pltpu.VMEM((2, tq, 2 * LANES), BF16),
            pltpu.VMEM((2, FOX_VROWS, tq), F32),
            pltpu.VMEM((2, 1, tq), F32),
            pltpu.VMEM((2, tk, tq), BF16),
            pltpu.VMEM((2, tk, tq), BF16),
        ],
    )
    return pl.pallas_call(
        functools.partial(_fox_attn_kernel, tq=tq, tk=tk, heads=d // FOX_HEAD),
        grid_spec=grid_spec,
        out_shape=jax.ShapeDtypeStruct((d, t), BF16),
        compiler_params=_cparams(("arbitrary", "arbitrary", "arbitrary")),
        name="fox_attn",
    )(flag, f_first, f_last, q, aq, k, ak, v)


def _post_kernel(x_ref, y_ref, gate_ref, mod_ref, nw_ref, fw_ref, wo_ref, w1_ref, w2_ref, o_ref, *,
                 d, ff_blk, final, y_features):
    mod = mod_ref[0]
    g1 = mod[:, 2 * d:3 * d]
    sh2, sc2, g2 = mod[:, 3 * d:4 * d], mod[:, 4 * d:5 * d], mod[:, 5 * d:6 * d]
    y = y_ref[...].astype(F32)
    if y_features:
        y = y.T
    yg = (y * gate_ref[...].astype(F32)).astype(BF16)
    x1 = x_ref[...] + g1 * jnp.dot(yg, wo_ref[...], preferred_element_type=F32)
    h = _modulate(x1, nw_ref[...], sh2, sc2).astype(BF16)
    acc = jnp.zeros_like(x1)
    for j in range(w1_ref.shape[2] // ff_blk):
        a = jnp.maximum(jnp.dot(h, w1_ref[0, :, j * ff_blk:(j + 1) * ff_blk], preferred_element_type=F32), 0.0)
        acc = acc + jnp.dot((a * a).astype(BF16), w2_ref[0, j * ff_blk:(j + 1) * ff_blk, :],
                            preferred_element_type=F32)
    x2 = x1 + g2 * acc
    if final:
        x2 = x2 * lax.rsqrt(jnp.mean(x2 * x2, axis=-1, keepdims=True) + EPS) * fw_ref[...]
    o_ref[...] = x2


def _post_call(x2, y, gate, mod3, nw, fw, w_out, w1, w2, *, layer, batch, tm, final, y_features):
    t, d = x2.shape
    tiles_per_batch = (t // batch) // tm
    row = pl.BlockSpec((tm, d), lambda i: (i, 0))
    assert y.shape == ((d, t) if y_features else (t, d))
    return pl.pallas_call(
        functools.partial(_post_kernel, d=d, ff_blk=POST_FF_BLK, final=final, y_features=y_features),
        grid=(t // tm,),
        in_specs=[
            row,
            pl.BlockSpec((d, tm), lambda i: (0, i)) if y_features else row,
            row,
            pl.BlockSpec((1, 1, N_MOD * d), lambda i: (layer * batch + i // tiles_per_batch, 0, 0)),
            _resident((1, d)), _resident((1, d)),
            _resident(w_out.shape),
            pl.BlockSpec((1,) + w1.shape[1:], lambda i: (layer, 0, 0), pipeline_mode=pl.Buffered(1)),
            pl.BlockSpec((1,) + w2.shape[1:], lambda i: (layer, 0, 0), pipeline_mode=pl.Buffered(1)),
        ],
        out_specs=row,
        out_shape=jax.ShapeDtypeStruct((t, d), F32),
        compiler_params=_cparams(("arbitrary",)),
        name="post",
    )(x2, y, gate, mod3, nw, fw, w_out, w1, w2)


def kernel(x, c, w_mod, b_mod, norm1_w, norm2_w, hg_w_in, hg_w_out, hg_lb, hg_gn_w, fox_w_in, fox_b_f,
           fox_qn_w, fox_kn_w, fox_w_out, mlp_w1, mlp_w2, final_w):
    batch, seq, d = x.shape
    depth = w_mod.shape[0]
    t = batch * seq
    fox_heads = d // FOX_HEAD
    assert seq % HG_BLOCK == 0 and seq % ATTN_TQ == 0 and HG_BLOCK % PROJ_TM == 0 and PROJ_TM % ROW_SLAB == 0
    assert d % LANES == 0 and fox_heads * FOX_AUG <= LANES and (N_MOD * d) % MOD_TN == 0
    assert mlp_w1.shape[2] % POST_FF_BLK == 0 and x.dtype == F32

    mod3 = _mod_call(c, w_mod, b_mod).reshape(depth * batch, 1, N_MOD * d)
    xs = x.reshape(t, d)
    fw = final_w.reshape(1, d)
    ff = mlp_w1.shape[2]

    for i in range(depth):
        j = i // 2
        n1 = norm1_w[i].reshape(1, d)
        if i % 2 == 0:
            q, k, lf, v, g, wmin = _hg_proj_call(xs, mod3, n1, hg_lb, hg_w_in[j],
                                                 layer=i, batch=batch, tm=PROJ_TM)
            wmin = wmin.reshape(batch, seq // HG_BLOCK, HG_BLOCK // PROJ_TM, d // HG_HEAD, HG_HEAD).min(axis=(2, 4))
            y, w1_all, w2_all = _hg_rec_call(wmin.transpose(0, 2, 1).reshape(-1), q, k, lf, v,
                                             hg_gn_w[j].reshape(1, HG_HEAD),
                                             mlp_w1.reshape(depth * d, ff), mlp_w2.reshape(depth * ff, d),
                                             batch=batch, tc=HG_BLOCK,
                                             unroll_bounded=HG_GROUP, unroll_exact=HG_GROUP_EXACT)
            w1_all, w2_all = w1_all.reshape(depth, d, ff), w2_all.reshape(depth, ff, d)
            w_out = hg_w_out[j]
        else:
            pad = ((0, 0), (0, LANES - fox_heads))
            q, k, v, g, lf = _fox_proj_call(
                xs, mod3, n1,
                jnp.tile(fox_qn_w[j], fox_heads).reshape(1, d),
                jnp.tile(fox_kn_w[j], fox_heads).reshape(1, d),
                jnp.pad(fox_b_f[j].reshape(1, fox_heads), pad),
                jnp.swapaxes(fox_w_in, 1, 2), jnp.pad(fox_w_in[j, :, 4 * d:], pad).astype(BF16),
                layer=i, w_index=j, batch=batch, tm=PROJ_TM)
            aq, ak, edge, flag = _fox_bias_call(lf, fox_qn_w[j].reshape(1, FOX_HEAD),
                                                fox_kn_w[j].reshape(1, FOX_HEAD), batch=batch, d=d, tp=ATTN_TK)
            edge = edge.reshape(batch, seq // ATTN_TK, 2, LANES)[..., :fox_heads].transpose(2, 0, 3, 1)
            y = _fox_attn_call(flag[0, :1], edge[0].reshape(-1), edge[1].reshape(-1), q, aq, k, ak, v,
                               batch=batch, seq=seq, tq=ATTN_TQ, tk=ATTN_TK)
            w_out = fox_w_out[j]
        xs = _post_call(xs, y, g, mod3, norm2_w[i].reshape(1, d), fw, w_out.astype(BF16), w1_all, w2_all,
                        layer=i, batch=batch, tm=PROJ_TM, final=(i == depth - 1), y_features=(i % 2 == 1))
    return xs.reshape(batch, seq, d)
```

```python
import functools

import numpy as np
import jax
import jax.numpy as jnp
from jax import lax
from jax.experimental import pallas as pl
from jax.experimental.pallas import tpu as pltpu

F32 = jnp.float32
BF16 = jnp.bfloat16
EPS = 1e-6
N_MOD = 6
HG_HEAD = 128
HG_CHUNK = 64
HG_SUB = 16
HG_SAFE_DECAY = 56.0
FOX_HEAD = 64
LANES = 128
SUBLANES = 8
LOG2E = float(np.log2(np.e))
FOX_STAB_MAX = 30.0
FOX_BOUND_SLACK = 1.01
FOX_SKIP = 160.0
ROW_SLAB = 256
VMEM_LIMIT = 56 * 1024 * 1024

PROJ_TM = 512
POST_FF_BLK = 1024
MOD_TN = 1024
HG_BLOCK = 2048
HG_GROUP = 32
HG_GROUP_EXACT = 4
ATTN_TK = 512
ATTN_TQ = 2 * ATTN_TK

_HI = lax.Precision.HIGHEST


def _cparams(sem):
    return pltpu.CompilerParams(dimension_semantics=sem, vmem_limit_bytes=VMEM_LIMIT)


def _resident(shape):
    nd = len(shape)
    return pl.BlockSpec(shape, lambda *_: (0,) * nd, pipeline_mode=pl.Buffered(1))


def _sigmoid(x):
    return 0.5 * jnp.tanh(0.5 * x) + 0.5


def _cast_weight_once(w_ref, wb_ref, cols, transposed=False):
    @pl.when(pl.program_id(0) == 0)
    def _():
        for c0 in range(0, wb_ref.shape[1], cols):
            src = w_ref[c0:c0 + cols, :].T if transposed else w_ref[:, c0:c0 + cols]
            wb_ref[:, c0:c0 + cols] = src.astype(BF16)


def _modulate(x, nw, shift, scale):
    ms = jnp.mean(x * x, axis=-1, keepdims=True)
    y = x * lax.rsqrt(ms + EPS)
    return (y * nw) * (1.0 + scale) + shift


def _mod_kernel(ct_ref, w_ref, b_ref, o_ref):
    ct = ct_ref[...]
    cat = ct * _sigmoid(ct)
    w = w_ref[0]
    rows = []
    for b in range(ct.shape[1]):
        col = jnp.broadcast_to(cat[:, b:b + 1], (ct.shape[0], LANES))
        rows.append(jnp.concatenate(
            [jnp.sum(w[:, j * LANES:(j + 1) * LANES] * col, axis=0, keepdims=True)
             for j in range(w.shape[1] // LANES)], axis=1))
    o_ref[0] = jnp.concatenate(rows, axis=0) + b_ref[0]


def _mod_call(c, w_mod, b_mod):
    depth, d, n = w_mod.shape
    b = c.shape[0]
    tn = MOD_TN
    return pl.pallas_call(
        _mod_kernel,
        grid=(depth, n // tn),
        in_specs=[
            pl.BlockSpec((d, b), lambda l, j: (0, 0)),
            pl.BlockSpec((1, d, tn), lambda l, j: (l, 0, j)),
            pl.BlockSpec((1, 1, tn), lambda l, j: (l, 0, j)),
        ],
        out_specs=pl.BlockSpec((1, b, tn), lambda l, j: (l, 0, j)),
        out_shape=jax.ShapeDtypeStruct((depth, b, n), F32),
        compiler_params=_cparams(("arbitrary", "arbitrary")),
        name="mod",
    )(c.T, w_mod, b_mod.reshape(depth, 1, n))


def _hg_proj_kernel(x_ref, mod_ref, nw_ref, lb_ref, wf32_ref, q_ref, k_ref, lf_ref, v_ref, g_ref, wmin_ref,
                    w_ref, *, layer, d):
    _cast_weight_once(wf32_ref, w_ref, d)
    mod = mod_ref[0]
    lbp = lb_ref[...]
    e = jnp.exp(lbp - jnp.max(lbp, axis=0, keepdims=True))
    lb = jnp.sum(e[0:layer + 1], axis=0, keepdims=True) / jnp.sum(e, axis=0, keepdims=True)

    w_r = lax.broadcasted_iota(jnp.int32, (ROW_SLAB // HG_SUB, ROW_SLAB), 0)
    w_c = lax.broadcasted_iota(jnp.int32, (ROW_SLAB // HG_SUB, ROW_SLAB), 1)
    win = (lax.shift_right_logical(w_c, HG_SUB.bit_length() - 1) == w_r).astype(BF16)
    lfs = []

    for r0 in range(0, x_ref.shape[0], ROW_SLAB):
        rows = slice(r0, r0 + ROW_SLAB)
        h = _modulate(x_ref[rows, :], nw_ref[...], mod[:, 0:d], mod[:, d:2 * d]).astype(BF16)

        pq = jnp.dot(h, w_ref[:, 0:d], preferred_element_type=F32)
        q_ref[rows, :] = (pq * _sigmoid(pq)).astype(BF16)

        z = jnp.dot(h, w_ref[:, d:2 * d], preferred_element_type=F32)
        th = 0.5 * jnp.tanh(0.5 * z)
        kk = (1.0 - lb) * (0.5 - th)
        f = lb + (1.0 - lb) * (0.5 + th)
        lf = jnp.log(jnp.where(kk < 0.5, 1.0 - kk, f))
        lf_ref[rows, :] = lf
        k_ref[rows, :] = kk.astype(BF16)
        lfs.append(lf.astype(BF16))

        v_ref[rows, :] = jnp.dot(h, w_ref[:, 2 * d:3 * d], preferred_element_type=F32).astype(BF16)
        pg = jnp.dot(h, w_ref[:, 3 * d:4 * d], preferred_element_type=F32)
        g_ref[rows, :] = (pg * _sigmoid(pg)).astype(BF16)

    wmin = None
    for lf16 in lfs:
        wsum = jnp.min(jnp.dot(win, lf16, preferred_element_type=F32), axis=0, keepdims=True)
        wmin = wsum if wmin is None else jnp.minimum(wmin, wsum)
    wmin_ref[0] = wmin


def _hg_proj_call(x2, mod3, nw, hg_lb, w_in, *, layer, batch, tm):
    t, d = x2.shape
    tiles_per_batch = (t // batch) // tm
    row = pl.BlockSpec((tm, d), lambda i: (i, 0))
    out_bf = jax.ShapeDtypeStruct((t, d), BF16)
    return pl.pallas_call(
        functools.partial(_hg_proj_kernel, layer=layer, d=d),
        grid=(t // tm,),
        in_specs=[
            row,
            pl.BlockSpec((1, 1, N_MOD * d), lambda i: (layer * batch + i // tiles_per_batch, 0, 0)),
            _resident((1, d)),
            _resident(hg_lb.shape),
            _resident(w_in.shape),
        ],
        out_specs=[row, row, row, row, row, pl.BlockSpec((1, 1, d), lambda i: (i, 0, 0))],
        out_shape=[out_bf, out_bf, jax.ShapeDtypeStruct((t, d), F32), out_bf, out_bf,
                   jax.ShapeDtypeStruct((t // tm, 1, d), F32)],
        scratch_shapes=[pltpu.VMEM(w_in.shape, BF16)],
        compiler_params=_cparams(("arbitrary",)),
        name="hg_proj",
    )(x2, mod3, nw, hg_lb, w_in)


def _hg_rec_kernel(wmin_ref, q_ref, k_ref, lf_ref, v_ref, gnw_ref, ca_ref, cb_ref, o_ref, cao_ref, cbo_ref,
                   st_ref, gs_ref, ks_ref, *, n_chunks, unroll_bounded, unroll_exact):
    cao_ref[...] = ca_ref[...].astype(BF16)
    cbo_ref[...] = cb_ref[...].astype(BF16)
    c, sub = HG_CHUNK, HG_SUB
    n_sub = c // sub
    half = sub // 2

    @pl.when(pl.program_id(2) == 0)
    def _():
        st_ref[...] = jnp.zeros_like(st_ref)

    r_i = lax.broadcasted_iota(jnp.int32, (c, c), 0)
    c_i = lax.broadcasted_iota(jnp.int32, (c, c), 1)
    tri = (c_i <= r_i).astype(BF16)
    ones_w = jnp.ones((HG_HEAD, c), BF16)
    half_row = lax.broadcasted_iota(jnp.int32, (half, HG_HEAD), 0)
    a_lane = lax.broadcasted_iota(jnp.int32, (half, c), 1)
    gnw = gnw_ref[...]
    nt = (((1,), (1,)), ((), ()))
    tn = (((0,), (0,)), ((), ()))

    def offdiag_operands(q, k, gcum, j, bounded):
        lo = j * sub
        hi = lo + sub if bounded else lo
        g_b = gcum[lo - 1:lo, :] if j > 0 else jnp.zeros((1, HG_HEAD), F32)
        qh = (q[lo:lo + sub, :] * jnp.exp2(gcum[lo:lo + sub, :] - g_b)).astype(BF16)
        kh = (k[0:hi, :] * jnp.exp2(g_b - gcum[0:hi, :])).astype(BF16)
        if hi < c:
            kh = jnp.concatenate([kh, jnp.zeros((c - hi, HG_HEAD), BF16)], axis=0)
        return qh, kh

    def diag_products(q, k, gcum, slot, j):
        lo = j * sub
        gt = (gcum[lo:lo + half, :], gcum[lo + half:lo + sub, :])
        qt = (q[lo:lo + half, :], q[lo + half:lo + sub, :])
        ps = []
        for s in range(sub):
            g_s = gs_ref[slot, pl.ds(lo + s, 1), :]
            k_s = ks_ref[slot, pl.ds(lo + s, 1), :]
            for hf in range(2):
                if s >= half and hf == 0:
                    continue
                dlt = gt[hf] - g_s
                if (s >= half) == (hf == 1):
                    dlt = jnp.where(half_row >= s % half, dlt, -jnp.inf)
                ps.append((qt[hf] * k_s) * jnp.exp2(dlt))
        return jnp.concatenate(ps, axis=0).astype(BF16)

    def diag_scatter(r, j):
        lo = j * sub
        a_top = jnp.zeros((half, c), F32)
        a_bot = jnp.zeros((half, c), F32)
        for s in range(sub):
            if s < half:
                a_top = jnp.where(a_lane == lo + s, r[2 * s * half:(2 * s + 1) * half, :], a_top)
                a_bot = jnp.where(a_lane == lo + s, r[(2 * s + 1) * half:(2 * s + 2) * half, :], a_bot)
            else:
                a_bot = jnp.where(a_lane == lo + s, r[(half + s) * half:(half + s + 1) * half, :], a_bot)
        return jnp.concatenate([a_top, a_bot], axis=0)

    def run(bounded, unroll):
        def group(i, st):
            us = range(unroll)
            rows = [pl.ds(pl.multiple_of((i * unroll + u) * c, c), c) for u in us]
            g3 = [jnp.dot(tri, _split3(lf_ref[r, :]), preferred_element_type=F32) for r in rows]
            q = [q_ref[r, :].astype(F32) for r in rows]
            k = [k_ref[r, :].astype(F32) for r in rows]
            gcum = [(g[:, 0:HG_HEAD] + g[:, HG_HEAD:2 * HG_HEAD] + g[:, 2 * HG_HEAD:]) * LOG2E for g in g3]
            g_last = [g[c - 1:c, :] for g in gcum]
            qg = [(q[u] * jnp.exp2(gcum[u])).astype(BF16) for u in us]
            kd = [(k[u] * jnp.exp2(g_last[u] - gcum[u])).astype(BF16) for u in us]
            first = 0 if bounded else 1
            off = [[offdiag_operands(q[u], k[u], gcum[u], j, bounded) for j in range(first, n_sub)] for u in us]
            if not bounded:
                for u in us:
                    gs_ref[u] = gcum[u]
                    ks_ref[u] = k[u]
                pst = [[diag_products(q[u], k[u], gcum[u], u, j) for j in range(n_sub)] for u in us]
            kv = [lax.dot_general(v_ref[rows[u], :], kd[u], tn, preferred_element_type=F32) for u in us]
            blocks = [[lax.dot_general(qh, kh, nt, preferred_element_type=F32) for qh, kh in off[u]] for u in us]
            if bounded:
                a = [jnp.where(c_i <= r_i, jnp.concatenate(blocks[u], axis=0), 0.0) for u in us]
            else:
                sums = [[jnp.dot(p, ones_w, preferred_element_type=F32) for p in pst[u]] for u in us]
                a = []
                for u in us:
                    a_rows = [diag_scatter(sums[u][j], j) for j in range(n_sub)]
                    for j in range(1, n_sub):
                        a_rows[j] = a_rows[j] + blocks[u][j - 1]
                    a.append(jnp.concatenate(a_rows, axis=0))
            o_intra = [jnp.dot(a[u].astype(BF16), v_ref[rows[u], :], preferred_element_type=F32) for u in us]
            for u in us:
                o = o_intra[u] + lax.dot_general(qg[u], st.astype(BF16), nt, preferred_element_type=F32)
                st = st * jnp.exp2(g_last[u]) + kv[u]
                on = o * lax.rsqrt(jnp.mean(o * o, axis=-1, keepdims=True) + EPS) * gnw
                o_ref[rows[u], :] = on.astype(BF16)
            return st

        st_ref[...] = lax.fori_loop(0, n_chunks // unroll, group, st_ref[...])

    blk_id = (pl.program_id(0) * pl.num_programs(1) + pl.program_id(1)) * pl.num_programs(2) + pl.program_id(2)
    bounded = wmin_ref[blk_id] * LOG2E >= -HG_SAFE_DECAY

    @pl.when(bounded)
    def _():
        run(True, unroll_bounded)

    @pl.when(jnp.logical_not(bounded))
    def _():
        run(False, unroll_exact)


def _hg_rec_call(wmin, q, k, lf, v, gnw, cast_a, cast_b, *, batch, tc, unroll_bounded, unroll_exact):
    t, d = q.shape
    heads = d // HG_HEAD
    steps = (t // batch) // tc
    n_steps = batch * heads * steps
    unroll = unroll_exact
    blk = pl.BlockSpec((tc, HG_HEAD), lambda b, h, s, *_: (b * steps + s, h))

    def cast_spec(a):
        assert a.shape[0] % (n_steps * 2 * SUBLANES) == 0
        return pl.BlockSpec((a.shape[0] // n_steps, a.shape[1]), lambda b, h, s, *_: ((b * heads + h) * steps + s, 0))

    grid_spec = pltpu.PrefetchScalarGridSpec(
        num_scalar_prefetch=1,
        grid=(batch, heads, steps),
        in_specs=[blk, blk, blk, blk, pl.BlockSpec((1, HG_HEAD), lambda b, h, s, *_: (0, 0)),
                  cast_spec(cast_a), cast_spec(cast_b)],
        out_specs=[blk, cast_spec(cast_a), cast_spec(cast_b)],
        scratch_shapes=[
            pltpu.VMEM((HG_HEAD, HG_HEAD), F32),
            pltpu.VMEM((unroll, HG_CHUNK, HG_HEAD), F32),
            pltpu.VMEM((unroll, HG_CHUNK, HG_HEAD), F32),
        ],
    )
    return pl.pallas_call(
        functools.partial(_hg_rec_kernel, n_chunks=tc // HG_CHUNK, unroll_bounded=unroll_bounded,
                          unroll_exact=unroll_exact),
        grid_spec=grid_spec,
        out_shape=[jax.ShapeDtypeStruct((t, d), BF16), jax.ShapeDtypeStruct(cast_a.shape, BF16),
                   jax.ShapeDtypeStruct(cast_b.shape, BF16)],
        compiler_params=_cparams(("arbitrary", "arbitrary", "arbitrary")),
        name="hg_rec",
    )(wmin, q, k, lf, v, gnw, cast_a, cast_b)


def _head_rmsnorm(p, w_row):
    low = lax.broadcasted_iota(jnp.int32, (1, LANES), 1) < FOX_HEAD
    outs = []
    for j in range(p.shape[1] // LANES):
        pj = p[:, j * LANES:(j + 1) * LANES]
        ss = pj * pj
        s_lo = jnp.sum(jnp.where(low, ss, 0.0), axis=-1, keepdims=True)
        s_hi = jnp.sum(jnp.where(low, 0.0, ss), axis=-1, keepdims=True)
        inv = lax.rsqrt(jnp.where(low, s_lo, s_hi) * (1.0 / FOX_HEAD) + EPS)
        outs.append(pj * inv)
    return jnp.concatenate(outs, axis=1) * w_row


def _fox_proj_kernel(x_ref, mod_ref, nw_ref, qw_ref, kw_ref, bf_ref, wf32_ref, wf_ref,
                     hqw_ref, hkw_ref, pq_ref, pk_ref, oq_ref, ok_ref,
                     q_ref, k_ref, v_ref, g_ref, aq_ref, ak_ref, edge_ref, flag_ref,
                     vs_ref, w_ref, carry_ref, *, d, tiles_per_batch):
    _cast_weight_once(wf32_ref.at[0], w_ref, d, transposed=True)

    @pl.when(pl.program_id(0) % tiles_per_batch == 0)
    def _():
        carry_ref[...] = jnp.zeros_like(carry_ref)

    mod = mod_ref[0]
    scale = LOG2E / np.sqrt(FOX_HEAD)
    lfs = []
    for r0 in range(0, x_ref.shape[0], ROW_SLAB):
        rows = slice(r0, r0 + ROW_SLAB)
        h = _modulate(x_ref[rows, :], nw_ref[...], mod[:, 0:d], mod[:, d:2 * d]).astype(BF16)
        pq = jnp.dot(h, w_ref[:, 0:d], preferred_element_type=F32)
        q_ref[rows, :] = (_head_rmsnorm(pq, qw_ref[...]) * scale).astype(BF16)
        pk = jnp.dot(h, w_ref[:, d:2 * d], preferred_element_type=F32)
        k_ref[rows, :] = _head_rmsnorm(pk, kw_ref[...]).astype(BF16)
        vs_ref[...] = jnp.dot(h, w_ref[:, 2 * d:3 * d], preferred_element_type=F32)
        v_ref[:, rows] = vs_ref[...].T.astype(BF16)
        pg = jnp.dot(h, w_ref[:, 3 * d:4 * d], preferred_element_type=F32)
        g_ref[rows, :] = _sigmoid(pg).astype(BF16)
        u = jnp.dot(h, wf_ref[...], preferred_element_type=F32) + bf_ref[...]
        lfs.append(jnp.minimum(u, 0.0) - jnp.log1p(jnp.exp(-jnp.abs(u))))

    tm = x_ref.shape[0]
    bound = FOX_BOUND_SLACK * LOG2E * np.sqrt(FOX_HEAD) * jnp.max(jnp.abs(hqw_ref[...] * hkw_ref[...]), axis=-1, keepdims=True)
    fast = bound <= FOX_STAB_MAX
    flag_ref[...] = jnp.broadcast_to(fast.astype(jnp.int32), flag_ref.shape)
    stab = jnp.where(fast, bound, 0.0)
    r_i = lax.broadcasted_iota(jnp.int32, (tm, tm), 0)
    c_i = lax.broadcasted_iota(jnp.int32, (tm, tm), 1)
    tri = (c_i <= r_i).astype(BF16)
    f3 = jnp.dot(tri, _split3(jnp.concatenate(lfs, axis=0)), preferred_element_type=F32)
    f = f3[:, 0:LANES] + f3[:, LANES:2 * LANES] + f3[:, 2 * LANES:] + carry_ref[...]
    carry_ref[...] = f[tm - 1:tm, :]
    f2 = f * LOG2E
    edge_ref[0] = jnp.concatenate([f2[0:1, :], f2[tm - 1:tm, :]], axis=0)
    aq_ref[...] = (jnp.dot(_split3(f2 - stab), pq_ref[...], preferred_element_type=F32) + oq_ref[...]).astype(BF16)
    ak_ref[...] = (jnp.dot(_split3(f2), pk_ref[...], preferred_element_type=F32) + ok_ref[...]).astype(BF16)


def _fox_proj_call(x2, mod3, nw, qw, kw, bf, w_main, w_f, hqw, hkw, *, layer, w_index, batch, tm):
    t, d = x2.shape
    nh = w_f.shape[1]
    heads = d // FOX_HEAD
    tiles_per_batch = (t // batch) // tm
    pq = np.zeros((3 * LANES, LANES), np.float32)
    pk = np.zeros((3 * LANES, LANES), np.float32)
    oq = np.zeros((1, LANES), np.float32)
    ok = np.zeros((1, LANES), np.float32)
    for h in range(heads):
        base = FOX_AUG * h
        for i in range(3):
            pq[i * LANES + h, base + i] = 1.0
            ok[0, base + i] = 1.0
            oq[0, base + 3 + i] = 1.0
            pk[i * LANES + h, base + 3 + i] = -1.0
    row = pl.BlockSpec((tm, d), lambda i: (i, 0))
    lane_row = pl.BlockSpec((tm, LANES), lambda i: (i, 0))
    out_bf = jax.ShapeDtypeStruct((t, d), BF16)
    lane_bf = jax.ShapeDtypeStruct((t, LANES), BF16)
    return pl.pallas_call(
        functools.partial(_fox_proj_kernel, d=d, tiles_per_batch=tiles_per_batch),
        grid=(t // tm,),
        in_specs=[
            row,
            pl.BlockSpec((1, 1, N_MOD * d), lambda i: (layer * batch + i // tiles_per_batch, 0, 0)),
            _resident((1, d)), _resident((1, d)), _resident((1, d)), _resident((1, nh)),
            pl.BlockSpec((1, 4 * d, d), lambda i: (w_index, 0, 0), pipeline_mode=pl.Buffered(1)),
            _resident(w_f.shape),
            _resident((1, FOX_HEAD)), _resident((1, FOX_HEAD)),
            _resident(pq.shape), _resident(pk.shape), _resident(oq.shape), _resident(ok.shape),
        ],
        out_specs=[row, row, pl.BlockSpec((d, tm), lambda i: (0, i)), row, lane_row, lane_row,
                   pl.BlockSpec((1, 2, LANES), lambda i: (i, 0, 0)),
                   pl.BlockSpec((SUBLANES, LANES), lambda i: (0, 0))],
        out_shape=[out_bf, out_bf, jax.ShapeDtypeStruct((d, t), BF16), out_bf, lane_bf, lane_bf,
                   jax.ShapeDtypeStruct((t // tm, 2, LANES), F32),
                   jax.ShapeDtypeStruct((SUBLANES, LANES), jnp.int32)],
        scratch_shapes=[pltpu.VMEM((ROW_SLAB, d), F32),
                        pltpu.VMEM((d, 4 * d), BF16),
                        pltpu.VMEM((1, LANES), F32)],
        compiler_params=_cparams(("arbitrary",)),
        name="fox_proj",
    )(x2, mod3, nw, qw, kw, bf, w_main, w_f, hqw, hkw,
      jnp.asarray(pq, BF16), jnp.asarray(pk, BF16), jnp.asarray(oq), jnp.asarray(ok))


FOX_AUG = 6


def _split3(x):
    hi = x.astype(BF16)
    r1 = x - hi.astype(F32)
    mid = r1.astype(BF16)
    lo = (r1 - mid.astype(F32)).astype(BF16)
    return jnp.concatenate([hi, mid, lo], axis=1)


FOX_VROWS = 80


def _fox_attn_kernel(flag_ref, fs_ref, fe_ref, q_ref, aq_ref, k_ref, ak_ref, v_ref, o_ref,
                     v1_ref, qm_ref, acc_ref, m_ref, pa_ref, pb_ref, *, tq, tk, heads):
    bi, pair, qi = pl.program_id(0), pl.program_id(1), pl.program_id(2)
    nk = k_ref.shape[0] // tk
    sub = tq // tk
    pad_rows = lax.broadcasted_iota(jnp.int32, (FOX_VROWS - FOX_HEAD, tk), 0)
    ones_blk = jnp.where(pad_rows == 0, 1.0, 0.0).astype(BF16)
    lane2 = lax.broadcasted_iota(jnp.int32, (1, 2 * LANES), 1)
    bias0 = LANES + 2 * FOX_AUG * pair
    head0 = (lane2 < FOX_HEAD) | ((lane2 >= bias0) & (lane2 < bias0 + FOX_AUG))
    head1 = ((lane2 >= FOX_HEAD) & (lane2 < LANES)) | ((lane2 >= bias0 + FOX_AUG) & (lane2 < bias0 + 2 * FOX_AUG))
    nt = (((1,), (1,)), ((), ()))

    @pl.when(qi == 0)
    def _():
        def build(j, carry):
            cols = pl.ds(pl.multiple_of(j * tk, tk), tk)
            for e in range(2):
                v1_ref[j, e] = jnp.concatenate([v_ref[e * FOX_HEAD:(e + 1) * FOX_HEAD, cols], ones_blk], axis=0)
            return carry

        lax.fori_loop(0, nk, build, 0)

    def probs(j, online, diag=None, dst=None):
        rows = pl.ds(pl.multiple_of(j * tk, tk), tk)
        kf = jnp.concatenate([k_ref[rows, :], ak_ref[rows, :]], axis=1)
        q_lo = 0 if diag is None else diag * tk
        ps, alphas = [], []
        for e in range(2):
            s = lax.dot_general(kf, qm_ref[e, q_lo:tq, :], nt, preferred_element_type=F32)
            if diag is not None:
                kpos = lax.broadcasted_iota(jnp.int32, (tk, tk), 0)
                qpos = lax.broadcasted_iota(jnp.int32, (tk, tk), 1)
                blk = jnp.where(kpos <= qpos, s[:, 0:tk], -jnp.inf)
                s = blk if q_lo + tk == tq else jnp.concatenate([blk, s[:, tk:]], axis=1)
            if online:
                m_prev = m_ref[e, :, q_lo:tq]
                m_new = jnp.maximum(m_prev, jnp.max(s, axis=0, keepdims=True))
                alphas.append(jnp.exp2(m_prev - m_new))
                m_ref[e, :, q_lo:tq] = m_new
                s = s - m_new
            p = jnp.exp2(s).astype(BF16)
            if dst is None:
                ps.append(p)
            else:
                dst[e] = p
        return ps, alphas

    def accumulate(ps, j, alphas=None, diag=None):
        q_lo = 0 if diag is None else diag * tk
        for e in range(2):
            pv = jnp.dot(v1_ref[j, e], ps[e], preferred_element_type=F32)
            if alphas:
                acc_ref[e, :, q_lo:tq] = acc_ref[e, :, q_lo:tq] * alphas[e] + pv
            else:
                acc_ref[e, :, q_lo:tq] += pv

    def start():
        qf = jnp.concatenate([q_ref[...], aq_ref[...]], axis=1)
        zero = jnp.zeros_like(qf)
        qm_ref[0] = jnp.where(head0, qf, zero)
        qm_ref[1] = jnp.where(head1, qf, zero)
        acc_ref[...] = jnp.zeros_like(acc_ref)

    def finish():
        outs = []
        for e in range(2):
            acc = acc_ref[e]
            outs.append(acc[0:FOX_HEAD, :] / acc[FOX_HEAD:FOX_HEAD + 1, :])
        o_ref[...] = jnp.concatenate(outs, axis=0).astype(BF16)

    n_full = qi * sub

    @pl.when(flag_ref[0] == 1)
    def _():
        start()
        held = None
        for a in range(sub - 1, 0, -1):
            nxt = probs(n_full + a, False, diag=a)[0]
            if held is not None:
                accumulate(held[0], n_full + held[1], diag=held[1])
            held = (nxt, a)
        probs(n_full, False, diag=0, dst=pa_ref)
        if held is not None:
            accumulate(held[0], n_full + held[1], diag=held[1])

        h0 = bi * heads + 2 * pair
        fq0 = fs_ref[h0 * nk + n_full]
        fq1 = fs_ref[(h0 + 1) * nk + n_full]

        def dead(j, cnt):
            gone0 = fq0 - fe_ref[h0 * nk + j] < -FOX_SKIP
            gone1 = fq1 - fe_ref[(h0 + 1) * nk + j] < -FOX_SKIP
            return cnt + jnp.logical_and(gone0, gone1).astype(jnp.int32)

        first = lax.fori_loop(0, n_full, dead, 0)
        n_live = n_full - first

        def two_tiles(i, carry):
            j = first + 2 * i
            probs(j, False, dst=pb_ref)
            accumulate(pa_ref, jnp.where(i == 0, n_full, j - 1))
            probs(j + 1, False, dst=pa_ref)
            accumulate(pb_ref, j)
            return carry

        lax.fori_loop(0, n_live // 2, two_tiles, 0)

        @pl.when(n_live % 2 == 1)
        def _():
            ps, _ = probs(n_full - 1, False)
            accumulate(pa_ref, jnp.where(n_live == 1, n_full, n_full - 2))
            accumulate(ps, n_full - 1)

        @pl.when(n_live % 2 == 0)
        def _():
            accumulate(pa_ref, jnp.where(n_live == 0, n_full, n_full - 1))

        finish()

    @pl.when(flag_ref[0] != 1)
    def _():
        start()
        m_ref[...] = jnp.full_like(m_ref, -jnp.inf)

        def body(j, carry):
            ps, alphas = probs(j, True)
            accumulate(ps, j, alphas)
            return carry

        lax.fori_loop(0, n_full, body, 0)
        for a in range(sub):
            ps, alphas = probs(n_full + a, True, diag=a)
            accumulate(ps, n_full + a, alphas, diag=a)
        finish()


def _fox_attn_call(flag, f_first, f_last, q, aq, k, ak, v, *, batch, seq, tq, tk):
    t, d = q.shape
    pairs = d // LANES
    nq = seq // tq
    q_spec = pl.BlockSpec((tq, LANES), lambda b, p, i, *_: (b * nq + i, p))
    kv_spec = pl.BlockSpec((seq, LANES), lambda b, p, i, *_: (b, p))
    grid_spec = pltpu.PrefetchScalarGridSpec(
        num_scalar_prefetch=3,
        grid=(batch, pairs, nq),
        in_specs=[q_spec, pl.BlockSpec((tq, LANES), lambda b, p, i, *_: (b * nq + i, 0)),
                  kv_spec, pl.BlockSpec((seq, LANES), lambda b, p, i, *_: (b, 0)),
                  pl.BlockSpec((LANES, seq), lambda b, p, i, *_: (p, b))],
        out_specs=pl.BlockSpec((LANES, tq), lambda b, p, i, *_: (p, b * nq + i)),
        scratch_shapes=[
            pltpu.VMEM((seq // tk, 2, FOX_VROWS, tk), BF16),
            pltpu.VMEM((2, tq, 2 * LANES), BF16),
            pltpu.VMEM((2, FOX_VROWS, tq), F32),
            pltpu.VMEM((2, 1, tq), F32),
            pltpu.VMEM((2, tk, tq), BF16),
            pltpu.VMEM((2, tk, tq), BF16),
        ],
    )
    return pl.pallas_call(
        functools.partial(_fox_attn_kernel, tq=tq, tk=tk, heads=d // FOX_HEAD),
        grid_spec=grid_spec,
        out_shape=jax.ShapeDtypeStruct((d, t), BF16),
        compiler_params=_cparams(("arbitrary", "arbitrary", "arbitrary")),
        name="fox_attn",
    )(flag, f_first, f_last, q, aq, k, ak, v)


def _post_kernel(x_ref, y_ref, gate_ref, mod_ref, nw_ref, fw_ref, wo_ref, w1_ref, w2_ref, o_ref, *,
                 d, ff_blk, final, y_features):
    mod = mod_ref[0]
    g1 = mod[:, 2 * d:3 * d]
    sh2, sc2, g2 = mod[:, 3 * d:4 * d], mod[:, 4 * d:5 * d], mod[:, 5 * d:6 * d]
    y = y_ref[...].astype(F32)
    if y_features:
        y = y.T
    yg = (y * gate_ref[...].astype(F32)).astype(BF16)
    x1 = x_ref[...] + g1 * jnp.dot(yg, wo_ref[...], preferred_element_type=F32)
    h = _modulate(x1, nw_ref[...], sh2, sc2).astype(BF16)
    acc = jnp.zeros_like(x1)
    for j in range(w1_ref.shape[2] // ff_blk):
        a = jnp.maximum(jnp.dot(h, w1_ref[0, :, j * ff_blk:(j + 1) * ff_blk], preferred_element_type=F32), 0.0)
        acc = acc + jnp.dot((a * a).astype(BF16), w2_ref[0, j * ff_blk:(j + 1) * ff_blk, :],
                            preferred_element_type=F32)
    x2 = x1 + g2 * acc
    if final:
        x2 = x2 * lax.rsqrt(jnp.mean(x2 * x2, axis=-1, keepdims=True) + EPS) * fw_ref[...]
    o_ref[...] = x2


def _post_call(x2, y, gate, mod3, nw, fw, w_out, w1, w2, *, layer, batch, tm, final, y_features):
    t, d = x2.shape
    tiles_per_batch = (t // batch) // tm
    row = pl.BlockSpec((tm, d), lambda i: (i, 0))
    assert y.shape == ((d, t) if y_features else (t, d))
    return pl.pallas_call(
        functools.partial(_post_kernel, d=d, ff_blk=POST_FF_BLK, final=final, y_features=y_features),
        grid=(t // tm,),
        in_specs=[
            row,
            pl.BlockSpec((d, tm), lambda i: (0, i)) if y_features else row,
            row,
            pl.BlockSpec((1, 1, N_MOD * d), lambda i: (layer * batch + i // tiles_per_batch, 0, 0)),
            _resident((1, d)), _resident((1, d)),
            _resident(w_out.shape),
            pl.BlockSpec((1,) + w1.shape[1:], lambda i: (layer, 0, 0), pipeline_mode=pl.Buffered(1)),
            pl.BlockSpec((1,) + w2.shape[1:], lambda i: (layer, 0, 0), pipeline_mode=pl.Buffered(1)),
        ],
        out_specs=row,
        out_shape=jax.ShapeDtypeStruct((t, d), F32),
        compiler_params=_cparams(("arbitrary",)),
        name="post",
    )(x2, y, gate, mod3, nw, fw, w_out, w1, w2)


def kernel(x, c, w_mod, b_mod, norm1_w, norm2_w, hg_w_in, hg_w_out, hg_lb, hg_gn_w, fox_w_in, fox_b_f,
           fox_qn_w, fox_kn_w, fox_w_out, mlp_w1, mlp_w2, final_w):
    batch, seq, d = x.shape
    depth = w_mod.shape[0]
    t = batch * seq
    fox_heads = d // FOX_HEAD
    assert seq % HG_BLOCK == 0 and seq % ATTN_TQ == 0 and HG_BLOCK % PROJ_TM == 0 and PROJ_TM % ROW_SLAB == 0
    assert d % LANES == 0 and fox_heads * FOX_AUG <= LANES and (N_MOD * d) % MOD_TN == 0
    assert mlp_w1.shape[2] % POST_FF_BLK == 0 and x.dtype == F32
    assert PROJ_TM == ATTN_TK

    mod3 = _mod_call(c, w_mod, b_mod).reshape(depth * batch, 1, N_MOD * d)
    xs = x.reshape(t, d)
    fw = final_w.reshape(1, d)
    ff = mlp_w1.shape[2]

    for i in range(depth):
        j = i // 2
        n1 = norm1_w[i].reshape(1, d)
        if i % 2 == 0:
            q, k, lf, v, g, wmin = _hg_proj_call(xs, mod3, n1, hg_lb, hg_w_in[j],
                                                 layer=i, batch=batch, tm=PROJ_TM)
            wmin = wmin.reshape(batch, seq // HG_BLOCK, HG_BLOCK // PROJ_TM, d // HG_HEAD, HG_HEAD).min(axis=(2, 4))
            y, w1_all, w2_all = _hg_rec_call(wmin.transpose(0, 2, 1).reshape(-1), q, k, lf, v,
                                             hg_gn_w[j].reshape(1, HG_HEAD),
                                             mlp_w1.reshape(depth * d, ff), mlp_w2.reshape(depth * ff, d),
                                             batch=batch, tc=HG_BLOCK,
                                             unroll_bounded=HG_GROUP, unroll_exact=HG_GROUP_EXACT)
            w1_all, w2_all = w1_all.reshape(depth, d, ff), w2_all.reshape(depth, ff, d)
            w_out = hg_w_out[j]
        else:
            pad = ((0, 0), (0, LANES - fox_heads))
            q, k, v, g, aq, ak, edge, flag = _fox_proj_call(
                xs, mod3, n1,
                jnp.tile(fox_qn_w[j], fox_heads).reshape(1, d),
                jnp.tile(fox_kn_w[j], fox_heads).reshape(1, d),
                jnp.pad(fox_b_f[j].reshape(1, fox_heads), pad),
                jnp.swapaxes(fox_w_in, 1, 2), jnp.pad(fox_w_in[j, :, 4 * d:], pad).astype(BF16),
                fox_qn_w[j].reshape(1, FOX_HEAD), fox_kn_w[j].reshape(1, FOX_HEAD),
                layer=i, w_index=j, batch=batch, tm=PROJ_TM)
            edge = edge.reshape(batch, seq // ATTN_TK, 2, LANES)[..., :fox_heads].transpose(2, 0, 3, 1)
            y = _fox_attn_call(flag[0, :1], edge[0].reshape(-1), edge[1].reshape(-1), q, aq, k, ak, v,
                               batch=batch, seq=seq, tq=ATTN_TQ, tk=ATTN_TK)
            w_out = fox_w_out[j]
        xs = _post_call(xs, y, g, mod3, norm2_w[i].reshape(1, d), fw, w_out.astype(BF16), w1_all, w2_all,
                        layer=i, batch=batch, tm=PROJ_TM, final=(i == depth - 1), y_features=(i % 2 == 1))
    return xs.reshape(batch, seq, d)
```

```python
import functools

import numpy as np
import jax
import jax.numpy as jnp
from jax import lax
from jax.experimental import pallas as pl
from jax.experimental.pallas import tpu as pltpu

F32 = jnp.float32
BF16 = jnp.bfloat16
EPS = 1e-6
N_MOD = 6
HG_HEAD = 128
HG_CHUNK = 64
HG_SUB = 16
HG_SAFE_DECAY = 56.0
FOX_HEAD = 64
LANES = 128
SUBLANES = 8
LOG2E = float(np.log2(np.e))
FOX_STAB_MAX = 30.0
FOX_BOUND_SLACK = 1.01
FOX_SKIP = 152.0
ROW_SLAB = 256
VMEM_LIMIT = 56 * 1024 * 1024

PROJ_TM = 512
POST_FF_BLK = 1024
MOD_TN = 1024
HG_BLOCK = 2048
HG_GROUP = 32
HG_GROUP_EXACT = 4
ATTN_TK = 512
ATTN_TQ = 2 * ATTN_TK

_HI = lax.Precision.HIGHEST


def _cparams(sem):
    return pltpu.CompilerParams(dimension_semantics=sem, vmem_limit_bytes=VMEM_LIMIT)


def _resident(shape):
    nd = len(shape)
    return pl.BlockSpec(shape, lambda *_: (0,) * nd, pipeline_mode=pl.Buffered(1))


def _sigmoid(x):
    return 0.5 * jnp.tanh(0.5 * x) + 0.5


def _cast_weight_once(w_ref, wb_ref, cols, transposed=False):
    @pl.when(pl.program_id(0) == 0)
    def _():
        for c0 in range(0, wb_ref.shape[1], cols):
            src = w_ref[c0:c0 + cols, :].T if transposed else w_ref[:, c0:c0 + cols]
            wb_ref[:, c0:c0 + cols] = src.astype(BF16)


def _modulate(x, nw, shift, scale):
    ms = jnp.mean(x * x, axis=-1, keepdims=True)
    y = x * lax.rsqrt(ms + EPS)
    return (y * nw) * (1.0 + scale) + shift


def _mod_kernel(ct_ref, w_ref, b_ref, o_ref):
    ct = ct_ref[...]
    cat = ct * _sigmoid(ct)
    w = w_ref[0]
    rows = []
    for b in range(ct.shape[1]):
        col = jnp.broadcast_to(cat[:, b:b + 1], (ct.shape[0], LANES))
        rows.append(jnp.concatenate(
            [jnp.sum(w[:, j * LANES:(j + 1) * LANES] * col, axis=0, keepdims=True)
             for j in range(w.shape[1] // LANES)], axis=1))
    o_ref[0] = jnp.concatenate(rows, axis=0) + b_ref[0]


def _mod_call(c, w_mod, b_mod):
    depth, d, n = w_mod.shape
    b = c.shape[0]
    tn = MOD_TN
    return pl.pallas_call(
        _mod_kernel,
        grid=(depth, n // tn),
        in_specs=[
            pl.BlockSpec((d, b), lambda l, j: (0, 0)),
            pl.BlockSpec((1, d, tn), lambda l, j: (l, 0, j)),
            pl.BlockSpec((1, 1, tn), lambda l, j: (l, 0, j)),
        ],
        out_specs=pl.BlockSpec((1, b, tn), lambda l, j: (l, 0, j)),
        out_shape=jax.ShapeDtypeStruct((depth, b, n), F32),
        compiler_params=_cparams(("arbitrary", "arbitrary")),
        name="mod",
    )(c.T, w_mod, b_mod.reshape(depth, 1, n))


def _hg_proj_kernel(x_ref, mod_ref, nw_ref, lb_ref, wf32_ref, q_ref, k_ref, lf_ref, v_ref, g_ref, wmin_ref,
                    w_ref, *, layer, d):
    _cast_weight_once(wf32_ref, w_ref, d)
    mod = mod_ref[0]
    lbp = lb_ref[...]
    e = jnp.exp(lbp - jnp.max(lbp, axis=0, keepdims=True))
    lb = jnp.sum(e[0:layer + 1], axis=0, keepdims=True) / jnp.sum(e, axis=0, keepdims=True)

    w_r = lax.broadcasted_iota(jnp.int32, (ROW_SLAB // HG_SUB, ROW_SLAB), 0)
    w_c = lax.broadcasted_iota(jnp.int32, (ROW_SLAB // HG_SUB, ROW_SLAB), 1)
    win = (lax.shift_right_logical(w_c, HG_SUB.bit_length() - 1) == w_r).astype(BF16)
    lfs = []

    for r0 in range(0, x_ref.shape[0], ROW_SLAB):
        rows = slice(r0, r0 + ROW_SLAB)
        h = _modulate(x_ref[rows, :], nw_ref[...], mod[:, 0:d], mod[:, d:2 * d]).astype(BF16)

        pq = jnp.dot(h, w_ref[:, 0:d], preferred_element_type=F32)
        q_ref[rows, :] = (pq * _sigmoid(pq)).astype(BF16)

        z = jnp.dot(h, w_ref[:, d:2 * d], preferred_element_type=F32)
        th = 0.5 * jnp.tanh(0.5 * z)
        kk = (1.0 - lb) * (0.5 - th)
        f = lb + (1.0 - lb) * (0.5 + th)
        lf = jnp.log(jnp.where(kk < 0.5, 1.0 - kk, f))
        lf_ref[rows, :] = lf
        k_ref[rows, :] = kk.astype(BF16)
        lfs.append(lf.astype(BF16))

        v_ref[rows, :] = jnp.dot(h, w_ref[:, 2 * d:3 * d], preferred_element_type=F32).astype(BF16)
        pg = jnp.dot(h, w_ref[:, 3 * d:4 * d], preferred_element_type=F32)
        g_ref[rows, :] = (pg * _sigmoid(pg)).astype(BF16)

    wmin = None
    for lf16 in lfs:
        wsum = jnp.min(jnp.dot(win, lf16, preferred_element_type=F32), axis=0, keepdims=True)
        wmin = wsum if wmin is None else jnp.minimum(wmin, wsum)
    wmin_ref[0] = wmin


def _hg_proj_call(x2, mod3, nw, hg_lb, w_in, *, layer, batch, tm):
    t, d = x2.shape
    tiles_per_batch = (t // batch) // tm
    row = pl.BlockSpec((tm, d), lambda i: (i, 0))
    out_bf = jax.ShapeDtypeStruct((t, d), BF16)
    return pl.pallas_call(
        functools.partial(_hg_proj_kernel, layer=layer, d=d),
        grid=(t // tm,),
        in_specs=[
            row,
            pl.BlockSpec((1, 1, N_MOD * d), lambda i: (layer * batch + i // tiles_per_batch, 0, 0)),
            _resident((1, d)),
            _resident(hg_lb.shape),
            _resident(w_in.shape),
        ],
        out_specs=[row, row, row, row, row, pl.BlockSpec((1, 1, d), lambda i: (i, 0, 0))],
        out_shape=[out_bf, out_bf, jax.ShapeDtypeStruct((t, d), F32), out_bf, out_bf,
                   jax.ShapeDtypeStruct((t // tm, 1, d), F32)],
        scratch_shapes=[pltpu.VMEM(w_in.shape, BF16)],
        compiler_params=_cparams(("arbitrary",)),
        name="hg_proj",
    )(x2, mod3, nw, hg_lb, w_in)


def _hg_rec_kernel(wmin_ref, q_ref, k_ref, lf_ref, v_ref, gnw_ref, ca_ref, cb_ref, o_ref, cao_ref, cbo_ref,
                   st_ref, gs_ref, ks_ref, *, n_chunks, unroll_bounded, unroll_exact):
    cao_ref[...] = ca_ref[...].astype(BF16)
    cbo_ref[...] = cb_ref[...].astype(BF16)
    c, sub = HG_CHUNK, HG_SUB
    n_sub = c // sub
    half = sub // 2

    @pl.when(pl.program_id(2) == 0)
    def _():
        st_ref[...] = jnp.zeros_like(st_ref)

    r_i = lax.broadcasted_iota(jnp.int32, (c, c), 0)
    c_i = lax.broadcasted_iota(jnp.int32, (c, c), 1)
    tri = (c_i <= r_i).astype(BF16)
    ones_w = jnp.ones((HG_HEAD, c), BF16)
    half_row = lax.broadcasted_iota(jnp.int32, (half, HG_HEAD), 0)
    a_lane = lax.broadcasted_iota(jnp.int32, (half, c), 1)
    gnw = gnw_ref[...]
    nt = (((1,), (1,)), ((), ()))
    tn = (((0,), (0,)), ((), ()))

    def offdiag_operands(q, k, gcum, j, bounded):
        lo = j * sub
        hi = lo + sub if bounded else lo
        g_b = gcum[lo - 1:lo, :] if j > 0 else jnp.zeros((1, HG_HEAD), F32)
        qh = (q[lo:lo + sub, :] * jnp.exp2(gcum[lo:lo + sub, :] - g_b)).astype(BF16)
        kh = (k[0:hi, :] * jnp.exp2(g_b - gcum[0:hi, :])).astype(BF16)
        if hi < c:
            kh = jnp.concatenate([kh, jnp.zeros((c - hi, HG_HEAD), BF16)], axis=0)
        return qh, kh

    def diag_products(q, k, gcum, slot, j):
        lo = j * sub
        gt = (gcum[lo:lo + half, :], gcum[lo + half:lo + sub, :])
        qt = (q[lo:lo + half, :], q[lo + half:lo + sub, :])
        ps = []
        for s in range(sub):
            g_s = gs_ref[slot, pl.ds(lo + s, 1), :]
            k_s = ks_ref[slot, pl.ds(lo + s, 1), :]
            for hf in range(2):
                if s >= half and hf == 0:
                    continue
                dlt = gt[hf] - g_s
                if (s >= half) == (hf == 1):
                    dlt = jnp.where(half_row >= s % half, dlt, -jnp.inf)
                ps.append((qt[hf] * k_s) * jnp.exp2(dlt))
        return jnp.concatenate(ps, axis=0).astype(BF16)

    def diag_scatter(r, j):
        lo = j * sub
        a_top = jnp.zeros((half, c), F32)
        a_bot = jnp.zeros((half, c), F32)
        for s in range(sub):
            if s < half:
                a_top = jnp.where(a_lane == lo + s, r[2 * s * half:(2 * s + 1) * half, :], a_top)
                a_bot = jnp.where(a_lane == lo + s, r[(2 * s + 1) * half:(2 * s + 2) * half, :], a_bot)
            else:
                a_bot = jnp.where(a_lane == lo + s, r[(half + s) * half:(half + s + 1) * half, :], a_bot)
        return jnp.concatenate([a_top, a_bot], axis=0)

    def run(bounded, unroll):
        def group(i, st):
            us = range(unroll)
            rows = [pl.ds(pl.multiple_of((i * unroll + u) * c, c), c) for u in us]
            g3 = [jnp.dot(tri, _split3(lf_ref[r, :]), preferred_element_type=F32) for r in rows]
            q = [q_ref[r, :].astype(F32) for r in rows]
            k = [k_ref[r, :].astype(F32) for r in rows]
            gcum = [(g[:, 0:HG_HEAD] + g[:, HG_HEAD:2 * HG_HEAD] + g[:, 2 * HG_HEAD:]) * LOG2E for g in g3]
            g_last = [g[c - 1:c, :] for g in gcum]
            qg = [(q[u] * jnp.exp2(gcum[u])).astype(BF16) for u in us]
            kd = [(k[u] * jnp.exp2(g_last[u] - gcum[u])).astype(BF16) for u in us]
            first = 0 if bounded else 1
            off = [[offdiag_operands(q[u], k[u], gcum[u], j, bounded) for j in range(first, n_sub)] for u in us]
            if not bounded:
                for u in us:
                    gs_ref[u] = gcum[u]
                    ks_ref[u] = k[u]
                pst = [[diag_products(q[u], k[u], gcum[u], u, j) for j in range(n_sub)] for u in us]
            kv = [lax.dot_general(v_ref[rows[u], :], kd[u], tn, preferred_element_type=F32) for u in us]
            blocks = [[lax.dot_general(qh, kh, nt, preferred_element_type=F32) for qh, kh in off[u]] for u in us]
            if bounded:
                a = [jnp.where(c_i <= r_i, jnp.concatenate(blocks[u], axis=0), 0.0) for u in us]
            else:
                sums = [[jnp.dot(p, ones_w, preferred_element_type=F32) for p in pst[u]] for u in us]
                a = []
                for u in us:
                    a_rows = [diag_scatter(sums[u][j], j) for j in range(n_sub)]
                    for j in range(1, n_sub):
                        a_rows[j] = a_rows[j] + blocks[u][j - 1]
                    a.append(jnp.concatenate(a_rows, axis=0))
            o_intra = [jnp.dot(a[u].astype(BF16), v_ref[rows[u], :], preferred_element_type=F32) for u in us]
            for u in us:
                o = o_intra[u] + lax.dot_general(qg[u], st.astype(BF16), nt, preferred_element_type=F32)
                st = st * jnp.exp2(g_last[u]) + kv[u]
                on = o * lax.rsqrt(jnp.mean(o * o, axis=-1, keepdims=True) + EPS) * gnw
                o_ref[rows[u], :] = on.astype(BF16)
            return st

        st_ref[...] = lax.fori_loop(0, n_chunks // unroll, group, st_ref[...])

    blk_id = (pl.program_id(0) * pl.num_programs(1) + pl.program_id(1)) * pl.num_programs(2) + pl.program_id(2)
    bounded = wmin_ref[blk_id] * LOG2E >= -HG_SAFE_DECAY

    @pl.when(bounded)
    def _():
        run(True, unroll_bounded)

    @pl.when(jnp.logical_not(bounded))
    def _():
        run(False, unroll_exact)


def _hg_rec_call(wmin, q, k, lf, v, gnw, cast_a, cast_b, *, batch, tc, unroll_bounded, unroll_exact):
    t, d = q.shape
    heads = d // HG_HEAD
    steps = (t // batch) // tc
    n_steps = batch * heads * steps
    unroll = unroll_exact
    blk = pl.BlockSpec((tc, HG_HEAD), lambda b, h, s, *_: (b * steps + s, h))

    def cast_spec(a):
        assert a.shape[0] % (n_steps * 2 * SUBLANES) == 0
        return pl.BlockSpec((a.shape[0] // n_steps, a.shape[1]), lambda b, h, s, *_: ((b * heads + h) * steps + s, 0))

    grid_spec = pltpu.PrefetchScalarGridSpec(
        num_scalar_prefetch=1,
        grid=(batch, heads, steps),
        in_specs=[blk, blk, blk, blk, pl.BlockSpec((1, HG_HEAD), lambda b, h, s, *_: (0, 0)),
                  cast_spec(cast_a), cast_spec(cast_b)],
        out_specs=[blk, cast_spec(cast_a), cast_spec(cast_b)],
        scratch_shapes=[
            pltpu.VMEM((HG_HEAD, HG_HEAD), F32),
            pltpu.VMEM((unroll, HG_CHUNK, HG_HEAD), F32),
            pltpu.VMEM((unroll, HG_CHUNK, HG_HEAD), F32),
        ],
    )
    return pl.pallas_call(
        functools.partial(_hg_rec_kernel, n_chunks=tc // HG_CHUNK, unroll_bounded=unroll_bounded,
                          unroll_exact=unroll_exact),
        grid_spec=grid_spec,
        out_shape=[jax.ShapeDtypeStruct((t, d), BF16), jax.ShapeDtypeStruct(cast_a.shape, BF16),
                   jax.ShapeDtypeStruct(cast_b.shape, BF16)],
        compiler_params=_cparams(("arbitrary", "arbitrary", "arbitrary")),
        name="hg_rec",
    )(wmin, q, k, lf, v, gnw, cast_a, cast_b)


def _head_rmsnorm(p, w_row):
    low = lax.broadcasted_iota(jnp.int32, (1, LANES), 1) < FOX_HEAD
    outs = []
    for j in range(p.shape[1] // LANES):
        pj = p[:, j * LANES:(j + 1) * LANES]
        ss = pj * pj
        s_lo = jnp.sum(jnp.where(low, ss, 0.0), axis=-1, keepdims=True)
        s_hi = jnp.sum(jnp.where(low, 0.0, ss), axis=-1, keepdims=True)
        inv = lax.rsqrt(jnp.where(low, s_lo, s_hi) * (1.0 / FOX_HEAD) + EPS)
        outs.append(pj * inv)
    return jnp.concatenate(outs, axis=1) * w_row


def _fox_proj_kernel(x_ref, mod_ref, nw_ref, qw_ref, kw_ref, bf_ref, wf32_ref, wf_ref,
                     hqw_ref, hkw_ref, pq_ref, pk_ref, oq_ref, ok_ref,
                     q_ref, k_ref, v_ref, g_ref, aq_ref, ak_ref, edge_ref, flag_ref,
                     vs_ref, w_ref, carry_ref, *, d, tiles_per_batch):
    _cast_weight_once(wf32_ref.at[0], w_ref, d, transposed=True)

    @pl.when(pl.program_id(0) % tiles_per_batch == 0)
    def _():
        carry_ref[...] = jnp.zeros_like(carry_ref)

    mod = mod_ref[0]
    scale = LOG2E / np.sqrt(FOX_HEAD)
    lfs = []
    for r0 in range(0, x_ref.shape[0], ROW_SLAB):
        rows = slice(r0, r0 + ROW_SLAB)
        h = _modulate(x_ref[rows, :], nw_ref[...], mod[:, 0:d], mod[:, d:2 * d]).astype(BF16)
        pq = jnp.dot(h, w_ref[:, 0:d], preferred_element_type=F32)
        q_ref[rows, :] = (_head_rmsnorm(pq, qw_ref[...]) * scale).astype(BF16)
        pk = jnp.dot(h, w_ref[:, d:2 * d], preferred_element_type=F32)
        k_ref[rows, :] = _head_rmsnorm(pk, kw_ref[...]).astype(BF16)
        vs_ref[...] = jnp.dot(h, w_ref[:, 2 * d:3 * d], preferred_element_type=F32)
        v_ref[:, rows] = vs_ref[...].T.astype(BF16)
        pg = jnp.dot(h, w_ref[:, 3 * d:4 * d], preferred_element_type=F32)
        g_ref[rows, :] = _sigmoid(pg).astype(BF16)
        u = jnp.dot(h, wf_ref[...], preferred_element_type=F32) + bf_ref[...]
        lfs.append(jnp.minimum(u, 0.0) - jnp.log1p(jnp.exp(-jnp.abs(u))))

    tm = x_ref.shape[0]
    bound = FOX_BOUND_SLACK * LOG2E * np.sqrt(FOX_HEAD) * jnp.max(jnp.abs(hqw_ref[...] * hkw_ref[...]), axis=-1, keepdims=True)
    fast = bound <= FOX_STAB_MAX
    flag_ref[...] = jnp.broadcast_to(fast.astype(jnp.int32), flag_ref.shape)
    stab = jnp.where(fast, bound, 0.0)
    r_i = lax.broadcasted_iota(jnp.int32, (tm, tm), 0)
    c_i = lax.broadcasted_iota(jnp.int32, (tm, tm), 1)
    tri = (c_i <= r_i).astype(BF16)
    f3 = jnp.dot(tri, _split3(jnp.concatenate(lfs, axis=0)), preferred_element_type=F32)
    f = f3[:, 0:LANES] + f3[:, LANES:2 * LANES] + f3[:, 2 * LANES:] + carry_ref[...]
    carry_ref[...] = f[tm - 1:tm, :]
    f2 = f * LOG2E
    edge_ref[0] = jnp.concatenate([f2[0:1, :], f2[tm - 1:tm, :]], axis=0)
    aq_ref[...] = (jnp.dot(_split3(f2 - stab), pq_ref[...], preferred_element_type=F32) + oq_ref[...]).astype(BF16)
    ak_ref[...] = (jnp.dot(_split3(f2), pk_ref[...], preferred_element_type=F32) + ok_ref[...]).astype(BF16)


def _fox_proj_call(x2, mod3, nw, qw, kw, bf, w_main, w_f, hqw, hkw, *, layer, w_index, batch, tm):
    t, d = x2.shape
    nh = w_f.shape[1]
    heads = d // FOX_HEAD
    tiles_per_batch = (t // batch) // tm
    pq = np.zeros((3 * LANES, LANES), np.float32)
    pk = np.zeros((3 * LANES, LANES), np.float32)
    oq = np.zeros((1, LANES), np.float32)
    ok = np.zeros((1, LANES), np.float32)
    for h in range(heads):
        base = FOX_AUG * h
        for i in range(3):
            pq[i * LANES + h, base + i] = 1.0
            ok[0, base + i] = 1.0
            oq[0, base + 3 + i] = 1.0
            pk[i * LANES + h, base + 3 + i] = -1.0
    row = pl.BlockSpec((tm, d), lambda i: (i, 0))
    lane_row = pl.BlockSpec((tm, LANES), lambda i: (i, 0))
    out_bf = jax.ShapeDtypeStruct((t, d), BF16)
    lane_bf = jax.ShapeDtypeStruct((t, LANES), BF16)
    return pl.pallas_call(
        functools.partial(_fox_proj_kernel, d=d, tiles_per_batch=tiles_per_batch),
        grid=(t // tm,),
        in_specs=[
            row,
            pl.BlockSpec((1, 1, N_MOD * d), lambda i: (layer * batch + i // tiles_per_batch, 0, 0)),
            _resident((1, d)), _resident((1, d)), _resident((1, d)), _resident((1, nh)),
            pl.BlockSpec((1, 4 * d, d), lambda i: (w_index, 0, 0), pipeline_mode=pl.Buffered(1)),
            _resident(w_f.shape),
            _resident((1, FOX_HEAD)), _resident((1, FOX_HEAD)),
            _resident(pq.shape), _resident(pk.shape), _resident(oq.shape), _resident(ok.shape),
        ],
        out_specs=[row, row, pl.BlockSpec((d, tm), lambda i: (0, i)), row, lane_row, lane_row,
                   pl.BlockSpec((1, 2, LANES), lambda i: (i, 0, 0)),
                   pl.BlockSpec((SUBLANES, LANES), lambda i: (0, 0))],
        out_shape=[out_bf, out_bf, jax.ShapeDtypeStruct((d, t), BF16), out_bf, lane_bf, lane_bf,
                   jax.ShapeDtypeStruct((t // tm, 2, LANES), F32),
                   jax.ShapeDtypeStruct((SUBLANES, LANES), jnp.int32)],
        scratch_shapes=[pltpu.VMEM((ROW_SLAB, d), F32),
                        pltpu.VMEM((d, 4 * d), BF16),
                        pltpu.VMEM((1, LANES), F32)],
        compiler_params=_cparams(("arbitrary",)),
        name="fox_proj",
    )(x2, mod3, nw, qw, kw, bf, w_main, w_f, hqw, hkw,
      jnp.asarray(pq, BF16), jnp.asarray(pk, BF16), jnp.asarray(oq), jnp.asarray(ok))


FOX_AUG = 6


def _split3(x):
    hi = x.astype(BF16)
    r1 = x - hi.astype(F32)
    mid = r1.astype(BF16)
    lo = (r1 - mid.astype(F32)).astype(BF16)
    return jnp.concatenate([hi, mid, lo], axis=1)


FOX_VROWS = 80


def _fox_attn_kernel(flag_ref, fs_ref, fe_ref, q_ref, aq_ref, k_ref, ak_ref, v_ref, o_ref,
                     v1_ref, qm_ref, acc_ref, m_ref, pa_ref, pb_ref, *, tq, tk, heads):
    bi, pair, qi = pl.program_id(0), pl.program_id(1), pl.program_id(2)
    nk = k_ref.shape[0] // tk
    sub = tq // tk
    pad_rows = lax.broadcasted_iota(jnp.int32, (FOX_VROWS - FOX_HEAD, tk), 0)
    ones_blk = jnp.where(pad_rows == 0, 1.0, 0.0).astype(BF16)
    lane2 = lax.broadcasted_iota(jnp.int32, (1, 2 * LANES), 1)
    bias0 = LANES + 2 * FOX_AUG * pair
    head0 = (lane2 < FOX_HEAD) | ((lane2 >= bias0) & (lane2 < bias0 + FOX_AUG))
    head1 = ((lane2 >= FOX_HEAD) & (lane2 < LANES)) | ((lane2 >= bias0 + FOX_AUG) & (lane2 < bias0 + 2 * FOX_AUG))
    nt = (((1,), (1,)), ((), ()))

    @pl.when(qi == 0)
    def _():
        def build(j, carry):
            cols = pl.ds(pl.multiple_of(j * tk, tk), tk)
            for e in range(2):
                v1_ref[j, e] = jnp.concatenate([v_ref[e * FOX_HEAD:(e + 1) * FOX_HEAD, cols], ones_blk], axis=0)
            return carry

        lax.fori_loop(0, nk, build, 0)

    def probs(j, online, diag=None, dst=None):
        rows = pl.ds(pl.multiple_of(j * tk, tk), tk)
        kf = jnp.concatenate([k_ref[rows, :], ak_ref[rows, :]], axis=1)
        q_lo = 0 if diag is None else diag * tk
        ps, alphas = [], []
        for e in range(2):
            s = lax.dot_general(kf, qm_ref[e, q_lo:tq, :], nt, preferred_element_type=F32)
            if diag is not None:
                kpos = lax.broadcasted_iota(jnp.int32, (tk, tk), 0)
                qpos = lax.broadcasted_iota(jnp.int32, (tk, tk), 1)
                blk = jnp.where(kpos <= qpos, s[:, 0:tk], -jnp.inf)
                s = blk if q_lo + tk == tq else jnp.concatenate([blk, s[:, tk:]], axis=1)
            if online:
                m_prev = m_ref[e, :, q_lo:tq]
                m_new = jnp.maximum(m_prev, jnp.max(s, axis=0, keepdims=True))
                alphas.append(jnp.exp2(m_prev - m_new))
                m_ref[e, :, q_lo:tq] = m_new
                s = s - m_new
            p = jnp.exp2(s).astype(BF16)
            if dst is None:
                ps.append(p)
            else:
                dst[e] = p
        return ps, alphas

    def accumulate(ps, j, alphas=None, diag=None):
        q_lo = 0 if diag is None else diag * tk
        for e in range(2):
            pv = jnp.dot(v1_ref[j, e], ps[e], preferred_element_type=F32)
            if alphas:
                acc_ref[e, :, q_lo:tq] = acc_ref[e, :, q_lo:tq] * alphas[e] + pv
            else:
                acc_ref[e, :, q_lo:tq] += pv

    def start():
        qf = jnp.concatenate([q_ref[...], aq_ref[...]], axis=1)
        zero = jnp.zeros_like(qf)
        qm_ref[0] = jnp.where(head0, qf, zero)
        qm_ref[1] = jnp.where(head1, qf, zero)
        acc_ref[...] = jnp.zeros_like(acc_ref)

    def finish():
        outs = []
        for e in range(2):
            acc = acc_ref[e]
            outs.append(acc[0:FOX_HEAD, :] / acc[FOX_HEAD:FOX_HEAD + 1, :])
        o_ref[...] = jnp.concatenate(outs, axis=0).astype(BF16)

    n_full = qi * sub

    @pl.when(flag_ref[0] == 1)
    def _():
        start()
        held = None
        for a in range(sub - 1, 0, -1):
            nxt = probs(n_full + a, False, diag=a)[0]
            if held is not None:
                accumulate(held[0], n_full + held[1], diag=held[1])
            held = (nxt, a)
        probs(n_full, False, diag=0, dst=pa_ref)
        if held is not None:
            accumulate(held[0], n_full + held[1], diag=held[1])

        h0 = bi * heads + 2 * pair
        fq0 = fs_ref[h0 * nk + n_full]
        fq1 = fs_ref[(h0 + 1) * nk + n_full]

        def dead(j, cnt):
            gone0 = fq0 - fe_ref[h0 * nk + j] < -FOX_SKIP
            gone1 = fq1 - fe_ref[(h0 + 1) * nk + j] < -FOX_SKIP
            return cnt + jnp.logical_and(gone0, gone1).astype(jnp.int32)

        first = lax.fori_loop(0, n_full, dead, 0)
        n_live = n_full - first

        def two_tiles(i, carry):
            j = first + 2 * i
            probs(j, False, dst=pb_ref)
            accumulate(pa_ref, jnp.where(i == 0, n_full, j - 1))
            probs(j + 1, False, dst=pa_ref)
            accumulate(pb_ref, j)
            return carry

        lax.fori_loop(0, n_live // 2, two_tiles, 0)

        @pl.when(n_live % 2 == 1)
        def _():
            ps, _ = probs(n_full - 1, False)
            accumulate(pa_ref, jnp.where(n_live == 1, n_full, n_full - 2))
            accumulate(ps, n_full - 1)

        @pl.when(n_live % 2 == 0)
        def _():
            accumulate(pa_ref, jnp.where(n_live == 0, n_full, n_full - 1))

        finish()

    @pl.when(flag_ref[0] != 1)
    def _():
        start()
        m_ref[...] = jnp.full_like(m_ref, -jnp.inf)

        def body(j, carry):
            ps, alphas = probs(j, True)
            accumulate(ps, j, alphas)
            return carry

        lax.fori_loop(0, n_full, body, 0)
        for a in range(sub):
            ps, alphas = probs(n_full + a, True, diag=a)
            accumulate(ps, n_full + a, alphas, diag=a)
        finish()


def _fox_attn_call(flag, f_first, f_last, q, aq, k, ak, v, *, batch, seq, tq, tk):
    t, d = q.shape
    pairs = d // LANES
    nq = seq // tq
    q_spec = pl.BlockSpec((tq, LANES), lambda b, p, i, *_: (b * nq + i, p))
    kv_spec = pl.BlockSpec((seq, LANES), lambda b, p, i, *_: (b, p))
    grid_spec = pltpu.PrefetchScalarGridSpec(
        num_scalar_prefetch=3,
        grid=(batch, pairs, nq),
        in_specs=[q_spec, pl.BlockSpec((tq, LANES), lambda b, p, i, *_: (b * nq + i, 0)),
                  kv_spec, pl.BlockSpec((seq, LANES), lambda b, p, i, *_: (b, 0)),
                  pl.BlockSpec((LANES, seq), lambda b, p, i, *_: (p, b))],
        out_specs=pl.BlockSpec((LANES, tq), lambda b, p, i, *_: (p, b * nq + i)),
        scratch_shapes=[
            pltpu.VMEM((seq // tk, 2, FOX_VROWS, tk), BF16),
            pltpu.VMEM((2, tq, 2 * LANES), BF16),
            pltpu.VMEM((2, FOX_VROWS, tq), F32),
            pltpu.VMEM((2, 1, tq), F32),
            pltpu.VMEM((2, tk, tq), BF16),
            pltpu.VMEM((2, tk, tq), BF16),
        ],
    )
    return pl.pallas_call(
        functools.partial(_fox_attn_kernel, tq=tq, tk=tk, heads=d // FOX_HEAD),
        grid_spec=grid_spec,
        out_shape=jax.ShapeDtypeStruct((d, t), BF16),
        compiler_params=_cparams(("arbitrary", "arbitrary", "arbitrary")),
        name="fox_attn",
    )(flag, f_first, f_last, q, aq, k, ak, v)


def _post_kernel(x_ref, y_ref, gate_ref, mod_ref, nw_ref, fw_ref, wo_ref, w1_ref, w2_ref, o_ref, *,
                 d, ff_blk, final, y_features):
    mod = mod_ref[0]
    g1 = mod[:, 2 * d:3 * d]
    sh2, sc2, g2 = mod[:, 3 * d:4 * d], mod[:, 4 * d:5 * d], mod[:, 5 * d:6 * d]
    y = y_ref[...].astype(F32)
    if y_features:
        y = y.T
    yg = (y * gate_ref[...].astype(F32)).astype(BF16)
    x1 = x_ref[...] + g1 * jnp.dot(yg, wo_ref[...], preferred_element_type=F32)
    h = _modulate(x1, nw_ref[...], sh2, sc2).astype(BF16)
    acc = jnp.zeros_like(x1)
    for j in range(w1_ref.shape[2] // ff_blk):
        a = jnp.maximum(jnp.dot(h, w1_ref[0, :, j * ff_blk:(j + 1) * ff_blk], preferred_element_type=F32), 0.0)
        acc = acc + jnp.dot((a * a).astype(BF16), w2_ref[0, j * ff_blk:(j + 1) * ff_blk, :],
                            preferred_element_type=F32)
    x2 = x1 + g2 * acc
    if final:
        x2 = x2 * lax.rsqrt(jnp.mean(x2 * x2, axis=-1, keepdims=True) + EPS) * fw_ref[...]
    o_ref[...] = x2


def _post_call(x2, y, gate, mod3, nw, fw, w_out, w1, w2, *, layer, batch, tm, final, y_features):
    t, d = x2.shape
    tiles_per_batch = (t // batch) // tm
    row = pl.BlockSpec((tm, d), lambda i: (i, 0))
    assert y.shape == ((d, t) if y_features else (t, d))
    return pl.pallas_call(
        functools.partial(_post_kernel, d=d, ff_blk=POST_FF_BLK, final=final, y_features=y_features),
        grid=(t // tm,),
        in_specs=[
            row,
            pl.BlockSpec((d, tm), lambda i: (0, i)) if y_features else row,
            row,
            pl.BlockSpec((1, 1, N_MOD * d), lambda i: (layer * batch + i // tiles_per_batch, 0, 0)),
            _resident((1, d)), _resident((1, d)),
            _resident(w_out.shape),
            pl.BlockSpec((1,) + w1.shape[1:], lambda i: (layer, 0, 0), pipeline_mode=pl.Buffered(1)),
            pl.BlockSpec((1,) + w2.shape[1:], lambda i: (layer, 0, 0), pipeline_mode=pl.Buffered(1)),
        ],
        out_specs=row,
        out_shape=jax.ShapeDtypeStruct((t, d), F32),
        compiler_params=_cparams(("arbitrary",)),
        name="post",
    )(x2, y, gate, mod3, nw, fw, w_out, w1, w2)


def kernel(x, c, w_mod, b_mod, norm1_w, norm2_w, hg_w_in, hg_w_out, hg_lb, hg_gn_w, fox_w_in, fox_b_f,
           fox_qn_w, fox_kn_w, fox_w_out, mlp_w1, mlp_w2, final_w):
    batch, seq, d = x.shape
    depth = w_mod.shape[0]
    t = batch * seq
    fox_heads = d // FOX_HEAD
    assert seq % HG_BLOCK == 0 and seq % ATTN_TQ == 0 and HG_BLOCK % PROJ_TM == 0 and PROJ_TM % ROW_SLAB == 0
    assert d % LANES == 0 and fox_heads * FOX_AUG <= LANES and (N_MOD * d) % MOD_TN == 0
    assert mlp_w1.shape[2] % POST_FF_BLK == 0 and x.dtype == F32
    assert PROJ_TM == ATTN_TK

    mod3 = _mod_call(c, w_mod, b_mod).reshape(depth * batch, 1, N_MOD * d)
    xs = x.reshape(t, d)
    fw = final_w.reshape(1, d)
    ff = mlp_w1.shape[2]

    for i in range(depth):
        j = i // 2
        n1 = norm1_w[i].reshape(1, d)
        if i % 2 == 0:
            q, k, lf, v, g, wmin = _hg_proj_call(xs, mod3, n1, hg_lb, hg_w_in[j],
                                                 layer=i, batch=batch, tm=PROJ_TM)
            wmin = wmin.reshape(batch, seq // HG_BLOCK, HG_BLOCK // PROJ_TM, d // HG_HEAD, HG_HEAD).min(axis=(2, 4))
            y, w1_all, w2_all = _hg_rec_call(wmin.transpose(0, 2, 1).reshape(-1), q, k, lf, v,
                                             hg_gn_w[j].reshape(1, HG_HEAD),
                                             mlp_w1.reshape(depth * d, ff), mlp_w2.reshape(depth * ff, d),
                                             batch=batch, tc=HG_BLOCK,
                                             unroll_bounded=HG_GROUP, unroll_exact=HG_GROUP_EXACT)
            w1_all, w2_all = w1_all.reshape(depth, d, ff), w2_all.reshape(depth, ff, d)
            w_out = hg_w_out[j]
        else:
            pad = ((0, 0), (0, LANES - fox_heads))
            q, k, v, g, aq, ak, edge, flag = _fox_proj_call(
                xs, mod3, n1,
                jnp.tile(fox_qn_w[j], fox_heads).reshape(1, d),
                jnp.tile(fox_kn_w[j], fox_heads).reshape(1, d),
                jnp.pad(fox_b_f[j].reshape(1, fox_heads), pad),
                jnp.swapaxes(fox_w_in, 1, 2), jnp.pad(fox_w_in[j, :, 4 * d:], pad).astype(BF16),
                fox_qn_w[j].reshape(1, FOX_HEAD), fox_kn_w[j].reshape(1, FOX_HEAD),
                layer=i, w_index=j, batch=batch, tm=PROJ_TM)
            edge = edge.reshape(batch, seq // ATTN_TK, 2, LANES)[..., :fox_heads].transpose(2, 0, 3, 1)
            y = _fox_attn_call(flag[0, :1], edge[0].reshape(-1), edge[1].reshape(-1), q, aq, k, ak, v,
                               batch=batch, seq=seq, tq=ATTN_TQ, tk=ATTN_TK)
            w_out = fox_w_out[j]
        xs = _post_call(xs, y, g, mod3, norm2_w[i].reshape(1, d), fw, w_out.astype(BF16), w1_all, w2_all,
                        layer=i, batch=batch, tm=PROJ_TM, final=(i == depth - 1), y_features=(i % 2 == 1))
    return xs.reshape(batch, seq, d)
```

```python
import functools

import numpy as np
import jax
import jax.numpy as jnp
from jax import lax
from jax.experimental import pallas as pl
from jax.experimental.pallas import tpu as pltpu

F32 = jnp.float32
BF16 = jnp.bfloat16
EPS = 1e-6
N_MOD = 6
HG_HEAD = 128
HG_CHUNK = 64
HG_SUB = 16
HG_SAFE_DECAY = 56.0
FOX_HEAD = 64
LANES = 128
SUBLANES = 8
LOG2E = float(np.log2(np.e))
FOX_STAB_MAX = 30.0
FOX_BOUND_SLACK = 1.01
FOX_SKIP = 152.0
ROW_SLAB = 256
VMEM_LIMIT = 56 * 1024 * 1024

PROJ_TM = 512
POST_FF_BLK = 1024
MOD_TN = 1024
HG_BLOCK = 2048
HG_GROUP = 32
HG_GROUP_EXACT = 4
ATTN_TK = 512
ATTN_TQ = 2 * ATTN_TK


def _cparams(sem):
    return pltpu.CompilerParams(dimension_semantics=sem, vmem_limit_bytes=VMEM_LIMIT)


def _resident(shape):
    nd = len(shape)
    return pl.BlockSpec(shape, lambda *_: (0,) * nd, pipeline_mode=pl.Buffered(1))


def _sigmoid(x):
    return 0.5 * jnp.tanh(0.5 * x) + 0.5


def _cast_weight_once(w_ref, wb_ref, cols, transposed=False):
    @pl.when(pl.program_id(0) == 0)
    def _():
        for c0 in range(0, wb_ref.shape[1], cols):
            src = w_ref[c0:c0 + cols, :].T if transposed else w_ref[:, c0:c0 + cols]
            wb_ref[:, c0:c0 + cols] = src.astype(BF16)


def _modulate(x, nw, shift, scale):
    ms = jnp.mean(x * x, axis=-1, keepdims=True)
    y = x * lax.rsqrt(ms + EPS)
    return (y * nw) * (1.0 + scale) + shift


def _mod_kernel(ct_ref, w_ref, b_ref, o_ref):
    ct = ct_ref[...]
    cat = ct * _sigmoid(ct)
    w = w_ref[0]
    rows = []
    for b in range(ct.shape[1]):
        col = jnp.broadcast_to(cat[:, b:b + 1], (ct.shape[0], LANES))
        rows.append(jnp.concatenate(
            [jnp.sum(w[:, j * LANES:(j + 1) * LANES] * col, axis=0, keepdims=True)
             for j in range(w.shape[1] // LANES)], axis=1))
    o_ref[0] = jnp.concatenate(rows, axis=0) + b_ref[0]


def _mod_call(c, w_mod, b_mod):
    depth, d, n = w_mod.shape
    b = c.shape[0]
    tn = MOD_TN
    return pl.pallas_call(
        _mod_kernel,
        grid=(depth, n // tn),
        in_specs=[
            pl.BlockSpec((d, b), lambda l, j: (0, 0)),
            pl.BlockSpec((1, d, tn), lambda l, j: (l, 0, j)),
            pl.BlockSpec((1, 1, tn), lambda l, j: (l, 0, j)),
        ],
        out_specs=pl.BlockSpec((1, b, tn), lambda l, j: (l, 0, j)),
        out_shape=jax.ShapeDtypeStruct((depth, b, n), F32),
        compiler_params=_cparams(("arbitrary", "arbitrary")),
        name="mod",
    )(c.T, w_mod, b_mod.reshape(depth, 1, n))


def _hg_proj_kernel(x_ref, mod_ref, nw_ref, lb_ref, wf32_ref, q_ref, k_ref, lf_ref, v_ref, g_ref, wmin_ref,
                    w_ref, *, layer, d):
    _cast_weight_once(wf32_ref, w_ref, d)
    mod = mod_ref[0]
    lbp = lb_ref[...]
    e = jnp.exp(lbp - jnp.max(lbp, axis=0, keepdims=True))
    lb = jnp.sum(e[0:layer + 1], axis=0, keepdims=True) / jnp.sum(e, axis=0, keepdims=True)

    w_r = lax.broadcasted_iota(jnp.int32, (ROW_SLAB // HG_SUB, ROW_SLAB), 0)
    w_c = lax.broadcasted_iota(jnp.int32, (ROW_SLAB // HG_SUB, ROW_SLAB), 1)
    win = (lax.shift_right_logical(w_c, HG_SUB.bit_length() - 1) == w_r).astype(BF16)
    lfs = []

    for r0 in range(0, x_ref.shape[0], ROW_SLAB):
        rows = slice(r0, r0 + ROW_SLAB)
        h = _modulate(x_ref[rows, :], nw_ref[...], mod[:, 0:d], mod[:, d:2 * d]).astype(BF16)

        pq = jnp.dot(h, w_ref[:, 0:d], preferred_element_type=F32)
        q_ref[rows, :] = (pq * _sigmoid(pq)).astype(BF16)

        z = jnp.dot(h, w_ref[:, d:2 * d], preferred_element_type=F32)
        th = 0.5 * jnp.tanh(0.5 * z)
        kk = (1.0 - lb) * (0.5 - th)
        f = lb + (1.0 - lb) * (0.5 + th)
        lf = jnp.log(jnp.where(kk < 0.5, 1.0 - kk, f))
        lf_ref[rows, :] = lf
        k_ref[rows, :] = kk.astype(BF16)
        lfs.append(lf.astype(BF16))

        v_ref[rows, :] = jnp.dot(h, w_ref[:, 2 * d:3 * d], preferred_element_type=F32).astype(BF16)
        pg = jnp.dot(h, w_ref[:, 3 * d:4 * d], preferred_element_type=F32)
        g_ref[rows, :] = (pg * _sigmoid(pg)).astype(BF16)

    wmin = None
    for lf16 in lfs:
        wsum = jnp.min(jnp.dot(win, lf16, preferred_element_type=F32), axis=0, keepdims=True)
        wmin = wsum if wmin is None else jnp.minimum(wmin, wsum)
    wmin_ref[0] = wmin


def _hg_proj_call(x2, mod3, nw, hg_lb, w_in, *, layer, batch, tm):
    t, d = x2.shape
    tiles_per_batch = (t // batch) // tm
    row = pl.BlockSpec((tm, d), lambda i: (i, 0))
    out_bf = jax.ShapeDtypeStruct((t, d), BF16)
    return pl.pallas_call(
        functools.partial(_hg_proj_kernel, layer=layer, d=d),
        grid=(t // tm,),
        in_specs=[
            row,
            pl.BlockSpec((1, 1, N_MOD * d), lambda i: (layer * batch + i // tiles_per_batch, 0, 0)),
            _resident((1, d)),
            _resident(hg_lb.shape),
            _resident(w_in.shape),
        ],
        out_specs=[row, row, row, row, row, pl.BlockSpec((1, 1, d), lambda i: (i, 0, 0))],
        out_shape=[out_bf, out_bf, jax.ShapeDtypeStruct((t, d), F32), out_bf, out_bf,
                   jax.ShapeDtypeStruct((t // tm, 1, d), F32)],
        scratch_shapes=[pltpu.VMEM(w_in.shape, BF16)],
        compiler_params=_cparams(("arbitrary",)),
        name="hg_proj",
    )(x2, mod3, nw, hg_lb, w_in)


def _hg_rec_kernel(wmin_ref, q_ref, k_ref, lf_ref, v_ref, gnw_ref, ca_ref, cb_ref, o_ref, cao_ref, cbo_ref,
                   st_ref, gs_ref, ks_ref, *, n_chunks, unroll_bounded, unroll_exact):
    cao_ref[...] = ca_ref[...].astype(BF16)
    cbo_ref[...] = cb_ref[...].astype(BF16)
    c, sub = HG_CHUNK, HG_SUB
    n_sub = c // sub
    half = sub // 2

    @pl.when(pl.program_id(2) == 0)
    def _():
        st_ref[...] = jnp.zeros_like(st_ref)

    r_i = lax.broadcasted_iota(jnp.int32, (c, c), 0)
    c_i = lax.broadcasted_iota(jnp.int32, (c, c), 1)
    tri = (c_i <= r_i).astype(BF16)
    ones_w = jnp.ones((HG_HEAD, c), BF16)
    half_row = lax.broadcasted_iota(jnp.int32, (half, HG_HEAD), 0)
    a_lane = lax.broadcasted_iota(jnp.int32, (half, c), 1)
    gnw = gnw_ref[...]
    nt = (((1,), (1,)), ((), ()))
    tn = (((0,), (0,)), ((), ()))

    def offdiag_operands(q, k, gcum, j, bounded):
        lo = j * sub
        hi = lo + sub if bounded else lo
        g_b = gcum[lo - 1:lo, :] if j > 0 else jnp.zeros((1, HG_HEAD), F32)
        qh = (q[lo:lo + sub, :] * jnp.exp2(gcum[lo:lo + sub, :] - g_b)).astype(BF16)
        kh = (k[0:hi, :] * jnp.exp2(g_b - gcum[0:hi, :])).astype(BF16)
        if hi < c:
            kh = jnp.concatenate([kh, jnp.zeros((c - hi, HG_HEAD), BF16)], axis=0)
        return qh, kh

    def diag_products(q, k, gcum, slot, j):
        lo = j * sub
        gt = (gcum[lo:lo + half, :], gcum[lo + half:lo + sub, :])
        qt = (q[lo:lo + half, :], q[lo + half:lo + sub, :])
        ps = []
        for s in range(sub):
            g_s = gs_ref[slot, pl.ds(lo + s, 1), :]
            k_s = ks_ref[slot, pl.ds(lo + s, 1), :]
            for hf in range(2):
                if s >= half and hf == 0:
                    continue
                dlt = gt[hf] - g_s
                if (s >= half) == (hf == 1):
                    dlt = jnp.where(half_row >= s % half, dlt, -jnp.inf)
                ps.append((qt[hf] * k_s) * jnp.exp2(dlt))
        return jnp.concatenate(ps, axis=0).astype(BF16)

    def diag_scatter(r, j):
        lo = j * sub
        a_top = jnp.zeros((half, c), F32)
        a_bot = jnp.zeros((half, c), F32)
        for s in range(sub):
            if s < half:
                a_top = jnp.where(a_lane == lo + s, r[2 * s * half:(2 * s + 1) * half, :], a_top)
                a_bot = jnp.where(a_lane == lo + s, r[(2 * s + 1) * half:(2 * s + 2) * half, :], a_bot)
            else:
                a_bot = jnp.where(a_lane == lo + s, r[(half + s) * half:(half + s + 1) * half, :], a_bot)
        return jnp.concatenate([a_top, a_bot], axis=0)

    def run(bounded, unroll):
        def group(i, st):
            us = range(unroll)
            rows = [pl.ds(pl.multiple_of((i * unroll + u) * c, c), c) for u in us]
            g3 = [jnp.dot(tri, _split3(lf_ref[r, :]), preferred_element_type=F32) for r in rows]
            q = [q_ref[r, :].astype(F32) for r in rows]
            k = [k_ref[r, :].astype(F32) for r in rows]
            gcum = [(g[:, 0:HG_HEAD] + g[:, HG_HEAD:2 * HG_HEAD] + g[:, 2 * HG_HEAD:]) * LOG2E for g in g3]
            g_last = [g[c - 1:c, :] for g in gcum]
            qg = [(q[u] * jnp.exp2(gcum[u])).astype(BF16) for u in us]
            kd = [(k[u] * jnp.exp2(g_last[u] - gcum[u])).astype(BF16) for u in us]
            first = 0 if bounded else 1
            off = [[offdiag_operands(q[u], k[u], gcum[u], j, bounded) for j in range(first, n_sub)] for u in us]
            if not bounded:
                for u in us:
                    gs_ref[u] = gcum[u]
                    ks_ref[u] = k[u]
                pst = [[diag_products(q[u], k[u], gcum[u], u, j) for j in range(n_sub)] for u in us]
            kv = [lax.dot_general(v_ref[rows[u], :], kd[u], tn, preferred_element_type=F32) for u in us]
            blocks = [[lax.dot_general(qh, kh, nt, preferred_element_type=F32) for qh, kh in off[u]] for u in us]
            if bounded:
                a = [jnp.where(c_i <= r_i, jnp.concatenate(blocks[u], axis=0), 0.0) for u in us]
            else:
                sums = [[jnp.dot(p, ones_w, preferred_element_type=F32) for p in pst[u]] for u in us]
                a = []
                for u in us:
                    a_rows = [diag_scatter(sums[u][j], j) for j in range(n_sub)]
                    for j in range(1, n_sub):
                        a_rows[j] = a_rows[j] + blocks[u][j - 1]
                    a.append(jnp.concatenate(a_rows, axis=0))
            o_intra = [jnp.dot(a[u].astype(BF16), v_ref[rows[u], :], preferred_element_type=F32) for u in us]
            for u in us:
                o = o_intra[u] + lax.dot_general(qg[u], st.astype(BF16), nt, preferred_element_type=F32)
                st = st * jnp.exp2(g_last[u]) + kv[u]
                on = o * lax.rsqrt(jnp.mean(o * o, axis=-1, keepdims=True) + EPS) * gnw
                o_ref[rows[u], :] = on.astype(BF16)
            return st

        st_ref[...] = lax.fori_loop(0, n_chunks // unroll, group, st_ref[...])

    blk_id = (pl.program_id(0) * pl.num_programs(1) + pl.program_id(1)) * pl.num_programs(2) + pl.program_id(2)
    bounded = wmin_ref[blk_id] * LOG2E >= -HG_SAFE_DECAY

    @pl.when(bounded)
    def _():
        run(True, unroll_bounded)

    @pl.when(jnp.logical_not(bounded))
    def _():
        run(False, unroll_exact)


def _hg_rec_call(wmin, q, k, lf, v, gnw, cast_a, cast_b, *, batch, tc, unroll_bounded, unroll_exact):
    t, d = q.shape
    heads = d // HG_HEAD
    steps = (t // batch) // tc
    n_steps = batch * heads * steps
    unroll = unroll_exact
    blk = pl.BlockSpec((tc, HG_HEAD), lambda b, h, s, *_: (b * steps + s, h))

    def cast_spec(a):
        assert a.shape[0] % (n_steps * 2 * SUBLANES) == 0
        return pl.BlockSpec((a.shape[0] // n_steps, a.shape[1]), lambda b, h, s, *_: ((b * heads + h) * steps + s, 0))

    grid_spec = pltpu.PrefetchScalarGridSpec(
        num_scalar_prefetch=1,
        grid=(batch, heads, steps),
        in_specs=[blk, blk, blk, blk, pl.BlockSpec((1, HG_HEAD), lambda b, h, s, *_: (0, 0)),
                  cast_spec(cast_a), cast_spec(cast_b)],
        out_specs=[blk, cast_spec(cast_a), cast_spec(cast_b)],
        scratch_shapes=[
            pltpu.VMEM((HG_HEAD, HG_HEAD), F32),
            pltpu.VMEM((unroll, HG_CHUNK, HG_HEAD), F32),
            pltpu.VMEM((unroll, HG_CHUNK, HG_HEAD), F32),
        ],
    )
    return pl.pallas_call(
        functools.partial(_hg_rec_kernel, n_chunks=tc // HG_CHUNK, unroll_bounded=unroll_bounded,
                          unroll_exact=unroll_exact),
        grid_spec=grid_spec,
        out_shape=[jax.ShapeDtypeStruct((t, d), BF16), jax.ShapeDtypeStruct(cast_a.shape, BF16),
                   jax.ShapeDtypeStruct(cast_b.shape, BF16)],
        compiler_params=_cparams(("arbitrary", "arbitrary", "arbitrary")),
        name="hg_rec",
    )(wmin, q, k, lf, v, gnw, cast_a, cast_b)


def _head_rmsnorm(p, w_row):
    low = lax.broadcasted_iota(jnp.int32, (1, LANES), 1) < FOX_HEAD
    outs = []
    for j in range(p.shape[1] // LANES):
        pj = p[:, j * LANES:(j + 1) * LANES]
        ss = pj * pj
        s_lo = jnp.sum(jnp.where(low, ss, 0.0), axis=-1, keepdims=True)
        s_hi = jnp.sum(jnp.where(low, 0.0, ss), axis=-1, keepdims=True)
        inv = lax.rsqrt(jnp.where(low, s_lo, s_hi) * (1.0 / FOX_HEAD) + EPS)
        outs.append(pj * inv)
    return jnp.concatenate(outs, axis=1) * w_row


def _fox_proj_kernel(x_ref, mod_ref, nw_ref, qw_ref, kw_ref, bf_ref, wf32_ref, wf_ref,
                     hqw_ref, hkw_ref, pq_ref, pk_ref, oq_ref, ok_ref,
                     q_ref, k_ref, v_ref, g_ref, aq_ref, ak_ref, edge_ref, flag_ref,
                     vs_ref, w_ref, carry_ref, *, d, tiles_per_batch):
    _cast_weight_once(wf32_ref.at[0], w_ref, d, transposed=True)

    @pl.when(pl.program_id(0) % tiles_per_batch == 0)
    def _():
        carry_ref[...] = jnp.zeros_like(carry_ref)

    mod = mod_ref[0]
    scale = LOG2E / np.sqrt(FOX_HEAD)
    lfs = []
    for r0 in range(0, x_ref.shape[0], ROW_SLAB):
        rows = slice(r0, r0 + ROW_SLAB)
        h = _modulate(x_ref[rows, :], nw_ref[...], mod[:, 0:d], mod[:, d:2 * d]).astype(BF16)
        pq = jnp.dot(h, w_ref[:, 0:d], preferred_element_type=F32)
        q_ref[rows, :] = (_head_rmsnorm(pq, qw_ref[...]) * scale).astype(BF16)
        pk = jnp.dot(h, w_ref[:, d:2 * d], preferred_element_type=F32)
        k_ref[rows, :] = _head_rmsnorm(pk, kw_ref[...]).astype(BF16)
        vs_ref[...] = jnp.dot(h, w_ref[:, 2 * d:3 * d], preferred_element_type=F32)
        v_ref[:, rows] = vs_ref[...].T.astype(BF16)
        pg = jnp.dot(h, w_ref[:, 3 * d:4 * d], preferred_element_type=F32)
        g_ref[rows, :] = _sigmoid(pg).astype(BF16)
        u = jnp.dot(h, wf_ref[...], preferred_element_type=F32) + bf_ref[...]
        lfs.append(jnp.minimum(u, 0.0) - jnp.log1p(jnp.exp(-jnp.abs(u))))

    tm = x_ref.shape[0]
    bound = FOX_BOUND_SLACK * LOG2E * np.sqrt(FOX_HEAD) * jnp.max(jnp.abs(hqw_ref[...] * hkw_ref[...]), axis=-1, keepdims=True)
    fast = bound <= FOX_STAB_MAX
    flag_ref[...] = jnp.broadcast_to(fast.astype(jnp.int32), flag_ref.shape)
    stab = jnp.where(fast, bound, 0.0)
    r_i = lax.broadcasted_iota(jnp.int32, (ROW_SLAB, ROW_SLAB), 0)
    c_i = lax.broadcasted_iota(jnp.int32, (ROW_SLAB, ROW_SLAB), 1)
    tri = (c_i <= r_i).astype(BF16)
    fs, carry = [], carry_ref[...]
    for lf in lfs:
        f3 = jnp.dot(tri, _split3(lf), preferred_element_type=F32)
        fs.append(f3[:, 0:LANES] + f3[:, LANES:2 * LANES] + f3[:, 2 * LANES:] + carry)
        carry = fs[-1][ROW_SLAB - 1:ROW_SLAB, :]
    f = jnp.concatenate(fs, axis=0)
    carry_ref[...] = carry
    f2 = f * LOG2E
    edge_ref[0] = jnp.concatenate([f2[0:1, :], f2[tm - 1:tm, :]], axis=0)
    aq_ref[...] = (jnp.dot(_split3(f2 - stab), pq_ref[...], preferred_element_type=F32) + oq_ref[...]).astype(BF16)
    ak_ref[...] = (jnp.dot(_split3(f2), pk_ref[...], preferred_element_type=F32) + ok_ref[...]).astype(BF16)


def _fox_proj_call(x2, mod3, nw, qw, kw, bf, w_main, w_f, hqw, hkw, *, layer, w_index, batch, tm):
    t, d = x2.shape
    nh = w_f.shape[1]
    heads = d // FOX_HEAD
    tiles_per_batch = (t // batch) // tm
    pq = np.zeros((3 * LANES, LANES), np.float32)
    pk = np.zeros((3 * LANES, LANES), np.float32)
    oq = np.zeros((1, LANES), np.float32)
    ok = np.zeros((1, LANES), np.float32)
    for h in range(heads):
        base = FOX_AUG * h
        for i in range(3):
            pq[i * LANES + h, base + i] = 1.0
            ok[0, base + i] = 1.0
            oq[0, base + 3 + i] = 1.0
            pk[i * LANES + h, base + 3 + i] = -1.0
    row = pl.BlockSpec((tm, d), lambda i: (i, 0))
    lane_row = pl.BlockSpec((tm, LANES), lambda i: (i, 0))
    out_bf = jax.ShapeDtypeStruct((t, d), BF16)
    lane_bf = jax.ShapeDtypeStruct((t, LANES), BF16)
    return pl.pallas_call(
        functools.partial(_fox_proj_kernel, d=d, tiles_per_batch=tiles_per_batch),
        grid=(t // tm,),
        in_specs=[
            row,
            pl.BlockSpec((1, 1, N_MOD * d), lambda i: (layer * batch + i // tiles_per_batch, 0, 0)),
            _resident((1, d)), _resident((1, d)), _resident((1, d)), _resident((1, nh)),
            pl.BlockSpec((1, 4 * d, d), lambda i: (w_index, 0, 0), pipeline_mode=pl.Buffered(1)),
            _resident(w_f.shape),
            _resident((1, FOX_HEAD)), _resident((1, FOX_HEAD)),
            _resident(pq.shape), _resident(pk.shape), _resident(oq.shape), _resident(ok.shape),
        ],
        out_specs=[row, row, pl.BlockSpec((d, tm), lambda i: (0, i)), row, lane_row, lane_row,
                   pl.BlockSpec((1, 2, LANES), lambda i: (i, 0, 0)),
                   pl.BlockSpec((SUBLANES, LANES), lambda i: (0, 0))],
        out_shape=[out_bf, out_bf, jax.ShapeDtypeStruct((d, t), BF16), out_bf, lane_bf, lane_bf,
                   jax.ShapeDtypeStruct((t // tm, 2, LANES), F32),
                   jax.ShapeDtypeStruct((SUBLANES, LANES), jnp.int32)],
        scratch_shapes=[pltpu.VMEM((ROW_SLAB, d), F32),
                        pltpu.VMEM((d, 4 * d), BF16),
                        pltpu.VMEM((1, LANES), F32)],
        compiler_params=_cparams(("arbitrary",)),
        name="fox_proj",
    )(x2, mod3, nw, qw, kw, bf, w_main, w_f, hqw, hkw,
      jnp.asarray(pq, BF16), jnp.asarray(pk, BF16), jnp.asarray(oq), jnp.asarray(ok))


FOX_AUG = 6


def _split3(x):
    hi = x.astype(BF16)
    r1 = x - hi.astype(F32)
    mid = r1.astype(BF16)
    lo = (r1 - mid.astype(F32)).astype(BF16)
    return jnp.concatenate([hi, mid, lo], axis=1)


FOX_VROWS = 80


def _fox_attn_kernel(flag_ref, fs_ref, fe_ref, q_ref, aq_ref, k_ref, ak_ref, v_ref, o_ref,
                     v1_ref, qm_ref, acc_ref, m_ref, pa_ref, pb_ref, *, tq, tk, heads):
    bi, pair, qi = pl.program_id(0), pl.program_id(1), pl.program_id(2)
    nk = k_ref.shape[0] // tk
    sub = tq // tk
    pad_rows = lax.broadcasted_iota(jnp.int32, (FOX_VROWS - FOX_HEAD, tk), 0)
    ones_blk = jnp.where(pad_rows == 0, 1.0, 0.0).astype(BF16)
    lane2 = lax.broadcasted_iota(jnp.int32, (1, 2 * LANES), 1)
    bias0 = LANES + 2 * FOX_AUG * pair
    head0 = (lane2 < FOX_HEAD) | ((lane2 >= bias0) & (lane2 < bias0 + FOX_AUG))
    head1 = ((lane2 >= FOX_HEAD) & (lane2 < LANES)) | ((lane2 >= bias0 + FOX_AUG) & (lane2 < bias0 + 2 * FOX_AUG))
    nt = (((1,), (1,)), ((), ()))

    @pl.when(qi == 0)
    def _():
        def build(j, carry):
            cols = pl.ds(pl.multiple_of(j * tk, tk), tk)
            for e in range(2):
                v1_ref[j, e] = jnp.concatenate([v_ref[e * FOX_HEAD:(e + 1) * FOX_HEAD, cols], ones_blk], axis=0)
            return carry

        lax.fori_loop(0, nk, build, 0)

    def probs(j, online, diag=None, dst=None):
        rows = pl.ds(pl.multiple_of(j * tk, tk), tk)
        kf = jnp.concatenate([k_ref[rows, :], ak_ref[rows, :]], axis=1)
        q_lo = 0 if diag is None else diag * tk
        ps, alphas = [], []
        for e in range(2):
            s = lax.dot_general(kf, qm_ref[e, q_lo:tq, :], nt, preferred_element_type=F32)
            if diag is not None:
                kpos = lax.broadcasted_iota(jnp.int32, (tk, tk), 0)
                qpos = lax.broadcasted_iota(jnp.int32, (tk, tk), 1)
                blk = jnp.where(kpos <= qpos, s[:, 0:tk], -jnp.inf)
                s = blk if q_lo + tk == tq else jnp.concatenate([blk, s[:, tk:]], axis=1)
            if online:
                m_prev = m_ref[e, :, q_lo:tq]
                m_new = jnp.maximum(m_prev, jnp.max(s, axis=0, keepdims=True))
                alphas.append(jnp.exp2(m_prev - m_new))
                m_ref[e, :, q_lo:tq] = m_new
                s = s - m_new
            p = jnp.exp2(s).astype(BF16)
            if dst is None:
                ps.append(p)
            else:
                dst[e] = p
        return ps, alphas

    def accumulate(ps, j, alphas=None, diag=None):
        q_lo = 0 if diag is None else diag * tk
        for e in range(2):
            pv = jnp.dot(v1_ref[j, e], ps[e], preferred_element_type=F32)
            if alphas:
                acc_ref[e, :, q_lo:tq] = acc_ref[e, :, q_lo:tq] * alphas[e] + pv
            else:
                acc_ref[e, :, q_lo:tq] += pv

    def start():
        qf = jnp.concatenate([q_ref[...], aq_ref[...]], axis=1)
        zero = jnp.zeros_like(qf)
        qm_ref[0] = jnp.where(head0, qf, zero)
        qm_ref[1] = jnp.where(head1, qf, zero)
        acc_ref[...] = jnp.zeros_like(acc_ref)

    def finish():
        outs = []
        for e in range(2):
            acc = acc_ref[e]
            outs.append(acc[0:FOX_HEAD, :] / acc[FOX_HEAD:FOX_HEAD + 1, :])
        o_ref[...] = jnp.concatenate(outs, axis=0).astype(BF16)

    n_full = qi * sub

    @pl.when(flag_ref[0] == 1)
    def _():
        start()
        held = None
        for a in range(sub - 1, 0, -1):
            nxt = probs(n_full + a, False, diag=a)[0]
            if held is not None:
                accumulate(held[0], n_full + held[1], diag=held[1])
            held = (nxt, a)
        probs(n_full, False, diag=0, dst=pa_ref)
        if held is not None:
            accumulate(held[0], n_full + held[1], diag=held[1])

        h0 = bi * heads + 2 * pair
        fq0 = fs_ref[h0 * nk + n_full]
        fq1 = fs_ref[(h0 + 1) * nk + n_full]

        def dead(j, cnt):
            gone0 = fq0 - fe_ref[h0 * nk + j] < -FOX_SKIP
            gone1 = fq1 - fe_ref[(h0 + 1) * nk + j] < -FOX_SKIP
            return cnt + jnp.logical_and(gone0, gone1).astype(jnp.int32)

        first = lax.fori_loop(0, n_full, dead, 0)
        n_live = n_full - first

        def two_tiles(i, carry):
            j = first + 2 * i
            probs(j, False, dst=pb_ref)
            accumulate(pa_ref, jnp.where(i == 0, n_full, j - 1))
            probs(j + 1, False, dst=pa_ref)
            accumulate(pb_ref, j)
            return carry

        lax.fori_loop(0, n_live // 2, two_tiles, 0)

        @pl.when(n_live % 2 == 1)
        def _():
            ps, _ = probs(n_full - 1, False)
            accumulate(pa_ref, jnp.where(n_live == 1, n_full, n_full - 2))
            accumulate(ps, n_full - 1)

        @pl.when(n_live % 2 == 0)
        def _():
            accumulate(pa_ref, jnp.where(n_live == 0, n_full, n_full - 1))

        finish()

    @pl.when(flag_ref[0] != 1)
    def _():
        start()
        m_ref[...] = jnp.full_like(m_ref, -jnp.inf)

        def body(j, carry):
            ps, alphas = probs(j, True)
            accumulate(ps, j, alphas)
            return carry

        lax.fori_loop(0, n_full, body, 0)
        for a in range(sub):
            ps, alphas = probs(n_full + a, True, diag=a)
            accumulate(ps, n_full + a, alphas, diag=a)
        finish()


def _fox_attn_call(flag, f_first, f_last, q, aq, k, ak, v, *, batch, seq, tq, tk):
    t, d = q.shape
    pairs = d // LANES
    nq = seq // tq
    q_spec = pl.BlockSpec((tq, LANES), lambda b, p, i, *_: (b * nq + i, p))
    kv_spec = pl.BlockSpec((seq, LANES), lambda b, p, i, *_: (b, p))
    grid_spec = pltpu.PrefetchScalarGridSpec(
        num_scalar_prefetch=3,
        grid=(batch, pairs, nq),
        in_specs=[q_spec, pl.BlockSpec((tq, LANES), lambda b, p, i, *_: (b * nq + i, 0)),
                  kv_spec, pl.BlockSpec((seq, LANES), lambda b, p, i, *_: (b, 0)),
                  pl.BlockSpec((LANES, seq), lambda b, p, i, *_: (p, b))],
        out_specs=pl.BlockSpec((LANES, tq), lambda b, p, i, *_: (p, b * nq + i)),
        scratch_shapes=[
            pltpu.VMEM((seq // tk, 2, FOX_VROWS, tk), BF16),
            pltpu.VMEM((2, tq, 2 * LANES), BF16),
            pltpu.VMEM((2, FOX_VROWS, tq), F32),
            pltpu.VMEM((2, 1, tq), F32),
            pltpu.VMEM((2, tk, tq), BF16),
            pltpu.VMEM((2, tk, tq), BF16),
        ],
    )
    return pl.pallas_call(
        functools.partial(_fox_attn_kernel, tq=tq, tk=tk, heads=d // FOX_HEAD),
        grid_spec=grid_spec,
        out_shape=jax.ShapeDtypeStruct((d, t), BF16),
        compiler_params=_cparams(("arbitrary", "arbitrary", "arbitrary")),
        name="fox_attn",
    )(flag, f_first, f_last, q, aq, k, ak, v)


def _post_kernel(x_ref, y_ref, gate_ref, mod_ref, nw_ref, fw_ref, wo_ref, w1_ref, w2_ref, o_ref, *,
                 d, ff_blk, final, y_features):
    mod = mod_ref[0]
    g1 = mod[:, 2 * d:3 * d]
    sh2, sc2, g2 = mod[:, 3 * d:4 * d], mod[:, 4 * d:5 * d], mod[:, 5 * d:6 * d]
    y = y_ref[...].astype(F32)
    if y_features:
        y = y.T
    yg = (y * gate_ref[...].astype(F32)).astype(BF16)
    x1 = x_ref[...] + g1 * jnp.dot(yg, wo_ref[...], preferred_element_type=F32)
    h = _modulate(x1, nw_ref[...], sh2, sc2).astype(BF16)
    acc = jnp.zeros_like(x1)
    for j in range(w1_ref.shape[2] // ff_blk):
        a = jnp.maximum(jnp.dot(h, w1_ref[0, :, j * ff_blk:(j + 1) * ff_blk], preferred_element_type=F32), 0.0)
        acc = acc + jnp.dot((a * a).astype(BF16), w2_ref[0, j * ff_blk:(j + 1) * ff_blk, :],
                            preferred_element_type=F32)
    x2 = x1 + g2 * acc
    if final:
        x2 = x2 * lax.rsqrt(jnp.mean(x2 * x2, axis=-1, keepdims=True) + EPS) * fw_ref[...]
    o_ref[...] = x2


def _post_call(x2, y, gate, mod3, nw, fw, w_out, w1, w2, *, layer, batch, tm, final, y_features):
    t, d = x2.shape
    tiles_per_batch = (t // batch) // tm
    row = pl.BlockSpec((tm, d), lambda i: (i, 0))
    assert y.shape == ((d, t) if y_features else (t, d))
    return pl.pallas_call(
        functools.partial(_post_kernel, d=d, ff_blk=POST_FF_BLK, final=final, y_features=y_features),
        grid=(t // tm,),
        in_specs=[
            row,
            pl.BlockSpec((d, tm), lambda i: (0, i)) if y_features else row,
            row,
            pl.BlockSpec((1, 1, N_MOD * d), lambda i: (layer * batch + i // tiles_per_batch, 0, 0)),
            _resident((1, d)), _resident((1, d)),
            _resident(w_out.shape),
            pl.BlockSpec((1,) + w1.shape[1:], lambda i: (layer, 0, 0), pipeline_mode=pl.Buffered(1)),
            pl.BlockSpec((1,) + w2.shape[1:], lambda i: (layer, 0, 0), pipeline_mode=pl.Buffered(1)),
        ],
        out_specs=row,
        out_shape=jax.ShapeDtypeStruct((t, d), F32),
        compiler_params=_cparams(("arbitrary",)),
        name="post",
    )(x2, y, gate, mod3, nw, fw, w_out, w1, w2)


def kernel(x, c, w_mod, b_mod, norm1_w, norm2_w, hg_w_in, hg_w_out, hg_lb, hg_gn_w, fox_w_in, fox_b_f,
           fox_qn_w, fox_kn_w, fox_w_out, mlp_w1, mlp_w2, final_w):
    batch, seq, d = x.shape
    depth = w_mod.shape[0]
    t = batch * seq
    fox_heads = d // FOX_HEAD
    assert seq % HG_BLOCK == 0 and seq % ATTN_TQ == 0 and HG_BLOCK % PROJ_TM == 0 and PROJ_TM % ROW_SLAB == 0
    assert d % LANES == 0 and fox_heads * FOX_AUG <= LANES and (N_MOD * d) % MOD_TN == 0
    assert mlp_w1.shape[2] % POST_FF_BLK == 0 and x.dtype == F32
    assert PROJ_TM == ATTN_TK

    mod3 = _mod_call(c, w_mod, b_mod).reshape(depth * batch, 1, N_MOD * d)
    xs = x.reshape(t, d)
    fw = final_w.reshape(1, d)
    ff = mlp_w1.shape[2]

    for i in range(depth):
        j = i // 2
        n1 = norm1_w[i].reshape(1, d)
        if i % 2 == 0:
            q, k, lf, v, g, wmin = _hg_proj_call(xs, mod3, n1, hg_lb, hg_w_in[j],
                                                 layer=i, batch=batch, tm=PROJ_TM)
            wmin = wmin.reshape(batch, seq // HG_BLOCK, HG_BLOCK // PROJ_TM, d // HG_HEAD, HG_HEAD).min(axis=(2, 4))
            y, w1_all, w2_all = _hg_rec_call(wmin.transpose(0, 2, 1).reshape(-1), q, k, lf, v,
                                             hg_gn_w[j].reshape(1, HG_HEAD),
                                             mlp_w1.reshape(depth * d, ff), mlp_w2.reshape(depth * ff, d),
                                             batch=batch, tc=HG_BLOCK,
                                             unroll_bounded=HG_GROUP, unroll_exact=HG_GROUP_EXACT)
            w1_all, w2_all = w1_all.reshape(depth, d, ff), w2_all.reshape(depth, ff, d)
            w_out = hg_w_out[j]
        else:
            pad = ((0, 0), (0, LANES - fox_heads))
            q, k, v, g, aq, ak, edge, flag = _fox_proj_call(
                xs, mod3, n1,
                jnp.tile(fox_qn_w[j], fox_heads).reshape(1, d),
                jnp.tile(fox_kn_w[j], fox_heads).reshape(1, d),
                jnp.pad(fox_b_f[j].reshape(1, fox_heads), pad),
                jnp.swapaxes(fox_w_in, 1, 2), jnp.pad(fox_w_in[j, :, 4 * d:], pad).astype(BF16),
                fox_qn_w[j].reshape(1, FOX_HEAD), fox_kn_w[j].reshape(1, FOX_HEAD),
                layer=i, w_index=j, batch=batch, tm=PROJ_TM)
            edge = edge.reshape(batch, seq // ATTN_TK, 2, LANES)[..., :fox_heads].transpose(2, 0, 3, 1)
            y = _fox_attn_call(flag[0, :1], edge[0].reshape(-1), edge[1].reshape(-1), q, aq, k, ak, v,
                               batch=batch, seq=seq, tq=ATTN_TQ, tk=ATTN_TK)
            w_out = fox_w_out[j]
        xs = _post_call(xs, y, g, mod3, norm2_w[i].reshape(1, d), fw, w_out.astype(BF16), w1_all, w2_all,
                        layer=i, batch=batch, tm=PROJ_TM, final=(i == depth - 1), y_features=(i % 2 == 1))
    return xs.reshape(batch, seq, d)
```
